```python
import math
import jax
import jax.numpy as jnp
from jax import lax
import numpy as np

D_MODEL = 2048
BATCH = 16
SEQ = 256
DEPTH = 2
DEC_BATCH = 2
DEC_SEQ = 1024
PAST_LEN = 512

GRID_W = 64
N_BRANCH = 4
BRANCH_W = 512
GDN_H = 4
GDN_DK = 128
GDN_DV = 128
GDN_CONV = 5
DELTA_CHUNK = 64
GLA_H = 4
GLA_DK = 64
GLA_DV = 128
GLA_RANK = 16
GLA_TAU = 16.0
GLA_CHUNK = 16
HG_H = 4
HG_DK = 128
HG_DV = 128
MLA_H = 4
MLA_NOPE = 128
MLA_ROPE = 64
MLA_V = 128
Q_LORA = 512
KV_LORA = 512
ATTN_QBLOCK = 128
ROPE_BASE = 10000.0
D_FF = ((8 * D_MODEL // 3 + 255) // 256) * 256
NORM_EPS = 1e-6

IN_SPLITS = (
    ('gdn_q', GDN_H * GDN_DK), ('gdn_k', GDN_H * GDN_DK), ('gdn_v', GDN_H * GDN_DV),
    ('gdn_z', GDN_H * GDN_DV), ('gdn_b', 2 * GDN_H), ('gdn_a', 2 * GDN_H),
    ('gla_q', GLA_H * GLA_DK), ('gla_k', GLA_H * GLA_DK), ('gla_v', GLA_H * GLA_DV),
    ('gla_r', GLA_H * GLA_DV), ('gla_g', 2 * GLA_RANK),
    ('hg_q', HG_H * HG_DK), ('hg_f', 2 * HG_H * HG_DK), ('hg_i', HG_H * HG_DV),
    ('hg_g', HG_H * HG_DV),
    ('mla_qa', Q_LORA), ('mla_kva', KV_LORA), ('mla_kpe', MLA_ROPE),
    ('gates', N_BRANCH * D_MODEL),
)
IN_WIDTH = sum(n for _, n in IN_SPLITS)
GDN_QKV = 2 * GDN_H * GDN_DK + GDN_H * GDN_DV

kernel_name = 'hybrid_diffusion_step'


def _split_in(p):
    out, off = {}, 0
    for name, n in IN_SPLITS:
        out[name] = p[..., off:off + n]
        off += n
    return out


def _rmsnorm(x, w):
    xf = x.astype(jnp.float32)
    y = xf * lax.rsqrt(jnp.mean(xf * xf, axis=-1, keepdims=True) + NORM_EPS)
    return (y * w.astype(jnp.float32)).astype(x.dtype)


def _layernorm(x, g, b):
    xf = x.astype(jnp.float32)
    mu = jnp.mean(xf, axis=-1, keepdims=True)
    var = jnp.mean(jnp.square(xf - mu), axis=-1, keepdims=True)
    y = (xf - mu) * lax.rsqrt(var + NORM_EPS) * g.astype(jnp.float32) + b.astype(jnp.float32)
    return y.astype(x.dtype)


def _l2norm(x):
    xf = x.astype(jnp.float32)
    return (xf * lax.rsqrt(jnp.sum(xf * xf, axis=-1, keepdims=True) + NORM_EPS)).astype(x.dtype)


def _heads(x, h):
    b, l, _ = x.shape
    return x.reshape(b, l, h, -1).transpose(0, 2, 1, 3)


def _merge_heads(x):
    b, h, l, d = x.shape
    return x.transpose(0, 2, 1, 3).reshape(b, l, h * d)


def _flip(x):
    return jnp.flip(x, axis=2)


def _centred_conv(x, w):
    pad = w.shape[0] // 2
    return lax.conv_general_dilated(
        x, w[:, None, :].astype(x.dtype), window_strides=(1,), padding=((pad, pad),),
        dimension_numbers=('NWC', 'WIO', 'NWC'), feature_group_count=x.shape[-1])


def _axial_angles(length):
    rows = length // GRID_W
    pos = jnp.arange(rows * GRID_W)
    row_id = (pos // GRID_W).astype(jnp.float32)
    col_id = (pos % GRID_W).astype(jnp.float32)
    half = MLA_ROPE // 2
    inv = ROPE_BASE ** (-jnp.arange(0, half, 2, dtype=jnp.float32) / half)
    return row_id[:, None] * inv, col_id[:, None] * inv


def _rotate(x, ang):
    m = x.shape[-1] // 2
    x1, x2 = x[..., :m], x[..., m:]
    cos, sin = jnp.cos(ang), jnp.sin(ang)
    return jnp.concatenate([x1 * cos - x2 * sin, x1 * sin + x2 * cos], axis=-1)


def _axial_rope(x, ang_row, ang_col):
    half = MLA_ROPE // 2
    xf = x.astype(jnp.float32)
    y = jnp.concatenate([_rotate(xf[..., :half], ang_row), _rotate(xf[..., half:], ang_col)], axis=-1)
    return y.astype(x.dtype)


def _gated_delta_chunked(q, k, v, g, beta, s0):
    out_dtype = v.dtype
    bsz, nh, length, _ = q.shape
    dv = v.shape[-1]
    n = length // DELTA_CHUNK
    f32 = jnp.float32
    cs = lambda t: t.astype(f32).reshape(bsz, nh, n, DELTA_CHUNK, t.shape[-1])
    q, k, v = cs(q), cs(k), cs(v)
    g = g.astype(f32).reshape(bsz, nh, n, DELTA_CHUNK)
    beta = beta.astype(f32).reshape(bsz, nh, n, DELTA_CHUNK)
    gam = jnp.cumsum(g, axis=-1)
    idx = jnp.arange(DELTA_CHUNK)
    incl = idx[:, None] >= idx[None, :]
    strict = idx[:, None] > idx[None, :]
    decay = jnp.exp(jnp.where(incl, gam[..., :, None] - gam[..., None, :], -jnp.inf))
    kb = k * beta[..., None]
    m = jnp.where(strict, jnp.einsum('bhnid,bhnjd->bhnij', kb, k) * decay, 0.0)
    rhs = jnp.concatenate([v * beta[..., None], kb * jnp.exp(gam)[..., None]], axis=-1)
    sol = lax.linalg.triangular_solve(m + jnp.eye(DELTA_CHUNK, dtype=f32), rhs,
                                      left_side=True, lower=True, unit_diagonal=True)
    u_base, w = sol[..., :dv], sol[..., dv:]
    a_qk = jnp.einsum('bhnid,bhnjd->bhnij', q, k) * decay
    q_dec = q * jnp.exp(gam)[..., None]
    k_dec = k * jnp.exp(gam[..., -1:] - gam)[..., None]
    c_dec = jnp.exp(gam[..., -1])

    def step(s, xs):
        u_c, w_c, a_c, q_c, k_c, d_c = xs
        u = u_c - jnp.einsum('bhcd,bhdv->bhcv', w_c, s)
        o = jnp.einsum('bhcd,bhdv->bhcv', q_c, s) + jnp.einsum('bhij,bhjv->bhiv', a_c, u)
        s = d_c[..., None, None] * s + jnp.einsum('bhcd,bhcv->bhdv', k_c, u)
        return s, o

    xs = tuple(jnp.moveaxis(t, 2, 0) for t in (u_base, w, a_qk, q_dec, k_dec, c_dec))
    s_fin, o = lax.scan(step, s0.astype(f32), xs)
    o = jnp.moveaxis(o, 0, 2).reshape(bsz, nh, length, dv)
    return o.astype(out_dtype), s_fin


def _gla_chunked(q, k, v, log_a, s0):
    out_dtype = v.dtype
    bsz, nh, length, _ = q.shape
    dv = v.shape[-1]
    n = length // GLA_CHUNK
    f32 = jnp.float32
    cs = lambda t: t.astype(f32).reshape(bsz, nh, n, GLA_CHUNK, t.shape[-1])
    q, k, v, log_a = cs(q), cs(k), cs(v), cs(log_a)
    b = jnp.cumsum(log_a, axis=-2)
    idx = jnp.arange(GLA_CHUNK)
    incl = (idx[:, None] >= idx[None, :])[:, :, None]
    pair = jnp.exp(jnp.where(incl, b[..., :, None, :] - b[..., None, :, :], -jnp.inf))
    a_qk = jnp.einsum('bhnid,bhnjd,bhnijd->bhnij', q, k, pair)
    o_intra = jnp.einsum('bhnij,bhnjv->bhniv', a_qk, v)
    q_dec = q * jnp.exp(b)
    k_dec = k * jnp.exp(b[..., -1:, :] - b)
    c_dec = jnp.exp(b[..., -1, :])

    def step(s, xs):
        q_c, k_c, v_c, d_c = xs
        o = jnp.einsum('bhcd,bhdv->bhcv', q_c, s)
        s = d_c[..., None] * s + jnp.einsum('bhcd,bhcv->bhdv', k_c, v_c)
        return s, o

    xs = tuple(jnp.moveaxis(t, 2, 0) for t in (q_dec, k_dec, v, c_dec))
    s_fin, o_inter = lax.scan(step, s0.astype(f32), xs)
    o = o_intra + jnp.moveaxis(o_inter, 0, 2)
    return o.reshape(bsz, nh, length, dv).astype(out_dtype), s_fin


def _gdn_branch(pr, conv_w, a_log, dt_bias, norm_w, s0):
    bsz, length, _ = pr['gdn_q'].shape
    qkv = jnp.concatenate([pr['gdn_q'], pr['gdn_k'], pr['gdn_v']], axis=-1)
    qkv = jax.nn.silu(_centred_conv(qkv, conv_w))
    q, k, v = jnp.split(qkv, [GDN_H * GDN_DK, 2 * GDN_H * GDN_DK], axis=-1)
    q = _l2norm(_heads(q, GDN_H)) * (GDN_DK ** -0.5)
    k = _l2norm(_heads(k, GDN_H))
    v = _heads(v, GDN_H)
    f32 = jnp.float32
    beta = jax.nn.sigmoid(pr['gdn_b'].astype(f32)).reshape(bsz, length, 2, GDN_H).transpose(2, 0, 3, 1)
    a = pr['gdn_a'].astype(f32).reshape(bsz, length, 2, GDN_H).transpose(2, 0, 3, 1)
    g = -jnp.exp(a_log.astype(f32))[:, None, :, None] * jax.nn.softplus(
        a + dt_bias.astype(f32)[:, None, :, None])
    o_f, s_f = _gated_delta_chunked(q, k, v, g[0], beta[0], s0[:, 0])
    o_b, s_b = _gated_delta_chunked(_flip(q), _flip(k), _flip(v), _flip(g[1]), _flip(beta[1]), s0[:, 1])
    o = _rmsnorm(o_f + _flip(o_b), norm_w) * jax.nn.silu(_heads(pr['gdn_z'], GDN_H))
    return _merge_heads(o), jnp.stack([s_f, s_b], axis=1)


def _gla_branch(pr, gate_w2, gate_b, norm_w, s0):
    bsz, length, _ = pr['gla_q'].shape
    q = _heads(pr['gla_q'], GLA_H) * (GLA_DK ** -0.5)
    k = _heads(pr['gla_k'], GLA_H)
    v = _heads(pr['gla_v'], GLA_H)
    glr = pr['gla_g'].reshape(bsz, length, 2, GLA_RANK)
    logits = jnp.einsum('bltr,trk->tblk', glr, gate_w2) + gate_b[:, None, None, :]
    log_a = jax.nn.log_sigmoid(logits.astype(jnp.float32)) / GLA_TAU
    log_a = log_a.reshape(2, bsz, length, GLA_H, GLA_DK).transpose(0, 1, 3, 2, 4)
    o_f, s_f = _gla_chunked(q, k, v, log_a[0], s0[:, 0])
    o_b, s_b = _gla_chunked(_flip(q), _flip(k), _flip(v), _flip(log_a[1]), s0[:, 1])
    o = _rmsnorm(o_f + _flip(o_b), norm_w) * jax.nn.silu(_heads(pr['gla_r'], GLA_H))
    return _merge_heads(o), jnp.stack([s_f, s_b], axis=1)


def _hgrn_branch(pr, lb, norm_w, s0):
    bsz, length, _ = pr['hg_q'].shape
    q = jax.nn.silu(_heads(pr['hg_q'], HG_H))
    v = _heads(pr['hg_i'], HG_H)
    zf = pr['hg_f'].astype(jnp.float32).reshape(bsz, length, 2, HG_H * HG_DK).transpose(2, 0, 1, 3)
    lbb = lb[:, None, None, :]
    log_f = jnp.logaddexp(jnp.log(lbb), jnp.log1p(-lbb) + jax.nn.log_sigmoid(zf))
    one_minus_f = (1.0 - lbb) * jax.nn.sigmoid(-zf)
    to_h = lambda t: t.reshape(2, bsz, length, HG_H, HG_DK).transpose(0, 1, 3, 2, 4)
    log_f, one_minus_f = to_h(log_f), to_h(one_minus_f)
    o_f, s_f = _gla_chunked(q, one_minus_f[0], v, log_f[0], s0[:, 0])
    o_b, s_b = _gla_chunked(_flip(q), _flip(one_minus_f[1]), _flip(v), _flip(log_f[1]), s0[:, 1])
    o = _rmsnorm(o_f + _flip(o_b), norm_w) * jax.nn.sigmoid(_heads(pr['hg_g'], HG_H))
    return _merge_heads(o), jnp.stack([s_f, s_b], axis=1)


def _mla_project(pr, q_norm, wq_b, kv_norm, ang):
    q = _heads(_rmsnorm(pr['mla_qa'], q_norm) @ wq_b, MLA_H)
    q_nope, q_pe = q[..., :MLA_NOPE], q[..., MLA_NOPE:]
    ckv = _rmsnorm(pr['mla_kva'], kv_norm)
    kpe = pr['mla_kpe']
    if ang is not None:
        q_pe = _axial_rope(q_pe, ang[0], ang[1])
        kpe = _axial_rope(kpe, ang[0], ang[1])
    return q_nope, q_pe, ckv, kpe


def _mla_attend(q_nope, q_pe, k_nope, k_pe, v):
    bsz, nh, lq, _ = q_nope.shape
    nb = lq // ATTN_QBLOCK
    scale = (MLA_NOPE + MLA_ROPE) ** -0.5

    def blocks(t):
        return t.reshape(bsz, nh, nb, ATTN_QBLOCK, t.shape[-1]).transpose(2, 0, 1, 3, 4)

    def attend(qs):
        qn, qp = qs
        s = jnp.einsum('bhqd,bhkd->bhqk', qn, k_nope) + jnp.einsum('bhqd,bkd->bhqk', qp, k_pe)
        p = jax.nn.softmax(s.astype(jnp.float32) * scale, axis=-1)
        return jnp.einsum('bhqk,bhkd->bhqd', p.astype(v.dtype), v)

    o = lax.map(attend, (blocks(q_nope), blocks(q_pe)))
    return o.transpose(1, 2, 0, 3, 4).reshape(bsz, nh, lq, MLA_V)


def _layer(x, mod, P, l, lb, s_gdn0, s_gla0, s_hg0, ctx_ckv, ctx_kpe, ang, alpha):
    bsz, length, _ = x.shape
    shift1, scale1, gate1, shift2, scale2, gate2 = jnp.split(mod[:, None, :], 6, axis=-1)
    h = x * (1.0 + scale1) + shift1
    pr = _split_in(h @ P['w_in'][l])
    o_gdn, s_gdn = _gdn_branch(pr, P['gdn_conv'][l], P['gdn_a_log'][l], P['gdn_dt_bias'][l],
                               P['gdn_norm'][l], s_gdn0)
    o_gla, s_gla = _gla_branch(pr, P['gla_gate_w2'][l], P['gla_gate_b'][l], P['gla_norm'][l], s_gla0)
    o_hg, s_hg = _hgrn_branch(pr, lb, P['hgrn_norm'][l], s_hg0)
    q_nope, q_pe, ckv, kpe = _mla_project(pr, P['mla_q_norm'][l], P['mla_wq_b'][l],
                                          P['mla_kv_norm'][l], ang)
    if ctx_ckv is None:
        keys_ckv, keys_kpe = ckv, kpe
    else:
        keys_ckv = jnp.concatenate([ckv, ctx_ckv.astype(ckv.dtype)], axis=1)
        keys_kpe = jnp.concatenate([kpe, ctx_kpe.astype(kpe.dtype)], axis=1)
    kv = _heads(keys_ckv @ P['mla_wkv_b'][l], MLA_H)
    o_mla = _merge_heads(_mla_attend(q_nope, q_pe, kv[..., :MLA_NOPE], keys_kpe, kv[..., MLA_NOPE:]))
    branches = jnp.stack([o_gdn, o_gla, o_hg, o_mla], axis=0)
    proj = jnp.einsum('kbln,knd->kbld', branches, P['w_branch'][l])
    gates = jax.nn.sigmoid(pr['gates'].reshape(bsz, length, N_BRANCH, D_MODEL) + P['b_gates'][l])
    merged = jnp.einsum('blkd,kbld->bld', gates, proj)
    x = _layernorm(alpha * x + gate1 * (merged @ P['w_out'][l]), P['ln1_g'][l], P['ln1_b'][l])
    h2 = x * (1.0 + scale2) + shift2
    ff = (jax.nn.silu(h2 @ P['ffn_w1'][l]) * (h2 @ P['ffn_w3'][l])) @ P['ffn_w2'][l]
    x = _layernorm(alpha * x + gate2 * ff, P['ln2_g'][l], P['ln2_b'][l])
    return x, s_gdn, s_gla, s_hg, ckv, kpe


def setup_inputs(seed: int = 0) -> dict:
    key = jax.random.key(seed)
    ks = iter(jax.random.split(key, 40))

    def nrm(shape, scale):
        return jax.random.normal(next(ks), shape, jnp.float32) * scale

    def gain(shape):
        return 1.0 + nrm(shape, 0.1)

    res_scale = (8.0 * DEPTH) ** -0.25
    x_prompt = nrm((BATCH, SEQ, D_MODEL), 1.0)
    x_sample = nrm((DEC_BATCH, DEC_SEQ, D_MODEL), 1.0)
    c = nrm((DEC_BATCH, D_MODEL), 1.0)
    state_gdn = nrm((DEC_BATCH, DEPTH, 2, GDN_H, GDN_DK, GDN_DV), 0.1)
    state_gla = nrm((DEC_BATCH, DEPTH, 2, GLA_H, GLA_DK, GLA_DV), 0.5)
    state_hgrn = nrm((DEC_BATCH, DEPTH, 2, HG_H, HG_DK, HG_DV), 0.5)
    cache_mla_ckv = nrm((DEC_BATCH, DEPTH, PAST_LEN, KV_LORA), 1.0)
    cache_mla_kpe = nrm((DEC_BATCH, DEPTH, PAST_LEN, MLA_ROPE), 1.0)
    c_ctx = nrm((D_MODEL,), 1.0)
    w_ada = nrm((DEPTH, D_MODEL, 6 * D_MODEL), 0.5 * D_MODEL ** -0.5)
    b_ada = nrm((DEPTH, 6 * D_MODEL), 0.02)
    w_in = nrm((DEPTH, D_MODEL, IN_WIDTH), D_MODEL ** -0.5)
    gdn_conv = nrm((DEPTH, GDN_CONV, GDN_QKV), GDN_CONV ** -0.5)
    gdn_a_log = jnp.log(jax.random.uniform(next(ks), (DEPTH, 2, GDN_H), jnp.float32, 1.0, 16.0))
    dt = jnp.exp(jax.random.uniform(next(ks), (DEPTH, 2, GDN_H), jnp.float32,
                                    math.log(1e-3), math.log(1e-1)))
    gdn_dt_bias = dt + jnp.log(-jnp.expm1(-dt))
    gdn_norm = gain((DEPTH, GDN_DV))
    gla_gate_w2 = nrm((DEPTH, 2, GLA_RANK, GLA_H * GLA_DK), GLA_RANK ** -0.5)
    gla_gate_b = nrm((DEPTH, 2, GLA_H * GLA_DK), 0.1)
    gla_norm = gain((DEPTH, GLA_DV))
    hgrn_lb = nrm((DEPTH, 2, HG_H * HG_DK), 0.5)
    hgrn_norm = gain((DEPTH, HG_DV))
    mla_q_norm = gain((DEPTH, Q_LORA))
    mla_wq_b = nrm((DEPTH, Q_LORA, MLA_H * (MLA_NOPE + MLA_ROPE)), Q_LORA ** -0.5)
    mla_kv_norm = gain((DEPTH, KV_LORA))
    mla_wkv_b = nrm((DEPTH, KV_LORA, MLA_H * (MLA_NOPE + MLA_V)), KV_LORA ** -0.5)
    w_branch = nrm((DEPTH, N_BRANCH, BRANCH_W, D_MODEL), BRANCH_W ** -0.5 * res_scale)
    b_gates = nrm((DEPTH, N_BRANCH, D_MODEL), 0.02)
    w_out = nrm((DEPTH, D_MODEL, D_MODEL), D_MODEL ** -0.5 * res_scale)
    ln1_g = gain((DEPTH, D_MODEL))
    ln1_b = nrm((DEPTH, D_MODEL), 0.02)
    ln2_g = gain((DEPTH, D_MODEL))
    ln2_b = nrm((DEPTH, D_MODEL), 0.02)
    ffn_w1 = nrm((DEPTH, D_MODEL, D_FF), D_MODEL ** -0.5)
    ffn_w3 = nrm((DEPTH, D_MODEL, D_FF), D_MODEL ** -0.5)
    ffn_w2 = nrm((DEPTH, D_FF, D_MODEL), D_FF ** -0.5 * res_scale)
    return {
        'x_prompt': x_prompt, 'x_sample': x_sample, 'c': c,
        'state_gdn': state_gdn, 'state_gla': state_gla, 'state_hgrn': state_hgrn,
        'cache_mla_ckv': cache_mla_ckv, 'cache_mla_kpe': cache_mla_kpe,
        'c_ctx': c_ctx, 'w_ada': w_ada, 'b_ada': b_ada, 'w_in': w_in,
        'gdn_conv': gdn_conv, 'gdn_a_log': gdn_a_log, 'gdn_dt_bias': gdn_dt_bias, 'gdn_norm': gdn_norm,
        'gla_gate_w2': gla_gate_w2, 'gla_gate_b': gla_gate_b, 'gla_norm': gla_norm,
        'hgrn_lb': hgrn_lb, 'hgrn_norm': hgrn_norm,
        'mla_q_norm': mla_q_norm, 'mla_wq_b': mla_wq_b, 'mla_kv_norm': mla_kv_norm, 'mla_wkv_b': mla_wkv_b,
        'w_branch': w_branch, 'b_gates': b_gates, 'w_out': w_out,
        'ln1_g': ln1_g, 'ln1_b': ln1_b, 'ln2_g': ln2_g, 'ln2_b': ln2_b,
        'ffn_w1': ffn_w1, 'ffn_w3': ffn_w3, 'ffn_w2': ffn_w2,
    }


def reference(x_prompt, x_sample, c, state_gdn, state_gla, state_hgrn, cache_mla_ckv, cache_mla_kpe,
              c_ctx, w_ada, b_ada, w_in, gdn_conv, gdn_a_log, gdn_dt_bias, gdn_norm,
              gla_gate_w2, gla_gate_b, gla_norm, hgrn_lb, hgrn_norm,
              mla_q_norm, mla_wq_b, mla_kv_norm, mla_wkv_b, w_branch, b_gates, w_out,
              ln1_g, ln1_b, ln2_g, ln2_b, ffn_w1, ffn_w3, ffn_w2):
    P = {
        'w_in': w_in, 'gdn_conv': gdn_conv, 'gdn_a_log': gdn_a_log, 'gdn_dt_bias': gdn_dt_bias,
        'gdn_norm': gdn_norm, 'gla_gate_w2': gla_gate_w2, 'gla_gate_b': gla_gate_b, 'gla_norm': gla_norm,
        'hgrn_norm': hgrn_norm, 'mla_q_norm': mla_q_norm, 'mla_wq_b': mla_wq_b,
        'mla_kv_norm': mla_kv_norm, 'mla_wkv_b': mla_wkv_b, 'w_branch': w_branch, 'b_gates': b_gates,
        'w_out': w_out, 'ln1_g': ln1_g, 'ln1_b': ln1_b, 'ln2_g': ln2_g, 'ln2_b': ln2_b,
        'ffn_w1': ffn_w1, 'ffn_w3': ffn_w3, 'ffn_w2': ffn_w2,
    }
    alpha = (2.0 * DEPTH) ** 0.25
    cum = jnp.cumsum(jax.nn.softmax(hgrn_lb.astype(jnp.float32), axis=0), axis=0)
    lower_bounds = cum - cum[:1]

    bp = x_prompt.shape[0]
    f32 = jnp.float32
    zero_gdn = jnp.zeros((bp, 2, GDN_H, GDN_DK, GDN_DV), f32)
    zero_gla = jnp.zeros((bp, 2, GLA_H, GLA_DK, GLA_DV), f32)
    zero_hg = jnp.zeros((bp, 2, HG_H, HG_DK, HG_DV), f32)
    new_gdn, new_gla, new_hg, new_ckv, new_kpe = [], [], [], [], []
    y = x_prompt
    for l in range(DEPTH):
        mod = jax.nn.silu(c_ctx)[None, :] @ w_ada[l] + b_ada[l]
        y, s_g, s_l, s_h, ckv, kpe = _layer(y, mod, P, l, lower_bounds[l], zero_gdn, zero_gla, zero_hg,
                                            None, None, None, alpha)
        new_gdn.append(s_g)
        new_gla.append(s_l)
        new_hg.append(s_h)
        new_ckv.append(ckv)
        new_kpe.append(kpe)
    y_prompt = y
    sdt = x_prompt.dtype
    state_gdn_new = jnp.stack(new_gdn, axis=1).astype(sdt)
    state_gla_new = jnp.stack(new_gla, axis=1).astype(sdt)
    state_hgrn_new = jnp.stack(new_hg, axis=1).astype(sdt)
    cache_mla_ckv_new = jnp.stack(new_ckv, axis=1)
    cache_mla_kpe_new = jnp.stack(new_kpe, axis=1)

    ang = _axial_angles(x_sample.shape[1])
    z = x_sample
    for l in range(DEPTH):
        mod = jax.nn.silu(c) @ w_ada[l] + b_ada[l]
        z = _layer(z, mod, P, l, lower_bounds[l], state_gdn[:, l], state_gla[:, l], state_hgrn[:, l],
                   cache_mla_ckv[:, l], cache_mla_kpe[:, l], ang, alpha)[0]
    y_sample = z
    return (y_prompt, y_sample, state_gdn_new, state_gla_new, state_hgrn_new, cache_mla_ckv_new, cache_mla_kpe_new)
```

```python
import functools

import numpy as np
import jax
import jax.numpy as jnp
from jax import lax
from jax.experimental import pallas as pl
from jax.experimental.pallas import tpu as pltpu

f32 = jnp.float32
bf16 = jnp.bfloat16

D_MODEL = 2048
N_HEADS = 4
HEAD_W = 128
BRANCH_W = N_HEADS * HEAD_W
GLA_DK = 64
GLA_RANK = 16
GLA_TAU = 16.0
GDN_CONV = 5
MLA_NOPE = 128
MLA_ROPE = 64
ROPE_BASE = 10000.0
GRID_W = 64
NORM_EPS = 1e-6
CHUNK = 64
SUB = 16
LANE = 128
VMEM_LIMIT = 56 * 1024 * 1024

_SEG = 512
OFF_GDN_Q, OFF_GDN_K, OFF_GDN_V, OFF_GDN_Z = 0, 512, 1024, 1536
OFF_GLA_Q, OFF_GLA_K, OFF_GLA_V, OFF_GLA_R = 2048, 2560, 3072, 3584
OFF_HG_Q, OFF_HG_FF, OFF_HG_FB, OFF_HG_I, OFF_HG_G = 4096, 4608, 5120, 5632, 6144
OFF_MLA_QA, OFF_MLA_KVA, OFF_SMALL, OFF_GATES = 6656, 7168, 7680, 8192
N_PROJ = OFF_GATES + N_HEADS * D_MODEL
SM_KPE, SM_KPE_SW, SM_GLA_G, SM_GDN_B, SM_GDN_A = 0, 64, 128, 160, 168

_SRC = {}
_o = 0
for _n, _w in (('gdn_q', 512), ('gdn_k', 512), ('gdn_v', 512), ('gdn_z', 512), ('gdn_b', 8), ('gdn_a', 8),
               ('gla_q', 256), ('gla_k', 256), ('gla_v', 512), ('gla_r', 512), ('gla_g', 32),
               ('hg_q', 512), ('hg_f', 1024), ('hg_i', 512), ('hg_g', 512),
               ('mla_qa', 512), ('mla_kva', 512), ('mla_kpe', 64), ('gates', 4 * D_MODEL)):
    _SRC[_n] = _o
    _o += _w
IN_WIDTH = _o


def _cparams(sem):
    return pltpu.CompilerParams(dimension_semantics=sem, vmem_limit_bytes=VMEM_LIMIT)


def _dot(a, b):
    return jnp.dot(a, b, preferred_element_type=f32)


def _dot_nt(a, b):
    return lax.dot_general(a, b, (((1,), (1,)), ((), ())), preferred_element_type=f32)


def _dot_tn(a, b):
    return lax.dot_general(a, b, (((0,), (0,)), ((), ())), preferred_element_type=f32)


def _split2(x):
    hi = x.astype(bf16)
    lo = (x - hi.astype(f32)).astype(bf16)
    return hi, lo


def _mm3(a, b):
    ah, al = _split2(a)
    bh, bl = _split2(b)
    return _dot(ah, bh) + (_dot(ah, bl) + _dot(al, bh))


def _layernorm(y, g, b):
    mu = jnp.mean(y, axis=-1, keepdims=True)
    yc = y - mu
    var = jnp.mean(yc * yc, axis=-1, keepdims=True)
    return yc * lax.rsqrt(var + NORM_EPS) * g + b


def _rmsnorm(y, w):
    return y * lax.rsqrt(jnp.mean(y * y, axis=-1, keepdims=True) + NORM_EPS) * w


def _ada_kernel(c_ref, w_ref, b_ref, o_ref):
    cs = jax.nn.silu(c_ref[...]).astype(bf16)
    o_ref[...] = _dot(cs, w_ref[...].astype(bf16)) + b_ref[...]


def _ada(cc, w_ada, b_ada):
    depth, d, n6 = w_ada.shape
    rows = cc.shape[0]
    tn = 1024
    return pl.pallas_call(
        _ada_kernel,
        out_shape=jax.ShapeDtypeStruct((depth, rows, n6), f32),
        grid=(depth, n6 // tn),
        in_specs=[pl.BlockSpec((rows, d), lambda l, j: (0, 0)),
                  pl.BlockSpec((None, d, tn), lambda l, j: (l, 0, j)),
                  pl.BlockSpec((None, 1, tn), lambda l, j: (l, 0, j))],
        out_specs=pl.BlockSpec((None, rows, tn), lambda l, j: (l, 0, j)),
        compiler_params=_cparams(("parallel", "parallel")),
        name="ada_mod",
    )(cc, w_ada, b_ada.reshape(depth, 1, n6))


def _inproj_kernel(x_ref, mod_ref, w_ref, o_ref, xb_ref):
    @pl.when(pl.program_id(1) == 0)
    def _():
        shift = mod_ref[0:1, :]
        scale = mod_ref[1:2, :]
        xb_ref[...] = (x_ref[...] * (1.0 + scale) + shift).astype(bf16)

    o_ref[...] = _dot(xb_ref[...], w_ref[...])


def _mod_index(tm, t_ctx, l_lat):
    def index(i):
        r = i * tm
        return jnp.where(r < t_ctx, 0, 1 + (r - t_ctx) // l_lat)
    return index


def _inproj(x, mod, w, t_ctx, l_lat):
    t, d = x.shape
    n = w.shape[1]
    tm, tn = 1024, 512
    midx = _mod_index(tm, t_ctx, l_lat)
    return pl.pallas_call(
        _inproj_kernel,
        out_shape=jax.ShapeDtypeStruct((t, n), f32),
        grid=(t // tm, n // tn),
        in_specs=[pl.BlockSpec((tm, d), lambda i, j: (i, 0)),
                  pl.BlockSpec((None, 6, d), lambda i, j: (midx(i), 0, 0)),
                  pl.BlockSpec((d, tn), lambda i, j: (0, j))],
        out_specs=pl.BlockSpec((tm, tn), lambda i, j: (i, j)),
        scratch_shapes=[pltpu.VMEM((tm, d), bf16)],
        compiler_params=_cparams(("parallel", "arbitrary")),
        name="in_proj",
    )(x, mod, w)


def _tri_masks():
    i = lax.broadcasted_iota(jnp.int32, (CHUNK, CHUNK), 0)
    j = lax.broadcasted_iota(jnp.int32, (CHUNK, CHUNK), 1)
    return i, j


def _unit_tri_inverse(m, eye, blockdiag):
    dg = jnp.where(blockdiag, m, 0.0)
    lo = m - dg
    d2 = _mm3(dg, dg)
    d4 = _mm3(d2, d2)
    d8 = _mm3(d4, d4)
    a = _mm3(eye - dg, eye + d2)
    a = _mm3(a, eye + d4)
    a = _mm3(a, eye + d8)
    nn = _mm3(a, lo)
    n2 = _mm3(nn, nn)
    return _mm3(_mm3(eye - nn, eye + n2), a)


def _gdn_kernel(q_ref, k_ref, v_ref, z_ref, cwq_ref, cwk_ref, cwv_ref, grow_ref, hp_ref, nw_ref, s0_ref,
                o_ref, so_ref, pad_ref, qs, ks, vs, gs, of, ob, s_ref, *, seq):
    n = seq // CHUNK

    def conv_silu(x_ref, w_ref):
        pad_ref[0:8, :] = jnp.zeros((8, HEAD_W), f32)
        pad_ref[8 + seq:16 + seq, :] = jnp.zeros((8, HEAD_W), f32)
        pad_ref[8:8 + seq, :] = x_ref[...]
        acc = pad_ref[pl.ds(6, seq), :] * w_ref[0:1, :]
        for j in range(1, GDN_CONV):
            acc = acc + pad_ref[pl.ds(6 + j, seq), :] * w_ref[j:j + 1, :]
        return jax.nn.silu(acc)

    def l2norm(x):
        return x * lax.rsqrt(jnp.sum(x * x, axis=-1, keepdims=True) + NORM_EPS)

    qs[...] = l2norm(conv_silu(q_ref, cwq_ref)) * (HEAD_W ** -0.5)
    ks[...] = l2norm(conv_silu(k_ref, cwk_ref))
    vs[...] = conv_silu(v_ref, cwv_ref)

    x = grow_ref[...]
    a_log = hp_ref[:, 0:1][None]
    dt_b = hp_ref[:, 1:2][None]
    rows = lax.broadcasted_iota(jnp.int32, x.shape, 1)
    gs[...] = jnp.where(rows < 2, jax.nn.sigmoid(x), -jnp.exp(a_log) * jax.nn.softplus(x + dt_b))
    s_ref[...] = s0_ref[...]

    ii, jj = _tri_masks()
    eye_b = ii == jj
    eye = jnp.where(eye_b, 1.0, 0.0).astype(f32)
    blockdiag = (ii // SUB) == (jj // SUB)

    def chunk(d, c):
        r0 = pl.multiple_of(c * CHUNK, CHUNK)
        qc = qs[pl.ds(r0, CHUNK), :]
        kc = ks[pl.ds(r0, CHUNK), :]
        vc = vs[pl.ds(r0, CHUNK), :]
        g8 = gs[c]
        beta_r = g8[d:d + 1, :]
        g_r = g8[2 + d:3 + d, :]
        tri = (jj <= ii) if d == 0 else (jj >= ii)
        strict = jnp.logical_and(tri, jnp.logical_not(eye_b))
        tg = jnp.where(tri, g_r, 0.0)
        gam_c = jnp.sum(tg, axis=1, keepdims=True)
        tot = jnp.sum(g_r, axis=1, keepdims=True)
        beta_c = jnp.sum(jnp.where(eye_b, beta_r, 0.0), axis=1, keepdims=True)
        th, tl = _split2(tg)
        tl2 = (tg - th.astype(f32) - tl.astype(f32)).astype(bf16)
        sm = jnp.where(strict, 1.0, 0.0).astype(bf16)
        dlt = _dot(th, sm) + (_dot(tl, sm) + _dot(tl2, sm))
        decay = jnp.where(tri, jnp.exp(jnp.minimum(dlt, 0.0)), 0.0)
        kb = kc * beta_c
        kcb = kc.astype(bf16)
        m = jnp.where(strict, _dot_nt(kb.astype(bf16), kcb) * decay, 0.0)
        t_inv = _unit_tri_inverse(m, eye, blockdiag)
        eg = jnp.exp(gam_c)
        rhs = jnp.concatenate([vc * beta_c, kb * eg], axis=1)
        sol = _mm3(t_inv, rhs)
        u_base = sol[:, :HEAD_W]
        w = sol[:, HEAD_W:]
        a_qk = _dot_nt(qc.astype(bf16), kcb) * decay
        q_dec = qc * eg
        k_dec = kc * jnp.exp(tot - gam_c)
        s = s_ref[d]
        sb = s.astype(bf16)
        u = u_base - _dot(w.astype(bf16), sb)
        ub = u.astype(bf16)
        o = _dot(q_dec.astype(bf16), sb) + _dot(a_qk.astype(bf16), ub)
        s_ref[d] = jnp.exp(tot) * s + _dot_tn(k_dec.astype(bf16), ub)
        return r0, o

    def body(c, carry):
        r0, o = chunk(0, c)
        of[pl.ds(r0, CHUNK), :] = o
        r1, o1 = chunk(1, n - 1 - c)
        ob[pl.ds(r1, CHUNK), :] = o1
        return carry

    lax.fori_loop(0, n, body, 0)
    o = of[...] + ob[...]
    o_ref[...] = (_rmsnorm(o, nw_ref[...]) * jax.nn.silu(z_ref[...])).astype(bf16)
    so_ref[...] = s_ref[...]


def _gdn(p, conv_w, grow, hp, norm_w, s0, *, nseq, seq, row_blk0):
    n = seq // CHUNK
    hb = _SEG // HEAD_W

    def pcol(off):
        return pl.BlockSpec((seq, HEAD_W), lambda b, h: (row_blk0 + b, off // HEAD_W + h))

    def wcol(off):
        return pl.BlockSpec((GDN_CONV, HEAD_W), lambda b, h: (0, off // HEAD_W + h))

    kern = functools.partial(_gdn_kernel, seq=seq)
    return pl.pallas_call(
        kern,
        out_shape=(jax.ShapeDtypeStruct((nseq * seq, BRANCH_W), bf16),
                   jax.ShapeDtypeStruct((nseq, N_HEADS, 2, HEAD_W, HEAD_W), f32)),
        grid=(nseq, N_HEADS),
        in_specs=[pcol(OFF_GDN_Q), pcol(OFF_GDN_K), pcol(OFF_GDN_V), pcol(OFF_GDN_Z),
                  wcol(0), wcol(hb * HEAD_W), wcol(2 * hb * HEAD_W),
                  pl.BlockSpec((None, n, 8, CHUNK), lambda b, h: (h, row_blk0 + b, 0, 0)),
                  pl.BlockSpec((None, 8, 2), lambda b, h: (h, 0, 0)),
                  pl.BlockSpec((1, HEAD_W), lambda b, h: (0, 0)),
                  pl.BlockSpec((None, None, 2, HEAD_W, HEAD_W), lambda b, h: (b, h, 0, 0, 0))],
        out_specs=(pl.BlockSpec((seq, HEAD_W), lambda b, h: (b, h)),
                   pl.BlockSpec((None, None, 2, HEAD_W, HEAD_W), lambda b, h: (b, h, 0, 0, 0))),
        scratch_shapes=[pltpu.VMEM((seq + 16, HEAD_W), f32),
                        pltpu.VMEM((seq, HEAD_W), f32), pltpu.VMEM((seq, HEAD_W), f32),
                        pltpu.VMEM((seq, HEAD_W), f32), pltpu.VMEM((n, 8, CHUNK), f32),
                        pltpu.VMEM((seq, HEAD_W), f32), pltpu.VMEM((seq, HEAD_W), f32),
                        pltpu.VMEM((2, HEAD_W, HEAD_W), f32)],
        compiler_params=_cparams(("parallel", "parallel")),
        name="gdn",
    )(p, p, p, p, conv_w, conv_w, conv_w, grow, hp, norm_w, s0)


_N_LEVELS = 6
_ROW_EQ, _ROW_EK, _ROW_TOT, _ROWS_EXP = 384, 448, 512, 520


def _scan_consts():
    c = CHUNK
    mexp = np.zeros((2, _ROWS_EXP, c), np.float32)
    lvl = np.full((2, c, c), -1.0, np.float32)
    for lv in range(_N_LEVELS):
        s = 32 >> lv
        for i in range(c):
            p = (i // (2 * s)) * (2 * s) + s
            right = (i % (2 * s)) >= s
            if right:
                mexp[0, lv * c + i, p:i + 1] = 1.0
                mexp[1, lv * c + i, p:i] = 1.0
            else:
                mexp[0, lv * c + i, i + 1:p] = 1.0
                mexp[1, lv * c + i, i:p] = 1.0
            for j in range(c):
                if (i // (2 * s)) != (j // (2 * s)):
                    continue
                jright = (j % (2 * s)) >= s
                if right and not jright:
                    lvl[0, i, j] = lv
                if (not right) and jright:
                    lvl[1, i, j] = lv
    for i in range(c):
        lvl[:, i, i] = _N_LEVELS
        mexp[0, _ROW_EQ + i, :i + 1] = 1.0
        mexp[0, _ROW_EK + i, i + 1:] = 1.0
        mexp[1, _ROW_EQ + i, i:] = 1.0
        mexp[1, _ROW_EK + i, :i] = 1.0
    mexp[:, _ROW_TOT:, :] = 1.0
    return mexp, lvl


def _scan_kernel(q_ref, kf_ref, kb_ref, v_ref, laf_ref, lab_ref, gate_ref, nw_ref, s0_ref, mexp_ref, lvl_ref,
                 o_ref, so_ref, of, ob, st_ref, *, seq, q_silu, q_scale, gate_silu):
    n = seq // CHUNK
    st_ref[...] = s0_ref[...]

    def chunk(d, c):
        r0 = pl.multiple_of(c * CHUNK, CHUNK)
        q = q_ref[pl.ds(r0, CHUNK), :]
        q = jax.nn.silu(q) if q_silu else q * q_scale
        k = (kf_ref if d == 0 else kb_ref)[pl.ds(r0, CHUNK), :]
        v = v_ref[pl.ds(r0, CHUNK), :].astype(bf16)
        la = (laf_ref if d == 0 else lab_ref)[pl.ds(r0, CHUNK), :]
        hi, lo = _split2(la)
        e2 = _dot(mexp_ref[d], jnp.concatenate([hi, lo], axis=1))
        e = e2[:, :HEAD_W] + e2[:, HEAD_W:]
        lvl = lvl_ref[d]
        a = jnp.where(lvl == float(_N_LEVELS), _dot_nt(q.astype(bf16), k.astype(bf16)), 0.0)
        for lv in range(_N_LEVELS):
            w = jnp.exp(e[lv * CHUNK:(lv + 1) * CHUNK])
            a = jnp.where(lvl == float(lv), _dot_nt((q * w).astype(bf16), (k * w).astype(bf16)), a)
        e_q = e[_ROW_EQ:_ROW_EQ + CHUNK]
        e_k = e[_ROW_EK:_ROW_EK + CHUNK]
        tot = e[_ROW_TOT:_ROW_TOT + 1]
        st = st_ref[d]
        o = _dot(a.astype(bf16), v) + _dot_nt((q * jnp.exp(e_q)).astype(bf16), st.astype(bf16))
        st_ref[d] = st * jnp.exp(tot) + _dot_tn(v, (k * jnp.exp(e_k)).astype(bf16))
        return r0, o

    def body(c, carry):
        r0, o = chunk(0, c)
        of[pl.ds(r0, CHUNK), :] = o
        r1, o1 = chunk(1, n - 1 - c)
        ob[pl.ds(r1, CHUNK), :] = o1
        return carry

    lax.fori_loop(0, n, body, 0)
    o = of[...] + ob[...]
    g = gate_ref[...]
    g = jax.nn.silu(g) if gate_silu else jax.nn.sigmoid(g)
    o_ref[...] = (_rmsnorm(o, nw_ref[...]) * g).astype(bf16)
    so_ref[...] = st_ref[...]


def _scan(q_src, kf_src, kb_src, v_src, laf, lab, gate_src, norm_w, s0t, mexp, lvl, *,
          nseq, seq, row_blk0, q_silu, q_scale, gate_silu):
    def spec(src):
        off, rb = src[1], src[2]
        return pl.BlockSpec((seq, HEAD_W), lambda b, h: (rb + b, off // HEAD_W + h))

    srcs = (q_src, kf_src, kb_src, v_src, (laf, 0, row_blk0), (lab, 0, row_blk0), gate_src)
    kern = functools.partial(_scan_kernel, seq=seq, q_silu=q_silu, q_scale=q_scale, gate_silu=gate_silu)
    return pl.pallas_call(
        kern,
        out_shape=(jax.ShapeDtypeStruct((nseq * seq, BRANCH_W), bf16),
                   jax.ShapeDtypeStruct((nseq, N_HEADS, 2, HEAD_W, HEAD_W), f32)),
        grid=(nseq, N_HEADS),
        in_specs=[spec(s) for s in srcs] + [
            pl.BlockSpec((1, HEAD_W), lambda b, h: (0, 0)),
            pl.BlockSpec((None, None, 2, HEAD_W, HEAD_W), lambda b, h: (b, h, 0, 0, 0)),
            pl.BlockSpec((2, _ROWS_EXP, CHUNK), lambda b, h: (0, 0, 0)),
            pl.BlockSpec((2, CHUNK, CHUNK), lambda b, h: (0, 0, 0))],
        out_specs=(pl.BlockSpec((seq, HEAD_W), lambda b, h: (b, h)),
                   pl.BlockSpec((None, None, 2, HEAD_W, HEAD_W), lambda b, h: (b, h, 0, 0, 0))),
        scratch_shapes=[pltpu.VMEM((seq, HEAD_W), f32), pltpu.VMEM((seq, HEAD_W), f32),
                        pltpu.VMEM((2, HEAD_W, HEAD_W), f32)],
        compiler_params=_cparams(("parallel", "parallel")),
        name="decay_scan",
    )(*[s[0] for s in srcs], norm_w, s0t, mexp, lvl)


_LOG_DECAY_FLOOR = -1.0e4


def _gla_prep_kernel(sm_ref, w2_ref, b_ref, laf_ref, lab_ref):
    x = sm_ref[...].astype(bf16)
    for d, out in ((0, laf_ref), (1, lab_ref)):
        logits = _dot(x, w2_ref[d]) + b_ref[d]
        out[...] = jnp.maximum(jax.nn.log_sigmoid(logits) * (1.0 / GLA_TAU), _LOG_DECAY_FLOOR)


def _gla_prep(p, w2big, bbig):
    t = p.shape[0]
    tm = 512
    out = jax.ShapeDtypeStruct((t, BRANCH_W), f32)
    return pl.pallas_call(
        _gla_prep_kernel,
        out_shape=(out, out),
        grid=(t // tm,),
        in_specs=[pl.BlockSpec((tm, _SEG), lambda i: (i, OFF_SMALL // _SEG)),
                  pl.BlockSpec((2, _SEG, BRANCH_W), lambda i: (0, 0, 0)),
                  pl.BlockSpec((2, 1, BRANCH_W), lambda i: (0, 0, 0))],
        out_specs=(pl.BlockSpec((tm, BRANCH_W), lambda i: (i, 0)),
                   pl.BlockSpec((tm, BRANCH_W), lambda i: (i, 0))),
        compiler_params=_cparams(("parallel",)),
        name="gla_prep",
    )(p, w2big, bbig)


def _hgrn_prep_kernel(zf_ref, zb_ref, lb_ref, laf_ref, lab_ref, kf_ref, kb_ref):
    for d, z_ref, la_out, k_out in ((0, zf_ref, laf_ref, kf_ref), (1, zb_ref, lab_ref, kb_ref)):
        z = z_ref[...]
        lb = lb_ref[d:d + 1, :]
        a = jnp.log(lb)
        b = jnp.log1p(-lb) + jax.nn.log_sigmoid(z)
        mx = jnp.maximum(a, b)
        lse = mx + jnp.log(jnp.exp(a - mx) + jnp.exp(b - mx))
        lse = jnp.where(mx == -jnp.inf, -jnp.inf, lse)
        la_out[...] = jnp.maximum(lse, _LOG_DECAY_FLOOR)
        k_out[...] = (1.0 - lb) * jax.nn.sigmoid(-z)


def _hgrn_prep(p, lb):
    t = p.shape[0]
    tm = 512
    out = jax.ShapeDtypeStruct((t, BRANCH_W), f32)
    ospec = pl.BlockSpec((tm, BRANCH_W), lambda i: (i, 0))
    return pl.pallas_call(
        _hgrn_prep_kernel,
        out_shape=(out, out, out, out),
        grid=(t // tm,),
        in_specs=[pl.BlockSpec((tm, _SEG), lambda i: (i, OFF_HG_FF // _SEG)),
                  pl.BlockSpec((tm, _SEG), lambda i: (i, OFF_HG_FB // _SEG)),
                  pl.BlockSpec((2, BRANCH_W), lambda i: (0, 0))],
        out_specs=(ospec, ospec, ospec, ospec),
        compiler_params=_cparams(("parallel",)),
        name="hgrn_prep",
    )(p, p, lb)


def _mla_proj_kernel(*refs, rope):
    if rope:
        (qa_ref, kva_ref, sm_ref, qnw_ref, kvnw_ref, wqa_ref, wkv_ref, wqb_ref, cos_ref, sin_ref,
         qn_ref, qp_ref, ckv_ref, kn_ref, vv_ref, kp_ref) = refs
    else:
        (qa_ref, kva_ref, sm_ref, qnw_ref, kvnw_ref, wqa_ref, wkv_ref,
         qn_ref, qp_ref, ckv_ref, kn_ref, vv_ref, kp_ref) = refs
    qh = _rmsnorm(qa_ref[...], qnw_ref[...]).astype(bf16)
    qa = _dot(qh, wqa_ref[...])
    qn_ref[...] = qa[:, :BRANCH_W].astype(bf16)
    pe = qa[:, BRANCH_W:]
    kpe = sm_ref[:, 0:LANE]
    if rope:
        cos = cos_ref[...]
        sin = sin_ref[...]
        cos4 = jnp.concatenate([cos] * N_HEADS, axis=1)
        sin4 = jnp.concatenate([sin] * N_HEADS, axis=1)
        pe = pe * cos4 + _dot(qh, wqb_ref[...]) * sin4
        kpe = kpe * cos + pltpu.roll(kpe, MLA_ROPE, 1) * sin
    qp_ref[...] = pe.astype(bf16)
    kp_ref[...] = kpe.astype(bf16)
    ckv = _rmsnorm(kva_ref[...], kvnw_ref[...])
    ckv_ref[...] = ckv
    kv = _dot(ckv.astype(bf16), wkv_ref[...])
    kn_ref[...] = kv[:, :BRANCH_W].astype(bf16)
    vv_ref[...] = kv[:, BRANCH_W:].astype(bf16)


def _mla_proj(p, qnw, kvnw, wqa, wkv, wqb, cos, sin, *, nrows, seq, row0):
    tm = 256
    rope = cos is not None
    rb0 = row0 // tm
    per_seq = seq // tm

    def pspec(off):
        return pl.BlockSpec((tm, _SEG), lambda i: (rb0 + i, off // _SEG))

    def full(a):
        return pl.BlockSpec(a.shape, lambda i: (0,) * a.ndim)

    in_specs = [pspec(OFF_MLA_QA), pspec(OFF_MLA_KVA), pspec(OFF_SMALL), full(qnw), full(kvnw), full(wqa), full(wkv)]
    args = [p, p, p, qnw, kvnw, wqa, wkv]
    if rope:
        tspec = pl.BlockSpec((tm, LANE), lambda i: (i % per_seq, 0))
        in_specs += [full(wqb), tspec, tspec]
        args += [wqb, cos, sin]
    wide = lambda dt: jax.ShapeDtypeStruct((nrows, BRANCH_W), dt)
    ospec = pl.BlockSpec((tm, BRANCH_W), lambda i: (i, 0))
    return pl.pallas_call(
        functools.partial(_mla_proj_kernel, rope=rope),
        out_shape=(wide(bf16), wide(bf16), wide(f32), wide(bf16), wide(bf16),
                   jax.ShapeDtypeStruct((nrows, LANE), bf16)),
        grid=(nrows // tm,),
        in_specs=in_specs,
        out_specs=(ospec, ospec, ospec, ospec, ospec, pl.BlockSpec((tm, LANE), lambda i: (i, 0))),
        compiler_params=_cparams(("parallel",)),
        name="mla_proj",
    )(*args)


def _kv_kernel(ckv_ref, w_ref, kn_ref, vv_ref):
    kv = _dot(ckv_ref[...].astype(bf16), w_ref[...])
    kn_ref[...] = kv[:, :BRANCH_W].astype(bf16)
    vv_ref[...] = kv[:, BRANCH_W:].astype(bf16)


def _kv_proj(ckv, wkv):
    rows = ckv.shape[0]
    tm = 256
    out = jax.ShapeDtypeStruct((rows, BRANCH_W), bf16)
    ospec = pl.BlockSpec((tm, BRANCH_W), lambda i: (i, 0))
    return pl.pallas_call(
        _kv_kernel, out_shape=(out, out), grid=(rows // tm,),
        in_specs=[pl.BlockSpec((tm, ckv.shape[1]), lambda i: (i, 0)),
                  pl.BlockSpec(wkv.shape, lambda i: (0, 0))],
        out_specs=(ospec, ospec),
        compiler_params=_cparams(("parallel",)),
        name="mla_ctx_kv",
    )(ckv, wkv)


def _attn_kernel(*refs, has_ctx):
    if has_ctx:
        qn_ref, qp_ref, kn_ref, kp_ref, vv_ref, knc_ref, kpc_ref, vvc_ref, o_ref = refs
    else:
        qn_ref, qp_ref, kn_ref, kp_ref, vv_ref, o_ref = refs
    scale = (MLA_NOPE + MLA_ROPE) ** -0.5
    qn = qn_ref[...]
    qp = qp_ref[...]
    s1 = (_dot_nt(qn, kn_ref[...]) + _dot_nt(qp, kp_ref[...])) * scale
    mx = jnp.max(s1, axis=-1, keepdims=True)
    if has_ctx:
        s2 = (_dot_nt(qn, knc_ref[...]) + _dot_nt(qp, kpc_ref[...])) * scale
        mx = jnp.maximum(mx, jnp.max(s2, axis=-1, keepdims=True))
    p1 = jnp.exp(s1 - mx)
    den = jnp.sum(p1, axis=-1, keepdims=True)
    if has_ctx:
        p2 = jnp.exp(s2 - mx)
        den = den + jnp.sum(p2, axis=-1, keepdims=True)
    inv = 1.0 / den
    o = _dot((p1 * inv).astype(bf16), vv_ref[...])
    if has_ctx:
        o = o + _dot((p2 * inv).astype(bf16), vvc_ref[...])
    o_ref[...] = o.astype(bf16)


def _attention(qn, qp, kn, kp, vv, ctx, *, nseq, seq):
    tq = 256
    nq = seq // tq
    has_ctx = ctx is not None
    qspec = pl.BlockSpec((tq, HEAD_W), lambda b, h, i: (b * nq + i, h))
    kspec = pl.BlockSpec((seq, HEAD_W), lambda b, h, i: (b, h))
    kpspec = pl.BlockSpec((seq, LANE), lambda b, h, i: (b, 0))
    in_specs = [qspec, qspec, kspec, kpspec, kspec]
    args = [qn, qp, kn, kp, vv]
    if has_ctx:
        knc, kpc, vvc = ctx
        lc = knc.shape[0] // nseq
        in_specs += [pl.BlockSpec((lc, HEAD_W), lambda b, h, i: (b, h)),
                     pl.BlockSpec((lc, LANE), lambda b, h, i: (b, 0)),
                     pl.BlockSpec((lc, HEAD_W), lambda b, h, i: (b, h))]
        args += [knc, kpc, vvc]
    return pl.pallas_call(
        functools.partial(_attn_kernel, has_ctx=has_ctx),
        out_shape=jax.ShapeDtypeStruct((nseq * seq, BRANCH_W), bf16),
        grid=(nseq, N_HEADS, nq),
        in_specs=in_specs,
        out_specs=pl.BlockSpec((tq, HEAD_W), lambda b, h, i: (b * nq + i, h)),
        compiler_params=_cparams(("parallel", "parallel", "parallel")),
        name="mla_attn",
    )(*args)


def _merge_kernel(o0_ref, o1_ref, o2_ref, o3_ref, g0_ref, g1_ref, g2_ref, g3_ref, bg_ref, wb_ref, wo_ref,
                  x_ref, mod_ref, lng_ref, lnb_ref, out_ref, *, alpha):
    m = None
    for k, (o_ref, g_ref) in enumerate(((o0_ref, g0_ref), (o1_ref, g1_ref), (o2_ref, g2_ref), (o3_ref, g3_ref))):
        term = jax.nn.sigmoid(g_ref[...] + bg_ref[k:k + 1, :]) * _dot(o_ref[...], wb_ref[k])
        m = term if m is None else m + term
    mix = _dot(m.astype(bf16), wo_ref[...])
    gate1 = mod_ref[2:3, :]
    out_ref[...] = _layernorm(alpha * x_ref[...] + gate1 * mix, lng_ref[...], lnb_ref[...])


def _merge(branches, p, bg, wb, wo, x, mod, lng, lnb, *, alpha, t_ctx, l_lat):
    t, d = x.shape
    tm = 256
    midx = _mod_index(tm, t_ctx, l_lat)
    ospec = pl.BlockSpec((tm, BRANCH_W), lambda i: (i, 0))
    gspecs = [pl.BlockSpec((tm, d), functools.partial(lambda i, k: (i, OFF_GATES // d + k), k=k)) for k in range(4)]
    single = dict(pipeline_mode=pl.Buffered(1))
    return pl.pallas_call(
        functools.partial(_merge_kernel, alpha=alpha),
        out_shape=jax.ShapeDtypeStruct((t, d), f32),
        grid=(t // tm,),
        in_specs=[ospec, ospec, ospec, ospec] + gspecs + [
            pl.BlockSpec((4, d), lambda i: (0, 0)),
            pl.BlockSpec((4, BRANCH_W, d), lambda i: (0, 0, 0), **single),
            pl.BlockSpec((d, d), lambda i: (0, 0), **single),
            pl.BlockSpec((tm, d), lambda i: (i, 0)),
            pl.BlockSpec((None, 6, d), lambda i: (midx(i), 0, 0)),
            pl.BlockSpec((1, d), lambda i: (0, 0)),
            pl.BlockSpec((1, d), lambda i: (0, 0))],
        out_specs=pl.BlockSpec((tm, d), lambda i: (i, 0)),
        compiler_params=_cparams(("parallel",)),
        name="merge_out_ln",
    )(*branches, p, p, p, p, bg, wb, wo, x, mod, lng, lnb)


def _ffn_kernel(x_ref, mod_ref, w1_ref, w3_ref, w2_ref, lng_ref, lnb_ref, out_ref, hb_ref, acc_ref, *, alpha):
    f = pl.program_id(1)

    @pl.when(f == 0)
    def _():
        shift = mod_ref[3:4, :]
        scale = mod_ref[4:5, :]
        hb_ref[...] = (x_ref[...] * (1.0 + scale) + shift).astype(bf16)
        acc_ref[...] = jnp.zeros_like(acc_ref)

    h = hb_ref[...]
    g = (jax.nn.silu(_dot(h, w1_ref[...])) * _dot(h, w3_ref[...])).astype(bf16)
    acc_ref[...] += _dot(g, w2_ref[...])

    @pl.when(f == pl.num_programs(1) - 1)
    def _():
        gate2 = mod_ref[5:6, :]
        out_ref[...] = _layernorm(alpha * x_ref[...] + gate2 * acc_ref[...], lng_ref[...], lnb_ref[...])


def _ffn(x, mod, w1, w3, w2, lng, lnb, *, alpha, t_ctx, l_lat):
    t, d = x.shape
    dff = w1.shape[1]
    tm, tf = 512, 512
    midx = _mod_index(tm, t_ctx, l_lat)
    return pl.pallas_call(
        functools.partial(_ffn_kernel, alpha=alpha),
        out_shape=jax.ShapeDtypeStruct((t, d), f32),
        grid=(t // tm, dff // tf),
        in_specs=[pl.BlockSpec((tm, d), lambda i, f: (i, 0)),
                  pl.BlockSpec((None, 6, d), lambda i, f: (midx(i), 0, 0)),
                  pl.BlockSpec((d, tf), lambda i, f: (0, f)),
                  pl.BlockSpec((d, tf), lambda i, f: (0, f)),
                  pl.BlockSpec((tf, d), lambda i, f: (f, 0)),
                  pl.BlockSpec((1, d), lambda i, f: (0, 0)),
                  pl.BlockSpec((1, d), lambda i, f: (0, 0))],
        out_specs=pl.BlockSpec((tm, d), lambda i, f: (i, 0)),
        scratch_shapes=[pltpu.VMEM((tm, d), bf16), pltpu.VMEM((tm, d), f32)],
        compiler_params=_cparams(("parallel", "arbitrary")),
        name="ffn_ln",
    )(x, mod, w1, w3, w2, lng, lnb)


def _relayout_w_in(w):
    d = w.shape[0]

    def seg(name, start=0, width=None):
        s = _SRC[name] + start
        return w[:, s:s + width]

    def zeros(n):
        return jnp.zeros((d, n), w.dtype)

    def head_padded(name):
        parts = []
        for h in range(N_HEADS):
            parts += [seg(name, h * GLA_DK, GLA_DK), zeros(HEAD_W - GLA_DK)]
        return parts

    q4 = MLA_ROPE // 4
    kpe_swapped = [seg('mla_kpe', q4, q4), seg('mla_kpe', 0, q4), seg('mla_kpe', 3 * q4, q4), seg('mla_kpe', 2 * q4, q4)]
    small = ([seg('mla_kpe', 0, MLA_ROPE)] + kpe_swapped + [seg('gla_g', 0, 2 * GLA_RANK), seg('gdn_b', 0, 8),
             seg('gdn_a', 0, 8), zeros(_SEG - SM_GDN_A - 8)])
    parts = ([seg('gdn_q', 0, 512), seg('gdn_k', 0, 512), seg('gdn_v', 0, 512), seg('gdn_z', 0, 512)]
             + head_padded('gla_q') + head_padded('gla_k')
             + [seg('gla_v', 0, 512), seg('gla_r', 0, 512), seg('hg_q', 0, 512), seg('hg_f', 0, 512), seg('hg_f', 512, 512),
                seg('hg_i', 0, 512), seg('hg_g', 0, 512), seg('mla_qa', 0, 512), seg('mla_kva', 0, 512)]
             + small + [seg('gates', 0, 4 * D_MODEL)])
    out = jnp.concatenate(parts, axis=1).astype(bf16)
    assert out.shape[1] == N_PROJ
    return out


def _relayout_wq(wq):
    hw = MLA_NOPE + MLA_ROPE
    z = jnp.zeros((wq.shape[0], HEAD_W - MLA_ROPE), wq.dtype)
    q4 = MLA_ROPE // 4
    nope, pe, pes = [], [], []
    for h in range(N_HEADS):
        base = h * hw
        nope.append(wq[:, base:base + MLA_NOPE])
        r = wq[:, base + MLA_NOPE:base + hw]
        pe += [r, z]
        pes += [r[:, q4:2 * q4], r[:, 0:q4], r[:, 3 * q4:], r[:, 2 * q4:3 * q4], z]
    return jnp.concatenate(nope + pe, axis=1).astype(bf16), jnp.concatenate(pes, axis=1).astype(bf16)


def _relayout_wkv(wkv):
    hw = MLA_NOPE + HEAD_W
    kn = [wkv[:, h * hw:h * hw + MLA_NOPE] for h in range(N_HEADS)]
    vv = [wkv[:, h * hw + MLA_NOPE:(h + 1) * hw] for h in range(N_HEADS)]
    return jnp.concatenate(kn + vv, axis=1).astype(bf16)


def _rope_tables(length):
    pos = jnp.arange(length)
    row_id = (pos // GRID_W).astype(f32)
    col_id = (pos % GRID_W).astype(f32)
    half = MLA_ROPE // 2
    inv = ROPE_BASE ** (-jnp.arange(0, half, 2, dtype=f32) / half)
    ar, ac = row_id[:, None] * inv, col_id[:, None] * inv
    z = jnp.zeros((length, LANE - MLA_ROPE), f32)
    cos = jnp.concatenate([jnp.cos(ar), jnp.cos(ar), jnp.cos(ac), jnp.cos(ac), z], axis=1)
    sin = jnp.concatenate([-jnp.sin(ar), jnp.sin(ar), -jnp.sin(ac), jnp.sin(ac), z], axis=1)
    return cos, sin


def _gla_gate_weights(w2, b):
    wbig = jnp.zeros((2, _SEG, BRANCH_W), f32)
    bbig = jnp.zeros((2, 1, BRANCH_W), f32)
    for d in range(2):
        r0 = SM_GLA_G + d * GLA_RANK
        for h in range(N_HEADS):
            wbig = wbig.at[d, r0:r0 + GLA_RANK, h * HEAD_W:h * HEAD_W + GLA_DK].set(w2[d, :, h * GLA_DK:(h + 1) * GLA_DK])
            bbig = bbig.at[d, 0, h * HEAD_W:h * HEAD_W + GLA_DK].set(b[d, h * GLA_DK:(h + 1) * GLA_DK])
    return wbig.astype(bf16), bbig


def kernel(x_prompt, x_sample, c, state_gdn, state_gla, state_hgrn, cache_mla_ckv, cache_mla_kpe, c_ctx, w_ada, b_ada, w_in, gdn_conv, gdn_a_log, gdn_dt_bias, gdn_norm, gla_gate_w2, gla_gate_b, gla_norm, hgrn_lb, hgrn_norm, mla_q_norm, mla_wq_b, mla_kv_norm, mla_wkv_b, w_branch, b_gates, w_out, ln1_g, ln1_b, ln2_g, ln2_b, ffn_w1, ffn_w3, ffn_w2):
    nb_c, l_c, d = x_prompt.shape
    nb_l, l_l, _ = x_sample.shape
    depth = w_in.shape[0]
    t_c, t_l = nb_c * l_c, nb_l * l_l
    past = cache_mla_ckv.shape[2]
    alpha = (2.0 * depth) ** 0.25
    assert d == D_MODEL and t_c % 1024 == 0 and l_l % 1024 == 0 and l_c % CHUNK == 0

    n_cond = 1 + nb_l
    cc = jnp.concatenate([c_ctx[None, :], c, jnp.zeros((-n_cond % 8, d), f32)], axis=0)
    mods = _ada(cc, w_ada, b_ada).reshape(depth, cc.shape[0], 6, d)

    mexp_np, lvl_np = _scan_consts()
    mexp = jnp.asarray(mexp_np, bf16)
    lvl = jnp.asarray(lvl_np, f32)
    cos_t, sin_t = _rope_tables(l_l)
    cum = jnp.cumsum(jax.nn.softmax(hgrn_lb.astype(f32), axis=0), axis=0)
    lower_bounds = cum - cum[:1]

    x = jnp.concatenate([x_prompt.reshape(t_c, d), x_sample.reshape(t_l, d)], axis=0)
    t = t_c + t_l
    zero_state = jnp.zeros((nb_c, N_HEADS, 2, HEAD_W, HEAD_W), f32)
    streams = (dict(nseq=nb_c, seq=l_c, row0=0), dict(nseq=nb_l, seq=l_l, row0=t_c))
    new_gdn, new_gla, new_hg, new_ckv, new_kpe = [], [], [], [], []

    for l in range(depth):
        mod = mods[l]
        p = _inproj(x, mod, _relayout_w_in(w_in[l]), t_c, l_l)

        gsm = p[:, OFF_SMALL + SM_GDN_B:OFF_SMALL + SM_GDN_B + 16].reshape(t // CHUNK, CHUNK, 4, N_HEADS)
        grow = jnp.pad(gsm.transpose(3, 0, 2, 1), ((0, 0), (0, 0), (0, 4), (0, 0)))
        zc = jnp.zeros((N_HEADS, 2), f32)
        hp = jnp.stack([jnp.concatenate([zc, gdn_a_log[l].T, zc, zc], axis=1),
                        jnp.concatenate([zc, gdn_dt_bias[l].T, zc, zc], axis=1)], axis=-1)

        la_f, la_b = _gla_prep(p, *_gla_gate_weights(gla_gate_w2[l], gla_gate_b[l]))
        hla_f, hla_b, hk_f, hk_b = _hgrn_prep(p, lower_bounds[l])
        wqa, wqb = _relayout_wq(mla_wq_b[l])
        wkv = _relayout_wkv(mla_wkv_b[l])
        qnw, kvnw = mla_q_norm[l][None, :], mla_kv_norm[l][None, :]

        outs = {k: [] for k in ('gdn', 'gla', 'hg', 'mla')}
        for si, st in enumerate(streams):
            nseq, seq, row0 = st['nseq'], st['seq'], st['row0']
            rb = row0 // seq
            if si == 0:
                s_gdn0 = s_gla0 = s_hg0 = zero_state
            else:
                s_gdn0 = state_gdn[:, l].transpose(0, 2, 1, 3, 4)
                s_gla0 = jnp.pad(state_gla[:, l], ((0, 0),) * 3 + ((0, HEAD_W - GLA_DK), (0, 0))).transpose(0, 2, 1, 4, 3)
                s_hg0 = state_hgrn[:, l].transpose(0, 2, 1, 4, 3)
            o_gdn, s_gdn = _gdn(p, gdn_conv[l], grow, hp, gdn_norm[l][None, :], s_gdn0, nseq=nseq, seq=seq, row_blk0=rb)
            o_gla, s_gla = _scan((p, OFF_GLA_Q, rb), (p, OFF_GLA_K, rb), (p, OFF_GLA_K, rb), (p, OFF_GLA_V, rb),
                                 la_f, la_b, (p, OFF_GLA_R, rb), gla_norm[l][None, :], s_gla0, mexp, lvl,
                                 nseq=nseq, seq=seq, row_blk0=rb, q_silu=False, q_scale=GLA_DK ** -0.5, gate_silu=True)
            o_hg, s_hg = _scan((p, OFF_HG_Q, rb), (hk_f, 0, rb), (hk_b, 0, rb), (p, OFF_HG_I, rb),
                               hla_f, hla_b, (p, OFF_HG_G, rb), hgrn_norm[l][None, :], s_hg0, mexp, lvl,
                               nseq=nseq, seq=seq, row_blk0=rb, q_silu=True, q_scale=1.0, gate_silu=False)
            rope = si == 1
            qn, qp, ckv, kn, vv, kp = _mla_proj(p, qnw, kvnw, wqa, wkv, wqb if rope else None,
                                                cos_t if rope else None, sin_t if rope else None,
                                                nrows=nseq * seq, seq=seq, row0=row0)
            ctx = None
            if si == 1:
                knc, vvc = _kv_proj(cache_mla_ckv[:, l].reshape(nb_l * past, -1), wkv)
                kpc = jnp.pad(cache_mla_kpe[:, l].reshape(nb_l * past, MLA_ROPE), ((0, 0), (0, LANE - MLA_ROPE))).astype(bf16)
                ctx = (knc, kpc, vvc)
            o_mla = _attention(qn, qp, kn, kp, vv, ctx, nseq=nseq, seq=seq)
            outs['gdn'].append(o_gdn)
            outs['gla'].append(o_gla)
            outs['hg'].append(o_hg)
            outs['mla'].append(o_mla)
            if si == 0:
                new_gdn.append(s_gdn.transpose(0, 2, 1, 3, 4))
                new_gla.append(s_gla.transpose(0, 2, 1, 4, 3)[:, :, :, :GLA_DK, :])
                new_hg.append(s_hg.transpose(0, 2, 1, 4, 3))
                new_ckv.append(ckv.reshape(nb_c, l_c, -1))
                new_kpe.append(p[:t_c, OFF_SMALL + SM_KPE:OFF_SMALL + SM_KPE + MLA_ROPE].reshape(nb_c, l_c, MLA_ROPE))

        branches = [jnp.concatenate(outs[k], axis=0) for k in ('gdn', 'gla', 'hg', 'mla')]
        x1 = _merge(branches, p, b_gates[l], w_branch[l].astype(bf16), w_out[l].astype(bf16), x, mod,
                    ln1_g[l][None, :], ln1_b[l][None, :], alpha=alpha, t_ctx=t_c, l_lat=l_l)
        x = _ffn(x1, mod, ffn_w1[l].astype(bf16), ffn_w3[l].astype(bf16), ffn_w2[l].astype(bf16),
                 ln2_g[l][None, :], ln2_b[l][None, :], alpha=alpha, t_ctx=t_c, l_lat=l_l)

    sdt = x_prompt.dtype
    return (x[:t_c].reshape(nb_c, l_c, d), x[t_c:].reshape(nb_l, l_l, d),
            jnp.stack(new_gdn, axis=1).astype(sdt), jnp.stack(new_gla, axis=1).astype(sdt),
            jnp.stack(new_hg, axis=1).astype(sdt), jnp.stack(new_ckv, axis=1), jnp.stack(new_kpe, axis=1))
```

```python
import functools

import numpy as np
import jax
import jax.numpy as jnp
from jax import lax
from jax.experimental import pallas as pl
from jax.experimental.pallas import tpu as pltpu

f32 = jnp.float32
bf16 = jnp.bfloat16

D_MODEL = 2048
N_HEADS = 4
HEAD_W = 128
BRANCH_W = N_HEADS * HEAD_W
GLA_DK = 64
GLA_RANK = 16
GLA_TAU = 16.0
GDN_CONV = 5
MLA_NOPE = 128
MLA_ROPE = 64
ROPE_BASE = 10000.0
GRID_W = 64
NORM_EPS = 1e-6
CHUNK = 64
SUB = 16
LANE = 128
VMEM_LIMIT = 56 * 1024 * 1024

_SEG = 512
OFF_GDN_Q, OFF_GDN_K, OFF_GDN_V, OFF_GDN_Z = 0, 512, 1024, 1536
OFF_GLA_Q, OFF_GLA_K, OFF_GLA_V, OFF_GLA_R = 2048, 2560, 3072, 3584
OFF_HG_Q, OFF_HG_FF, OFF_HG_FB, OFF_HG_I, OFF_HG_G = 4096, 4608, 5120, 5632, 6144
OFF_MLA_QA, OFF_MLA_KVA, OFF_SMALL, OFF_GATES = 6656, 7168, 7680, 8192
N_PROJ = OFF_GATES + N_HEADS * D_MODEL
SM_KPE, SM_KPE_SW, SM_GLA_G, SM_GDN_B, SM_GDN_A = 0, 64, 128, 160, 168

_SRC = {}
_o = 0
for _n, _w in (('gdn_q', 512), ('gdn_k', 512), ('gdn_v', 512), ('gdn_z', 512), ('gdn_b', 8), ('gdn_a', 8),
               ('gla_q', 256), ('gla_k', 256), ('gla_v', 512), ('gla_r', 512), ('gla_g', 32),
               ('hg_q', 512), ('hg_f', 1024), ('hg_i', 512), ('hg_g', 512),
               ('mla_qa', 512), ('mla_kva', 512), ('mla_kpe', 64), ('gates', 4 * D_MODEL)):
    _SRC[_n] = _o
    _o += _w
IN_WIDTH = _o


def _cparams(sem):
    return pltpu.CompilerParams(dimension_semantics=sem, vmem_limit_bytes=VMEM_LIMIT)


def _dot(a, b):
    return jnp.dot(a, b, preferred_element_type=f32)


def _dot_nt(a, b):
    return lax.dot_general(a, b, (((1,), (1,)), ((), ())), preferred_element_type=f32)


def _dot_tn(a, b):
    return lax.dot_general(a, b, (((0,), (0,)), ((), ())), preferred_element_type=f32)


def _split2(x):
    hi = x.astype(bf16)
    lo = (x - hi.astype(f32)).astype(bf16)
    return hi, lo


def _mm3(a, b):
    ah, al = _split2(a)
    bh, bl = _split2(b)
    return _dot(ah, bh) + (_dot(ah, bl) + _dot(al, bh))


def _layernorm(y, g, b):
    mu = jnp.mean(y, axis=-1, keepdims=True)
    yc = y - mu
    var = jnp.mean(yc * yc, axis=-1, keepdims=True)
    return yc * lax.rsqrt(var + NORM_EPS) * g + b


def _rmsnorm(y, w):
    return y * lax.rsqrt(jnp.mean(y * y, axis=-1, keepdims=True) + NORM_EPS) * w


def _ada_kernel(c_ref, w_ref, b_ref, o_ref):
    cs = jax.nn.silu(c_ref[...]).astype(bf16)
    o_ref[...] = _dot(cs, w_ref[...].astype(bf16)) + b_ref[...]


def _ada(cc, w_ada, b_ada):
    depth, d, n6 = w_ada.shape
    rows = cc.shape[0]
    tn = 1024
    return pl.pallas_call(
        _ada_kernel,
        out_shape=jax.ShapeDtypeStruct((depth, rows, n6), f32),
        grid=(depth, n6 // tn),
        in_specs=[pl.BlockSpec((rows, d), lambda l, j: (0, 0)),
                  pl.BlockSpec((None, d, tn), lambda l, j: (l, 0, j)),
                  pl.BlockSpec((None, 1, tn), lambda l, j: (l, 0, j))],
        out_specs=pl.BlockSpec((None, rows, tn), lambda l, j: (l, 0, j)),
        compiler_params=_cparams(("parallel", "parallel")),
        name="ada_mod",
    )(cc, w_ada, b_ada.reshape(depth, 1, n6))


def _inproj_kernel(x_ref, mod_ref, w_ref, o_ref, xb_ref):
    @pl.when(pl.program_id(1) == 0)
    def _():
        shift = mod_ref[0:1, :]
        scale = mod_ref[1:2, :]
        xb_ref[...] = (x_ref[...] * (1.0 + scale) + shift).astype(bf16)

    o_ref[...] = _dot(xb_ref[...], w_ref[...])


def _mod_index(tm, t_ctx, l_lat):
    def index(i):
        r = i * tm
        return jnp.where(r < t_ctx, 0, 1 + (r - t_ctx) // l_lat)
    return index


def _inproj(x, mod, w, t_ctx, l_lat):
    t, d = x.shape
    n = w.shape[1]
    tm, tn = 1024, 512
    midx = _mod_index(tm, t_ctx, l_lat)
    return pl.pallas_call(
        _inproj_kernel,
        out_shape=jax.ShapeDtypeStruct((t, n), f32),
        grid=(t // tm, n // tn),
        in_specs=[pl.BlockSpec((tm, d), lambda i, j: (i, 0)),
                  pl.BlockSpec((None, 6, d), lambda i, j: (midx(i), 0, 0)),
                  pl.BlockSpec((d, tn), lambda i, j: (0, j))],
        out_specs=pl.BlockSpec((tm, tn), lambda i, j: (i, j)),
        scratch_shapes=[pltpu.VMEM((tm, d), bf16)],
        compiler_params=_cparams(("parallel", "arbitrary")),
        name="in_proj",
    )(x, mod, w)


_PACK = 4
GROUP = _PACK * CHUNK


def _gdn_kernel(q_ref, k_ref, v_ref, z_ref, cwq_ref, cwk_ref, cwv_ref, grow_ref, hp_ref, nw_ref, s0_ref,
                o_ref, so_ref, pad_ref, qs, ks, vs, gs, ub_s, wq_s, aq_s, kd_s, ct_s, of, ob, s_ref, *, seq):
    n = seq // CHUNK

    def conv_silu(x_ref, w_ref):
        pad_ref[0:8, :] = jnp.zeros((8, HEAD_W), f32)
        pad_ref[8 + seq:16 + seq, :] = jnp.zeros((8, HEAD_W), f32)
        pad_ref[8:8 + seq, :] = x_ref[...]
        acc = pad_ref[pl.ds(6, seq), :] * w_ref[0:1, :]
        for j in range(1, GDN_CONV):
            acc = acc + pad_ref[pl.ds(6 + j, seq), :] * w_ref[j:j + 1, :]
        return jax.nn.silu(acc)

    def l2norm(x):
        return x * lax.rsqrt(jnp.sum(x * x, axis=-1, keepdims=True) + NORM_EPS)

    qs[...] = l2norm(conv_silu(q_ref, cwq_ref)) * (HEAD_W ** -0.5)
    ks[...] = l2norm(conv_silu(k_ref, cwk_ref))
    vs[...] = conv_silu(v_ref, cwv_ref)

    x = grow_ref[...]
    a_log = hp_ref[:, 0:1][None]
    dt_b = hp_ref[:, 1:2][None]
    rows = lax.broadcasted_iota(jnp.int32, x.shape, 1)
    gs[...] = jnp.where(rows < 2, jax.nn.sigmoid(x), -jnp.exp(a_log) * jax.nn.softplus(x + dt_b))
    s_ref[...] = s0_ref[...]

    ri = lax.broadcasted_iota(jnp.int32, (CHUNK, GROUP), 0)
    li = lax.broadcasted_iota(jnp.int32, (CHUNK, GROUP), 1)
    cj = li % CHUNK
    blk = [(li // CHUNK) == r for r in range(_PACK)]
    blk_bf = [jnp.where(b, 1.0, 0.0).astype(bf16) for b in blk]
    l1 = lax.broadcasted_iota(jnp.int32, (1, GROUP), 1) // CHUNK
    eye_b = ri == cj
    eye = jnp.where(eye_b, 1.0, 0.0).astype(f32)
    bd16 = (ri // SUB) == (cj // SUB)
    bd32 = (ri // (2 * SUB)) == (cj // (2 * SUB))
    r2 = lax.broadcasted_iota(jnp.int32, (GROUP, GROUP), 0)
    c2 = lax.broadcasted_iota(jnp.int32, (GROUP, GROUP), 1)
    same_blk = (r2 // CHUNK) == (c2 // CHUNK)

    def bdiag(yb):
        return jnp.concatenate([yb * mk for mk in blk_bf], axis=0)

    def unpack_diag(xf):
        out = xf[0:CHUNK]
        for r in range(1, _PACK):
            out = jnp.where(blk[r], xf[r * CHUNK:(r + 1) * CHUNK], out)
        return out

    def pmm3(a, b):
        ah, al = _split2(a)
        bh, bl = _split2(b)
        r = _dot(jnp.concatenate([ah, al], axis=0), bdiag(bh))
        return r[:CHUNK] + r[CHUNK:] + _dot(ah, bdiag(bl))

    def unit_tri_inverse(m):
        m32 = jnp.where(bd32, m, 0.0)
        dg = jnp.where(bd16, m, 0.0)
        d2 = pmm3(dg, dg)
        d4 = pmm3(d2, d2)
        d8 = pmm3(d4, d4)
        xi = pmm3(eye - dg, eye + d2)
        xi = pmm3(xi, eye + d4)
        xi = pmm3(xi, eye + d8)
        xi = xi - pmm3(pmm3(xi, m32 - dg), xi)
        return xi - pmm3(pmm3(xi, m - m32), xi)

    def block_cols(xp):
        return [jnp.sum(jnp.where(blk[r], xp, 0.0), axis=1, keepdims=True) for r in range(_PACK)]

    def spread(cols):
        out = jnp.broadcast_to(cols[0], (CHUNK, GROUP))
        for r in range(1, _PACK):
            out = jnp.where(blk[r], cols[r], out)
        return out

    def prepare_group(g, carry):
        rows_g = pl.ds(pl.multiple_of(g * GROUP, GROUP), GROUP)
        k4 = ks[rows_g, :]
        q4 = qs[rows_g, :]
        v4 = vs[rows_g, :]
        k4b = k4.astype(bf16)
        kq = _dot_nt(jnp.concatenate([k4b, q4.astype(bf16)], axis=0), k4b)
        kk_p = unpack_diag(kq[:GROUP])
        qk_p = unpack_diag(kq[GROUP:])
        g8 = gs[g]
        for d in range(2):
            beta_r = g8[d:d + 1, :]
            g_r = g8[2 + d:3 + d, :]
            tri = (cj <= ri) if d == 0 else (cj >= ri)
            strict = jnp.logical_and(tri, jnp.logical_not(eye_b))
            tg = jnp.where(tri, g_r, 0.0)
            gam_cols = block_cols(tg)
            beta_cols = block_cols(jnp.where(eye_b, beta_r, 0.0))
            tot_cols = [jnp.sum(jnp.where(l1 == r, g_r, 0.0), axis=1, keepdims=True) for r in range(_PACK)]
            th, tl = _split2(tg)
            tl2 = (tg - th.astype(f32) - tl.astype(f32)).astype(bf16)
            strict2 = ((c2 % CHUNK) < (r2 % CHUNK)) if d == 0 else ((c2 % CHUNK) > (r2 % CHUNK))
            sm = jnp.where(jnp.logical_and(same_blk, strict2), 1.0, 0.0).astype(bf16)
            r3 = _dot(jnp.concatenate([th, tl, tl2], axis=0), sm)
            dlt = r3[:CHUNK] + r3[CHUNK:2 * CHUNK] + r3[2 * CHUNK:]
            decay = jnp.where(tri, jnp.exp(jnp.minimum(dlt, 0.0)), 0.0)
            m = jnp.where(strict, kk_p * spread(beta_cols) * decay, 0.0)
            t_inv = unit_tri_inverse(m)
            gam = jnp.concatenate(gam_cols, axis=0)
            beta = jnp.concatenate(beta_cols, axis=0)
            tot = jnp.concatenate([jnp.broadcast_to(t, (CHUNK, 1)) for t in tot_cols], axis=0)
            eg = jnp.exp(gam)
            kb = k4 * beta
            rhs = jnp.concatenate([v4 * beta, kb * eg], axis=1)
            ih, il = _split2(t_inv)
            rh, rl = _split2(rhs)
            bih = bdiag(ih)
            r = _dot(jnp.concatenate([bih, bdiag(il)], axis=0), rh)
            sol = r[:GROUP] + r[GROUP:] + _dot(bih, rl)
            ub_s[d, rows_g, :] = sol[:, :HEAD_W]
            w_b = sol[:, HEAD_W:].astype(bf16)
            qd_b = (q4 * eg).astype(bf16)
            a_p = qk_p * decay
            for r_ in range(_PACK):
                c = g * _PACK + r_
                wq_s[d, c, 0:CHUNK, :] = w_b[r_ * CHUNK:(r_ + 1) * CHUNK]
                wq_s[d, c, CHUNK:2 * CHUNK, :] = qd_b[r_ * CHUNK:(r_ + 1) * CHUNK]
                aq_s[d, c] = a_p[:, r_ * CHUNK:(r_ + 1) * CHUNK].astype(bf16)
                ct_s[d, c] = jnp.broadcast_to(jnp.exp(tot_cols[r_]), (8, HEAD_W))
            kd_s[d, rows_g, :] = (k4 * jnp.exp(tot - gam)).astype(bf16)
        return carry

    lax.fori_loop(0, n // _PACK, prepare_group, 0)

    def advance(d, c, out):
        rows_c = pl.ds(pl.multiple_of(c * CHUNK, CHUNK), CHUNK)
        s = s_ref[d]
        sb = s.astype(bf16)
        r = _dot(wq_s[d, c], sb)
        u = (ub_s[d, rows_c, :] - r[:CHUNK]).astype(bf16)
        out[rows_c, :] = r[CHUNK:] + _dot(aq_s[d, c], u)
        s_ref[d] = ct_s[d, c][0:1, :] * s + _dot_tn(kd_s[d, rows_c, :], u)

    def step(c, carry):
        advance(0, c, of)
        advance(1, n - 1 - c, ob)
        return carry

    lax.fori_loop(0, n, step, 0)
    o = of[...] + ob[...]
    o_ref[...] = (_rmsnorm(o, nw_ref[...]) * jax.nn.silu(z_ref[...])).astype(bf16)
    so_ref[...] = s_ref[...]


def _gdn(p, conv_w, grow, hp, norm_w, s0, *, nseq, seq, row_blk0):
    n = seq // CHUNK
    ng = seq // GROUP
    hb = _SEG // HEAD_W

    def pcol(off):
        return pl.BlockSpec((seq, HEAD_W), lambda b, h: (row_blk0 + b, off // HEAD_W + h))

    def wcol(off):
        return pl.BlockSpec((GDN_CONV, HEAD_W), lambda b, h: (0, off // HEAD_W + h))

    kern = functools.partial(_gdn_kernel, seq=seq)
    return pl.pallas_call(
        kern,
        out_shape=(jax.ShapeDtypeStruct((nseq * seq, BRANCH_W), bf16),
                   jax.ShapeDtypeStruct((nseq, N_HEADS, 2, HEAD_W, HEAD_W), f32)),
        grid=(nseq, N_HEADS),
        in_specs=[pcol(OFF_GDN_Q), pcol(OFF_GDN_K), pcol(OFF_GDN_V), pcol(OFF_GDN_Z),
                  wcol(0), wcol(hb * HEAD_W), wcol(2 * hb * HEAD_W),
                  pl.BlockSpec((None, ng, 8, GROUP), lambda b, h: (h, row_blk0 + b, 0, 0)),
                  pl.BlockSpec((None, 8, 2), lambda b, h: (h, 0, 0)),
                  pl.BlockSpec((1, HEAD_W), lambda b, h: (0, 0)),
                  pl.BlockSpec((None, None, 2, HEAD_W, HEAD_W), lambda b, h: (b, h, 0, 0, 0))],
        out_specs=(pl.BlockSpec((seq, HEAD_W), lambda b, h: (b, h)),
                   pl.BlockSpec((None, None, 2, HEAD_W, HEAD_W), lambda b, h: (b, h, 0, 0, 0))),
        scratch_shapes=[pltpu.VMEM((seq + 16, HEAD_W), f32),
                        pltpu.VMEM((seq, HEAD_W), f32), pltpu.VMEM((seq, HEAD_W), f32),
                        pltpu.VMEM((seq, HEAD_W), f32), pltpu.VMEM((ng, 8, GROUP), f32),
                        pltpu.VMEM((2, seq, HEAD_W), f32), pltpu.VMEM((2, n, 2 * CHUNK, HEAD_W), bf16),
                        pltpu.VMEM((2, n, CHUNK, CHUNK), bf16),
                        pltpu.VMEM((2, seq, HEAD_W), bf16), pltpu.VMEM((2, n, 8, HEAD_W), f32),
                        pltpu.VMEM((seq, HEAD_W), f32), pltpu.VMEM((seq, HEAD_W), f32),
                        pltpu.VMEM((2, HEAD_W, HEAD_W), f32)],
        compiler_params=_cparams(("parallel", "parallel")),
        name="gdn",
    )(p, p, p, p, conv_w, conv_w, conv_w, grow, hp, norm_w, s0)


_N_LEVELS = 6
_ROW_EQ, _ROW_EK, _ROW_TOT, _ROWS_EXP = 384, 448, 512, 520


def _scan_consts():
    c = CHUNK
    mexp = np.zeros((2, _ROWS_EXP, c), np.float32)
    lvl = np.full((2, c, c), -1.0, np.float32)
    for lv in range(_N_LEVELS):
        s = 32 >> lv
        for i in range(c):
            p = (i // (2 * s)) * (2 * s) + s
            right = (i % (2 * s)) >= s
            if right:
                mexp[0, lv * c + i, p:i + 1] = 1.0
                mexp[1, lv * c + i, p:i] = 1.0
            else:
                mexp[0, lv * c + i, i + 1:p] = 1.0
                mexp[1, lv * c + i, i:p] = 1.0
            for j in range(c):
                if (i // (2 * s)) != (j // (2 * s)):
                    continue
                jright = (j % (2 * s)) >= s
                if right and not jright:
                    lvl[0, i, j] = lv
                if (not right) and jright:
                    lvl[1, i, j] = lv
    for i in range(c):
        lvl[:, i, i] = _N_LEVELS
        mexp[0, _ROW_EQ + i, :i + 1] = 1.0
        mexp[0, _ROW_EK + i, i + 1:] = 1.0
        mexp[1, _ROW_EQ + i, i:] = 1.0
        mexp[1, _ROW_EK + i, :i] = 1.0
    mexp[:, _ROW_TOT:, :] = 1.0
    return mexp, lvl


def _scan_kernel(q_ref, kf_ref, kb_ref, v_ref, laf_ref, lab_ref, gate_ref, nw_ref, s0_ref, mexp_ref, lvl_ref,
                 o_ref, so_ref, of, ob, st_ref, *, seq, q_silu, q_scale, gate_silu):
    n = seq // CHUNK
    st_ref[...] = s0_ref[...]

    def chunk(d, c):
        r0 = pl.multiple_of(c * CHUNK, CHUNK)
        q = q_ref[pl.ds(r0, CHUNK), :]
        q = jax.nn.silu(q) if q_silu else q * q_scale
        k = (kf_ref if d == 0 else kb_ref)[pl.ds(r0, CHUNK), :]
        v = v_ref[pl.ds(r0, CHUNK), :].astype(bf16)
        la = (laf_ref if d == 0 else lab_ref)[pl.ds(r0, CHUNK), :]
        hi, lo = _split2(la)
        e2 = _dot(mexp_ref[d], jnp.concatenate([hi, lo], axis=1))
        e = e2[:, :HEAD_W] + e2[:, HEAD_W:]
        lvl = lvl_ref[d]
        a = jnp.where(lvl == float(_N_LEVELS), _dot_nt(q.astype(bf16), k.astype(bf16)), 0.0)
        for lv in range(_N_LEVELS):
            w = jnp.exp(e[lv * CHUNK:(lv + 1) * CHUNK])
            a = jnp.where(lvl == float(lv), _dot_nt((q * w).astype(bf16), (k * w).astype(bf16)), a)
        e_q = e[_ROW_EQ:_ROW_EQ + CHUNK]
        e_k = e[_ROW_EK:_ROW_EK + CHUNK]
        tot = e[_ROW_TOT:_ROW_TOT + 1]
        st = st_ref[d]
        o = _dot(a.astype(bf16), v) + _dot_nt((q * jnp.exp(e_q)).astype(bf16), st.astype(bf16))
        st_ref[d] = st * jnp.exp(tot) + _dot_tn(v, (k * jnp.exp(e_k)).astype(bf16))
        return r0, o

    def body(c, carry):
        r0, o = chunk(0, c)
        of[pl.ds(r0, CHUNK), :] = o
        r1, o1 = chunk(1, n - 1 - c)
        ob[pl.ds(r1, CHUNK), :] = o1
        return carry

    lax.fori_loop(0, n, body, 0)
    o = of[...] + ob[...]
    g = gate_ref[...]
    g = jax.nn.silu(g) if gate_silu else jax.nn.sigmoid(g)
    o_ref[...] = (_rmsnorm(o, nw_ref[...]) * g).astype(bf16)
    so_ref[...] = st_ref[...]


def _scan(q_src, kf_src, kb_src, v_src, laf, lab, gate_src, norm_w, s0t, mexp, lvl, *,
          nseq, seq, row_blk0, q_silu, q_scale, gate_silu):
    def spec(src):
        off, rb = src[1], src[2]
        return pl.BlockSpec((seq, HEAD_W), lambda b, h: (rb + b, off // HEAD_W + h))

    srcs = (q_src, kf_src, kb_src, v_src, (laf, 0, row_blk0), (lab, 0, row_blk0), gate_src)
    kern = functools.partial(_scan_kernel, seq=seq, q_silu=q_silu, q_scale=q_scale, gate_silu=gate_silu)
    return pl.pallas_call(
        kern,
        out_shape=(jax.ShapeDtypeStruct((nseq * seq, BRANCH_W), bf16),
                   jax.ShapeDtypeStruct((nseq, N_HEADS, 2, HEAD_W, HEAD_W), f32)),
        grid=(nseq, N_HEADS),
        in_specs=[spec(s) for s in srcs] + [
            pl.BlockSpec((1, HEAD_W), lambda b, h: (0, 0)),
            pl.BlockSpec((None, None, 2, HEAD_W, HEAD_W), lambda b, h: (b, h, 0, 0, 0)),
            pl.BlockSpec((2, _ROWS_EXP, CHUNK), lambda b, h: (0, 0, 0)),
            pl.BlockSpec((2, CHUNK, CHUNK), lambda b, h: (0, 0, 0))],
        out_specs=(pl.BlockSpec((seq, HEAD_W), lambda b, h: (b, h)),
                   pl.BlockSpec((None, None, 2, HEAD_W, HEAD_W), lambda b, h: (b, h, 0, 0, 0))),
        scratch_shapes=[pltpu.VMEM((seq, HEAD_W), f32), pltpu.VMEM((seq, HEAD_W), f32),
                        pltpu.VMEM((2, HEAD_W, HEAD_W), f32)],
        compiler_params=_cparams(("parallel", "parallel")),
        name="decay_scan",
    )(*[s[0] for s in srcs], norm_w, s0t, mexp, lvl)


_LOG_DECAY_FLOOR = -1.0e4


def _gla_prep_kernel(sm_ref, w2_ref, b_ref, laf_ref, lab_ref):
    x = sm_ref[...].astype(bf16)
    for d, out in ((0, laf_ref), (1, lab_ref)):
        logits = _dot(x, w2_ref[d]) + b_ref[d]
        out[...] = jnp.maximum(jax.nn.log_sigmoid(logits) * (1.0 / GLA_TAU), _LOG_DECAY_FLOOR)


def _gla_prep(p, w2big, bbig):
    t = p.shape[0]
    tm = 512
    out = jax.ShapeDtypeStruct((t, BRANCH_W), f32)
    return pl.pallas_call(
        _gla_prep_kernel,
        out_shape=(out, out),
        grid=(t // tm,),
        in_specs=[pl.BlockSpec((tm, _SEG), lambda i: (i, OFF_SMALL // _SEG)),
                  pl.BlockSpec((2, _SEG, BRANCH_W), lambda i: (0, 0, 0)),
                  pl.BlockSpec((2, 1, BRANCH_W), lambda i: (0, 0, 0))],
        out_specs=(pl.BlockSpec((tm, BRANCH_W), lambda i: (i, 0)),
                   pl.BlockSpec((tm, BRANCH_W), lambda i: (i, 0))),
        compiler_params=_cparams(("parallel",)),
        name="gla_prep",
    )(p, w2big, bbig)


def _hgrn_prep_kernel(zf_ref, zb_ref, lb_ref, laf_ref, lab_ref, kf_ref, kb_ref):
    for d, z_ref, la_out, k_out in ((0, zf_ref, laf_ref, kf_ref), (1, zb_ref, lab_ref, kb_ref)):
        z = z_ref[...]
        lb = lb_ref[d:d + 1, :]
        a = jnp.log(lb)
        b = jnp.log1p(-lb) + jax.nn.log_sigmoid(z)
        mx = jnp.maximum(a, b)
        lse = mx + jnp.log(jnp.exp(a - mx) + jnp.exp(b - mx))
        lse = jnp.where(mx == -jnp.inf, -jnp.inf, lse)
        la_out[...] = jnp.maximum(lse, _LOG_DECAY_FLOOR)
        k_out[...] = (1.0 - lb) * jax.nn.sigmoid(-z)


def _hgrn_prep(p, lb):
    t = p.shape[0]
    tm = 512
    out = jax.ShapeDtypeStruct((t, BRANCH_W), f32)
    ospec = pl.BlockSpec((tm, BRANCH_W), lambda i: (i, 0))
    return pl.pallas_call(
        _hgrn_prep_kernel,
        out_shape=(out, out, out, out),
        grid=(t // tm,),
        in_specs=[pl.BlockSpec((tm, _SEG), lambda i: (i, OFF_HG_FF // _SEG)),
                  pl.BlockSpec((tm, _SEG), lambda i: (i, OFF_HG_FB // _SEG)),
                  pl.BlockSpec((2, BRANCH_W), lambda i: (0, 0))],
        out_specs=(ospec, ospec, ospec, ospec),
        compiler_params=_cparams(("parallel",)),
        name="hgrn_prep",
    )(p, p, lb)


def _mla_proj_kernel(*refs, rope):
    if rope:
        (qa_ref, kva_ref, sm_ref, qnw_ref, kvnw_ref, wqa_ref, wkv_ref, wqb_ref, cos_ref, sin_ref,
         qn_ref, qp_ref, ckv_ref, kn_ref, vv_ref, kp_ref) = refs
    else:
        (qa_ref, kva_ref, sm_ref, qnw_ref, kvnw_ref, wqa_ref, wkv_ref,
         qn_ref, qp_ref, ckv_ref, kn_ref, vv_ref, kp_ref) = refs
    qh = _rmsnorm(qa_ref[...], qnw_ref[...]).astype(bf16)
    qa = _dot(qh, wqa_ref[...])
    qn_ref[...] = qa[:, :BRANCH_W].astype(bf16)
    pe = qa[:, BRANCH_W:]
    kpe = sm_ref[:, 0:LANE]
    if rope:
        cos = cos_ref[...]
        sin = sin_ref[...]
        cos4 = jnp.concatenate([cos] * N_HEADS, axis=1)
        sin4 = jnp.concatenate([sin] * N_HEADS, axis=1)
        pe = pe * cos4 + _dot(qh, wqb_ref[...]) * sin4
        kpe = kpe * cos + pltpu.roll(kpe, MLA_ROPE, 1) * sin
    qp_ref[...] = pe.astype(bf16)
    kp_ref[...] = kpe.astype(bf16)
    ckv = _rmsnorm(kva_ref[...], kvnw_ref[...])
    ckv_ref[...] = ckv
    kv = _dot(ckv.astype(bf16), wkv_ref[...])
    kn_ref[...] = kv[:, :BRANCH_W].astype(bf16)
    vv_ref[...] = kv[:, BRANCH_W:].astype(bf16)


def _mla_proj(p, qnw, kvnw, wqa, wkv, wqb, cos, sin, *, nrows, seq, row0):
    tm = 256
    rope = cos is not None
    rb0 = row0 // tm
    per_seq = seq // tm

    def pspec(off):
        return pl.BlockSpec((tm, _SEG), lambda i: (rb0 + i, off // _SEG))

    def full(a):
        return pl.BlockSpec(a.shape, lambda i: (0,) * a.ndim)

    in_specs = [pspec(OFF_MLA_QA), pspec(OFF_MLA_KVA), pspec(OFF_SMALL), full(qnw), full(kvnw), full(wqa), full(wkv)]
    args = [p, p, p, qnw, kvnw, wqa, wkv]
    if rope:
        tspec = pl.BlockSpec((tm, LANE), lambda i: (i % per_seq, 0))
        in_specs += [full(wqb), tspec, tspec]
        args += [wqb, cos, sin]
    wide = lambda dt: jax.ShapeDtypeStruct((nrows, BRANCH_W), dt)
    ospec = pl.BlockSpec((tm, BRANCH_W), lambda i: (i, 0))
    return pl.pallas_call(
        functools.partial(_mla_proj_kernel, rope=rope),
        out_shape=(wide(bf16), wide(bf16), wide(f32), wide(bf16), wide(bf16),
                   jax.ShapeDtypeStruct((nrows, LANE), bf16)),
        grid=(nrows // tm,),
        in_specs=in_specs,
        out_specs=(ospec, ospec, ospec, ospec, ospec, pl.BlockSpec((tm, LANE), lambda i: (i, 0))),
        compiler_params=_cparams(("parallel",)),
        name="mla_proj",
    )(*args)


def _kv_kernel(ckv_ref, w_ref, kn_ref, vv_ref):
    kv = _dot(ckv_ref[...].astype(bf16), w_ref[...])
    kn_ref[...] = kv[:, :BRANCH_W].astype(bf16)
    vv_ref[...] = kv[:, BRANCH_W:].astype(bf16)


def _kv_proj(ckv, wkv):
    rows = ckv.shape[0]
    tm = 256
    out = jax.ShapeDtypeStruct((rows, BRANCH_W), bf16)
    ospec = pl.BlockSpec((tm, BRANCH_W), lambda i: (i, 0))
    return pl.pallas_call(
        _kv_kernel, out_shape=(out, out), grid=(rows // tm,),
        in_specs=[pl.BlockSpec((tm, ckv.shape[1]), lambda i: (i, 0)),
                  pl.BlockSpec(wkv.shape, lambda i: (0, 0))],
        out_specs=(ospec, ospec),
        compiler_params=_cparams(("parallel",)),
        name="mla_ctx_kv",
    )(ckv, wkv)


def _attn_kernel(*refs, has_ctx):
    if has_ctx:
        qn_ref, qp_ref, kn_ref, kp_ref, vv_ref, knc_ref, kpc_ref, vvc_ref, o_ref = refs
    else:
        qn_ref, qp_ref, kn_ref, kp_ref, vv_ref, o_ref = refs
    scale = (MLA_NOPE + MLA_ROPE) ** -0.5
    qn = qn_ref[...]
    qp = qp_ref[...]
    s1 = (_dot_nt(qn, kn_ref[...]) + _dot_nt(qp, kp_ref[...])) * scale
    mx = jnp.max(s1, axis=-1, keepdims=True)
    if has_ctx:
        s2 = (_dot_nt(qn, knc_ref[...]) + _dot_nt(qp, kpc_ref[...])) * scale
        mx = jnp.maximum(mx, jnp.max(s2, axis=-1, keepdims=True))
    p1 = jnp.exp(s1 - mx)
    den = jnp.sum(p1, axis=-1, keepdims=True)
    if has_ctx:
        p2 = jnp.exp(s2 - mx)
        den = den + jnp.sum(p2, axis=-1, keepdims=True)
    inv = 1.0 / den
    o = _dot((p1 * inv).astype(bf16), vv_ref[...])
    if has_ctx:
        o = o + _dot((p2 * inv).astype(bf16), vvc_ref[...])
    o_ref[...] = o.astype(bf16)


def _attention(qn, qp, kn, kp, vv, ctx, *, nseq, seq):
    tq = 256
    nq = seq // tq
    has_ctx = ctx is not None
    qspec = pl.BlockSpec((tq, HEAD_W), lambda b, h, i: (b * nq + i, h))
    kspec = pl.BlockSpec((seq, HEAD_W), lambda b, h, i: (b, h))
    kpspec = pl.BlockSpec((seq, LANE), lambda b, h, i: (b, 0))
    in_specs = [qspec, qspec, kspec, kpspec, kspec]
    args = [qn, qp, kn, kp, vv]
    if has_ctx:
        knc, kpc, vvc = ctx
        lc = knc.shape[0] // nseq
        in_specs += [pl.BlockSpec((lc, HEAD_W), lambda b, h, i: (b, h)),
                     pl.BlockSpec((lc, LANE), lambda b, h, i: (b, 0)),
                     pl.BlockSpec((lc, HEAD_W), lambda b, h, i: (b, h))]
        args += [knc, kpc, vvc]
    return pl.pallas_call(
        functools.partial(_attn_kernel, has_ctx=has_ctx),
        out_shape=jax.ShapeDtypeStruct((nseq * seq, BRANCH_W), bf16),
        grid=(nseq, N_HEADS, nq),
        in_specs=in_specs,
        out_specs=pl.BlockSpec((tq, HEAD_W), lambda b, h, i: (b * nq + i, h)),
        compiler_params=_cparams(("parallel", "parallel", "parallel")),
        name="mla_attn",
    )(*args)


def _merge_kernel(o0_ref, o1_ref, o2_ref, o3_ref, g0_ref, g1_ref, g2_ref, g3_ref, bg_ref, wb_ref, wo_ref,
                  x_ref, mod_ref, lng_ref, lnb_ref, out_ref, *, alpha):
    m = None
    for k, (o_ref, g_ref) in enumerate(((o0_ref, g0_ref), (o1_ref, g1_ref), (o2_ref, g2_ref), (o3_ref, g3_ref))):
        term = jax.nn.sigmoid(g_ref[...] + bg_ref[k:k + 1, :]) * _dot(o_ref[...], wb_ref[k])
        m = term if m is None else m + term
    mix = _dot(m.astype(bf16), wo_ref[...])
    gate1 = mod_ref[2:3, :]
    out_ref[...] = _layernorm(alpha * x_ref[...] + gate1 * mix, lng_ref[...], lnb_ref[...])


def _merge(branches, p, bg, wb, wo, x, mod, lng, lnb, *, alpha, t_ctx, l_lat):
    t, d = x.shape
    tm = 256
    midx = _mod_index(tm, t_ctx, l_lat)
    ospec = pl.BlockSpec((tm, BRANCH_W), lambda i: (i, 0))
    gspecs = [pl.BlockSpec((tm, d), functools.partial(lambda i, k: (i, OFF_GATES // d + k), k=k)) for k in range(4)]
    single = dict(pipeline_mode=pl.Buffered(1))
    return pl.pallas_call(
        functools.partial(_merge_kernel, alpha=alpha),
        out_shape=jax.ShapeDtypeStruct((t, d), f32),
        grid=(t // tm,),
        in_specs=[ospec, ospec, ospec, ospec] + gspecs + [
            pl.BlockSpec((4, d), lambda i: (0, 0)),
            pl.BlockSpec((4, BRANCH_W, d), lambda i: (0, 0, 0), **single),
            pl.BlockSpec((d, d), lambda i: (0, 0), **single),
            pl.BlockSpec((tm, d), lambda i: (i, 0)),
            pl.BlockSpec((None, 6, d), lambda i: (midx(i), 0, 0)),
            pl.BlockSpec((1, d), lambda i: (0, 0)),
            pl.BlockSpec((1, d), lambda i: (0, 0))],
        out_specs=pl.BlockSpec((tm, d), lambda i: (i, 0)),
        compiler_params=_cparams(("parallel",)),
        name="merge_out_ln",
    )(*branches, p, p, p, p, bg, wb, wo, x, mod, lng, lnb)


def _ffn_kernel(x_ref, mod_ref, w1_ref, w3_ref, w2_ref, lng_ref, lnb_ref, out_ref, hb_ref, acc_ref, *, alpha):
    f = pl.program_id(1)

    @pl.when(f == 0)
    def _():
        shift = mod_ref[3:4, :]
        scale = mod_ref[4:5, :]
        hb_ref[...] = (x_ref[...] * (1.0 + scale) + shift).astype(bf16)
        acc_ref[...] = jnp.zeros_like(acc_ref)

    h = hb_ref[...]
    g = (jax.nn.silu(_dot(h, w1_ref[...])) * _dot(h, w3_ref[...])).astype(bf16)
    acc_ref[...] += _dot(g, w2_ref[...])

    @pl.when(f == pl.num_programs(1) - 1)
    def _():
        gate2 = mod_ref[5:6, :]
        out_ref[...] = _layernorm(alpha * x_ref[...] + gate2 * acc_ref[...], lng_ref[...], lnb_ref[...])


def _ffn(x, mod, w1, w3, w2, lng, lnb, *, alpha, t_ctx, l_lat):
    t, d = x.shape
    dff = w1.shape[1]
    tm, tf = 512, 512
    midx = _mod_index(tm, t_ctx, l_lat)
    return pl.pallas_call(
        functools.partial(_ffn_kernel, alpha=alpha),
        out_shape=jax.ShapeDtypeStruct((t, d), f32),
        grid=(t // tm, dff // tf),
        in_specs=[pl.BlockSpec((tm, d), lambda i, f: (i, 0)),
                  pl.BlockSpec((None, 6, d), lambda i, f: (midx(i), 0, 0)),
                  pl.BlockSpec((d, tf), lambda i, f: (0, f)),
                  pl.BlockSpec((d, tf), lambda i, f: (0, f)),
                  pl.BlockSpec((tf, d), lambda i, f: (f, 0)),
                  pl.BlockSpec((1, d), lambda i, f: (0, 0)),
                  pl.BlockSpec((1, d), lambda i, f: (0, 0))],
        out_specs=pl.BlockSpec((tm, d), lambda i, f: (i, 0)),
        scratch_shapes=[pltpu.VMEM((tm, d), bf16), pltpu.VMEM((tm, d), f32)],
        compiler_params=_cparams(("parallel", "arbitrary")),
        name="ffn_ln",
    )(x, mod, w1, w3, w2, lng, lnb)


def _relayout_w_in(w):
    d = w.shape[0]

    def seg(name, start=0, width=None):
        s = _SRC[name] + start
        return w[:, s:s + width]

    def zeros(n):
        return jnp.zeros((d, n), w.dtype)

    def head_padded(name):
        parts = []
        for h in range(N_HEADS):
            parts += [seg(name, h * GLA_DK, GLA_DK), zeros(HEAD_W - GLA_DK)]
        return parts

    q4 = MLA_ROPE // 4
    kpe_swapped = [seg('mla_kpe', q4, q4), seg('mla_kpe', 0, q4), seg('mla_kpe', 3 * q4, q4), seg('mla_kpe', 2 * q4, q4)]
    small = ([seg('mla_kpe', 0, MLA_ROPE)] + kpe_swapped + [seg('gla_g', 0, 2 * GLA_RANK), seg('gdn_b', 0, 8),
             seg('gdn_a', 0, 8), zeros(_SEG - SM_GDN_A - 8)])
    parts = ([seg('gdn_q', 0, 512), seg('gdn_k', 0, 512), seg('gdn_v', 0, 512), seg('gdn_z', 0, 512)]
             + head_padded('gla_q') + head_padded('gla_k')
             + [seg('gla_v', 0, 512), seg('gla_r', 0, 512), seg('hg_q', 0, 512), seg('hg_f', 0, 512), seg('hg_f', 512, 512),
                seg('hg_i', 0, 512), seg('hg_g', 0, 512), seg('mla_qa', 0, 512), seg('mla_kva', 0, 512)]
             + small + [seg('gates', 0, 4 * D_MODEL)])
    out = jnp.concatenate(parts, axis=1).astype(bf16)
    assert out.shape[1] == N_PROJ
    return out


def _relayout_wq(wq):
    hw = MLA_NOPE + MLA_ROPE
    z = jnp.zeros((wq.shape[0], HEAD_W - MLA_ROPE), wq.dtype)
    q4 = MLA_ROPE // 4
    nope, pe, pes = [], [], []
    for h in range(N_HEADS):
        base = h * hw
        nope.append(wq[:, base:base + MLA_NOPE])
        r = wq[:, base + MLA_NOPE:base + hw]
        pe += [r, z]
        pes += [r[:, q4:2 * q4], r[:, 0:q4], r[:, 3 * q4:], r[:, 2 * q4:3 * q4], z]
    return jnp.concatenate(nope + pe, axis=1).astype(bf16), jnp.concatenate(pes, axis=1).astype(bf16)


def _relayout_wkv(wkv):
    hw = MLA_NOPE + HEAD_W
    kn = [wkv[:, h * hw:h * hw + MLA_NOPE] for h in range(N_HEADS)]
    vv = [wkv[:, h * hw + MLA_NOPE:(h + 1) * hw] for h in range(N_HEADS)]
    return jnp.concatenate(kn + vv, axis=1).astype(bf16)


def _rope_tables(length):
    pos = jnp.arange(length)
    row_id = (pos // GRID_W).astype(f32)
    col_id = (pos % GRID_W).astype(f32)
    half = MLA_ROPE // 2
    inv = ROPE_BASE ** (-jnp.arange(0, half, 2, dtype=f32) / half)
    ar, ac = row_id[:, None] * inv, col_id[:, None] * inv
    z = jnp.zeros((length, LANE - MLA_ROPE), f32)
    cos = jnp.concatenate([jnp.cos(ar), jnp.cos(ar), jnp.cos(ac), jnp.cos(ac), z], axis=1)
    sin = jnp.concatenate([-jnp.sin(ar), jnp.sin(ar), -jnp.sin(ac), jnp.sin(ac), z], axis=1)
    return cos, sin


def _gla_gate_weights(w2, b):
    wbig = jnp.zeros((2, _SEG, BRANCH_W), f32)
    bbig = jnp.zeros((2, 1, BRANCH_W), f32)
    for d in range(2):
        r0 = SM_GLA_G + d * GLA_RANK
        for h in range(N_HEADS):
            wbig = wbig.at[d, r0:r0 + GLA_RANK, h * HEAD_W:h * HEAD_W + GLA_DK].set(w2[d, :, h * GLA_DK:(h + 1) * GLA_DK])
            bbig = bbig.at[d, 0, h * HEAD_W:h * HEAD_W + GLA_DK].set(b[d, h * GLA_DK:(h + 1) * GLA_DK])
    return wbig.astype(bf16), bbig


def kernel(x_prompt, x_sample, c, state_gdn, state_gla, state_hgrn, cache_mla_ckv, cache_mla_kpe, c_ctx, w_ada, b_ada, w_in, gdn_conv, gdn_a_log, gdn_dt_bias, gdn_norm, gla_gate_w2, gla_gate_b, gla_norm, hgrn_lb, hgrn_norm, mla_q_norm, mla_wq_b, mla_kv_norm, mla_wkv_b, w_branch, b_gates, w_out, ln1_g, ln1_b, ln2_g, ln2_b, ffn_w1, ffn_w3, ffn_w2):
    nb_c, l_c, d = x_prompt.shape
    nb_l, l_l, _ = x_sample.shape
    depth = w_in.shape[0]
    t_c, t_l = nb_c * l_c, nb_l * l_l
    past = cache_mla_ckv.shape[2]
    alpha = (2.0 * depth) ** 0.25
    assert d == D_MODEL and t_c % 1024 == 0 and l_l % 1024 == 0 and l_c % CHUNK == 0

    n_cond = 1 + nb_l
    cc = jnp.concatenate([c_ctx[None, :], c, jnp.zeros((-n_cond % 8, d), f32)], axis=0)
    mods = _ada(cc, w_ada, b_ada).reshape(depth, cc.shape[0], 6, d)

    mexp_np, lvl_np = _scan_consts()
    mexp = jnp.asarray(mexp_np, bf16)
    lvl = jnp.asarray(lvl_np, f32)
    cos_t, sin_t = _rope_tables(l_l)
    cum = jnp.cumsum(jax.nn.softmax(hgrn_lb.astype(f32), axis=0), axis=0)
    lower_bounds = cum - cum[:1]

    x = jnp.concatenate([x_prompt.reshape(t_c, d), x_sample.reshape(t_l, d)], axis=0)
    t = t_c + t_l
    zero_state = jnp.zeros((nb_c, N_HEADS, 2, HEAD_W, HEAD_W), f32)
    streams = (dict(nseq=nb_c, seq=l_c, row0=0), dict(nseq=nb_l, seq=l_l, row0=t_c))
    new_gdn, new_gla, new_hg, new_ckv, new_kpe = [], [], [], [], []

    for l in range(depth):
        mod = mods[l]
        p = _inproj(x, mod, _relayout_w_in(w_in[l]), t_c, l_l)

        gsm = p[:, OFF_SMALL + SM_GDN_B:OFF_SMALL + SM_GDN_B + 16].reshape(t // GROUP, GROUP, 4, N_HEADS)
        grow = jnp.pad(gsm.transpose(3, 0, 2, 1), ((0, 0), (0, 0), (0, 4), (0, 0)))
        zc = jnp.zeros((N_HEADS, 2), f32)
        hp = jnp.stack([jnp.concatenate([zc, gdn_a_log[l].T, zc, zc], axis=1),
                        jnp.concatenate([zc, gdn_dt_bias[l].T, zc, zc], axis=1)], axis=-1)

        la_f, la_b = _gla_prep(p, *_gla_gate_weights(gla_gate_w2[l], gla_gate_b[l]))
        hla_f, hla_b, hk_f, hk_b = _hgrn_prep(p, lower_bounds[l])
        wqa, wqb = _relayout_wq(mla_wq_b[l])
        wkv = _relayout_wkv(mla_wkv_b[l])
        qnw, kvnw = mla_q_norm[l][None, :], mla_kv_norm[l][None, :]

        outs = {k: [] for k in ('gdn', 'gla', 'hg', 'mla')}
        for si, st in enumerate(streams):
            nseq, seq, row0 = st['nseq'], st['seq'], st['row0']
            rb = row0 // seq
            if si == 0:
                s_gdn0 = s_gla0 = s_hg0 = zero_state
            else:
                s_gdn0 = state_gdn[:, l].transpose(0, 2, 1, 3, 4)
                s_gla0 = jnp.pad(state_gla[:, l], ((0, 0),) * 3 + ((0, HEAD_W - GLA_DK), (0, 0))).transpose(0, 2, 1, 4, 3)
                s_hg0 = state_hgrn[:, l].transpose(0, 2, 1, 4, 3)
            o_gdn, s_gdn = _gdn(p, gdn_conv[l], grow, hp, gdn_norm[l][None, :], s_gdn0, nseq=nseq, seq=seq, row_blk0=rb)
            o_gla, s_gla = _scan((p, OFF_GLA_Q, rb), (p, OFF_GLA_K, rb), (p, OFF_GLA_K, rb), (p, OFF_GLA_V, rb),
                                 la_f, la_b, (p, OFF_GLA_R, rb), gla_norm[l][None, :], s_gla0, mexp, lvl,
                                 nseq=nseq, seq=seq, row_blk0=rb, q_silu=False, q_scale=GLA_DK ** -0.5, gate_silu=True)
            o_hg, s_hg = _scan((p, OFF_HG_Q, rb), (hk_f, 0, rb), (hk_b, 0, rb), (p, OFF_HG_I, rb),
                               hla_f, hla_b, (p, OFF_HG_G, rb), hgrn_norm[l][None, :], s_hg0, mexp, lvl,
                               nseq=nseq, seq=seq, row_blk0=rb, q_silu=True, q_scale=1.0, gate_silu=False)
            rope = si == 1
            qn, qp, ckv, kn, vv, kp = _mla_proj(p, qnw, kvnw, wqa, wkv, wqb if rope else None,
                                                cos_t if rope else None, sin_t if rope else None,
                                                nrows=nseq * seq, seq=seq, row0=row0)
            ctx = None
            if si == 1:
                knc, vvc = _kv_proj(cache_mla_ckv[:, l].reshape(nb_l * past, -1), wkv)
                kpc = jnp.pad(cache_mla_kpe[:, l].reshape(nb_l * past, MLA_ROPE), ((0, 0), (0, LANE - MLA_ROPE))).astype(bf16)
                ctx = (knc, kpc, vvc)
            o_mla = _attention(qn, qp, kn, kp, vv, ctx, nseq=nseq, seq=seq)
            outs['gdn'].append(o_gdn)
            outs['gla'].append(o_gla)
            outs['hg'].append(o_hg)
            outs['mla'].append(o_mla)
            if si == 0:
                new_gdn.append(s_gdn.transpose(0, 2, 1, 3, 4))
                new_gla.append(s_gla.transpose(0, 2, 1, 4, 3)[:, :, :, :GLA_DK, :])
                new_hg.append(s_hg.transpose(0, 2, 1, 4, 3))
                new_ckv.append(ckv.reshape(nb_c, l_c, -1))
                new_kpe.append(p[:t_c, OFF_SMALL + SM_KPE:OFF_SMALL + SM_KPE + MLA_ROPE].reshape(nb_c, l_c, MLA_ROPE))

        branches = [jnp.concatenate(outs[k], axis=0) for k in ('gdn', 'gla', 'hg', 'mla')]
        x1 = _merge(branches, p, b_gates[l], w_branch[l].astype(bf16), w_out[l].astype(bf16), x, mod,
                    ln1_g[l][None, :], ln1_b[l][None, :], alpha=alpha, t_ctx=t_c, l_lat=l_l)
        x = _ffn(x1, mod, ffn_w1[l].astype(bf16), ffn_w3[l].astype(bf16), ffn_w2[l].astype(bf16),
                 ln2_g[l][None, :], ln2_b[l][None, :], alpha=alpha, t_ctx=t_c, l_lat=l_l)

    sdt = x_prompt.dtype
    return (x[:t_c].reshape(nb_c, l_c, d), x[t_c:].reshape(nb_l, l_l, d),
            jnp.stack(new_gdn, axis=1).astype(sdt), jnp.stack(new_gla, axis=1).astype(sdt),
            jnp.stack(new_hg, axis=1).astype(sdt), jnp.stack(new_ckv, axis=1), jnp.stack(new_kpe, axis=1))
```

```python
import functools

import numpy as np
import jax
import jax.numpy as jnp
from jax import lax
from jax.experimental import pallas as pl
from jax.experimental.pallas import tpu as pltpu

f32 = jnp.float32
bf16 = jnp.bfloat16

D_MODEL = 2048
N_HEADS = 4
HEAD_W = 128
BRANCH_W = N_HEADS * HEAD_W
GLA_DK = 64
GLA_RANK = 16
GLA_TAU = 16.0
GDN_CONV = 5
MLA_NOPE = 128
MLA_ROPE = 64
ROPE_BASE = 10000.0
GRID_W = 64
NORM_EPS = 1e-6
CHUNK = 64
SUB = 16
LANE = 128
VMEM_LIMIT = 56 * 1024 * 1024

_SEG = 512
OFF_GDN_Q, OFF_GDN_K, OFF_GDN_V, OFF_GDN_Z = 0, 512, 1024, 1536
OFF_GLA_Q, OFF_GLA_K, OFF_GLA_V, OFF_GLA_R = 2048, 2560, 3072, 3584
OFF_HG_Q, OFF_HG_FF, OFF_HG_FB, OFF_HG_I, OFF_HG_G = 4096, 4608, 5120, 5632, 6144
OFF_MLA_QA, OFF_MLA_KVA, OFF_SMALL, OFF_GATES = 6656, 7168, 7680, 8192
N_PROJ = OFF_GATES + N_HEADS * D_MODEL
SM_KPE, SM_KPE_SW, SM_GLA_G, SM_GDN_B, SM_GDN_A = 0, 64, 128, 160, 168

_SRC = {}
_o = 0
for _n, _w in (('gdn_q', 512), ('gdn_k', 512), ('gdn_v', 512), ('gdn_z', 512), ('gdn_b', 8), ('gdn_a', 8),
               ('gla_q', 256), ('gla_k', 256), ('gla_v', 512), ('gla_r', 512), ('gla_g', 32),
               ('hg_q', 512), ('hg_f', 1024), ('hg_i', 512), ('hg_g', 512),
               ('mla_qa', 512), ('mla_kva', 512), ('mla_kpe', 64), ('gates', 4 * D_MODEL)):
    _SRC[_n] = _o
    _o += _w
IN_WIDTH = _o


def _cparams(sem):
    return pltpu.CompilerParams(dimension_semantics=sem, vmem_limit_bytes=VMEM_LIMIT)


def _dot(a, b):
    return jnp.dot(a, b, preferred_element_type=f32)


def _dot_nt(a, b):
    return lax.dot_general(a, b, (((1,), (1,)), ((), ())), preferred_element_type=f32)


def _dot_tn(a, b):
    return lax.dot_general(a, b, (((0,), (0,)), ((), ())), preferred_element_type=f32)


def _split2(x):
    hi = x.astype(bf16)
    lo = (x - hi.astype(f32)).astype(bf16)
    return hi, lo


def _mm3(a, b):
    ah, al = _split2(a)
    bh, bl = _split2(b)
    return _dot(ah, bh) + (_dot(ah, bl) + _dot(al, bh))


def _layernorm(y, g, b):
    mu = jnp.mean(y, axis=-1, keepdims=True)
    yc = y - mu
    var = jnp.mean(yc * yc, axis=-1, keepdims=True)
    return yc * lax.rsqrt(var + NORM_EPS) * g + b


def _rmsnorm(y, w):
    return y * lax.rsqrt(jnp.mean(y * y, axis=-1, keepdims=True) + NORM_EPS) * w


def _ada_kernel(c_ref, w_ref, b_ref, o_ref):
    cs = jax.nn.silu(c_ref[...]).astype(bf16)
    o_ref[...] = _dot(cs, w_ref[...].astype(bf16)) + b_ref[...]


def _ada(cc, w_ada, b_ada):
    depth, d, n6 = w_ada.shape
    rows = cc.shape[0]
    tn = 1024
    return pl.pallas_call(
        _ada_kernel,
        out_shape=jax.ShapeDtypeStruct((depth, rows, n6), f32),
        grid=(depth, n6 // tn),
        in_specs=[pl.BlockSpec((rows, d), lambda l, j: (0, 0)),
                  pl.BlockSpec((None, d, tn), lambda l, j: (l, 0, j)),
                  pl.BlockSpec((None, 1, tn), lambda l, j: (l, 0, j))],
        out_specs=pl.BlockSpec((None, rows, tn), lambda l, j: (l, 0, j)),
        compiler_params=_cparams(("parallel", "parallel")),
        name="ada_mod",
    )(cc, w_ada, b_ada.reshape(depth, 1, n6))


def _inproj_kernel(x_ref, mod_ref, w_ref, o_ref, xb_ref):
    @pl.when(pl.program_id(1) == 0)
    def _():
        shift = mod_ref[0:1, :]
        scale = mod_ref[1:2, :]
        xb_ref[...] = (x_ref[...] * (1.0 + scale) + shift).astype(bf16)

    o_ref[...] = _dot(xb_ref[...], w_ref[...])


def _mod_index(tm, t_ctx, l_lat):
    def index(i):
        r = i * tm
        return jnp.where(r < t_ctx, 0, 1 + (r - t_ctx) // l_lat)
    return index


def _inproj(x, mod, w, t_ctx, l_lat):
    t, d = x.shape
    n = w.shape[1]
    tm, tn = 1024, 512
    midx = _mod_index(tm, t_ctx, l_lat)
    return pl.pallas_call(
        _inproj_kernel,
        out_shape=jax.ShapeDtypeStruct((t, n), f32),
        grid=(t // tm, n // tn),
        in_specs=[pl.BlockSpec((tm, d), lambda i, j: (i, 0)),
                  pl.BlockSpec((None, 6, d), lambda i, j: (midx(i), 0, 0)),
                  pl.BlockSpec((d, tn), lambda i, j: (0, j))],
        out_specs=pl.BlockSpec((tm, tn), lambda i, j: (i, j)),
        scratch_shapes=[pltpu.VMEM((tm, d), bf16)],
        compiler_params=_cparams(("parallel", "arbitrary")),
        name="in_proj",
    )(x, mod, w)


_PACK = 4
GROUP = _PACK * CHUNK
_GDN_HEADS = 2


def _gdn_kernel(q_ref, k_ref, v_ref, z_ref, cwq_ref, cwk_ref, cwv_ref, grow_ref, hp_ref, nw_ref, s0_ref,
                o_ref, so_ref, pad_ref, qs, ks, vs, gs, ub_s, wq_s, aq_s, kd_s, ct_s, of, ob, s_ref, *, seq):
    n = seq // CHUNK
    width = _GDN_HEADS * HEAD_W
    chains = [(hd, d) for hd in range(_GDN_HEADS) for d in range(2)]

    def conv_silu(x_ref, w_ref):
        pad_ref[0:8, :] = jnp.zeros((8, width), f32)
        pad_ref[8 + seq:16 + seq, :] = jnp.zeros((8, width), f32)
        pad_ref[8:8 + seq, :] = x_ref[...]
        acc = pad_ref[pl.ds(6, seq), :] * w_ref[0:1, :]
        for j in range(1, GDN_CONV):
            acc = acc + pad_ref[pl.ds(6 + j, seq), :] * w_ref[j:j + 1, :]
        return jax.nn.silu(acc)

    def l2norm_heads(x):
        parts = []
        for hd in range(_GDN_HEADS):
            xh = x[:, hd * HEAD_W:(hd + 1) * HEAD_W]
            parts.append(xh * lax.rsqrt(jnp.sum(xh * xh, axis=-1, keepdims=True) + NORM_EPS))
        return jnp.concatenate(parts, axis=1)

    qs[...] = l2norm_heads(conv_silu(q_ref, cwq_ref)) * (HEAD_W ** -0.5)
    ks[...] = l2norm_heads(conv_silu(k_ref, cwk_ref))
    vs[...] = conv_silu(v_ref, cwv_ref)

    x = grow_ref[...]
    a_log = hp_ref[:, :, 0:1][:, None]
    dt_b = hp_ref[:, :, 1:2][:, None]
    rows = lax.broadcasted_iota(jnp.int32, x.shape, 2)
    gs[...] = jnp.where(rows < 2, jax.nn.sigmoid(x), -jnp.exp(a_log) * jax.nn.softplus(x + dt_b))
    s_ref[...] = s0_ref[...].reshape(2 * _GDN_HEADS, HEAD_W, HEAD_W)

    ri = lax.broadcasted_iota(jnp.int32, (CHUNK, GROUP), 0)
    li = lax.broadcasted_iota(jnp.int32, (CHUNK, GROUP), 1)
    cj = li % CHUNK
    blk = [(li // CHUNK) == r for r in range(_PACK)]
    blk_bf = [jnp.where(b, 1.0, 0.0).astype(bf16) for b in blk]
    l1 = lax.broadcasted_iota(jnp.int32, (1, GROUP), 1) // CHUNK
    eye_b = ri == cj
    eye = jnp.where(eye_b, 1.0, 0.0).astype(f32)
    bd16 = (ri // SUB) == (cj // SUB)
    bd32 = (ri // (2 * SUB)) == (cj // (2 * SUB))
    r2 = lax.broadcasted_iota(jnp.int32, (GROUP, GROUP), 0)
    c2 = lax.broadcasted_iota(jnp.int32, (GROUP, GROUP), 1)
    same_blk = (r2 // CHUNK) == (c2 // CHUNK)

    def bdiag(yb):
        return jnp.concatenate([yb * mk for mk in blk_bf], axis=0)

    def unpack_diag(xf):
        out = xf[0:CHUNK]
        for r in range(1, _PACK):
            out = jnp.where(blk[r], xf[r * CHUNK:(r + 1) * CHUNK], out)
        return out

    def pmm3(pairs):
        sp = [(_split2(a), _split2(b)) for a, b in pairs]
        r1 = [_dot(jnp.concatenate([ah, al], axis=0), bdiag(bh)) for (ah, al), (bh, _) in sp]
        r2 = [_dot(ah, bdiag(bl)) for (ah, _), (_, bl) in sp]
        return [x1[:CHUNK] + x1[CHUNK:] + x2 for x1, x2 in zip(r1, r2)]

    def unit_tri_inverse(ms):
        k = len(ms)
        m32 = [jnp.where(bd32, m, 0.0) for m in ms]
        dg = [jnp.where(bd16, m, 0.0) for m in ms]
        d2 = pmm3([(a, a) for a in dg])
        both = pmm3([(a, a) for a in d2] + [(eye - a, eye + b) for a, b in zip(dg, d2)])
        d4, xi = both[:k], both[k:]
        both = pmm3([(a, a) for a in d4] + [(a, eye + b) for a, b in zip(xi, d4)])
        d8, xi = both[:k], both[k:]
        xi = pmm3([(a, eye + b) for a, b in zip(xi, d8)])
        for lo, hi in ((dg, m32), (m32, ms)):
            t = pmm3([(a, h_ - l_) for a, h_, l_ in zip(xi, hi, lo)])
            t = pmm3(list(zip(t, xi)))
            xi = [a - b for a, b in zip(xi, t)]
        return xi

    def block_cols(xp):
        return [jnp.sum(jnp.where(blk[r], xp, 0.0), axis=1, keepdims=True) for r in range(_PACK)]

    def spread(cols):
        out = jnp.broadcast_to(cols[0], (CHUNK, GROUP))
        for r in range(1, _PACK):
            out = jnp.where(blk[r], cols[r], out)
        return out

    def prepare_group(g, carry):
        rows_g = pl.ds(pl.multiple_of(g * GROUP, GROUP), GROUP)
        heads = range(_GDN_HEADS)
        k4 = [ks[rows_g, hd * HEAD_W:(hd + 1) * HEAD_W] for hd in heads]
        q4 = [qs[rows_g, hd * HEAD_W:(hd + 1) * HEAD_W] for hd in heads]
        v4 = [vs[rows_g, hd * HEAD_W:(hd + 1) * HEAD_W] for hd in heads]
        k4b = [x.astype(bf16) for x in k4]
        kq = [_dot_nt(jnp.concatenate([k4b[hd], q4[hd].astype(bf16)], axis=0), k4b[hd]) for hd in heads]
        kk_p = [unpack_diag(x[:GROUP]) for x in kq]
        qk_p = [unpack_diag(x[GROUP:]) for x in kq]
        g8 = [gs[hd, g] for hd in heads]
        ids = range(len(chains))
        beta_r = [g8[hd][d:d + 1, :] for hd, d in chains]
        g_r = [g8[hd][2 + d:3 + d, :] for hd, d in chains]
        tri2 = [cj <= ri, cj >= ri]
        tri = [tri2[d] for _, d in chains]
        strict = [jnp.logical_and(t, jnp.logical_not(eye_b)) for t in tri]
        tg = [jnp.where(tri[i], g_r[i], 0.0) for i in ids]
        gam_cols = [block_cols(tg[i]) for i in ids]
        beta_cols = [block_cols(jnp.where(eye_b, beta_r[i], 0.0)) for i in ids]
        tot_cols = [[jnp.sum(jnp.where(l1 == r, g_r[i], 0.0), axis=1, keepdims=True) for r in range(_PACK)]
                    for i in ids]
        strict2 = [(c2 % CHUNK) < (r2 % CHUNK), (c2 % CHUNK) > (r2 % CHUNK)]
        sm = [jnp.where(jnp.logical_and(same_blk, s_), 1.0, 0.0).astype(bf16) for s_ in strict2]
        r3 = []
        for i, (_, d) in enumerate(chains):
            th, tl = _split2(tg[i])
            tl2 = (tg[i] - th.astype(f32) - tl.astype(f32)).astype(bf16)
            r3.append(_dot(jnp.concatenate([th, tl, tl2], axis=0), sm[d]))
        dlt = [x[:CHUNK] + x[CHUNK:2 * CHUNK] + x[2 * CHUNK:] for x in r3]
        decay = [jnp.where(tri[i], jnp.exp(jnp.minimum(dlt[i], 0.0)), 0.0) for i in ids]
        t_inv = unit_tri_inverse([jnp.where(strict[i], kk_p[hd] * spread(beta_cols[i]) * decay[i], 0.0)
                                  for i, (hd, _) in enumerate(chains)])
        gam = [jnp.concatenate(gam_cols[i], axis=0) for i in ids]
        beta = [jnp.concatenate(beta_cols[i], axis=0) for i in ids]
        tot = [jnp.concatenate([jnp.broadcast_to(t, (CHUNK, 1)) for t in tot_cols[i]], axis=0) for i in ids]
        eg = [jnp.exp(x) for x in gam]
        rhs = [jnp.concatenate([v4[hd] * beta[i], k4[hd] * beta[i] * eg[i]], axis=1)
               for i, (hd, _) in enumerate(chains)]
        isp = [_split2(x) for x in t_inv]
        rsp = [_split2(x) for x in rhs]
        bih = [bdiag(isp[i][0]) for i in ids]
        s1 = [_dot(jnp.concatenate([bih[i], bdiag(isp[i][1])], axis=0), rsp[i][0]) for i in ids]
        s2 = [_dot(bih[i], rsp[i][1]) for i in ids]
        for i, (hd, d) in enumerate(chains):
            sol = s1[i][:GROUP] + s1[i][GROUP:] + s2[i]
            ub_s[i, rows_g, :] = sol[:, :HEAD_W]
            w_b = sol[:, HEAD_W:].astype(bf16)
            qd_b = (q4[hd] * eg[i]).astype(bf16)
            a_p = qk_p[hd] * decay[i]
            for r_ in range(_PACK):
                c = g * _PACK + r_
                wq_s[i, c, 0:CHUNK, :] = w_b[r_ * CHUNK:(r_ + 1) * CHUNK]
                wq_s[i, c, CHUNK:2 * CHUNK, :] = qd_b[r_ * CHUNK:(r_ + 1) * CHUNK]
                aq_s[i, c] = a_p[:, r_ * CHUNK:(r_ + 1) * CHUNK].astype(bf16)
                ct_s[i, c] = jnp.broadcast_to(jnp.exp(tot_cols[i][r_]), (8, HEAD_W))
            kd_s[i, rows_g, :] = (k4[hd] * jnp.exp(tot[i] - gam[i])).astype(bf16)
        return carry

    lax.fori_loop(0, n // _PACK, prepare_group, 0)

    def step(t, carry):
        ids = range(len(chains))
        cs = [t if d == 0 else n - 1 - t for _, d in chains]
        rows = [pl.ds(pl.multiple_of(c * CHUNK, CHUNK), CHUNK) for c in cs]
        s = [s_ref[i] for i in ids]
        sb = [x.astype(bf16) for x in s]
        r = [_dot(wq_s[i, cs[i]], sb[i]) for i in ids]
        u = [(ub_s[i, rows[i], :] - r[i][:CHUNK]).astype(bf16) for i in ids]
        o_c = [r[i][CHUNK:] + _dot(aq_s[i, cs[i]], u[i]) for i in ids]
        s_new = [ct_s[i, cs[i]][0:1, :] * s[i] + _dot_tn(kd_s[i, rows[i], :], u[i]) for i in ids]
        for i, (hd, d) in enumerate(chains):
            (of if d == 0 else ob)[rows[i], hd * HEAD_W:(hd + 1) * HEAD_W] = o_c[i]
            s_ref[i] = s_new[i]
        return carry

    lax.fori_loop(0, n, step, 0)
    o = of[...] + ob[...]
    z = z_ref[...]
    nw = nw_ref[...]
    o_ref[...] = jnp.concatenate(
        [_rmsnorm(o[:, hd * HEAD_W:(hd + 1) * HEAD_W], nw) * jax.nn.silu(z[:, hd * HEAD_W:(hd + 1) * HEAD_W])
         for hd in range(_GDN_HEADS)], axis=1).astype(bf16)
    so_ref[...] = s_ref[...].reshape(_GDN_HEADS, 2, HEAD_W, HEAD_W)


def _gdn(p, conv_w, grow, hp, norm_w, s0, *, nseq, seq, row_blk0):
    n = seq // CHUNK
    ng = seq // GROUP
    hh = _GDN_HEADS
    width = hh * HEAD_W
    nch = 2 * hh

    def pcol(off):
        return pl.BlockSpec((seq, width), lambda b, h: (row_blk0 + b, off // width + h))

    def wcol(off):
        return pl.BlockSpec((GDN_CONV, width), lambda b, h: (0, off // width + h))

    kern = functools.partial(_gdn_kernel, seq=seq)
    return pl.pallas_call(
        kern,
        out_shape=(jax.ShapeDtypeStruct((nseq * seq, BRANCH_W), bf16),
                   jax.ShapeDtypeStruct((nseq, N_HEADS, 2, HEAD_W, HEAD_W), f32)),
        grid=(nseq, N_HEADS // hh),
        in_specs=[pcol(OFF_GDN_Q), pcol(OFF_GDN_K), pcol(OFF_GDN_V), pcol(OFF_GDN_Z),
                  wcol(0), wcol(BRANCH_W), wcol(2 * BRANCH_W),
                  pl.BlockSpec((hh, ng, 8, GROUP), lambda b, h: (h, row_blk0 + b, 0, 0)),
                  pl.BlockSpec((hh, 8, 2), lambda b, h: (h, 0, 0)),
                  pl.BlockSpec((1, HEAD_W), lambda b, h: (0, 0)),
                  pl.BlockSpec((None, hh, 2, HEAD_W, HEAD_W), lambda b, h: (b, h, 0, 0, 0))],
        out_specs=(pl.BlockSpec((seq, width), lambda b, h: (b, h)),
                   pl.BlockSpec((None, hh, 2, HEAD_W, HEAD_W), lambda b, h: (b, h, 0, 0, 0))),
        scratch_shapes=[pltpu.VMEM((seq + 16, width), f32),
                        pltpu.VMEM((seq, width), f32), pltpu.VMEM((seq, width), f32),
                        pltpu.VMEM((seq, width), f32), pltpu.VMEM((hh, ng, 8, GROUP), f32),
                        pltpu.VMEM((nch, seq, HEAD_W), f32), pltpu.VMEM((nch, n, 2 * CHUNK, HEAD_W), bf16),
                        pltpu.VMEM((nch, n, CHUNK, CHUNK), bf16),
                        pltpu.VMEM((nch, seq, HEAD_W), bf16), pltpu.VMEM((nch, n, 8, HEAD_W), f32),
                        pltpu.VMEM((seq, width), f32), pltpu.VMEM((seq, width), f32),
                        pltpu.VMEM((nch, HEAD_W, HEAD_W), f32)],
        compiler_params=_cparams(("parallel", "parallel")),
        name="gdn",
    )(p, p, p, p, conv_w, conv_w, conv_w, grow, hp, norm_w, s0)


_N_LEVELS = 6
_ROW_EQ, _ROW_EK, _ROW_TOT, _ROWS_EXP = 384, 448, 512, 520
_SCAN_UNROLL = 4


def _scan_consts():
    c = CHUNK
    mexp = np.zeros((2, _ROWS_EXP, c), np.float32)
    lvl = np.full((2, c, c), -1.0, np.float32)
    for lv in range(_N_LEVELS):
        s = 32 >> lv
        for i in range(c):
            p = (i // (2 * s)) * (2 * s) + s
            right = (i % (2 * s)) >= s
            if right:
                mexp[0, lv * c + i, p:i + 1] = 1.0
                mexp[1, lv * c + i, p:i] = 1.0
            else:
                mexp[0, lv * c + i, i + 1:p] = 1.0
                mexp[1, lv * c + i, i:p] = 1.0
            for j in range(c):
                if (i // (2 * s)) != (j // (2 * s)):
                    continue
                jright = (j % (2 * s)) >= s
                if right and not jright:
                    lvl[0, i, j] = lv
                if (not right) and jright:
                    lvl[1, i, j] = lv
    for i in range(c):
        lvl[:, i, i] = _N_LEVELS
        mexp[0, _ROW_EQ + i, :i + 1] = 1.0
        mexp[0, _ROW_EK + i, i + 1:] = 1.0
        mexp[1, _ROW_EQ + i, i:] = 1.0
        mexp[1, _ROW_EK + i, :i] = 1.0
    mexp[:, _ROW_TOT:, :] = 1.0
    return mexp, lvl


def _scan_kernel(q_ref, kf_ref, kb_ref, v_ref, laf_ref, lab_ref, gate_ref, nw_ref, s0_ref, mexp_ref, lvl_ref,
                 o_ref, so_ref, of, ob, qd_s, g_s, ct_s, stp_s, st_ref, *, seq, q_silu, q_scale, gate_silu):
    n = seq // CHUNK
    st_ref[...] = s0_ref[...]
    outs = (of, ob)

    def rows_of(c):
        return pl.ds(pl.multiple_of(c * CHUNK, CHUNK), CHUNK)

    def prepare_group(g, carry):
        todo = [(d, g * _SCAN_UNROLL + u) for u in range(_SCAN_UNROLL) for d in range(2)]
        items = range(len(todo))
        q, k, v, e2 = [], [], [], []
        for d, c in todo:
            rows_c = rows_of(c)
            qi = q_ref[rows_c, :]
            q.append(jax.nn.silu(qi) if q_silu else qi * q_scale)
            k.append((kf_ref if d == 0 else kb_ref)[rows_c, :])
            v.append(v_ref[rows_c, :].astype(bf16))
            hi, lo = _split2((laf_ref if d == 0 else lab_ref)[rows_c, :])
            e2.append(_dot(mexp_ref[d], jnp.concatenate([hi, lo], axis=1)))
        e = [x[:, :HEAD_W] + x[:, HEAD_W:] for x in e2]
        lvl = [lvl_ref[d] for d, _ in todo]
        qk = [_dot_nt(q[i].astype(bf16), k[i].astype(bf16)) for i in items]
        a = [jnp.where(lvl[i] == float(_N_LEVELS), qk[i], 0.0) for i in items]
        for lv in range(_N_LEVELS):
            w = [jnp.exp(e[i][lv * CHUNK:(lv + 1) * CHUNK]) for i in items]
            p = [_dot_nt((q[i] * w[i]).astype(bf16), (k[i] * w[i]).astype(bf16)) for i in items]
            a = [jnp.where(lvl[i] == float(lv), p[i], a[i]) for i in items]
        o_intra = [_dot(a[i].astype(bf16), v[i]) for i in items]
        incr = [_dot_tn(v[i], (k[i] * jnp.exp(e[i][_ROW_EK:_ROW_EK + CHUNK])).astype(bf16)) for i in items]
        for i, (d, c) in enumerate(todo):
            rows_c = rows_of(c)
            outs[d][rows_c, :] = o_intra[i]
            qd_s[d, rows_c, :] = (q[i] * jnp.exp(e[i][_ROW_EQ:_ROW_EQ + CHUNK])).astype(bf16)
            g_s[d, c] = incr[i]
            ct_s[d, c] = jnp.exp(e[i][_ROW_TOT:_ROW_TOT + 8])
        return carry

    lax.fori_loop(0, n // _SCAN_UNROLL, prepare_group, 0)

    def scan_step(i, carry):
        for d, c in ((0, i), (1, n - 1 - i)):
            st = st_ref[d]
            stp_s[d, c] = st.astype(bf16)
            st_ref[d] = st * ct_s[d, c][0:1, :] + g_s[d, c]
        return carry

    lax.fori_loop(0, n, scan_step, 0)

    def inter_group(g, carry):
        todo = [(d, g * _SCAN_UNROLL + u) for u in range(_SCAN_UNROLL) for d in range(2)]
        res = [_dot_nt(qd_s[d, rows_of(c), :], stp_s[d, c]) + outs[d][rows_of(c), :] for d, c in todo]
        for (d, c), o_c in zip(todo, res):
            outs[d][rows_of(c), :] = o_c
        return carry

    lax.fori_loop(0, n // _SCAN_UNROLL, inter_group, 0)
    o = of[...] + ob[...]
    g = gate_ref[...]
    g = jax.nn.silu(g) if gate_silu else jax.nn.sigmoid(g)
    o_ref[...] = (_rmsnorm(o, nw_ref[...]) * g).astype(bf16)
    so_ref[...] = st_ref[...]


def _scan(q_src, kf_src, kb_src, v_src, laf, lab, gate_src, norm_w, s0t, mexp, lvl, *,
          nseq, seq, row_blk0, q_silu, q_scale, gate_silu):
    def spec(src):
        off, rb = src[1], src[2]
        return pl.BlockSpec((seq, HEAD_W), lambda b, h: (rb + b, off // HEAD_W + h))

    srcs = (q_src, kf_src, kb_src, v_src, (laf, 0, row_blk0), (lab, 0, row_blk0), gate_src)
    kern = functools.partial(_scan_kernel, seq=seq, q_silu=q_silu, q_scale=q_scale, gate_silu=gate_silu)
    return pl.pallas_call(
        kern,
        out_shape=(jax.ShapeDtypeStruct((nseq * seq, BRANCH_W), bf16),
                   jax.ShapeDtypeStruct((nseq, N_HEADS, 2, HEAD_W, HEAD_W), f32)),
        grid=(nseq, N_HEADS),
        in_specs=[spec(s) for s in srcs] + [
            pl.BlockSpec((1, HEAD_W), lambda b, h: (0, 0)),
            pl.BlockSpec((None, None, 2, HEAD_W, HEAD_W), lambda b, h: (b, h, 0, 0, 0)),
            pl.BlockSpec((2, _ROWS_EXP, CHUNK), lambda b, h: (0, 0, 0)),
            pl.BlockSpec((2, CHUNK, CHUNK), lambda b, h: (0, 0, 0))],
        out_specs=(pl.BlockSpec((seq, HEAD_W), lambda b, h: (b, h)),
                   pl.BlockSpec((None, None, 2, HEAD_W, HEAD_W), lambda b, h: (b, h, 0, 0, 0))),
        scratch_shapes=[pltpu.VMEM((seq, HEAD_W), f32), pltpu.VMEM((seq, HEAD_W), f32),
                        pltpu.VMEM((2, seq, HEAD_W), bf16),
                        pltpu.VMEM((2, seq // CHUNK, HEAD_W, HEAD_W), f32),
                        pltpu.VMEM((2, seq // CHUNK, 8, HEAD_W), f32),
                        pltpu.VMEM((2, seq // CHUNK, HEAD_W, HEAD_W), bf16),
                        pltpu.VMEM((2, HEAD_W, HEAD_W), f32)],
        compiler_params=_cparams(("parallel", "parallel")),
        name="decay_scan",
    )(*[s[0] for s in srcs], norm_w, s0t, mexp, lvl)


_LOG_DECAY_FLOOR = -1.0e4


def _gla_prep_kernel(sm_ref, w2_ref, b_ref, laf_ref, lab_ref):
    x = sm_ref[...].astype(bf16)
    for d, out in ((0, laf_ref), (1, lab_ref)):
        logits = _dot(x, w2_ref[d]) + b_ref[d]
        out[...] = jnp.maximum(jax.nn.log_sigmoid(logits) * (1.0 / GLA_TAU), _LOG_DECAY_FLOOR)


def _gla_prep(p, w2big, bbig):
    t = p.shape[0]
    tm = 512
    out = jax.ShapeDtypeStruct((t, BRANCH_W), f32)
    return pl.pallas_call(
        _gla_prep_kernel,
        out_shape=(out, out),
        grid=(t // tm,),
        in_specs=[pl.BlockSpec((tm, _SEG), lambda i: (i, OFF_SMALL // _SEG)),
                  pl.BlockSpec((2, _SEG, BRANCH_W), lambda i: (0, 0, 0)),
                  pl.BlockSpec((2, 1, BRANCH_W), lambda i: (0, 0, 0))],
        out_specs=(pl.BlockSpec((tm, BRANCH_W), lambda i: (i, 0)),
                   pl.BlockSpec((tm, BRANCH_W), lambda i: (i, 0))),
        compiler_params=_cparams(("parallel",)),
        name="gla_prep",
    )(p, w2big, bbig)


def _hgrn_prep_kernel(zf_ref, zb_ref, lb_ref, laf_ref, lab_ref, kf_ref, kb_ref):
    for d, z_ref, la_out, k_out in ((0, zf_ref, laf_ref, kf_ref), (1, zb_ref, lab_ref, kb_ref)):
        z = z_ref[...]
        lb = lb_ref[d:d + 1, :]
        a = jnp.log(lb)
        b = jnp.log1p(-lb) + jax.nn.log_sigmoid(z)
        mx = jnp.maximum(a, b)
        lse = mx + jnp.log(jnp.exp(a - mx) + jnp.exp(b - mx))
        lse = jnp.where(mx == -jnp.inf, -jnp.inf, lse)
        la_out[...] = jnp.maximum(lse, _LOG_DECAY_FLOOR)
        k_out[...] = (1.0 - lb) * jax.nn.sigmoid(-z)


def _hgrn_prep(p, lb):
    t = p.shape[0]
    tm = 512
    out = jax.ShapeDtypeStruct((t, BRANCH_W), f32)
    ospec = pl.BlockSpec((tm, BRANCH_W), lambda i: (i, 0))
    return pl.pallas_call(
        _hgrn_prep_kernel,
        out_shape=(out, out, out, out),
        grid=(t // tm,),
        in_specs=[pl.BlockSpec((tm, _SEG), lambda i: (i, OFF_HG_FF // _SEG)),
                  pl.BlockSpec((tm, _SEG), lambda i: (i, OFF_HG_FB // _SEG)),
                  pl.BlockSpec((2, BRANCH_W), lambda i: (0, 0))],
        out_specs=(ospec, ospec, ospec, ospec),
        compiler_params=_cparams(("parallel",)),
        name="hgrn_prep",
    )(p, p, lb)


def _mla_proj_kernel(*refs, rope):
    if rope:
        (qa_ref, kva_ref, sm_ref, qnw_ref, kvnw_ref, wqa_ref, wkv_ref, wqb_ref, cos_ref, sin_ref,
         qn_ref, qp_ref, ckv_ref, kn_ref, vv_ref, kp_ref) = refs
    else:
        (qa_ref, kva_ref, sm_ref, qnw_ref, kvnw_ref, wqa_ref, wkv_ref,
         qn_ref, qp_ref, ckv_ref, kn_ref, vv_ref, kp_ref) = refs
    qh = _rmsnorm(qa_ref[...], qnw_ref[...]).astype(bf16)
    qa = _dot(qh, wqa_ref[...])
    qn_ref[...] = qa[:, :BRANCH_W].astype(bf16)
    pe = qa[:, BRANCH_W:]
    kpe = sm_ref[:, 0:LANE]
    if rope:
        cos = cos_ref[...]
        sin = sin_ref[...]
        cos4 = jnp.concatenate([cos] * N_HEADS, axis=1)
        sin4 = jnp.concatenate([sin] * N_HEADS, axis=1)
        pe = pe * cos4 + _dot(qh, wqb_ref[...]) * sin4
        kpe = kpe * cos + pltpu.roll(kpe, MLA_ROPE, 1) * sin
    qp_ref[...] = pe.astype(bf16)
    kp_ref[...] = kpe.astype(bf16)
    ckv = _rmsnorm(kva_ref[...], kvnw_ref[...])
    ckv_ref[...] = ckv
    kv = _dot(ckv.astype(bf16), wkv_ref[...])
    kn_ref[...] = kv[:, :BRANCH_W].astype(bf16)
    vv_ref[...] = kv[:, BRANCH_W:].astype(bf16)


def _mla_proj(p, qnw, kvnw, wqa, wkv, wqb, cos, sin, *, nrows, seq, row0):
    tm = 256
    rope = cos is not None
    rb0 = row0 // tm
    per_seq = seq // tm

    def pspec(off):
        return pl.BlockSpec((tm, _SEG), lambda i: (rb0 + i, off // _SEG))

    def full(a):
        return pl.BlockSpec(a.shape, lambda i: (0,) * a.ndim)

    in_specs = [pspec(OFF_MLA_QA), pspec(OFF_MLA_KVA), pspec(OFF_SMALL), full(qnw), full(kvnw), full(wqa), full(wkv)]
    args = [p, p, p, qnw, kvnw, wqa, wkv]
    if rope:
        tspec = pl.BlockSpec((tm, LANE), lambda i: (i % per_seq, 0))
        in_specs += [full(wqb), tspec, tspec]
        args += [wqb, cos, sin]
    wide = lambda dt: jax.ShapeDtypeStruct((nrows, BRANCH_W), dt)
    ospec = pl.BlockSpec((tm, BRANCH_W), lambda i: (i, 0))
    return pl.pallas_call(
        functools.partial(_mla_proj_kernel, rope=rope),
        out_shape=(wide(bf16), wide(bf16), wide(f32), wide(bf16), wide(bf16),
                   jax.ShapeDtypeStruct((nrows, LANE), bf16)),
        grid=(nrows // tm,),
        in_specs=in_specs,
        out_specs=(ospec, ospec, ospec, ospec, ospec, pl.BlockSpec((tm, LANE), lambda i: (i, 0))),
        compiler_params=_cparams(("parallel",)),
        name="mla_proj",
    )(*args)


def _kv_kernel(ckv_ref, w_ref, kn_ref, vv_ref):
    kv = _dot(ckv_ref[...].astype(bf16), w_ref[...])
    kn_ref[...] = kv[:, :BRANCH_W].astype(bf16)
    vv_ref[...] = kv[:, BRANCH_W:].astype(bf16)


def _kv_proj(ckv, wkv):
    rows = ckv.shape[0]
    tm = 256
    out = jax.ShapeDtypeStruct((rows, BRANCH_W), bf16)
    ospec = pl.BlockSpec((tm, BRANCH_W), lambda i: (i, 0))
    return pl.pallas_call(
        _kv_kernel, out_shape=(out, out), grid=(rows // tm,),
        in_specs=[pl.BlockSpec((tm, ckv.shape[1]), lambda i: (i, 0)),
                  pl.BlockSpec(wkv.shape, lambda i: (0, 0))],
        out_specs=(ospec, ospec),
        compiler_params=_cparams(("parallel",)),
        name="mla_ctx_kv",
    )(ckv, wkv)


def _attn_kernel(*refs, has_ctx):
    if has_ctx:
        qn_ref, qp_ref, kn_ref, kp_ref, vv_ref, knc_ref, kpc_ref, vvc_ref, o_ref = refs
    else:
        qn_ref, qp_ref, kn_ref, kp_ref, vv_ref, o_ref = refs
    scale = (MLA_NOPE + MLA_ROPE) ** -0.5
    heads = range(N_HEADS)

    def head(ref, h):
        return ref[:, h * HEAD_W:(h + 1) * HEAD_W]

    kp = kp_ref[...]
    s1 = [(_dot_nt(head(qn_ref, h), head(kn_ref, h)) + _dot_nt(head(qp_ref, h), kp)) * scale for h in heads]
    mx = [jnp.max(x, axis=-1, keepdims=True) for x in s1]
    if has_ctx:
        kpc = kpc_ref[...]
        s2 = [(_dot_nt(head(qn_ref, h), head(knc_ref, h)) + _dot_nt(head(qp_ref, h), kpc)) * scale for h in heads]
        mx = [jnp.maximum(m, jnp.max(x, axis=-1, keepdims=True)) for m, x in zip(mx, s2)]
    p1 = [jnp.exp(x - m) for x, m in zip(s1, mx)]
    den = [jnp.sum(x, axis=-1, keepdims=True) for x in p1]
    if has_ctx:
        p2 = [jnp.exp(x - m) for x, m in zip(s2, mx)]
        den = [d_ + jnp.sum(x, axis=-1, keepdims=True) for d_, x in zip(den, p2)]
    inv = [1.0 / d_ for d_ in den]
    o = [_dot((p1[h] * inv[h]).astype(bf16), head(vv_ref, h)) for h in heads]
    if has_ctx:
        o = [o[h] + _dot((p2[h] * inv[h]).astype(bf16), head(vvc_ref, h)) for h in heads]
    o_ref[...] = jnp.concatenate(o, axis=1).astype(bf16)


def _attention(qn, qp, kn, kp, vv, ctx, *, nseq, seq):
    tq = 256
    nq = seq // tq
    has_ctx = ctx is not None
    qspec = pl.BlockSpec((tq, BRANCH_W), lambda b, i: (b * nq + i, 0))
    kspec = pl.BlockSpec((seq, BRANCH_W), lambda b, i: (b, 0))
    kpspec = pl.BlockSpec((seq, LANE), lambda b, i: (b, 0))
    in_specs = [qspec, qspec, kspec, kpspec, kspec]
    args = [qn, qp, kn, kp, vv]
    if has_ctx:
        knc, kpc, vvc = ctx
        lc = knc.shape[0] // nseq
        in_specs += [pl.BlockSpec((lc, BRANCH_W), lambda b, i: (b, 0)),
                     pl.BlockSpec((lc, LANE), lambda b, i: (b, 0)),
                     pl.BlockSpec((lc, BRANCH_W), lambda b, i: (b, 0))]
        args += [knc, kpc, vvc]
    return pl.pallas_call(
        functools.partial(_attn_kernel, has_ctx=has_ctx),
        out_shape=jax.ShapeDtypeStruct((nseq * seq, BRANCH_W), bf16),
        grid=(nseq, nq),
        in_specs=in_specs,
        out_specs=pl.BlockSpec((tq, BRANCH_W), lambda b, i: (b * nq + i, 0)),
        compiler_params=_cparams(("parallel", "parallel")),
        name="mla_attn",
    )(*args)


def _merge_kernel(o0_ref, o1_ref, o2_ref, o3_ref, g0_ref, g1_ref, g2_ref, g3_ref, bg_ref, wb_ref, wo_ref,
                  x_ref, mod_ref, lng_ref, lnb_ref, out_ref, *, alpha):
    m = None
    for k, (o_ref, g_ref) in enumerate(((o0_ref, g0_ref), (o1_ref, g1_ref), (o2_ref, g2_ref), (o3_ref, g3_ref))):
        term = jax.nn.sigmoid(g_ref[...] + bg_ref[k:k + 1, :]) * _dot(o_ref[...], wb_ref[k])
        m = term if m is None else m + term
    mix = _dot(m.astype(bf16), wo_ref[...])
    gate1 = mod_ref[2:3, :]
    out_ref[...] = _layernorm(alpha * x_ref[...] + gate1 * mix, lng_ref[...], lnb_ref[...])


def _merge(branches, p, bg, wb, wo, x, mod, lng, lnb, *, alpha, t_ctx, l_lat):
    t, d = x.shape
    tm = 256
    midx = _mod_index(tm, t_ctx, l_lat)
    ospec = pl.BlockSpec((tm, BRANCH_W), lambda i: (i, 0))
    gspecs = [pl.BlockSpec((tm, d), functools.partial(lambda i, k: (i, OFF_GATES // d + k), k=k)) for k in range(4)]
    single = dict(pipeline_mode=pl.Buffered(1))
    return pl.pallas_call(
        functools.partial(_merge_kernel, alpha=alpha),
        out_shape=jax.ShapeDtypeStruct((t, d), f32),
        grid=(t // tm,),
        in_specs=[ospec, ospec, ospec, ospec] + gspecs + [
            pl.BlockSpec((4, d), lambda i: (0, 0)),
            pl.BlockSpec((4, BRANCH_W, d), lambda i: (0, 0, 0), **single),
            pl.BlockSpec((d, d), lambda i: (0, 0), **single),
            pl.BlockSpec((tm, d), lambda i: (i, 0)),
            pl.BlockSpec((None, 6, d), lambda i: (midx(i), 0, 0)),
            pl.BlockSpec((1, d), lambda i: (0, 0)),
            pl.BlockSpec((1, d), lambda i: (0, 0))],
        out_specs=pl.BlockSpec((tm, d), lambda i: (i, 0)),
        compiler_params=_cparams(("parallel",)),
        name="merge_out_ln",
    )(*branches, p, p, p, p, bg, wb, wo, x, mod, lng, lnb)


def _ffn_kernel(x_ref, mod_ref, w1_ref, w3_ref, w2_ref, lng_ref, lnb_ref, out_ref, hb_ref, acc_ref, *, alpha):
    f = pl.program_id(1)

    @pl.when(f == 0)
    def _():
        shift = mod_ref[3:4, :]
        scale = mod_ref[4:5, :]
        hb_ref[...] = (x_ref[...] * (1.0 + scale) + shift).astype(bf16)
        acc_ref[...] = jnp.zeros_like(acc_ref)

    h = hb_ref[...]
    g = (jax.nn.silu(_dot(h, w1_ref[...])) * _dot(h, w3_ref[...])).astype(bf16)
    acc_ref[...] += _dot(g, w2_ref[...])

    @pl.when(f == pl.num_programs(1) - 1)
    def _():
        gate2 = mod_ref[5:6, :]
        out_ref[...] = _layernorm(alpha * x_ref[...] + gate2 * acc_ref[...], lng_ref[...], lnb_ref[...])


def _ffn(x, mod, w1, w3, w2, lng, lnb, *, alpha, t_ctx, l_lat):
    t, d = x.shape
    dff = w1.shape[1]
    tm, tf = 512, 512
    midx = _mod_index(tm, t_ctx, l_lat)
    return pl.pallas_call(
        functools.partial(_ffn_kernel, alpha=alpha),
        out_shape=jax.ShapeDtypeStruct((t, d), f32),
        grid=(t // tm, dff // tf),
        in_specs=[pl.BlockSpec((tm, d), lambda i, f: (i, 0)),
                  pl.BlockSpec((None, 6, d), lambda i, f: (midx(i), 0, 0)),
                  pl.BlockSpec((d, tf), lambda i, f: (0, f)),
                  pl.BlockSpec((d, tf), lambda i, f: (0, f)),
                  pl.BlockSpec((tf, d), lambda i, f: (f, 0)),
                  pl.BlockSpec((1, d), lambda i, f: (0, 0)),
                  pl.BlockSpec((1, d), lambda i, f: (0, 0))],
        out_specs=pl.BlockSpec((tm, d), lambda i, f: (i, 0)),
        scratch_shapes=[pltpu.VMEM((tm, d), bf16), pltpu.VMEM((tm, d), f32)],
        compiler_params=_cparams(("parallel", "arbitrary")),
        name="ffn_ln",
    )(x, mod, w1, w3, w2, lng, lnb)


def _cast_kernel(x_ref, o_ref):
    o_ref[...] = x_ref[...].astype(bf16)


def _cast_layer(w, layer):
    _, rows, cols = w.shape
    tr = 256 if cols > 4096 else 512
    return pl.pallas_call(
        _cast_kernel,
        out_shape=jax.ShapeDtypeStruct((rows, cols), bf16),
        grid=(rows // tr,),
        in_specs=[pl.BlockSpec((None, tr, cols), lambda i: (layer, i, 0))],
        out_specs=pl.BlockSpec((tr, cols), lambda i: (i, 0)),
        compiler_params=_cparams(("parallel",)),
        name="cast_bf16",
    )(w)


def _relayout_w_in(w):
    d = w.shape[0]

    def seg(name, start=0, width=None):
        s = _SRC[name] + start
        return w[:, s:s + width]

    def zeros(n):
        return jnp.zeros((d, n), w.dtype)

    def head_padded(name):
        parts = []
        for h in range(N_HEADS):
            parts += [seg(name, h * GLA_DK, GLA_DK), zeros(HEAD_W - GLA_DK)]
        return parts

    q4 = MLA_ROPE // 4
    kpe_swapped = [seg('mla_kpe', q4, q4), seg('mla_kpe', 0, q4), seg('mla_kpe', 3 * q4, q4), seg('mla_kpe', 2 * q4, q4)]
    small = ([seg('mla_kpe', 0, MLA_ROPE)] + kpe_swapped + [seg('gla_g', 0, 2 * GLA_RANK), seg('gdn_b', 0, 8),
             seg('gdn_a', 0, 8), zeros(_SEG - SM_GDN_A - 8)])
    parts = ([seg('gdn_q', 0, 512), seg('gdn_k', 0, 512), seg('gdn_v', 0, 512), seg('gdn_z', 0, 512)]
             + head_padded('gla_q') + head_padded('gla_k')
             + [seg('gla_v', 0, 512), seg('gla_r', 0, 512), seg('hg_q', 0, 512), seg('hg_f', 0, 512), seg('hg_f', 512, 512),
                seg('hg_i', 0, 512), seg('hg_g', 0, 512), seg('mla_qa', 0, 512), seg('mla_kva', 0, 512)]
             + small + [seg('gates', 0, 4 * D_MODEL)])
    out = jnp.concatenate(parts, axis=1).astype(bf16)
    assert out.shape[1] == N_PROJ
    return out


def _relayout_wq(wq):
    hw = MLA_NOPE + MLA_ROPE
    z = jnp.zeros((wq.shape[0], HEAD_W - MLA_ROPE), wq.dtype)
    q4 = MLA_ROPE // 4
    nope, pe, pes = [], [], []
    for h in range(N_HEADS):
        base = h * hw
        nope.append(wq[:, base:base + MLA_NOPE])
        r = wq[:, base + MLA_NOPE:base + hw]
        pe += [r, z]
        pes += [r[:, q4:2 * q4], r[:, 0:q4], r[:, 3 * q4:], r[:, 2 * q4:3 * q4], z]
    return jnp.concatenate(nope + pe, axis=1).astype(bf16), jnp.concatenate(pes, axis=1).astype(bf16)


def _relayout_wkv(wkv):
    hw = MLA_NOPE + HEAD_W
    kn = [wkv[:, h * hw:h * hw + MLA_NOPE] for h in range(N_HEADS)]
    vv = [wkv[:, h * hw + MLA_NOPE:(h + 1) * hw] for h in range(N_HEADS)]
    return jnp.concatenate(kn + vv, axis=1).astype(bf16)


def _rope_tables(length):
    pos = jnp.arange(length)
    row_id = (pos // GRID_W).astype(f32)
    col_id = (pos % GRID_W).astype(f32)
    half = MLA_ROPE // 2
    inv = ROPE_BASE ** (-jnp.arange(0, half, 2, dtype=f32) / half)
    ar, ac = row_id[:, None] * inv, col_id[:, None] * inv
    z = jnp.zeros((length, LANE - MLA_ROPE), f32)
    cos = jnp.concatenate([jnp.cos(ar), jnp.cos(ar), jnp.cos(ac), jnp.cos(ac), z], axis=1)
    sin = jnp.concatenate([-jnp.sin(ar), jnp.sin(ar), -jnp.sin(ac), jnp.sin(ac), z], axis=1)
    return cos, sin


def _gla_gate_weights(w2, b):
    wbig = jnp.zeros((2, _SEG, BRANCH_W), f32)
    bbig = jnp.zeros((2, 1, BRANCH_W), f32)
    for d in range(2):
        r0 = SM_GLA_G + d * GLA_RANK
        for h in range(N_HEADS):
            wbig = wbig.at[d, r0:r0 + GLA_RANK, h * HEAD_W:h * HEAD_W + GLA_DK].set(w2[d, :, h * GLA_DK:(h + 1) * GLA_DK])
            bbig = bbig.at[d, 0, h * HEAD_W:h * HEAD_W + GLA_DK].set(b[d, h * GLA_DK:(h + 1) * GLA_DK])
    return wbig.astype(bf16), bbig


def kernel(x_prompt, x_sample, c, state_gdn, state_gla, state_hgrn, cache_mla_ckv, cache_mla_kpe, c_ctx, w_ada, b_ada, w_in, gdn_conv, gdn_a_log, gdn_dt_bias, gdn_norm, gla_gate_w2, gla_gate_b, gla_norm, hgrn_lb, hgrn_norm, mla_q_norm, mla_wq_b, mla_kv_norm, mla_wkv_b, w_branch, b_gates, w_out, ln1_g, ln1_b, ln2_g, ln2_b, ffn_w1, ffn_w3, ffn_w2):
    nb_c, l_c, d = x_prompt.shape
    nb_l, l_l, _ = x_sample.shape
    depth = w_in.shape[0]
    t_c, t_l = nb_c * l_c, nb_l * l_l
    past = cache_mla_ckv.shape[2]
    alpha = (2.0 * depth) ** 0.25
    assert d == D_MODEL and t_c % 1024 == 0 and l_l % 1024 == 0 and l_c % CHUNK == 0

    n_cond = 1 + nb_l
    cc = jnp.concatenate([c_ctx[None, :], c, jnp.zeros((-n_cond % 8, d), f32)], axis=0)
    mods = _ada(cc, w_ada, b_ada).reshape(depth, cc.shape[0], 6, d)

    mexp_np, lvl_np = _scan_consts()
    mexp = jnp.asarray(mexp_np, bf16)
    lvl = jnp.asarray(lvl_np, f32)
    cos_t, sin_t = _rope_tables(l_l)
    cum = jnp.cumsum(jax.nn.softmax(hgrn_lb.astype(f32), axis=0), axis=0)
    lower_bounds = cum - cum[:1]

    x = jnp.concatenate([x_prompt.reshape(t_c, d), x_sample.reshape(t_l, d)], axis=0)
    t = t_c + t_l
    zero_state = jnp.zeros((nb_c, N_HEADS, 2, HEAD_W, HEAD_W), f32)
    streams = (dict(nseq=nb_c, seq=l_c, row0=0), dict(nseq=nb_l, seq=l_l, row0=t_c))
    new_gdn, new_gla, new_hg, new_ckv, new_kpe = [], [], [], [], []

    for l in range(depth):
        mod = mods[l]
        p = _inproj(x, mod, _relayout_w_in(w_in[l]), t_c, l_l)

        gsm = p[:, OFF_SMALL + SM_GDN_B:OFF_SMALL + SM_GDN_B + 16].reshape(t // GROUP, GROUP, 4, N_HEADS)
        grow = jnp.pad(gsm.transpose(3, 0, 2, 1), ((0, 0), (0, 0), (0, 4), (0, 0)))
        zc = jnp.zeros((N_HEADS, 2), f32)
        hp = jnp.stack([jnp.concatenate([zc, gdn_a_log[l].T, zc, zc], axis=1),
                        jnp.concatenate([zc, gdn_dt_bias[l].T, zc, zc], axis=1)], axis=-1)

        la_f, la_b = _gla_prep(p, *_gla_gate_weights(gla_gate_w2[l], gla_gate_b[l]))
        hla_f, hla_b, hk_f, hk_b = _hgrn_prep(p, lower_bounds[l])
        wqa, wqb = _relayout_wq(mla_wq_b[l])
        wkv = _relayout_wkv(mla_wkv_b[l])
        qnw, kvnw = mla_q_norm[l][None, :], mla_kv_norm[l][None, :]

        outs = {k: [] for k in ('gdn', 'gla', 'hg', 'mla')}
        for si, st in enumerate(streams):
            nseq, seq, row0 = st['nseq'], st['seq'], st['row0']
            rb = row0 // seq
            if si == 0:
                s_gdn0 = s_gla0 = s_hg0 = zero_state
            else:
                s_gdn0 = state_gdn[:, l].transpose(0, 2, 1, 3, 4)
                s_gla0 = jnp.pad(state_gla[:, l], ((0, 0),) * 3 + ((0, HEAD_W - GLA_DK), (0, 0))).transpose(0, 2, 1, 4, 3)
                s_hg0 = state_hgrn[:, l].transpose(0, 2, 1, 4, 3)
            o_gdn, s_gdn = _gdn(p, gdn_conv[l], grow, hp, gdn_norm[l][None, :], s_gdn0, nseq=nseq, seq=seq, row_blk0=rb)
            o_gla, s_gla = _scan((p, OFF_GLA_Q, rb), (p, OFF_GLA_K, rb), (p, OFF_GLA_K, rb), (p, OFF_GLA_V, rb),
                                 la_f, la_b, (p, OFF_GLA_R, rb), gla_norm[l][None, :], s_gla0, mexp, lvl,
                                 nseq=nseq, seq=seq, row_blk0=rb, q_silu=False, q_scale=GLA_DK ** -0.5, gate_silu=True)
            o_hg, s_hg = _scan((p, OFF_HG_Q, rb), (hk_f, 0, rb), (hk_b, 0, rb), (p, OFF_HG_I, rb),
                               hla_f, hla_b, (p, OFF_HG_G, rb), hgrn_norm[l][None, :], s_hg0, mexp, lvl,
                               nseq=nseq, seq=seq, row_blk0=rb, q_silu=True, q_scale=1.0, gate_silu=False)
            rope = si == 1
            qn, qp, ckv, kn, vv, kp = _mla_proj(p, qnw, kvnw, wqa, wkv, wqb if rope else None,
                                                cos_t if rope else None, sin_t if rope else None,
                                                nrows=nseq * seq, seq=seq, row0=row0)
            ctx = None
            if si == 1:
                knc, vvc = _kv_proj(cache_mla_ckv[:, l].reshape(nb_l * past, -1), wkv)
                kpc = jnp.pad(cache_mla_kpe[:, l].reshape(nb_l * past, MLA_ROPE), ((0, 0), (0, LANE - MLA_ROPE))).astype(bf16)
                ctx = (knc, kpc, vvc)
            o_mla = _attention(qn, qp, kn, kp, vv, ctx, nseq=nseq, seq=seq)
            outs['gdn'].append(o_gdn)
            outs['gla'].append(o_gla)
            outs['hg'].append(o_hg)
            outs['mla'].append(o_mla)
            if si == 0:
                new_gdn.append(s_gdn.transpose(0, 2, 1, 3, 4))
                new_gla.append(s_gla.transpose(0, 2, 1, 4, 3)[:, :, :, :GLA_DK, :])
                new_hg.append(s_hg.transpose(0, 2, 1, 4, 3))
                new_ckv.append(ckv.reshape(nb_c, l_c, -1))
                new_kpe.append(p[:t_c, OFF_SMALL + SM_KPE:OFF_SMALL + SM_KPE + MLA_ROPE].reshape(nb_c, l_c, MLA_ROPE))

        branches = [jnp.concatenate(outs[k], axis=0) for k in ('gdn', 'gla', 'hg', 'mla')]
        wb = _cast_layer(w_branch.reshape(depth, 4 * BRANCH_W, d), l).reshape(4, BRANCH_W, d)
        x1 = _merge(branches, p, b_gates[l], wb, _cast_layer(w_out, l), x, mod,
                    ln1_g[l][None, :], ln1_b[l][None, :], alpha=alpha, t_ctx=t_c, l_lat=l_l)
        x = _ffn(x1, mod, _cast_layer(ffn_w1, l), _cast_layer(ffn_w3, l), _cast_layer(ffn_w2, l),
                 ln2_g[l][None, :], ln2_b[l][None, :], alpha=alpha, t_ctx=t_c, l_lat=l_l)

    sdt = x_prompt.dtype
    return (x[:t_c].reshape(nb_c, l_c, d), x[t_c:].reshape(nb_l, l_l, d),
            jnp.stack(new_gdn, axis=1).astype(sdt), jnp.stack(new_gla, axis=1).astype(sdt),
            jnp.stack(new_hg, axis=1).astype(sdt), jnp.stack(new_ckv, axis=1), jnp.stack(new_kpe, axis=1))
```

```python
import functools

import numpy as np
import jax
import jax.numpy as jnp
from jax import lax
from jax.experimental import pallas as pl
from jax.experimental.pallas import tpu as pltpu

f32 = jnp.float32
bf16 = jnp.bfloat16

D_MODEL = 2048
N_HEADS = 4
HEAD_W = 128
BRANCH_W = N_HEADS * HEAD_W
GLA_DK = 64
GLA_RANK = 16
GLA_TAU = 16.0
GDN_CONV = 5
MLA_NOPE = 128
MLA_ROPE = 64
ROPE_BASE = 10000.0
GRID_W = 64
NORM_EPS = 1e-6
CHUNK = 64
SUB = 16
LANE = 128
VMEM_LIMIT = 56 * 1024 * 1024

_SEG = 512
OFF_GDN_Q, OFF_GDN_K, OFF_GDN_V, OFF_GDN_Z = 0, 512, 1024, 1536
OFF_GLA_Q, OFF_GLA_K, OFF_GLA_V, OFF_GLA_R = 2048, 2560, 3072, 3584
OFF_HG_Q, OFF_HG_FF, OFF_HG_FB, OFF_HG_I, OFF_HG_G = 4096, 4608, 5120, 5632, 6144
OFF_MLA_QA, OFF_MLA_KVA, OFF_SMALL, OFF_GATES = 6656, 7168, 7680, 8192
N_PROJ = OFF_GATES + N_HEADS * D_MODEL
SM_KPE, SM_KPE_SW, SM_GLA_G, SM_GDN_B, SM_GDN_A = 0, 64, 128, 160, 168

_SRC = {}
_o = 0
for _n, _w in (('gdn_q', 512), ('gdn_k', 512), ('gdn_v', 512), ('gdn_z', 512), ('gdn_b', 8), ('gdn_a', 8),
               ('gla_q', 256), ('gla_k', 256), ('gla_v', 512), ('gla_r', 512), ('gla_g', 32),
               ('hg_q', 512), ('hg_f', 1024), ('hg_i', 512), ('hg_g', 512),
               ('mla_qa', 512), ('mla_kva', 512), ('mla_kpe', 64), ('gates', 4 * D_MODEL)):
    _SRC[_n] = _o
    _o += _w
IN_WIDTH = _o


def _cparams(sem):
    return pltpu.CompilerParams(dimension_semantics=sem, vmem_limit_bytes=VMEM_LIMIT)


def _dot(a, b):
    return jnp.dot(a, b, preferred_element_type=f32)


def _dot_nt(a, b):
    return lax.dot_general(a, b, (((1,), (1,)), ((), ())), preferred_element_type=f32)


def _dot_tn(a, b):
    return lax.dot_general(a, b, (((0,), (0,)), ((), ())), preferred_element_type=f32)


def _split2(x):
    hi = x.astype(bf16)
    lo = (x - hi.astype(f32)).astype(bf16)
    return hi, lo


def _mm3(a, b):
    ah, al = _split2(a)
    bh, bl = _split2(b)
    return _dot(ah, bh) + (_dot(ah, bl) + _dot(al, bh))


def _layernorm(y, g, b):
    mu = jnp.mean(y, axis=-1, keepdims=True)
    yc = y - mu
    var = jnp.mean(yc * yc, axis=-1, keepdims=True)
    return yc * lax.rsqrt(var + NORM_EPS) * g + b


def _rmsnorm(y, w):
    return y * lax.rsqrt(jnp.mean(y * y, axis=-1, keepdims=True) + NORM_EPS) * w


def _ada_kernel(c_ref, w_ref, b_ref, o_ref):
    cs = jax.nn.silu(c_ref[...]).astype(bf16)
    o_ref[...] = _dot(cs, w_ref[...].astype(bf16)) + b_ref[...]


def _ada(cc, w_ada, b_ada):
    depth, d, n6 = w_ada.shape
    rows = cc.shape[0]
    tn = 1024
    return pl.pallas_call(
        _ada_kernel,
        out_shape=jax.ShapeDtypeStruct((depth, rows, n6), f32),
        grid=(depth, n6 // tn),
        in_specs=[pl.BlockSpec((rows, d), lambda l, j: (0, 0)),
                  pl.BlockSpec((None, d, tn), lambda l, j: (l, 0, j)),
                  pl.BlockSpec((None, 1, tn), lambda l, j: (l, 0, j))],
        out_specs=pl.BlockSpec((None, rows, tn), lambda l, j: (l, 0, j)),
        compiler_params=_cparams(("parallel", "parallel")),
        name="ada_mod",
    )(cc, w_ada, b_ada.reshape(depth, 1, n6))


def _inproj_kernel(x_ref, mod_ref, w_ref, o_ref, xb_ref):
    @pl.when(pl.program_id(1) == 0)
    def _():
        shift = mod_ref[0:1, :]
        scale = mod_ref[1:2, :]
        xb_ref[...] = (x_ref[...] * (1.0 + scale) + shift).astype(bf16)

    o_ref[...] = _dot(xb_ref[...], w_ref[...])


def _mod_index(tm, t_ctx, l_lat):
    def index(i):
        r = i * tm
        return jnp.where(r < t_ctx, 0, 1 + (r - t_ctx) // l_lat)
    return index


def _inproj(x, mod, w, layer, t_ctx, l_lat):
    t, d = x.shape
    n = w.shape[2]
    tm, tn = 1024, 1024
    midx = _mod_index(tm, t_ctx, l_lat)
    return pl.pallas_call(
        _inproj_kernel,
        out_shape=jax.ShapeDtypeStruct((t, n), f32),
        grid=(t // tm, n // tn),
        in_specs=[pl.BlockSpec((tm, d), lambda i, j: (i, 0)),
                  pl.BlockSpec((None, 6, d), lambda i, j: (midx(i), 0, 0)),
                  pl.BlockSpec((None, d, tn), lambda i, j: (layer, 0, j))],
        out_specs=pl.BlockSpec((tm, tn), lambda i, j: (i, j)),
        scratch_shapes=[pltpu.VMEM((tm, d), bf16)],
        compiler_params=_cparams(("parallel", "arbitrary")),
        name="in_proj",
    )(x, mod, w)


_PACK = 4
GROUP = _PACK * CHUNK
_GDN_HEADS = 2


def _gdn_kernel(q_ref, k_ref, v_ref, z_ref, cwq_ref, cwk_ref, cwv_ref, grow_ref, hp_ref, nw_ref, s0_ref,
                o_ref, so_ref, pad_ref, qs, ks, vs, gs, ub_s, wq_s, aq_s, kd_s, ct_s, of, ob, s_ref, *, seq):
    n = seq // CHUNK
    width = _GDN_HEADS * HEAD_W
    chains = [(hd, d) for hd in range(_GDN_HEADS) for d in range(2)]

    def conv_silu(x_ref, w_ref):
        pad_ref[0:8, :] = jnp.zeros((8, width), f32)
        pad_ref[8 + seq:16 + seq, :] = jnp.zeros((8, width), f32)
        pad_ref[8:8 + seq, :] = x_ref[...]
        acc = pad_ref[pl.ds(6, seq), :] * w_ref[0:1, :]
        for j in range(1, GDN_CONV):
            acc = acc + pad_ref[pl.ds(6 + j, seq), :] * w_ref[j:j + 1, :]
        return jax.nn.silu(acc)

    def l2norm_heads(x):
        parts = []
        for hd in range(_GDN_HEADS):
            xh = x[:, hd * HEAD_W:(hd + 1) * HEAD_W]
            parts.append(xh * lax.rsqrt(jnp.sum(xh * xh, axis=-1, keepdims=True) + NORM_EPS))
        return jnp.concatenate(parts, axis=1)

    qs[...] = l2norm_heads(conv_silu(q_ref, cwq_ref)) * (HEAD_W ** -0.5)
    ks[...] = l2norm_heads(conv_silu(k_ref, cwk_ref))
    vs[...] = conv_silu(v_ref, cwv_ref)

    x = grow_ref[...]
    a_log = hp_ref[:, :, 0:1][:, None]
    dt_b = hp_ref[:, :, 1:2][:, None]
    rows = lax.broadcasted_iota(jnp.int32, x.shape, 2)
    gs[...] = jnp.where(rows < 2, jax.nn.sigmoid(x), -jnp.exp(a_log) * jax.nn.softplus(x + dt_b))
    s_ref[...] = s0_ref[...].reshape(2 * _GDN_HEADS, HEAD_W, HEAD_W)

    ri = lax.broadcasted_iota(jnp.int32, (CHUNK, GROUP), 0)
    li = lax.broadcasted_iota(jnp.int32, (CHUNK, GROUP), 1)
    cj = li % CHUNK
    blk = [(li // CHUNK) == r for r in range(_PACK)]
    blk_bf = [jnp.where(b, 1.0, 0.0).astype(bf16) for b in blk]
    l1 = lax.broadcasted_iota(jnp.int32, (1, GROUP), 1) // CHUNK
    eye_b = ri == cj
    eye = jnp.where(eye_b, 1.0, 0.0).astype(f32)
    bd16 = (ri // SUB) == (cj // SUB)
    bd32 = (ri // (2 * SUB)) == (cj // (2 * SUB))
    r2 = lax.broadcasted_iota(jnp.int32, (GROUP, GROUP), 0)
    c2 = lax.broadcasted_iota(jnp.int32, (GROUP, GROUP), 1)
    same_blk = (r2 // CHUNK) == (c2 // CHUNK)

    def bdiag(yb):
        return jnp.concatenate([yb * mk for mk in blk_bf], axis=0)

    def unpack_diag(xf):
        out = xf[0:CHUNK]
        for r in range(1, _PACK):
            out = jnp.where(blk[r], xf[r * CHUNK:(r + 1) * CHUNK], out)
        return out

    def pmm3(pairs):
        sp = [(_split2(a), _split2(b)) for a, b in pairs]
        r1 = [_dot(jnp.concatenate([ah, al], axis=0), bdiag(bh)) for (ah, al), (bh, _) in sp]
        r2 = [_dot(ah, bdiag(bl)) for (ah, _), (_, bl) in sp]
        return [x1[:CHUNK] + x1[CHUNK:] + x2 for x1, x2 in zip(r1, r2)]

    def unit_tri_inverse(ms):
        k = len(ms)
        m32 = [jnp.where(bd32, m, 0.0) for m in ms]
        dg = [jnp.where(bd16, m, 0.0) for m in ms]
        d2 = pmm3([(a, a) for a in dg])
        both = pmm3([(a, a) for a in d2] + [(eye - a, eye + b) for a, b in zip(dg, d2)])
        d4, xi = both[:k], both[k:]
        both = pmm3([(a, a) for a in d4] + [(a, eye + b) for a, b in zip(xi, d4)])
        d8, xi = both[:k], both[k:]
        xi = pmm3([(a, eye + b) for a, b in zip(xi, d8)])
        for lo, hi in ((dg, m32), (m32, ms)):
            t = pmm3([(a, h_ - l_) for a, h_, l_ in zip(xi, hi, lo)])
            t = pmm3(list(zip(t, xi)))
            xi = [a - b for a, b in zip(xi, t)]
        return xi

    def block_cols(xp):
        return [jnp.sum(jnp.where(blk[r], xp, 0.0), axis=1, keepdims=True) for r in range(_PACK)]

    def spread(cols):
        out = jnp.broadcast_to(cols[0], (CHUNK, GROUP))
        for r in range(1, _PACK):
            out = jnp.where(blk[r], cols[r], out)
        return out

    def prepare_group(g, carry):
        rows_g = pl.ds(pl.multiple_of(g * GROUP, GROUP), GROUP)
        heads = range(_GDN_HEADS)
        k4 = [ks[rows_g, hd * HEAD_W:(hd + 1) * HEAD_W] for hd in heads]
        q4 = [qs[rows_g, hd * HEAD_W:(hd + 1) * HEAD_W] for hd in heads]
        v4 = [vs[rows_g, hd * HEAD_W:(hd + 1) * HEAD_W] for hd in heads]
        k4b = [x.astype(bf16) for x in k4]
        kq = [_dot_nt(jnp.concatenate([k4b[hd], q4[hd].astype(bf16)], axis=0), k4b[hd]) for hd in heads]
        kk_p = [unpack_diag(x[:GROUP]) for x in kq]
        qk_p = [unpack_diag(x[GROUP:]) for x in kq]
        g8 = [gs[hd, g] for hd in heads]
        ids = range(len(chains))
        beta_r = [g8[hd][d:d + 1, :] for hd, d in chains]
        g_r = [g8[hd][2 + d:3 + d, :] for hd, d in chains]
        tri2 = [cj <= ri, cj >= ri]
        tri = [tri2[d] for _, d in chains]
        strict = [jnp.logical_and(t, jnp.logical_not(eye_b)) for t in tri]
        tg = [jnp.where(tri[i], g_r[i], 0.0) for i in ids]
        gam_cols = [block_cols(tg[i]) for i in ids]
        beta_cols = [block_cols(jnp.where(eye_b, beta_r[i], 0.0)) for i in ids]
        tot_cols = [[jnp.sum(jnp.where(l1 == r, g_r[i], 0.0), axis=1, keepdims=True) for r in range(_PACK)]
                    for i in ids]
        strict2 = [(c2 % CHUNK) < (r2 % CHUNK), (c2 % CHUNK) > (r2 % CHUNK)]
        sm = [jnp.where(jnp.logical_and(same_blk, s_), 1.0, 0.0).astype(bf16) for s_ in strict2]
        r3 = []
        for i, (_, d) in enumerate(chains):
            th, tl = _split2(tg[i])
            tl2 = (tg[i] - th.astype(f32) - tl.astype(f32)).astype(bf16)
            r3.append(_dot(jnp.concatenate([th, tl, tl2], axis=0), sm[d]))
        dlt = [x[:CHUNK] + x[CHUNK:2 * CHUNK] + x[2 * CHUNK:] for x in r3]
        decay = [jnp.where(tri[i], jnp.exp(jnp.minimum(dlt[i], 0.0)), 0.0) for i in ids]
        t_inv = unit_tri_inverse([jnp.where(strict[i], kk_p[hd] * spread(beta_cols[i]) * decay[i], 0.0)
                                  for i, (hd, _) in enumerate(chains)])
        gam = [jnp.concatenate(gam_cols[i], axis=0) for i in ids]
        beta = [jnp.concatenate(beta_cols[i], axis=0) for i in ids]
        tot = [jnp.concatenate([jnp.broadcast_to(t, (CHUNK, 1)) for t in tot_cols[i]], axis=0) for i in ids]
        eg = [jnp.exp(x) for x in gam]
        rhs = [jnp.concatenate([v4[hd] * beta[i], k4[hd] * beta[i] * eg[i]], axis=1)
               for i, (hd, _) in enumerate(chains)]
        isp = [_split2(x) for x in t_inv]
        rsp = [_split2(x) for x in rhs]
        bih = [bdiag(isp[i][0]) for i in ids]
        s1 = [_dot(jnp.concatenate([bih[i], bdiag(isp[i][1])], axis=0), rsp[i][0]) for i in ids]
        s2 = [_dot(bih[i], rsp[i][1]) for i in ids]
        for i, (hd, d) in enumerate(chains):
            sol = s1[i][:GROUP] + s1[i][GROUP:] + s2[i]
            ub_s[i, rows_g, :] = sol[:, :HEAD_W]
            w_b = sol[:, HEAD_W:].astype(bf16)
            qd_b = (q4[hd] * eg[i]).astype(bf16)
            a_p = qk_p[hd] * decay[i]
            for r_ in range(_PACK):
                c = g * _PACK + r_
                wq_s[i, c, 0:CHUNK, :] = w_b[r_ * CHUNK:(r_ + 1) * CHUNK]
                wq_s[i, c, CHUNK:2 * CHUNK, :] = qd_b[r_ * CHUNK:(r_ + 1) * CHUNK]
                aq_s[i, c] = a_p[:, r_ * CHUNK:(r_ + 1) * CHUNK].astype(bf16)
                ct_s[i, c] = jnp.broadcast_to(jnp.exp(tot_cols[i][r_]), (8, HEAD_W))
            kd_s[i, rows_g, :] = (k4[hd] * jnp.exp(tot[i] - gam[i])).astype(bf16)
        return carry

    lax.fori_loop(0, n // _PACK, prepare_group, 0)

    def step(t, carry):
        ids = range(len(chains))
        cs = [t if d == 0 else n - 1 - t for _, d in chains]
        rows = [pl.ds(pl.multiple_of(c * CHUNK, CHUNK), CHUNK) for c in cs]
        s = [s_ref[i] for i in ids]
        sb = [x.astype(bf16) for x in s]
        r = [_dot(wq_s[i, cs[i]], sb[i]) for i in ids]
        u = [(ub_s[i, rows[i], :] - r[i][:CHUNK]).astype(bf16) for i in ids]
        o_c = [r[i][CHUNK:] + _dot(aq_s[i, cs[i]], u[i]) for i in ids]
        s_new = [ct_s[i, cs[i]][0:1, :] * s[i] + _dot_tn(kd_s[i, rows[i], :], u[i]) for i in ids]
        for i, (hd, d) in enumerate(chains):
            (of if d == 0 else ob)[rows[i], hd * HEAD_W:(hd + 1) * HEAD_W] = o_c[i]
            s_ref[i] = s_new[i]
        return carry

    lax.fori_loop(0, n, step, 0)
    o = of[...] + ob[...]
    z = z_ref[...]
    nw = nw_ref[...]
    o_ref[...] = jnp.concatenate(
        [_rmsnorm(o[:, hd * HEAD_W:(hd + 1) * HEAD_W], nw) * jax.nn.silu(z[:, hd * HEAD_W:(hd + 1) * HEAD_W])
         for hd in range(_GDN_HEADS)], axis=1).astype(bf16)
    so_ref[...] = s_ref[...].reshape(_GDN_HEADS, 2, HEAD_W, HEAD_W)


def _gdn(p, conv_w, grow, hp, norm_w, s0, *, nseq, seq, row_blk0):
    n = seq // CHUNK
    ng = seq // GROUP
    hh = _GDN_HEADS
    width = hh * HEAD_W
    nch = 2 * hh

    def pcol(off):
        return pl.BlockSpec((seq, width), lambda b, h: (row_blk0 + b, off // width + h))

    def wcol(off):
        return pl.BlockSpec((GDN_CONV, width), lambda b, h: (0, off // width + h))

    kern = functools.partial(_gdn_kernel, seq=seq)
    return pl.pallas_call(
        kern,
        out_shape=(jax.ShapeDtypeStruct((nseq * seq, BRANCH_W), bf16),
                   jax.ShapeDtypeStruct((nseq, N_HEADS, 2, HEAD_W, HEAD_W), f32)),
        grid=(nseq, N_HEADS // hh),
        in_specs=[pcol(OFF_GDN_Q), pcol(OFF_GDN_K), pcol(OFF_GDN_V), pcol(OFF_GDN_Z),
                  wcol(0), wcol(BRANCH_W), wcol(2 * BRANCH_W),
                  pl.BlockSpec((hh, ng, 8, GROUP), lambda b, h: (h, row_blk0 + b, 0, 0)),
                  pl.BlockSpec((hh, 8, 2), lambda b, h: (h, 0, 0)),
                  pl.BlockSpec((1, HEAD_W), lambda b, h: (0, 0)),
                  pl.BlockSpec((None, hh, 2, HEAD_W, HEAD_W), lambda b, h: (b, h, 0, 0, 0))],
        out_specs=(pl.BlockSpec((seq, width), lambda b, h: (b, h)),
                   pl.BlockSpec((None, hh, 2, HEAD_W, HEAD_W), lambda b, h: (b, h, 0, 0, 0))),
        scratch_shapes=[pltpu.VMEM((seq + 16, width), f32),
                        pltpu.VMEM((seq, width), f32), pltpu.VMEM((seq, width), f32),
                        pltpu.VMEM((seq, width), f32), pltpu.VMEM((hh, ng, 8, GROUP), f32),
                        pltpu.VMEM((nch, seq, HEAD_W), f32), pltpu.VMEM((nch, n, 2 * CHUNK, HEAD_W), bf16),
                        pltpu.VMEM((nch, n, CHUNK, CHUNK), bf16),
                        pltpu.VMEM((nch, seq, HEAD_W), bf16), pltpu.VMEM((nch, n, 8, HEAD_W), f32),
                        pltpu.VMEM((seq, width), f32), pltpu.VMEM((seq, width), f32),
                        pltpu.VMEM((nch, HEAD_W, HEAD_W), f32)],
        compiler_params=_cparams(("parallel", "parallel")),
        name="gdn",
    )(p, p, p, p, conv_w, conv_w, conv_w, grow, hp, norm_w, s0)


_N_LEVELS = 6
_N_MM_LEVELS = 4
_ROW_EQ = _N_MM_LEVELS * CHUNK
_ROW_EK, _ROW_TOT, _ROWS_EXP = _ROW_EQ + CHUNK, _ROW_EQ + 2 * CHUNK, _ROW_EQ + 2 * CHUNK + 8
_SCAN_UNROLL = 4


def _scan_consts():
    c = CHUNK
    mexp = np.zeros((2, _ROWS_EXP, c), np.float32)
    lvl = np.full((2, c, c), -1.0, np.float32)
    for lv in range(_N_LEVELS):
        s = 32 >> lv
        for i in range(c):
            p = (i // (2 * s)) * (2 * s) + s
            right = (i % (2 * s)) >= s
            if lv >= _N_MM_LEVELS:
                pass
            elif right:
                mexp[0, lv * c + i, p:i + 1] = 1.0
                mexp[1, lv * c + i, p:i] = 1.0
            else:
                mexp[0, lv * c + i, i + 1:p] = 1.0
                mexp[1, lv * c + i, i:p] = 1.0
            for j in range(c):
                if (i // (2 * s)) != (j // (2 * s)):
                    continue
                jright = (j % (2 * s)) >= s
                if right and not jright:
                    lvl[0, i, j] = lv
                if (not right) and jright:
                    lvl[1, i, j] = lv
    for i in range(c):
        lvl[:, i, i] = _N_LEVELS
        mexp[0, _ROW_EQ + i, :i + 1] = 1.0
        mexp[0, _ROW_EK + i, i + 1:] = 1.0
        mexp[1, _ROW_EQ + i, i:] = 1.0
        mexp[1, _ROW_EK + i, :i] = 1.0
    mexp[:, _ROW_TOT:, :] = 1.0
    return mexp, lvl


def _scan_kernel(q_ref, kf_ref, kb_ref, v_ref, laf_ref, lab_ref, gate_ref, nw_ref, s0_ref, mexp_ref, lvl_ref,
                 o_ref, so_ref, of, ob, qd_s, g_s, ct_s, stp_s, st_ref, *, seq, q_silu, q_scale, gate_silu):
    n = seq // CHUNK
    st_ref[...] = s0_ref[...]
    outs = (of, ob)

    def rows_of(c):
        return pl.ds(pl.multiple_of(c * CHUNK, CHUNK), CHUNK)

    def prepare_group(g, carry):
        todo = [(d, g * _SCAN_UNROLL + u) for u in range(_SCAN_UNROLL) for d in range(2)]
        items = range(len(todo))
        q, k, v, la, e2 = [], [], [], [], []
        for d, c in todo:
            rows_c = rows_of(c)
            qi = q_ref[rows_c, :]
            q.append(jax.nn.silu(qi) if q_silu else qi * q_scale)
            k.append((kf_ref if d == 0 else kb_ref)[rows_c, :])
            v.append(v_ref[rows_c, :].astype(bf16))
            la.append((laf_ref if d == 0 else lab_ref)[rows_c, :])
            hi, lo = _split2(la[-1])
            e2.append(_dot(mexp_ref[d], jnp.concatenate([hi, lo], axis=1)))
        e = [x[:, :HEAD_W] + x[:, HEAD_W:] for x in e2]
        r4 = lax.broadcasted_iota(jnp.int32, (CHUNK, HEAD_W), 0) % 4
        fine = []
        for i, (d, _) in enumerate(todo):
            prev = pltpu.roll(la[i], 1, 0)
            nxt = pltpu.roll(la[i], CHUNK - 1, 0)
            if d == 0:
                w2 = jnp.where(r4 == 0, nxt, jnp.where(r4 == 1, 0.0, jnp.where(r4 == 2, la[i], la[i] + prev)))
                w1 = jnp.where(r4 % 2 == 1, la[i], 0.0)
            else:
                w2 = jnp.where(r4 == 0, la[i] + nxt, jnp.where(r4 == 1, la[i], jnp.where(r4 == 2, 0.0, prev)))
                w1 = jnp.where(r4 % 2 == 0, la[i], 0.0)
            fine.append((w2, w1))
        lvl = [lvl_ref[d] for d, _ in todo]
        qk = [_dot_nt(q[i].astype(bf16), k[i].astype(bf16)) for i in items]
        a = [jnp.where(lvl[i] == float(_N_LEVELS), qk[i], 0.0) for i in items]
        for lv in range(_N_LEVELS):
            if lv < _N_MM_LEVELS:
                w = [jnp.exp(e[i][lv * CHUNK:(lv + 1) * CHUNK]) for i in items]
            else:
                w = [jnp.exp(fine[i][lv - _N_MM_LEVELS]) for i in items]
            p = [_dot_nt((q[i] * w[i]).astype(bf16), (k[i] * w[i]).astype(bf16)) for i in items]
            a = [jnp.where(lvl[i] == float(lv), p[i], a[i]) for i in items]
        o_intra = [_dot(a[i].astype(bf16), v[i]) for i in items]
        incr = [_dot_tn(v[i], (k[i] * jnp.exp(e[i][_ROW_EK:_ROW_EK + CHUNK])).astype(bf16)) for i in items]
        for i, (d, c) in enumerate(todo):
            rows_c = rows_of(c)
            outs[d][rows_c, :] = o_intra[i]
            qd_s[d, rows_c, :] = (q[i] * jnp.exp(e[i][_ROW_EQ:_ROW_EQ + CHUNK])).astype(bf16)
            g_s[d, c] = incr[i]
            ct_s[d, c] = jnp.exp(e[i][_ROW_TOT:_ROW_TOT + 8])
        return carry

    lax.fori_loop(0, n // _SCAN_UNROLL, prepare_group, 0)

    def scan_step(i, carry):
        for d, c in ((0, i), (1, n - 1 - i)):
            st = st_ref[d]
            stp_s[d, c] = st.astype(bf16)
            st_ref[d] = st * ct_s[d, c][0:1, :] + g_s[d, c]
        return carry

    lax.fori_loop(0, n, scan_step, 0)

    def inter_group(g, carry):
        todo = [(d, g * _SCAN_UNROLL + u) for u in range(_SCAN_UNROLL) for d in range(2)]
        res = [_dot_nt(qd_s[d, rows_of(c), :], stp_s[d, c]) + outs[d][rows_of(c), :] for d, c in todo]
        for (d, c), o_c in zip(todo, res):
            outs[d][rows_of(c), :] = o_c
        return carry

    lax.fori_loop(0, n // _SCAN_UNROLL, inter_group, 0)
    o = of[...] + ob[...]
    g = gate_ref[...]
    g = jax.nn.silu(g) if gate_silu else jax.nn.sigmoid(g)
    o_ref[...] = (_rmsnorm(o, nw_ref[...]) * g).astype(bf16)
    so_ref[...] = st_ref[...]


def _scan(q_src, kf_src, kb_src, v_src, laf, lab, gate_src, norm_w, s0t, mexp, lvl, *,
          nseq, seq, row_blk0, q_silu, q_scale, gate_silu):
    def spec(src):
        off, rb = src[1], src[2]
        return pl.BlockSpec((seq, HEAD_W), lambda b, h: (rb + b, off // HEAD_W + h))

    srcs = (q_src, kf_src, kb_src, v_src, (laf, 0, row_blk0), (lab, 0, row_blk0), gate_src)
    kern = functools.partial(_scan_kernel, seq=seq, q_silu=q_silu, q_scale=q_scale, gate_silu=gate_silu)
    return pl.pallas_call(
        kern,
        out_shape=(jax.ShapeDtypeStruct((nseq * seq, BRANCH_W), bf16),
                   jax.ShapeDtypeStruct((nseq, N_HEADS, 2, HEAD_W, HEAD_W), f32)),
        grid=(nseq, N_HEADS),
        in_specs=[spec(s) for s in srcs] + [
            pl.BlockSpec((1, HEAD_W), lambda b, h: (0, 0)),
            pl.BlockSpec((None, None, 2, HEAD_W, HEAD_W), lambda b, h: (b, h, 0, 0, 0)),
            pl.BlockSpec((2, _ROWS_EXP, CHUNK), lambda b, h: (0, 0, 0)),
            pl.BlockSpec((2, CHUNK, CHUNK), lambda b, h: (0, 0, 0))],
        out_specs=(pl.BlockSpec((seq, HEAD_W), lambda b, h: (b, h)),
                   pl.BlockSpec((None, None, 2, HEAD_W, HEAD_W), lambda b, h: (b, h, 0, 0, 0))),
        scratch_shapes=[pltpu.VMEM((seq, HEAD_W), f32), pltpu.VMEM((seq, HEAD_W), f32),
                        pltpu.VMEM((2, seq, HEAD_W), bf16),
                        pltpu.VMEM((2, seq // CHUNK, HEAD_W, HEAD_W), f32),
                        pltpu.VMEM((2, seq // CHUNK, 8, HEAD_W), f32),
                        pltpu.VMEM((2, seq // CHUNK, HEAD_W, HEAD_W), bf16),
                        pltpu.VMEM((2, HEAD_W, HEAD_W), f32)],
        compiler_params=_cparams(("parallel", "parallel")),
        name="decay_scan",
    )(*[s[0] for s in srcs], norm_w, s0t, mexp, lvl)


_LOG_DECAY_FLOOR = -1.0e4


def _gla_prep_kernel(sm_ref, w2_ref, b_ref, laf_ref, lab_ref):
    x = sm_ref[...].astype(bf16)
    for d, out in ((0, laf_ref), (1, lab_ref)):
        logits = _dot(x, w2_ref[d]) + b_ref[d]
        out[...] = jnp.maximum(jax.nn.log_sigmoid(logits) * (1.0 / GLA_TAU), _LOG_DECAY_FLOOR)


def _gla_prep(p, w2big, bbig):
    t = p.shape[0]
    tm = 512
    out = jax.ShapeDtypeStruct((t, BRANCH_W), f32)
    return pl.pallas_call(
        _gla_prep_kernel,
        out_shape=(out, out),
        grid=(t // tm,),
        in_specs=[pl.BlockSpec((tm, _SEG), lambda i: (i, OFF_SMALL // _SEG)),
                  pl.BlockSpec((2, _SEG, BRANCH_W), lambda i: (0, 0, 0)),
                  pl.BlockSpec((2, 1, BRANCH_W), lambda i: (0, 0, 0))],
        out_specs=(pl.BlockSpec((tm, BRANCH_W), lambda i: (i, 0)),
                   pl.BlockSpec((tm, BRANCH_W), lambda i: (i, 0))),
        compiler_params=_cparams(("parallel",)),
        name="gla_prep",
    )(p, w2big, bbig)


def _hgrn_prep_kernel(zf_ref, zb_ref, lb_ref, laf_ref, lab_ref, kf_ref, kb_ref):
    for d, z_ref, la_out, k_out in ((0, zf_ref, laf_ref, kf_ref), (1, zb_ref, lab_ref, kb_ref)):
        z = z_ref[...]
        lb = lb_ref[d:d + 1, :]
        a = jnp.log(lb)
        b = jnp.log1p(-lb) + jax.nn.log_sigmoid(z)
        mx = jnp.maximum(a, b)
        lse = mx + jnp.log(jnp.exp(a - mx) + jnp.exp(b - mx))
        lse = jnp.where(mx == -jnp.inf, -jnp.inf, lse)
        la_out[...] = jnp.maximum(lse, _LOG_DECAY_FLOOR)
        k_out[...] = (1.0 - lb) * jax.nn.sigmoid(-z)


def _hgrn_prep(p, lb):
    t = p.shape[0]
    tm = 512
    out = jax.ShapeDtypeStruct((t, BRANCH_W), f32)
    ospec = pl.BlockSpec((tm, BRANCH_W), lambda i: (i, 0))
    return pl.pallas_call(
        _hgrn_prep_kernel,
        out_shape=(out, out, out, out),
        grid=(t // tm,),
        in_specs=[pl.BlockSpec((tm, _SEG), lambda i: (i, OFF_HG_FF // _SEG)),
                  pl.BlockSpec((tm, _SEG), lambda i: (i, OFF_HG_FB // _SEG)),
                  pl.BlockSpec((2, BRANCH_W), lambda i: (0, 0))],
        out_specs=(ospec, ospec, ospec, ospec),
        compiler_params=_cparams(("parallel",)),
        name="hgrn_prep",
    )(p, p, lb)


def _mla_proj_kernel(*refs, rope):
    if rope:
        (qa_ref, kva_ref, sm_ref, qnw_ref, kvnw_ref, wqa_ref, wkv_ref, wqb_ref, cos_ref, sin_ref,
         qn_ref, qp_ref, ckv_ref, kn_ref, vv_ref, kp_ref) = refs
    else:
        (qa_ref, kva_ref, sm_ref, qnw_ref, kvnw_ref, wqa_ref, wkv_ref,
         qn_ref, qp_ref, ckv_ref, kn_ref, vv_ref, kp_ref) = refs
    qh = _rmsnorm(qa_ref[...], qnw_ref[...]).astype(bf16)
    qa = _dot(qh, wqa_ref[...])
    qn_ref[...] = qa[:, :BRANCH_W].astype(bf16)
    pe = qa[:, BRANCH_W:]
    kpe = sm_ref[:, 0:LANE]
    if rope:
        cos = cos_ref[...]
        sin = sin_ref[...]
        cos4 = jnp.concatenate([cos] * N_HEADS, axis=1)
        sin4 = jnp.concatenate([sin] * N_HEADS, axis=1)
        pe = pe * cos4 + _dot(qh, wqb_ref[...]) * sin4
        kpe = kpe * cos + pltpu.roll(kpe, MLA_ROPE, 1) * sin
    qp_ref[...] = pe.astype(bf16)
    kp_ref[...] = kpe.astype(bf16)
    ckv = _rmsnorm(kva_ref[...], kvnw_ref[...])
    ckv_ref[...] = ckv
    kv = _dot(ckv.astype(bf16), wkv_ref[...])
    kn_ref[...] = kv[:, :BRANCH_W].astype(bf16)
    vv_ref[...] = kv[:, BRANCH_W:].astype(bf16)


def _mla_proj(p, qnw, kvnw, wqa, wkv, wqb, cos, sin, *, nrows, seq, row0):
    tm = 256
    rope = cos is not None
    rb0 = row0 // tm
    per_seq = seq // tm

    def pspec(off):
        return pl.BlockSpec((tm, _SEG), lambda i: (rb0 + i, off // _SEG))

    def full(a):
        return pl.BlockSpec(a.shape, lambda i: (0,) * a.ndim)

    in_specs = [pspec(OFF_MLA_QA), pspec(OFF_MLA_KVA), pspec(OFF_SMALL), full(qnw), full(kvnw), full(wqa), full(wkv)]
    args = [p, p, p, qnw, kvnw, wqa, wkv]
    if rope:
        tspec = pl.BlockSpec((tm, LANE), lambda i: (i % per_seq, 0))
        in_specs += [full(wqb), tspec, tspec]
        args += [wqb, cos, sin]
    wide = lambda dt: jax.ShapeDtypeStruct((nrows, BRANCH_W), dt)
    ospec = pl.BlockSpec((tm, BRANCH_W), lambda i: (i, 0))
    return pl.pallas_call(
        functools.partial(_mla_proj_kernel, rope=rope),
        out_shape=(wide(bf16), wide(bf16), wide(f32), wide(bf16), wide(bf16),
                   jax.ShapeDtypeStruct((nrows, LANE), bf16)),
        grid=(nrows // tm,),
        in_specs=in_specs,
        out_specs=(ospec, ospec, ospec, ospec, ospec, pl.BlockSpec((tm, LANE), lambda i: (i, 0))),
        compiler_params=_cparams(("parallel",)),
        name="mla_proj",
    )(*args)


def _kv_kernel(ckv_ref, w_ref, kn_ref, vv_ref):
    kv = _dot(ckv_ref[...].astype(bf16), w_ref[...])
    kn_ref[...] = kv[:, :BRANCH_W].astype(bf16)
    vv_ref[...] = kv[:, BRANCH_W:].astype(bf16)


def _kv_proj(ckv, wkv):
    rows = ckv.shape[0]
    tm = 256
    out = jax.ShapeDtypeStruct((rows, BRANCH_W), bf16)
    ospec = pl.BlockSpec((tm, BRANCH_W), lambda i: (i, 0))
    return pl.pallas_call(
        _kv_kernel, out_shape=(out, out), grid=(rows // tm,),
        in_specs=[pl.BlockSpec((tm, ckv.shape[1]), lambda i: (i, 0)),
                  pl.BlockSpec(wkv.shape, lambda i: (0, 0))],
        out_specs=(ospec, ospec),
        compiler_params=_cparams(("parallel",)),
        name="mla_ctx_kv",
    )(ckv, wkv)


def _attn_kernel(*refs, has_ctx):
    if has_ctx:
        qn_ref, qp_ref, kn_ref, kp_ref, vv_ref, knc_ref, kpc_ref, vvc_ref, o_ref = refs
    else:
        qn_ref, qp_ref, kn_ref, kp_ref, vv_ref, o_ref = refs
    scale = (MLA_NOPE + MLA_ROPE) ** -0.5
    heads = range(N_HEADS)

    def head(ref, h):
        return ref[:, h * HEAD_W:(h + 1) * HEAD_W]

    kp = kp_ref[...]
    s1 = [(_dot_nt(head(qn_ref, h), head(kn_ref, h)) + _dot_nt(head(qp_ref, h), kp)) * scale for h in heads]
    mx = [jnp.max(x, axis=-1, keepdims=True) for x in s1]
    if has_ctx:
        kpc = kpc_ref[...]
        s2 = [(_dot_nt(head(qn_ref, h), head(knc_ref, h)) + _dot_nt(head(qp_ref, h), kpc)) * scale for h in heads]
        mx = [jnp.maximum(m, jnp.max(x, axis=-1, keepdims=True)) for m, x in zip(mx, s2)]
    p1 = [jnp.exp(x - m) for x, m in zip(s1, mx)]
    den = [jnp.sum(x, axis=-1, keepdims=True) for x in p1]
    if has_ctx:
        p2 = [jnp.exp(x - m) for x, m in zip(s2, mx)]
        den = [d_ + jnp.sum(x, axis=-1, keepdims=True) for d_, x in zip(den, p2)]
    inv = [1.0 / d_ for d_ in den]
    o = [_dot((p1[h] * inv[h]).astype(bf16), head(vv_ref, h)) for h in heads]
    if has_ctx:
        o = [o[h] + _dot((p2[h] * inv[h]).astype(bf16), head(vvc_ref, h)) for h in heads]
    o_ref[...] = jnp.concatenate(o, axis=1).astype(bf16)


def _attention(qn, qp, kn, kp, vv, ctx, *, nseq, seq):
    tq = 256
    nq = seq // tq
    has_ctx = ctx is not None
    qspec = pl.BlockSpec((tq, BRANCH_W), lambda b, i: (b * nq + i, 0))
    kspec = pl.BlockSpec((seq, BRANCH_W), lambda b, i: (b, 0))
    kpspec = pl.BlockSpec((seq, LANE), lambda b, i: (b, 0))
    in_specs = [qspec, qspec, kspec, kpspec, kspec]
    args = [qn, qp, kn, kp, vv]
    if has_ctx:
        knc, kpc, vvc = ctx
        lc = knc.shape[0] // nseq
        in_specs += [pl.BlockSpec((lc, BRANCH_W), lambda b, i: (b, 0)),
                     pl.BlockSpec((lc, LANE), lambda b, i: (b, 0)),
                     pl.BlockSpec((lc, BRANCH_W), lambda b, i: (b, 0))]
        args += [knc, kpc, vvc]
    return pl.pallas_call(
        functools.partial(_attn_kernel, has_ctx=has_ctx),
        out_shape=jax.ShapeDtypeStruct((nseq * seq, BRANCH_W), bf16),
        grid=(nseq, nq),
        in_specs=in_specs,
        out_specs=pl.BlockSpec((tq, BRANCH_W), lambda b, i: (b * nq + i, 0)),
        compiler_params=_cparams(("parallel", "parallel")),
        name="mla_attn",
    )(*args)


def _merge_kernel(o0_ref, o1_ref, o2_ref, o3_ref, g0_ref, g1_ref, g2_ref, g3_ref, bg_ref, wb_ref, wo_ref,
                  x_ref, mod_ref, lng_ref, lnb_ref, out_ref, *, alpha):
    m = None
    for k, (o_ref, g_ref) in enumerate(((o0_ref, g0_ref), (o1_ref, g1_ref), (o2_ref, g2_ref), (o3_ref, g3_ref))):
        term = jax.nn.sigmoid(g_ref[...] + bg_ref[k:k + 1, :]) * _dot(o_ref[...], wb_ref[k])
        m = term if m is None else m + term
    mix = _dot(m.astype(bf16), wo_ref[...])
    gate1 = mod_ref[2:3, :]
    out_ref[...] = _layernorm(alpha * x_ref[...] + gate1 * mix, lng_ref[...], lnb_ref[...])


def _merge(branches, p, bg, wb, wo, x, mod, lng, lnb, *, alpha, t_ctx, l_lat):
    t, d = x.shape
    tm = 256
    midx = _mod_index(tm, t_ctx, l_lat)
    ospec = pl.BlockSpec((tm, BRANCH_W), lambda i: (i, 0))
    gspecs = [pl.BlockSpec((tm, d), functools.partial(lambda i, k: (i, OFF_GATES // d + k), k=k)) for k in range(4)]
    single = dict(pipeline_mode=pl.Buffered(1))
    return pl.pallas_call(
        functools.partial(_merge_kernel, alpha=alpha),
        out_shape=jax.ShapeDtypeStruct((t, d), f32),
        grid=(t // tm,),
        in_specs=[ospec, ospec, ospec, ospec] + gspecs + [
            pl.BlockSpec((4, d), lambda i: (0, 0)),
            pl.BlockSpec((4, BRANCH_W, d), lambda i: (0, 0, 0), **single),
            pl.BlockSpec((d, d), lambda i: (0, 0), **single),
            pl.BlockSpec((tm, d), lambda i: (i, 0)),
            pl.BlockSpec((None, 6, d), lambda i: (midx(i), 0, 0)),
            pl.BlockSpec((1, d), lambda i: (0, 0)),
            pl.BlockSpec((1, d), lambda i: (0, 0))],
        out_specs=pl.BlockSpec((tm, d), lambda i: (i, 0)),
        compiler_params=_cparams(("parallel",)),
        name="merge_out_ln",
    )(*branches, p, p, p, p, bg, wb, wo, x, mod, lng, lnb)


def _ffn_kernel(x_ref, mod_ref, w1_ref, w3_ref, w2_ref, lng_ref, lnb_ref, out_ref, hb_ref, acc_ref, *, alpha):
    f = pl.program_id(1)

    @pl.when(f == 0)
    def _():
        shift = mod_ref[3:4, :]
        scale = mod_ref[4:5, :]
        hb_ref[...] = (x_ref[...] * (1.0 + scale) + shift).astype(bf16)
        acc_ref[...] = jnp.zeros_like(acc_ref)

    h = hb_ref[...]
    g = (jax.nn.silu(_dot(h, w1_ref[...])) * _dot(h, w3_ref[...])).astype(bf16)
    acc_ref[...] += _dot(g, w2_ref[...])

    @pl.when(f == pl.num_programs(1) - 1)
    def _():
        gate2 = mod_ref[5:6, :]
        out_ref[...] = _layernorm(alpha * x_ref[...] + gate2 * acc_ref[...], lng_ref[...], lnb_ref[...])


def _ffn(x, mod, w1, w3, w2, lng, lnb, *, alpha, t_ctx, l_lat):
    t, d = x.shape
    dff = w1.shape[1]
    tm, tf = 512, 512
    midx = _mod_index(tm, t_ctx, l_lat)
    return pl.pallas_call(
        functools.partial(_ffn_kernel, alpha=alpha),
        out_shape=jax.ShapeDtypeStruct((t, d), f32),
        grid=(t // tm, dff // tf),
        in_specs=[pl.BlockSpec((tm, d), lambda i, f: (i, 0)),
                  pl.BlockSpec((None, 6, d), lambda i, f: (midx(i), 0, 0)),
                  pl.BlockSpec((d, tf), lambda i, f: (0, f)),
                  pl.BlockSpec((d, tf), lambda i, f: (0, f)),
                  pl.BlockSpec((tf, d), lambda i, f: (f, 0)),
                  pl.BlockSpec((1, d), lambda i, f: (0, 0)),
                  pl.BlockSpec((1, d), lambda i, f: (0, 0))],
        out_specs=pl.BlockSpec((tm, d), lambda i, f: (i, 0)),
        scratch_shapes=[pltpu.VMEM((tm, d), bf16), pltpu.VMEM((tm, d), f32)],
        compiler_params=_cparams(("parallel", "arbitrary")),
        name="ffn_ln",
    )(x, mod, w1, w3, w2, lng, lnb)


def _cast_kernel(x_ref, o_ref):
    o_ref[...] = x_ref[...].astype(bf16)


def _cast_layer(w, layer):
    _, rows, cols = w.shape
    tr = 256 if cols > 4096 else 512
    return pl.pallas_call(
        _cast_kernel,
        out_shape=jax.ShapeDtypeStruct((rows, cols), bf16),
        grid=(rows // tr,),
        in_specs=[pl.BlockSpec((None, tr, cols), lambda i: (layer, i, 0))],
        out_specs=pl.BlockSpec((tr, cols), lambda i: (i, 0)),
        compiler_params=_cparams(("parallel",)),
        name="cast_bf16",
    )(w)


_RELAYOUT_SLOTS = 5
_RELAYOUT_SHIFTS = (0, 16, 48, 112)
_CLS_GLA, _CLS_SMALL = len(_RELAYOUT_SHIFTS), len(_RELAYOUT_SHIFTS) + 1


def _relayout_plan():
    def plain(src):
        return _RELAYOUT_SHIFTS.index(src % LANE), [src // LANE + r for r in range(_RELAYOUT_SLOTS)]

    def three(t0, t1, t2):
        return [t0, t1, t2, t2, t2]

    glq, glk = _SRC['gla_q'], _SRC['gla_k']
    kpe, glg, gdb = _SRC['mla_kpe'], _SRC['gla_g'], _SRC['gdn_b']
    segs = [plain(_SRC[n]) for n in ('gdn_q', 'gdn_k', 'gdn_v', 'gdn_z')]
    segs += [(_CLS_GLA, three(glq // LANE, glq // LANE + 1, glq // LANE + 2)),
             (_CLS_GLA, three(glk // LANE, glk // LANE + 1, glk // LANE + 2))]
    segs += [plain(_SRC['gla_v']), plain(_SRC['gla_r']), plain(_SRC['hg_q']), plain(_SRC['hg_f']),
             plain(_SRC['hg_f'] + 512), plain(_SRC['hg_i']), plain(_SRC['hg_g']), plain(_SRC['mla_qa']),
             plain(_SRC['mla_kva'])]
    segs += [(_CLS_SMALL, three(kpe // LANE, glg // LANE, gdb // LANE))]
    segs += [plain(_SRC['gates'] + _SEG * i) for i in range(N_HEADS * D_MODEL // _SEG)]
    assert len(segs) == N_PROJ // _SEG
    table = np.array([[c for c, _ in segs]] + [[t[r] for _, t in segs] for r in range(_RELAYOUT_SLOTS)], np.int32)

    place_gla = np.zeros((3 * LANE, _SEG), np.float32)
    assert glq % LANE == glk % LANE
    for h in range(N_HEADS):
        for j in range(GLA_DK):
            place_gla[glq % LANE + h * GLA_DK + j, h * HEAD_W + j] = 1.0
    place_small = np.zeros((3 * LANE, _SEG), np.float32)
    q4 = MLA_ROPE // 4
    swapped = [q4 + j for j in range(q4)] + list(range(q4)) + [3 * q4 + j for j in range(q4)] + [2 * q4 + j for j in range(q4)]
    for j in range(MLA_ROPE):
        place_small[kpe % LANE + j, SM_KPE + j] = 1.0
        place_small[kpe % LANE + swapped[j], SM_KPE_SW + j] = 1.0
    for j in range(2 * GLA_RANK):
        place_small[LANE + glg % LANE + j, SM_GLA_G + j] = 1.0
    for j in range(16):
        place_small[2 * LANE + gdb % LANE + j, SM_GDN_B + j] = 1.0
    return table, place_gla, place_small


def _relayout_kernel(tab_ref, s0, s1, s2, s3, s4, pg_ref, ps_ref, o_ref):
    slots = (s0, s1, s2, s3, s4)
    cls = tab_ref[0, pl.program_id(1)]
    rb = 256
    nrb = o_ref.shape[0] // rb

    def run(assemble):
        def body(i, carry):
            rows = pl.ds(pl.multiple_of(i * rb, rb), rb)
            o_ref[rows, :] = assemble(lambda r: slots[r][rows, :]).astype(bf16)
            return carry
        lax.fori_loop(0, nrb, body, 0)

    def plain(shift):
        def assemble(tile):
            if shift == 0:
                return jnp.concatenate([tile(r) for r in range(_SEG // LANE)], axis=1)
            wide = jnp.concatenate([tile(r) for r in range(_RELAYOUT_SLOTS)], axis=1)
            return wide[:, shift:shift + _SEG]
        return assemble

    def placed(p_ref):
        def assemble(tile):
            return _dot(jnp.concatenate([tile(0), tile(1), tile(2)], axis=1).astype(bf16), p_ref[...])
        return assemble

    branches = [plain(s) for s in _RELAYOUT_SHIFTS] + [placed(pg_ref), placed(ps_ref)]
    for cid, assemble in enumerate(branches):
        pl.when(cls == cid)(functools.partial(run, assemble))


def _relayout_w_in(w_in):
    depth, d, _ = w_in.shape
    table, place_gla, place_small = _relayout_plan()
    nseg = N_PROJ // _SEG

    def slot_spec(r):
        return pl.BlockSpec((None, d, LANE), lambda l, j, tab: (l, 0, tab[1 + r, j]))

    pspec = pl.BlockSpec((3 * LANE, _SEG), lambda l, j, tab: (0, 0))
    return pl.pallas_call(
        _relayout_kernel,
        out_shape=jax.ShapeDtypeStruct((depth, d, N_PROJ), bf16),
        grid_spec=pltpu.PrefetchScalarGridSpec(
            num_scalar_prefetch=1,
            grid=(depth, nseg),
            in_specs=[slot_spec(r) for r in range(_RELAYOUT_SLOTS)] + [pspec, pspec],
            out_specs=pl.BlockSpec((None, d, _SEG), lambda l, j, tab: (l, 0, j))),
        compiler_params=_cparams(("parallel", "parallel")),
        name="w_in_relayout",
    )(jnp.asarray(table), w_in, w_in, w_in, w_in, w_in, jnp.asarray(place_gla, bf16), jnp.asarray(place_small, bf16))


def _relayout_wq(wq):
    hw = MLA_NOPE + MLA_ROPE
    z = jnp.zeros((wq.shape[0], HEAD_W - MLA_ROPE), wq.dtype)
    q4 = MLA_ROPE // 4
    nope, pe, pes = [], [], []
    for h in range(N_HEADS):
        base = h * hw
        nope.append(wq[:, base:base + MLA_NOPE])
        r = wq[:, base + MLA_NOPE:base + hw]
        pe += [r, z]
        pes += [r[:, q4:2 * q4], r[:, 0:q4], r[:, 3 * q4:], r[:, 2 * q4:3 * q4], z]
    return jnp.concatenate(nope + pe, axis=1).astype(bf16), jnp.concatenate(pes, axis=1).astype(bf16)


def _relayout_wkv(wkv):
    hw = MLA_NOPE + HEAD_W
    kn = [wkv[:, h * hw:h * hw + MLA_NOPE] for h in range(N_HEADS)]
    vv = [wkv[:, h * hw + MLA_NOPE:(h + 1) * hw] for h in range(N_HEADS)]
    return jnp.concatenate(kn + vv, axis=1).astype(bf16)


def _rope_tables(length):
    pos = jnp.arange(length)
    row_id = (pos // GRID_W).astype(f32)
    col_id = (pos % GRID_W).astype(f32)
    half = MLA_ROPE // 2
    inv = ROPE_BASE ** (-jnp.arange(0, half, 2, dtype=f32) / half)
    ar, ac = row_id[:, None] * inv, col_id[:, None] * inv
    z = jnp.zeros((length, LANE - MLA_ROPE), f32)
    cos = jnp.concatenate([jnp.cos(ar), jnp.cos(ar), jnp.cos(ac), jnp.cos(ac), z], axis=1)
    sin = jnp.concatenate([-jnp.sin(ar), jnp.sin(ar), -jnp.sin(ac), jnp.sin(ac), z], axis=1)
    return cos, sin


def _gla_gate_weights(w2, b):
    lane_pad = ((0, 0), (0, 0), (0, 0), (0, HEAD_W - GLA_DK))
    w4 = jnp.pad(w2.reshape(2, GLA_RANK, N_HEADS, GLA_DK), lane_pad).reshape(2, GLA_RANK, BRANCH_W)
    wbig = jnp.stack([jnp.pad(w4[d], ((SM_GLA_G + d * GLA_RANK, _SEG - SM_GLA_G - (d + 1) * GLA_RANK), (0, 0)))
                      for d in range(2)])
    bbig = jnp.pad(b.reshape(2, 1, N_HEADS, GLA_DK), lane_pad).reshape(2, 1, BRANCH_W)
    return wbig.astype(bf16), bbig


def kernel(x_prompt, x_sample, c, state_gdn, state_gla, state_hgrn, cache_mla_ckv, cache_mla_kpe, c_ctx, w_ada, b_ada, w_in, gdn_conv, gdn_a_log, gdn_dt_bias, gdn_norm, gla_gate_w2, gla_gate_b, gla_norm, hgrn_lb, hgrn_norm, mla_q_norm, mla_wq_b, mla_kv_norm, mla_wkv_b, w_branch, b_gates, w_out, ln1_g, ln1_b, ln2_g, ln2_b, ffn_w1, ffn_w3, ffn_w2):
    nb_c, l_c, d = x_prompt.shape
    nb_l, l_l, _ = x_sample.shape
    depth = w_in.shape[0]
    t_c, t_l = nb_c * l_c, nb_l * l_l
    past = cache_mla_ckv.shape[2]
    alpha = (2.0 * depth) ** 0.25
    assert d == D_MODEL and t_c % 1024 == 0 and l_l % 1024 == 0 and l_c % CHUNK == 0

    n_cond = 1 + nb_l
    cc = jnp.concatenate([c_ctx[None, :], c, jnp.zeros((-n_cond % 8, d), f32)], axis=0)
    mods = _ada(cc, w_ada, b_ada).reshape(depth, cc.shape[0], 6, d)

    w_in_r = _relayout_w_in(w_in)
    mexp_np, lvl_np = _scan_consts()
    mexp = jnp.asarray(mexp_np, bf16)
    lvl = jnp.asarray(lvl_np, f32)
    cos_t, sin_t = _rope_tables(l_l)
    cum = jnp.cumsum(jax.nn.softmax(hgrn_lb.astype(f32), axis=0), axis=0)
    lower_bounds = cum - cum[:1]

    x = jnp.concatenate([x_prompt.reshape(t_c, d), x_sample.reshape(t_l, d)], axis=0)
    t = t_c + t_l
    zero_state = jnp.zeros((nb_c, N_HEADS, 2, HEAD_W, HEAD_W), f32)
    streams = (dict(nseq=nb_c, seq=l_c, row0=0), dict(nseq=nb_l, seq=l_l, row0=t_c))
    new_gdn, new_gla, new_hg, new_ckv, new_kpe = [], [], [], [], []

    for l in range(depth):
        mod = mods[l]
        p = _inproj(x, mod, w_in_r, l, t_c, l_l)

        gsm = p[:, OFF_SMALL + SM_GDN_B:OFF_SMALL + SM_GDN_B + 16].reshape(t // GROUP, GROUP, 4, N_HEADS)
        grow = jnp.pad(gsm.transpose(3, 0, 2, 1), ((0, 0), (0, 0), (0, 4), (0, 0)))
        zc = jnp.zeros((N_HEADS, 2), f32)
        hp = jnp.stack([jnp.concatenate([zc, gdn_a_log[l].T, zc, zc], axis=1),
                        jnp.concatenate([zc, gdn_dt_bias[l].T, zc, zc], axis=1)], axis=-1)

        la_f, la_b = _gla_prep(p, *_gla_gate_weights(gla_gate_w2[l], gla_gate_b[l]))
        hla_f, hla_b, hk_f, hk_b = _hgrn_prep(p, lower_bounds[l])
        wqa, wqb = _relayout_wq(mla_wq_b[l])
        wkv = _relayout_wkv(mla_wkv_b[l])
        qnw, kvnw = mla_q_norm[l][None, :], mla_kv_norm[l][None, :]

        outs = {k: [] for k in ('gdn', 'gla', 'hg', 'mla')}
        for si, st in enumerate(streams):
            nseq, seq, row0 = st['nseq'], st['seq'], st['row0']
            rb = row0 // seq
            if si == 0:
                s_gdn0 = s_gla0 = s_hg0 = zero_state
            else:
                s_gdn0 = state_gdn[:, l].transpose(0, 2, 1, 3, 4)
                s_gla0 = jnp.pad(state_gla[:, l], ((0, 0),) * 3 + ((0, HEAD_W - GLA_DK), (0, 0))).transpose(0, 2, 1, 4, 3)
                s_hg0 = state_hgrn[:, l].transpose(0, 2, 1, 4, 3)
            o_gdn, s_gdn = _gdn(p, gdn_conv[l], grow, hp, gdn_norm[l][None, :], s_gdn0, nseq=nseq, seq=seq, row_blk0=rb)
            o_gla, s_gla = _scan((p, OFF_GLA_Q, rb), (p, OFF_GLA_K, rb), (p, OFF_GLA_K, rb), (p, OFF_GLA_V, rb),
                                 la_f, la_b, (p, OFF_GLA_R, rb), gla_norm[l][None, :], s_gla0, mexp, lvl,
                                 nseq=nseq, seq=seq, row_blk0=rb, q_silu=False, q_scale=GLA_DK ** -0.5, gate_silu=True)
            o_hg, s_hg = _scan((p, OFF_HG_Q, rb), (hk_f, 0, rb), (hk_b, 0, rb), (p, OFF_HG_I, rb),
                               hla_f, hla_b, (p, OFF_HG_G, rb), hgrn_norm[l][None, :], s_hg0, mexp, lvl,
                               nseq=nseq, seq=seq, row_blk0=rb, q_silu=True, q_scale=1.0, gate_silu=False)
            rope = si == 1
            qn, qp, ckv, kn, vv, kp = _mla_proj(p, qnw, kvnw, wqa, wkv, wqb if rope else None,
                                                cos_t if rope else None, sin_t if rope else None,
                                                nrows=nseq * seq, seq=seq, row0=row0)
            ctx = None
            if si == 1:
                knc, vvc = _kv_proj(cache_mla_ckv[:, l].reshape(nb_l * past, -1), wkv)
                kpc = jnp.pad(cache_mla_kpe[:, l].reshape(nb_l * past, MLA_ROPE), ((0, 0), (0, LANE - MLA_ROPE))).astype(bf16)
                ctx = (knc, kpc, vvc)
            o_mla = _attention(qn, qp, kn, kp, vv, ctx, nseq=nseq, seq=seq)
            outs['gdn'].append(o_gdn)
            outs['gla'].append(o_gla)
            outs['hg'].append(o_hg)
            outs['mla'].append(o_mla)
            if si == 0:
                new_gdn.append(s_gdn.transpose(0, 2, 1, 3, 4))
                new_gla.append(s_gla.transpose(0, 2, 1, 4, 3)[:, :, :, :GLA_DK, :])
                new_hg.append(s_hg.transpose(0, 2, 1, 4, 3))
                new_ckv.append(ckv.reshape(nb_c, l_c, -1))
                new_kpe.append(p[:t_c, OFF_SMALL + SM_KPE:OFF_SMALL + SM_KPE + MLA_ROPE].reshape(nb_c, l_c, MLA_ROPE))

        branches = [jnp.concatenate(outs[k], axis=0) for k in ('gdn', 'gla', 'hg', 'mla')]
        wb = _cast_layer(w_branch.reshape(depth, 4 * BRANCH_W, d), l).reshape(4, BRANCH_W, d)
        x1 = _merge(branches, p, b_gates[l], wb, _cast_layer(w_out, l), x, mod,
                    ln1_g[l][None, :], ln1_b[l][None, :], alpha=alpha, t_ctx=t_c, l_lat=l_l)
        x = _ffn(x1, mod, _cast_layer(ffn_w1, l), _cast_layer(ffn_w3, l), _cast_layer(ffn_w2, l),
                 ln2_g[l][None, :], ln2_b[l][None, :], alpha=alpha, t_ctx=t_c, l_lat=l_l)

    sdt = x_prompt.dtype
    return (x[:t_c].reshape(nb_c, l_c, d), x[t_c:].reshape(nb_l, l_l, d),
            jnp.stack(new_gdn, axis=1).astype(sdt), jnp.stack(new_gla, axis=1).astype(sdt),
            jnp.stack(new_hg, axis=1).astype(sdt), jnp.stack(new_ckv, axis=1), jnp.stack(new_kpe, axis=1))
```

```python
import functools

import numpy as np
import jax
import jax.numpy as jnp
from jax import lax
from jax.experimental import pallas as pl
from jax.experimental.pallas import tpu as pltpu

f32 = jnp.float32
bf16 = jnp.bfloat16

D_MODEL = 2048
N_HEADS = 4
HEAD_W = 128
BRANCH_W = N_HEADS * HEAD_W
GLA_DK = 64
GLA_RANK = 16
GLA_TAU = 16.0
GDN_CONV = 5
MLA_NOPE = 128
MLA_ROPE = 64
ROPE_BASE = 10000.0
GRID_W = 64
NORM_EPS = 1e-6
CHUNK = 64
SUB = 16
LANE = 128
VMEM_LIMIT = 56 * 1024 * 1024

_SEG = 512
OFF_GDN_Q, OFF_GDN_K, OFF_GDN_V, OFF_GDN_Z = 0, 512, 1024, 1536
OFF_GLA_Q, OFF_GLA_K, OFF_GLA_V, OFF_GLA_R = 2048, 2560, 3072, 3584
OFF_HG_Q, OFF_HG_FF, OFF_HG_FB, OFF_HG_I, OFF_HG_G = 4096, 4608, 5120, 5632, 6144
OFF_MLA_QA, OFF_MLA_KVA, OFF_SMALL, OFF_GATES = 6656, 7168, 7680, 8192
N_PROJ = OFF_GATES + N_HEADS * D_MODEL
SM_KPE, SM_KPE_SW, SM_GLA_G, SM_GDN_B, SM_GDN_A = 0, 64, 128, 160, 168

_SRC = {}
_o = 0
for _n, _w in (('gdn_q', 512), ('gdn_k', 512), ('gdn_v', 512), ('gdn_z', 512), ('gdn_b', 8), ('gdn_a', 8),
               ('gla_q', 256), ('gla_k', 256), ('gla_v', 512), ('gla_r', 512), ('gla_g', 32),
               ('hg_q', 512), ('hg_f', 1024), ('hg_i', 512), ('hg_g', 512),
               ('mla_qa', 512), ('mla_kva', 512), ('mla_kpe', 64), ('gates', 4 * D_MODEL)):
    _SRC[_n] = _o
    _o += _w
IN_WIDTH = _o


def _cparams(sem):
    return pltpu.CompilerParams(dimension_semantics=sem, vmem_limit_bytes=VMEM_LIMIT)


def _dot(a, b):
    return jnp.dot(a, b, preferred_element_type=f32)


def _dot_nt(a, b):
    return lax.dot_general(a, b, (((1,), (1,)), ((), ())), preferred_element_type=f32)


def _dot_tn(a, b):
    return lax.dot_general(a, b, (((0,), (0,)), ((), ())), preferred_element_type=f32)


def _split2(x):
    hi = x.astype(bf16)
    lo = (x - hi.astype(f32)).astype(bf16)
    return hi, lo


def _mm3(a, b):
    ah, al = _split2(a)
    bh, bl = _split2(b)
    return _dot(ah, bh) + (_dot(ah, bl) + _dot(al, bh))


def _layernorm(y, g, b):
    mu = jnp.mean(y, axis=-1, keepdims=True)
    yc = y - mu
    var = jnp.mean(yc * yc, axis=-1, keepdims=True)
    return yc * lax.rsqrt(var + NORM_EPS) * g + b


def _rmsnorm(y, w):
    return y * lax.rsqrt(jnp.mean(y * y, axis=-1, keepdims=True) + NORM_EPS) * w


def _ada_kernel(c_ref, w_ref, b_ref, o_ref):
    cs = jax.nn.silu(c_ref[...]).astype(bf16)
    o_ref[...] = _dot(cs, w_ref[...].astype(bf16)) + b_ref[...]


def _ada(cc, w_ada, b_ada):
    depth, d, n6 = w_ada.shape
    rows = cc.shape[0]
    tn = 1024
    return pl.pallas_call(
        _ada_kernel,
        out_shape=jax.ShapeDtypeStruct((depth, rows, n6), f32),
        grid=(depth, n6 // tn),
        in_specs=[pl.BlockSpec((rows, d), lambda l, j: (0, 0)),
                  pl.BlockSpec((None, d, tn), lambda l, j: (l, 0, j)),
                  pl.BlockSpec((None, 1, tn), lambda l, j: (l, 0, j))],
        out_specs=pl.BlockSpec((None, rows, tn), lambda l, j: (l, 0, j)),
        compiler_params=_cparams(("parallel", "parallel")),
        name="ada_mod",
    )(cc, w_ada, b_ada.reshape(depth, 1, n6))


def _inproj_kernel(x_ref, mod_ref, w_ref, o_ref, xb_ref):
    @pl.when(pl.program_id(1) == 0)
    def _():
        shift = mod_ref[0:1, :]
        scale = mod_ref[1:2, :]
        xb_ref[...] = (x_ref[...] * (1.0 + scale) + shift).astype(bf16)

    o_ref[...] = _dot_nt(xb_ref[...], w_ref[...])


def _mod_index(tm, t_ctx, l_lat):
    def index(i):
        r = i * tm
        return jnp.where(r < t_ctx, 0, 1 + (r - t_ctx) // l_lat)
    return index


def _inproj(x, mod, w, layer, t_ctx, l_lat):
    t, d = x.shape
    n = w.shape[1]
    tm, tn = 1024, 1024
    midx = _mod_index(tm, t_ctx, l_lat)
    return pl.pallas_call(
        _inproj_kernel,
        out_shape=jax.ShapeDtypeStruct((t, n), f32),
        grid=(t // tm, n // tn),
        in_specs=[pl.BlockSpec((tm, d), lambda i, j: (i, 0)),
                  pl.BlockSpec((None, 6, d), lambda i, j: (midx(i), 0, 0)),
                  pl.BlockSpec((None, tn, d), lambda i, j: (layer, j, 0))],
        out_specs=pl.BlockSpec((tm, tn), lambda i, j: (i, j)),
        scratch_shapes=[pltpu.VMEM((tm, d), bf16)],
        compiler_params=_cparams(("parallel", "arbitrary")),
        name="in_proj",
    )(x, mod, w)


_PACK = 4
GROUP = _PACK * CHUNK
_GDN_HEADS = 2


def _gdn_kernel(q_ref, k_ref, v_ref, z_ref, cwq_ref, cwk_ref, cwv_ref, grow_ref, hp_ref, nw_ref, s0_ref,
                o_ref, so_ref, pad_ref, qs, ks, vs, gs, ub_s, wq_s, aq_s, kd_s, ct_s, of, ob, s_ref, *, seq):
    n = seq // CHUNK
    width = _GDN_HEADS * HEAD_W
    chains = [(hd, d) for hd in range(_GDN_HEADS) for d in range(2)]

    def conv_silu(x_ref, w_ref):
        pad_ref[0:8, :] = jnp.zeros((8, width), f32)
        pad_ref[8 + seq:16 + seq, :] = jnp.zeros((8, width), f32)
        pad_ref[8:8 + seq, :] = x_ref[...]
        acc = pad_ref[pl.ds(6, seq), :] * w_ref[0:1, :]
        for j in range(1, GDN_CONV):
            acc = acc + pad_ref[pl.ds(6 + j, seq), :] * w_ref[j:j + 1, :]
        return jax.nn.silu(acc)

    def l2norm_heads(x):
        parts = []
        for hd in range(_GDN_HEADS):
            xh = x[:, hd * HEAD_W:(hd + 1) * HEAD_W]
            parts.append(xh * lax.rsqrt(jnp.sum(xh * xh, axis=-1, keepdims=True) + NORM_EPS))
        return jnp.concatenate(parts, axis=1)

    qs[...] = l2norm_heads(conv_silu(q_ref, cwq_ref)) * (HEAD_W ** -0.5)
    ks[...] = l2norm_heads(conv_silu(k_ref, cwk_ref))
    vs[...] = conv_silu(v_ref, cwv_ref)

    x = grow_ref[...]
    a_log = hp_ref[:, :, 0:1][:, None]
    dt_b = hp_ref[:, :, 1:2][:, None]
    rows = lax.broadcasted_iota(jnp.int32, x.shape, 2)
    gs[...] = jnp.where(rows < 2, jax.nn.sigmoid(x), -jnp.exp(a_log) * jax.nn.softplus(x + dt_b))
    s_ref[...] = s0_ref[...].reshape(2 * _GDN_HEADS, HEAD_W, HEAD_W)

    ri = lax.broadcasted_iota(jnp.int32, (CHUNK, GROUP), 0)
    li = lax.broadcasted_iota(jnp.int32, (CHUNK, GROUP), 1)
    cj = li % CHUNK
    blk = [(li // CHUNK) == r for r in range(_PACK)]
    blk_bf = [jnp.where(b, 1.0, 0.0).astype(bf16) for b in blk]
    l1 = lax.broadcasted_iota(jnp.int32, (1, GROUP), 1) // CHUNK
    eye_b = ri == cj
    eye = jnp.where(eye_b, 1.0, 0.0).astype(f32)
    bd16 = (ri // SUB) == (cj // SUB)
    bd32 = (ri // (2 * SUB)) == (cj // (2 * SUB))
    r2 = lax.broadcasted_iota(jnp.int32, (GROUP, GROUP), 0)
    c2 = lax.broadcasted_iota(jnp.int32, (GROUP, GROUP), 1)
    same_blk = (r2 // CHUNK) == (c2 // CHUNK)

    def bdiag(yb):
        return jnp.concatenate([yb * mk for mk in blk_bf], axis=0)

    def unpack_diag(xf):
        out = xf[0:CHUNK]
        for r in range(1, _PACK):
            out = jnp.where(blk[r], xf[r * CHUNK:(r + 1) * CHUNK], out)
        return out

    def pmm3(pairs):
        sp = [(_split2(a), _split2(b)) for a, b in pairs]
        r1 = [_dot(jnp.concatenate([ah, al], axis=0), bdiag(bh)) for (ah, al), (bh, _) in sp]
        r2 = [_dot(ah, bdiag(bl)) for (ah, _), (_, bl) in sp]
        return [x1[:CHUNK] + x1[CHUNK:] + x2 for x1, x2 in zip(r1, r2)]

    def unit_tri_inverse(ms):
        k = len(ms)
        m32 = [jnp.where(bd32, m, 0.0) for m in ms]
        dg = [jnp.where(bd16, m, 0.0) for m in ms]
        d2 = pmm3([(a, a) for a in dg])
        both = pmm3([(a, a) for a in d2] + [(eye - a, eye + b) for a, b in zip(dg, d2)])
        d4, xi = both[:k], both[k:]
        both = pmm3([(a, a) for a in d4] + [(a, eye + b) for a, b in zip(xi, d4)])
        d8, xi = both[:k], both[k:]
        xi = pmm3([(a, eye + b) for a, b in zip(xi, d8)])
        for lo, hi in ((dg, m32), (m32, ms)):
            t = pmm3([(a, h_ - l_) for a, h_, l_ in zip(xi, hi, lo)])
            t = pmm3(list(zip(t, xi)))
            xi = [a - b for a, b in zip(xi, t)]
        return xi

    def block_cols(xp):
        return [jnp.sum(jnp.where(blk[r], xp, 0.0), axis=1, keepdims=True) for r in range(_PACK)]

    def spread(cols):
        out = jnp.broadcast_to(cols[0], (CHUNK, GROUP))
        for r in range(1, _PACK):
            out = jnp.where(blk[r], cols[r], out)
        return out

    def prepare_group(g, carry):
        rows_g = pl.ds(pl.multiple_of(g * GROUP, GROUP), GROUP)
        heads = range(_GDN_HEADS)
        k4 = [ks[rows_g, hd * HEAD_W:(hd + 1) * HEAD_W] for hd in heads]
        q4 = [qs[rows_g, hd * HEAD_W:(hd + 1) * HEAD_W] for hd in heads]
        v4 = [vs[rows_g, hd * HEAD_W:(hd + 1) * HEAD_W] for hd in heads]
        k4b = [x.astype(bf16) for x in k4]
        kq = [_dot_nt(jnp.concatenate([k4b[hd], q4[hd].astype(bf16)], axis=0), k4b[hd]) for hd in heads]
        kk_p = [unpack_diag(x[:GROUP]) for x in kq]
        qk_p = [unpack_diag(x[GROUP:]) for x in kq]
        g8 = [gs[hd, g] for hd in heads]
        ids = range(len(chains))
        beta_r = [g8[hd][d:d + 1, :] for hd, d in chains]
        g_r = [g8[hd][2 + d:3 + d, :] for hd, d in chains]
        tri2 = [cj <= ri, cj >= ri]
        tri = [tri2[d] for _, d in chains]
        strict = [jnp.logical_and(t, jnp.logical_not(eye_b)) for t in tri]
        tg = [jnp.where(tri[i], g_r[i], 0.0) for i in ids]
        gam_cols = [block_cols(tg[i]) for i in ids]
        beta_cols = [block_cols(jnp.where(eye_b, beta_r[i], 0.0)) for i in ids]
        tot_cols = [[jnp.sum(jnp.where(l1 == r, g_r[i], 0.0), axis=1, keepdims=True) for r in range(_PACK)]
                    for i in ids]
        strict2 = [(c2 % CHUNK) < (r2 % CHUNK), (c2 % CHUNK) > (r2 % CHUNK)]
        sm = [jnp.where(jnp.logical_and(same_blk, s_), 1.0, 0.0).astype(bf16) for s_ in strict2]
        r3 = []
        for i, (_, d) in enumerate(chains):
            th, tl = _split2(tg[i])
            tl2 = (tg[i] - th.astype(f32) - tl.astype(f32)).astype(bf16)
            r3.append(_dot(jnp.concatenate([th, tl, tl2], axis=0), sm[d]))
        dlt = [x[:CHUNK] + x[CHUNK:2 * CHUNK] + x[2 * CHUNK:] for x in r3]
        decay = [jnp.where(tri[i], jnp.exp(jnp.minimum(dlt[i], 0.0)), 0.0) for i in ids]
        t_inv = unit_tri_inverse([jnp.where(strict[i], kk_p[hd] * spread(beta_cols[i]) * decay[i], 0.0)
                                  for i, (hd, _) in enumerate(chains)])
        gam = [jnp.concatenate(gam_cols[i], axis=0) for i in ids]
        beta = [jnp.concatenate(beta_cols[i], axis=0) for i in ids]
        tot = [jnp.concatenate([jnp.broadcast_to(t, (CHUNK, 1)) for t in tot_cols[i]], axis=0) for i in ids]
        eg = [jnp.exp(x) for x in gam]
        rhs = [jnp.concatenate([v4[hd] * beta[i], k4[hd] * beta[i] * eg[i]], axis=1)
               for i, (hd, _) in enumerate(chains)]
        isp = [_split2(x) for x in t_inv]
        rsp = [_split2(x) for x in rhs]
        bih = [bdiag(isp[i][0]) for i in ids]
        s1 = [_dot(jnp.concatenate([bih[i], bdiag(isp[i][1])], axis=0), rsp[i][0]) for i in ids]
        s2 = [_dot(bih[i], rsp[i][1]) for i in ids]
        for i, (hd, d) in enumerate(chains):
            sol = s1[i][:GROUP] + s1[i][GROUP:] + s2[i]
            ub_s[i, rows_g, :] = sol[:, :HEAD_W]
            w_b = sol[:, HEAD_W:].astype(bf16)
            qd_b = (q4[hd] * eg[i]).astype(bf16)
            a_p = qk_p[hd] * decay[i]
            for r_ in range(_PACK):
                c = g * _PACK + r_
                wq_s[i, c, 0:CHUNK, :] = w_b[r_ * CHUNK:(r_ + 1) * CHUNK]
                wq_s[i, c, CHUNK:2 * CHUNK, :] = qd_b[r_ * CHUNK:(r_ + 1) * CHUNK]
                aq_s[i, c] = a_p[:, r_ * CHUNK:(r_ + 1) * CHUNK].astype(bf16)
                ct_s[i, c] = jnp.broadcast_to(jnp.exp(tot_cols[i][r_]), (8, HEAD_W))
            kd_s[i, rows_g, :] = (k4[hd] * jnp.exp(tot[i] - gam[i])).astype(bf16)
        return carry

    lax.fori_loop(0, n // _PACK, prepare_group, 0)

    def step(t, carry):
        ids = range(len(chains))
        cs = [t if d == 0 else n - 1 - t for _, d in chains]
        rows = [pl.ds(pl.multiple_of(c * CHUNK, CHUNK), CHUNK) for c in cs]
        s = [s_ref[i] for i in ids]
        sb = [x.astype(bf16) for x in s]
        r = [_dot(wq_s[i, cs[i]], sb[i]) for i in ids]
        u = [(ub_s[i, rows[i], :] - r[i][:CHUNK]).astype(bf16) for i in ids]
        o_c = [r[i][CHUNK:] + _dot(aq_s[i, cs[i]], u[i]) for i in ids]
        s_new = [ct_s[i, cs[i]][0:1, :] * s[i] + _dot_tn(kd_s[i, rows[i], :], u[i]) for i in ids]
        for i, (hd, d) in enumerate(chains):
            (of if d == 0 else ob)[rows[i], hd * HEAD_W:(hd + 1) * HEAD_W] = o_c[i]
            s_ref[i] = s_new[i]
        return carry

    lax.fori_loop(0, n, step, 0)
    o = of[...] + ob[...]
    z = z_ref[...]
    nw = nw_ref[...]
    o_ref[...] = jnp.concatenate(
        [_rmsnorm(o[:, hd * HEAD_W:(hd + 1) * HEAD_W], nw) * jax.nn.silu(z[:, hd * HEAD_W:(hd + 1) * HEAD_W])
         for hd in range(_GDN_HEADS)], axis=1).astype(bf16)
    so_ref[...] = s_ref[...].reshape(_GDN_HEADS, 2, HEAD_W, HEAD_W)


def _gdn(p, conv_w, grow, hp, norm_w, s0, *, nseq, seq, row_blk0):
    n = seq // CHUNK
    ng = seq // GROUP
    hh = _GDN_HEADS
    width = hh * HEAD_W
    nch = 2 * hh

    def pcol(off):
        return pl.BlockSpec((seq, width), lambda b, h: (row_blk0 + b, off // width + h))

    def wcol(off):
        return pl.BlockSpec((GDN_CONV, width), lambda b, h: (0, off // width + h))

    kern = functools.partial(_gdn_kernel, seq=seq)
    return pl.pallas_call(
        kern,
        out_shape=(jax.ShapeDtypeStruct((nseq * seq, BRANCH_W), bf16),
                   jax.ShapeDtypeStruct((nseq, N_HEADS, 2, HEAD_W, HEAD_W), f32)),
        grid=(nseq, N_HEADS // hh),
        in_specs=[pcol(OFF_GDN_Q), pcol(OFF_GDN_K), pcol(OFF_GDN_V), pcol(OFF_GDN_Z),
                  wcol(0), wcol(BRANCH_W), wcol(2 * BRANCH_W),
                  pl.BlockSpec((hh, ng, 8, GROUP), lambda b, h: (h, row_blk0 + b, 0, 0)),
                  pl.BlockSpec((hh, 8, 2), lambda b, h: (h, 0, 0)),
                  pl.BlockSpec((1, HEAD_W), lambda b, h: (0, 0)),
                  pl.BlockSpec((None, hh, 2, HEAD_W, HEAD_W), lambda b, h: (b, h, 0, 0, 0))],
        out_specs=(pl.BlockSpec((seq, width), lambda b, h: (b, h)),
                   pl.BlockSpec((None, hh, 2, HEAD_W, HEAD_W), lambda b, h: (b, h, 0, 0, 0))),
        scratch_shapes=[pltpu.VMEM((seq + 16, width), f32),
                        pltpu.VMEM((seq, width), f32), pltpu.VMEM((seq, width), f32),
                        pltpu.VMEM((seq, width), f32), pltpu.VMEM((hh, ng, 8, GROUP), f32),
                        pltpu.VMEM((nch, seq, HEAD_W), f32), pltpu.VMEM((nch, n, 2 * CHUNK, HEAD_W), bf16),
                        pltpu.VMEM((nch, n, CHUNK, CHUNK), bf16),
                        pltpu.VMEM((nch, seq, HEAD_W), bf16), pltpu.VMEM((nch, n, 8, HEAD_W), f32),
                        pltpu.VMEM((seq, width), f32), pltpu.VMEM((seq, width), f32),
                        pltpu.VMEM((nch, HEAD_W, HEAD_W), f32)],
        compiler_params=_cparams(("parallel", "parallel")),
        name="gdn",
    )(p, p, p, p, conv_w, conv_w, conv_w, grow, hp, norm_w, s0)


_N_LEVELS = 6
_N_MM_LEVELS = 4
_ROW_EQ = _N_MM_LEVELS * CHUNK
_ROW_EK, _ROW_TOT, _ROWS_EXP = _ROW_EQ + CHUNK, _ROW_EQ + 2 * CHUNK, _ROW_EQ + 2 * CHUNK + 8
_SCAN_UNROLL = 4


def _scan_consts():
    c = CHUNK
    mexp = np.zeros((2, _ROWS_EXP, c), np.float32)
    lvl = np.full((2, c, c), -1.0, np.float32)
    for lv in range(_N_LEVELS):
        s = 32 >> lv
        for i in range(c):
            p = (i // (2 * s)) * (2 * s) + s
            right = (i % (2 * s)) >= s
            if lv >= _N_MM_LEVELS:
                pass
            elif right:
                mexp[0, lv * c + i, p:i + 1] = 1.0
                mexp[1, lv * c + i, p:i] = 1.0
            else:
                mexp[0, lv * c + i, i + 1:p] = 1.0
                mexp[1, lv * c + i, i:p] = 1.0
            for j in range(c):
                if (i // (2 * s)) != (j // (2 * s)):
                    continue
                jright = (j % (2 * s)) >= s
                if right and not jright:
                    lvl[0, i, j] = lv
                if (not right) and jright:
                    lvl[1, i, j] = lv
    for i in range(c):
        lvl[:, i, i] = _N_LEVELS
        mexp[0, _ROW_EQ + i, :i + 1] = 1.0
        mexp[0, _ROW_EK + i, i + 1:] = 1.0
        mexp[1, _ROW_EQ + i, i:] = 1.0
        mexp[1, _ROW_EK + i, :i] = 1.0
    mexp[:, _ROW_TOT:, :] = 1.0
    return mexp, lvl


def _scan_kernel(q_ref, kf_ref, kb_ref, v_ref, laf_ref, lab_ref, gate_ref, nw_ref, s0_ref, mexp_ref, lvl_ref,
                 o_ref, so_ref, of, ob, qd_s, g_s, ct_s, stp_s, st_ref, *, seq, q_silu, q_scale, gate_silu):
    n = seq // CHUNK
    st_ref[...] = s0_ref[...]
    outs = (of, ob)

    def rows_of(c):
        return pl.ds(pl.multiple_of(c * CHUNK, CHUNK), CHUNK)

    def prepare_group(g, carry):
        todo = [(d, g * _SCAN_UNROLL + u) for u in range(_SCAN_UNROLL) for d in range(2)]
        items = range(len(todo))
        q, k, v, la, e2 = [], [], [], [], []
        for d, c in todo:
            rows_c = rows_of(c)
            qi = q_ref[rows_c, :]
            q.append(jax.nn.silu(qi) if q_silu else qi * q_scale)
            k.append((kf_ref if d == 0 else kb_ref)[rows_c, :])
            v.append(v_ref[rows_c, :].astype(bf16))
            la.append((laf_ref if d == 0 else lab_ref)[rows_c, :])
            hi, lo = _split2(la[-1])
            e2.append(_dot(mexp_ref[d], jnp.concatenate([hi, lo], axis=1)))
        e = [x[:, :HEAD_W] + x[:, HEAD_W:] for x in e2]
        r4 = lax.broadcasted_iota(jnp.int32, (CHUNK, HEAD_W), 0) % 4
        fine = []
        for i, (d, _) in enumerate(todo):
            prev = pltpu.roll(la[i], 1, 0)
            nxt = pltpu.roll(la[i], CHUNK - 1, 0)
            if d == 0:
                w2 = jnp.where(r4 == 0, nxt, jnp.where(r4 == 1, 0.0, jnp.where(r4 == 2, la[i], la[i] + prev)))
                w1 = jnp.where(r4 % 2 == 1, la[i], 0.0)
            else:
                w2 = jnp.where(r4 == 0, la[i] + nxt, jnp.where(r4 == 1, la[i], jnp.where(r4 == 2, 0.0, prev)))
                w1 = jnp.where(r4 % 2 == 0, la[i], 0.0)
            fine.append((w2, w1))
        lvl = [lvl_ref[d] for d, _ in todo]
        qk = [_dot_nt(q[i].astype(bf16), k[i].astype(bf16)) for i in items]
        a = [jnp.where(lvl[i] == float(_N_LEVELS), qk[i], 0.0) for i in items]
        for lv in range(_N_LEVELS):
            if lv < _N_MM_LEVELS:
                w = [jnp.exp(e[i][lv * CHUNK:(lv + 1) * CHUNK]) for i in items]
            else:
                w = [jnp.exp(fine[i][lv - _N_MM_LEVELS]) for i in items]
            p = [_dot_nt((q[i] * w[i]).astype(bf16), (k[i] * w[i]).astype(bf16)) for i in items]
            a = [jnp.where(lvl[i] == float(lv), p[i], a[i]) for i in items]
        o_intra = [_dot(a[i].astype(bf16), v[i]) for i in items]
        incr = [_dot_tn(v[i], (k[i] * jnp.exp(e[i][_ROW_EK:_ROW_EK + CHUNK])).astype(bf16)) for i in items]
        for i, (d, c) in enumerate(todo):
            rows_c = rows_of(c)
            outs[d][rows_c, :] = o_intra[i]
            qd_s[d, rows_c, :] = (q[i] * jnp.exp(e[i][_ROW_EQ:_ROW_EQ + CHUNK])).astype(bf16)
            g_s[d, c] = incr[i]
            ct_s[d, c] = jnp.exp(e[i][_ROW_TOT:_ROW_TOT + 8])
        return carry

    lax.fori_loop(0, n // _SCAN_UNROLL, prepare_group, 0)

    def scan_step(i, carry):
        for d, c in ((0, i), (1, n - 1 - i)):
            st = st_ref[d]
            stp_s[d, c] = st.astype(bf16)
            st_ref[d] = st * ct_s[d, c][0:1, :] + g_s[d, c]
        return carry

    lax.fori_loop(0, n, scan_step, 0)

    def inter_group(g, carry):
        todo = [(d, g * _SCAN_UNROLL + u) for u in range(_SCAN_UNROLL) for d in range(2)]
        res = [_dot_nt(qd_s[d, rows_of(c), :], stp_s[d, c]) + outs[d][rows_of(c), :] for d, c in todo]
        for (d, c), o_c in zip(todo, res):
            outs[d][rows_of(c), :] = o_c
        return carry

    lax.fori_loop(0, n // _SCAN_UNROLL, inter_group, 0)
    o = of[...] + ob[...]
    g = gate_ref[...]
    g = jax.nn.silu(g) if gate_silu else jax.nn.sigmoid(g)
    o_ref[...] = (_rmsnorm(o, nw_ref[...]) * g).astype(bf16)
    so_ref[...] = st_ref[...]


def _scan(q_src, kf_src, kb_src, v_src, laf, lab, gate_src, norm_w, s0t, mexp, lvl, *,
          nseq, seq, row_blk0, q_silu, q_scale, gate_silu):
    def spec(src):
        off, rb = src[1], src[2]
        return pl.BlockSpec((seq, HEAD_W), lambda b, h: (rb + b, off // HEAD_W + h))

    srcs = (q_src, kf_src, kb_src, v_src, (laf, 0, row_blk0), (lab, 0, row_blk0), gate_src)
    kern = functools.partial(_scan_kernel, seq=seq, q_silu=q_silu, q_scale=q_scale, gate_silu=gate_silu)
    return pl.pallas_call(
        kern,
        out_shape=(jax.ShapeDtypeStruct((nseq * seq, BRANCH_W), bf16),
                   jax.ShapeDtypeStruct((nseq, N_HEADS, 2, HEAD_W, HEAD_W), f32)),
        grid=(nseq, N_HEADS),
        in_specs=[spec(s) for s in srcs] + [
            pl.BlockSpec((1, HEAD_W), lambda b, h: (0, 0)),
            pl.BlockSpec((None, None, 2, HEAD_W, HEAD_W), lambda b, h: (b, h, 0, 0, 0)),
            pl.BlockSpec((2, _ROWS_EXP, CHUNK), lambda b, h: (0, 0, 0)),
            pl.BlockSpec((2, CHUNK, CHUNK), lambda b, h: (0, 0, 0))],
        out_specs=(pl.BlockSpec((seq, HEAD_W), lambda b, h: (b, h)),
                   pl.BlockSpec((None, None, 2, HEAD_W, HEAD_W), lambda b, h: (b, h, 0, 0, 0))),
        scratch_shapes=[pltpu.VMEM((seq, HEAD_W), f32), pltpu.VMEM((seq, HEAD_W), f32),
                        pltpu.VMEM((2, seq, HEAD_W), bf16),
                        pltpu.VMEM((2, seq // CHUNK, HEAD_W, HEAD_W), f32),
                        pltpu.VMEM((2, seq // CHUNK, 8, HEAD_W), f32),
                        pltpu.VMEM((2, seq // CHUNK, HEAD_W, HEAD_W), bf16),
                        pltpu.VMEM((2, HEAD_W, HEAD_W), f32)],
        compiler_params=_cparams(("parallel", "parallel")),
        name="decay_scan",
    )(*[s[0] for s in srcs], norm_w, s0t, mexp, lvl)


_LOG_DECAY_FLOOR = -1.0e4


def _gla_prep_kernel(sm_ref, w2_ref, b_ref, laf_ref, lab_ref):
    x = sm_ref[...].astype(bf16)
    for d, out in ((0, laf_ref), (1, lab_ref)):
        logits = _dot(x, w2_ref[d]) + b_ref[d]
        out[...] = jnp.maximum(jax.nn.log_sigmoid(logits) * (1.0 / GLA_TAU), _LOG_DECAY_FLOOR)


def _gla_prep(p, w2big, bbig):
    t = p.shape[0]
    tm = 512
    out = jax.ShapeDtypeStruct((t, BRANCH_W), f32)
    return pl.pallas_call(
        _gla_prep_kernel,
        out_shape=(out, out),
        grid=(t // tm,),
        in_specs=[pl.BlockSpec((tm, _SEG), lambda i: (i, OFF_SMALL // _SEG)),
                  pl.BlockSpec((2, _SEG, BRANCH_W), lambda i: (0, 0, 0)),
                  pl.BlockSpec((2, 1, BRANCH_W), lambda i: (0, 0, 0))],
        out_specs=(pl.BlockSpec((tm, BRANCH_W), lambda i: (i, 0)),
                   pl.BlockSpec((tm, BRANCH_W), lambda i: (i, 0))),
        compiler_params=_cparams(("parallel",)),
        name="gla_prep",
    )(p, w2big, bbig)


def _hgrn_prep_kernel(zf_ref, zb_ref, lb_ref, laf_ref, lab_ref, kf_ref, kb_ref):
    for d, z_ref, la_out, k_out in ((0, zf_ref, laf_ref, kf_ref), (1, zb_ref, lab_ref, kb_ref)):
        z = z_ref[...]
        lb = lb_ref[d:d + 1, :]
        a = jnp.log(lb)
        b = jnp.log1p(-lb) + jax.nn.log_sigmoid(z)
        mx = jnp.maximum(a, b)
        lse = mx + jnp.log(jnp.exp(a - mx) + jnp.exp(b - mx))
        lse = jnp.where(mx == -jnp.inf, -jnp.inf, lse)
        la_out[...] = jnp.maximum(lse, _LOG_DECAY_FLOOR)
        k_out[...] = (1.0 - lb) * jax.nn.sigmoid(-z)


def _hgrn_prep(p, lb):
    t = p.shape[0]
    tm = 512
    out = jax.ShapeDtypeStruct((t, BRANCH_W), f32)
    ospec = pl.BlockSpec((tm, BRANCH_W), lambda i: (i, 0))
    return pl.pallas_call(
        _hgrn_prep_kernel,
        out_shape=(out, out, out, out),
        grid=(t // tm,),
        in_specs=[pl.BlockSpec((tm, _SEG), lambda i: (i, OFF_HG_FF // _SEG)),
                  pl.BlockSpec((tm, _SEG), lambda i: (i, OFF_HG_FB // _SEG)),
                  pl.BlockSpec((2, BRANCH_W), lambda i: (0, 0))],
        out_specs=(ospec, ospec, ospec, ospec),
        compiler_params=_cparams(("parallel",)),
        name="hgrn_prep",
    )(p, p, lb)


def _mla_proj_kernel(*refs, rope):
    if rope:
        (qa_ref, kva_ref, sm_ref, qnw_ref, kvnw_ref, wqa_ref, wkv_ref, wqb_ref, cos_ref, sin_ref,
         qn_ref, qp_ref, ckv_ref, kn_ref, vv_ref, kp_ref) = refs
    else:
        (qa_ref, kva_ref, sm_ref, qnw_ref, kvnw_ref, wqa_ref, wkv_ref,
         qn_ref, qp_ref, ckv_ref, kn_ref, vv_ref, kp_ref) = refs
    qh = _rmsnorm(qa_ref[...], qnw_ref[...]).astype(bf16)
    qa = _dot(qh, wqa_ref[...])
    qn_ref[...] = qa[:, :BRANCH_W].astype(bf16)
    pe = qa[:, BRANCH_W:]
    kpe = sm_ref[:, 0:LANE]
    if rope:
        cos = cos_ref[...]
        sin = sin_ref[...]
        cos4 = jnp.concatenate([cos] * N_HEADS, axis=1)
        sin4 = jnp.concatenate([sin] * N_HEADS, axis=1)
        pe = pe * cos4 + _dot(qh, wqb_ref[...]) * sin4
        kpe = kpe * cos + pltpu.roll(kpe, MLA_ROPE, 1) * sin
    qp_ref[...] = pe.astype(bf16)
    kp_ref[...] = kpe.astype(bf16)
    ckv = _rmsnorm(kva_ref[...], kvnw_ref[...])
    ckv_ref[...] = ckv
    kv = _dot(ckv.astype(bf16), wkv_ref[...])
    kn_ref[...] = kv[:, :BRANCH_W].astype(bf16)
    vv_ref[...] = kv[:, BRANCH_W:].astype(bf16)


def _mla_proj(p, qnw, kvnw, wqa, wkv, wqb, cos, sin, *, nrows, seq, row0):
    tm = 256
    rope = cos is not None
    rb0 = row0 // tm
    per_seq = seq // tm

    def pspec(off):
        return pl.BlockSpec((tm, _SEG), lambda i: (rb0 + i, off // _SEG))

    def full(a):
        return pl.BlockSpec(a.shape, lambda i: (0,) * a.ndim)

    in_specs = [pspec(OFF_MLA_QA), pspec(OFF_MLA_KVA), pspec(OFF_SMALL), full(qnw), full(kvnw), full(wqa), full(wkv)]
    args = [p, p, p, qnw, kvnw, wqa, wkv]
    if rope:
        tspec = pl.BlockSpec((tm, LANE), lambda i: (i % per_seq, 0))
        in_specs += [full(wqb), tspec, tspec]
        args += [wqb, cos, sin]
    wide = lambda dt: jax.ShapeDtypeStruct((nrows, BRANCH_W), dt)
    ospec = pl.BlockSpec((tm, BRANCH_W), lambda i: (i, 0))
    return pl.pallas_call(
        functools.partial(_mla_proj_kernel, rope=rope),
        out_shape=(wide(bf16), wide(bf16), wide(f32), wide(bf16), wide(bf16),
                   jax.ShapeDtypeStruct((nrows, LANE), bf16)),
        grid=(nrows // tm,),
        in_specs=in_specs,
        out_specs=(ospec, ospec, ospec, ospec, ospec, pl.BlockSpec((tm, LANE), lambda i: (i, 0))),
        compiler_params=_cparams(("parallel",)),
        name="mla_proj",
    )(*args)


def _kv_kernel(ckv_ref, w_ref, kn_ref, vv_ref):
    kv = _dot(ckv_ref[...].astype(bf16), w_ref[...])
    kn_ref[...] = kv[:, :BRANCH_W].astype(bf16)
    vv_ref[...] = kv[:, BRANCH_W:].astype(bf16)


def _kv_proj(ckv, wkv):
    rows = ckv.shape[0]
    tm = 256
    out = jax.ShapeDtypeStruct((rows, BRANCH_W), bf16)
    ospec = pl.BlockSpec((tm, BRANCH_W), lambda i: (i, 0))
    return pl.pallas_call(
        _kv_kernel, out_shape=(out, out), grid=(rows // tm,),
        in_specs=[pl.BlockSpec((tm, ckv.shape[1]), lambda i: (i, 0)),
                  pl.BlockSpec(wkv.shape, lambda i: (0, 0))],
        out_specs=(ospec, ospec),
        compiler_params=_cparams(("parallel",)),
        name="mla_ctx_kv",
    )(ckv, wkv)


def _attn_kernel(*refs, has_ctx):
    if has_ctx:
        qn_ref, qp_ref, kn_ref, kp_ref, vv_ref, knc_ref, kpc_ref, vvc_ref, o_ref = refs
    else:
        qn_ref, qp_ref, kn_ref, kp_ref, vv_ref, o_ref = refs
    scale = (MLA_NOPE + MLA_ROPE) ** -0.5
    heads = range(N_HEADS)

    def head(ref, h):
        return ref[:, h * HEAD_W:(h + 1) * HEAD_W]

    kp = kp_ref[...]
    s1 = [(_dot_nt(head(qn_ref, h), head(kn_ref, h)) + _dot_nt(head(qp_ref, h), kp)) * scale for h in heads]
    mx = [jnp.max(x, axis=-1, keepdims=True) for x in s1]
    if has_ctx:
        kpc = kpc_ref[...]
        s2 = [(_dot_nt(head(qn_ref, h), head(knc_ref, h)) + _dot_nt(head(qp_ref, h), kpc)) * scale for h in heads]
        mx = [jnp.maximum(m, jnp.max(x, axis=-1, keepdims=True)) for m, x in zip(mx, s2)]
    p1 = [jnp.exp(x - m) for x, m in zip(s1, mx)]
    den = [jnp.sum(x, axis=-1, keepdims=True) for x in p1]
    if has_ctx:
        p2 = [jnp.exp(x - m) for x, m in zip(s2, mx)]
        den = [d_ + jnp.sum(x, axis=-1, keepdims=True) for d_, x in zip(den, p2)]
    inv = [1.0 / d_ for d_ in den]
    o = [_dot((p1[h] * inv[h]).astype(bf16), head(vv_ref, h)) for h in heads]
    if has_ctx:
        o = [o[h] + _dot((p2[h] * inv[h]).astype(bf16), head(vvc_ref, h)) for h in heads]
    o_ref[...] = jnp.concatenate(o, axis=1).astype(bf16)


def _attention(qn, qp, kn, kp, vv, ctx, *, nseq, seq):
    tq = 256
    nq = seq // tq
    has_ctx = ctx is not None
    qspec = pl.BlockSpec((tq, BRANCH_W), lambda b, i: (b * nq + i, 0))
    kspec = pl.BlockSpec((seq, BRANCH_W), lambda b, i: (b, 0))
    kpspec = pl.BlockSpec((seq, LANE), lambda b, i: (b, 0))
    in_specs = [qspec, qspec, kspec, kpspec, kspec]
    args = [qn, qp, kn, kp, vv]
    if has_ctx:
        knc, kpc, vvc = ctx
        lc = knc.shape[0] // nseq
        in_specs += [pl.BlockSpec((lc, BRANCH_W), lambda b, i: (b, 0)),
                     pl.BlockSpec((lc, LANE), lambda b, i: (b, 0)),
                     pl.BlockSpec((lc, BRANCH_W), lambda b, i: (b, 0))]
        args += [knc, kpc, vvc]
    return pl.pallas_call(
        functools.partial(_attn_kernel, has_ctx=has_ctx),
        out_shape=jax.ShapeDtypeStruct((nseq * seq, BRANCH_W), bf16),
        grid=(nseq, nq),
        in_specs=in_specs,
        out_specs=pl.BlockSpec((tq, BRANCH_W), lambda b, i: (b * nq + i, 0)),
        compiler_params=_cparams(("parallel", "parallel")),
        name="mla_attn",
    )(*args)


def _merge_kernel(o0_ref, o1_ref, o2_ref, o3_ref, g0_ref, g1_ref, g2_ref, g3_ref, bg_ref, wb_ref, wo_ref,
                  x_ref, mod_ref, lng_ref, lnb_ref, out_ref, *, alpha):
    m = None
    for k, (o_ref, g_ref) in enumerate(((o0_ref, g0_ref), (o1_ref, g1_ref), (o2_ref, g2_ref), (o3_ref, g3_ref))):
        term = jax.nn.sigmoid(g_ref[...] + bg_ref[k:k + 1, :]) * _dot(o_ref[...], wb_ref[k])
        m = term if m is None else m + term
    mix = _dot(m.astype(bf16), wo_ref[...])
    gate1 = mod_ref[2:3, :]
    out_ref[...] = _layernorm(alpha * x_ref[...] + gate1 * mix, lng_ref[...], lnb_ref[...])


def _merge(branches, p, bg, wb, wo, x, mod, lng, lnb, *, alpha, t_ctx, l_lat):
    t, d = x.shape
    tm = 256
    midx = _mod_index(tm, t_ctx, l_lat)
    ospec = pl.BlockSpec((tm, BRANCH_W), lambda i: (i, 0))
    gspecs = [pl.BlockSpec((tm, d), functools.partial(lambda i, k: (i, OFF_GATES // d + k), k=k)) for k in range(4)]
    single = dict(pipeline_mode=pl.Buffered(1))
    return pl.pallas_call(
        functools.partial(_merge_kernel, alpha=alpha),
        out_shape=jax.ShapeDtypeStruct((t, d), f32),
        grid=(t // tm,),
        in_specs=[ospec, ospec, ospec, ospec] + gspecs + [
            pl.BlockSpec((4, d), lambda i: (0, 0)),
            pl.BlockSpec((4, BRANCH_W, d), lambda i: (0, 0, 0), **single),
            pl.BlockSpec((d, d), lambda i: (0, 0), **single),
            pl.BlockSpec((tm, d), lambda i: (i, 0)),
            pl.BlockSpec((None, 6, d), lambda i: (midx(i), 0, 0)),
            pl.BlockSpec((1, d), lambda i: (0, 0)),
            pl.BlockSpec((1, d), lambda i: (0, 0))],
        out_specs=pl.BlockSpec((tm, d), lambda i: (i, 0)),
        compiler_params=_cparams(("parallel",)),
        name="merge_out_ln",
    )(*branches, p, p, p, p, bg, wb, wo, x, mod, lng, lnb)


def _ffn_kernel(x_ref, mod_ref, w1_ref, w3_ref, w2_ref, lng_ref, lnb_ref, out_ref, hb_ref, acc_ref, *, alpha):
    f = pl.program_id(1)

    @pl.when(f == 0)
    def _():
        shift = mod_ref[3:4, :]
        scale = mod_ref[4:5, :]
        hb_ref[...] = (x_ref[...] * (1.0 + scale) + shift).astype(bf16)
        acc_ref[...] = jnp.zeros_like(acc_ref)

    h = hb_ref[...]
    g = (jax.nn.silu(_dot(h, w1_ref[...])) * _dot(h, w3_ref[...])).astype(bf16)
    acc_ref[...] += _dot(g, w2_ref[...])

    @pl.when(f == pl.num_programs(1) - 1)
    def _():
        gate2 = mod_ref[5:6, :]
        out_ref[...] = _layernorm(alpha * x_ref[...] + gate2 * acc_ref[...], lng_ref[...], lnb_ref[...])


def _ffn(x, mod, w1, w3, w2, lng, lnb, *, alpha, t_ctx, l_lat):
    t, d = x.shape
    dff = w1.shape[1]
    tm, tf = 512, 512
    midx = _mod_index(tm, t_ctx, l_lat)
    return pl.pallas_call(
        functools.partial(_ffn_kernel, alpha=alpha),
        out_shape=jax.ShapeDtypeStruct((t, d), f32),
        grid=(t // tm, dff // tf),
        in_specs=[pl.BlockSpec((tm, d), lambda i, f: (i, 0)),
                  pl.BlockSpec((None, 6, d), lambda i, f: (midx(i), 0, 0)),
                  pl.BlockSpec((d, tf), lambda i, f: (0, f)),
                  pl.BlockSpec((d, tf), lambda i, f: (0, f)),
                  pl.BlockSpec((tf, d), lambda i, f: (f, 0)),
                  pl.BlockSpec((1, d), lambda i, f: (0, 0)),
                  pl.BlockSpec((1, d), lambda i, f: (0, 0))],
        out_specs=pl.BlockSpec((tm, d), lambda i, f: (i, 0)),
        scratch_shapes=[pltpu.VMEM((tm, d), bf16), pltpu.VMEM((tm, d), f32)],
        compiler_params=_cparams(("parallel", "arbitrary")),
        name="ffn_ln",
    )(x, mod, w1, w3, w2, lng, lnb)


def _cast_kernel(x_ref, o_ref):
    o_ref[...] = x_ref[...].astype(bf16)


def _cast_layer(w, layer):
    _, rows, cols = w.shape
    tr = 256 if cols > 4096 else 512
    return pl.pallas_call(
        _cast_kernel,
        out_shape=jax.ShapeDtypeStruct((rows, cols), bf16),
        grid=(rows // tr,),
        in_specs=[pl.BlockSpec((None, tr, cols), lambda i: (layer, i, 0))],
        out_specs=pl.BlockSpec((tr, cols), lambda i: (i, 0)),
        compiler_params=_cparams(("parallel",)),
        name="cast_bf16",
    )(w)


_ROW_UNIT = 16


def _relayout_plan():
    plain = lambda src: (0, src)
    special = lambda k: (1, k)
    segs = [plain(_SRC[n]) for n in ('gdn_q', 'gdn_k', 'gdn_v', 'gdn_z')]
    segs += [special(0), special(1)]
    segs += [plain(_SRC['gla_v']), plain(_SRC['gla_r']), plain(_SRC['hg_q']), plain(_SRC['hg_f']),
             plain(_SRC['hg_f'] + 512), plain(_SRC['hg_i']), plain(_SRC['hg_g']), plain(_SRC['mla_qa']),
             plain(_SRC['mla_kva'])]
    segs += [special(2)]
    segs += [plain(_SRC['gates'] + _SEG * i) for i in range(N_HEADS * D_MODEL // _SEG)]
    assert len(segs) == N_PROJ // _SEG
    table = np.zeros((3, len(segs)), np.int32)
    row, blk = segs[0][1], 0
    for j, (is_special, val) in enumerate(segs):
        if is_special:
            blk = val
        else:
            row = val
        assert row % _ROW_UNIT == 0
        table[:, j] = (is_special, row // _ROW_UNIT, blk)
    return table


def _special_rows(wt):
    d = wt.shape[1]

    def rows(name, start, n):
        s = _SRC[name] + start
        return wt[s:s + n]

    def head_padded(name):
        parts = []
        for h in range(N_HEADS):
            parts += [rows(name, h * GLA_DK, GLA_DK), jnp.zeros((HEAD_W - GLA_DK, d), wt.dtype)]
        return parts

    q4 = MLA_ROPE // 4
    small = [rows('mla_kpe', 0, MLA_ROPE), rows('mla_kpe', q4, q4), rows('mla_kpe', 0, q4),
             rows('mla_kpe', 3 * q4, q4), rows('mla_kpe', 2 * q4, q4), rows('gla_g', 0, 2 * GLA_RANK),
             rows('gdn_b', 0, 8), rows('gdn_a', 0, 8), jnp.zeros((_SEG - SM_GDN_A - 8, d), wt.dtype)]
    return jnp.concatenate(head_padded('gla_q') + head_padded('gla_k') + small, axis=0)


def _relayout_kernel(tab_ref, wt_ref, sp_ref, o_ref):
    is_special = tab_ref[0, pl.program_id(1)]

    @pl.when(is_special == 0)
    def _():
        o_ref[...] = wt_ref[0].astype(bf16)

    @pl.when(is_special == 1)
    def _():
        o_ref[...] = sp_ref[...].astype(bf16)


def _relayout_w_in(w_in):
    depth, d, _ = w_in.shape
    wt = jnp.swapaxes(w_in, 1, 2)
    special = jnp.stack([_special_rows(wt[l]) for l in range(depth)])
    table = _relayout_plan()
    nseg = N_PROJ // _SEG
    return pl.pallas_call(
        _relayout_kernel,
        out_shape=jax.ShapeDtypeStruct((depth, N_PROJ, d), bf16),
        grid_spec=pltpu.PrefetchScalarGridSpec(
            num_scalar_prefetch=1,
            grid=(depth, nseg),
            in_specs=[pl.BlockSpec((pl.Element(1), pl.Element(_SEG), pl.Element(d)),
                                   lambda l, j, tab: (l, tab[1, j] * _ROW_UNIT, 0)),
                      pl.BlockSpec((None, _SEG, d), lambda l, j, tab: (l, tab[2, j], 0))],
            out_specs=pl.BlockSpec((None, _SEG, d), lambda l, j, tab: (l, j, 0))),
        compiler_params=_cparams(("parallel", "parallel")),
        name="w_in_relayout",
    )(jnp.asarray(table), wt, special)


def _relayout_wq(wq):
    hw = MLA_NOPE + MLA_ROPE
    z = jnp.zeros((wq.shape[0], HEAD_W - MLA_ROPE), wq.dtype)
    q4 = MLA_ROPE // 4
    nope, pe, pes = [], [], []
    for h in range(N_HEADS):
        base = h * hw
        nope.append(wq[:, base:base + MLA_NOPE])
        r = wq[:, base + MLA_NOPE:base + hw]
        pe += [r, z]
        pes += [r[:, q4:2 * q4], r[:, 0:q4], r[:, 3 * q4:], r[:, 2 * q4:3 * q4], z]
    return jnp.concatenate(nope + pe, axis=1).astype(bf16), jnp.concatenate(pes, axis=1).astype(bf16)


def _relayout_wkv(wkv):
    hw = MLA_NOPE + HEAD_W
    kn = [wkv[:, h * hw:h * hw + MLA_NOPE] for h in range(N_HEADS)]
    vv = [wkv[:, h * hw + MLA_NOPE:(h + 1) * hw] for h in range(N_HEADS)]
    return jnp.concatenate(kn + vv, axis=1).astype(bf16)


def _rope_tables(length):
    pos = jnp.arange(length)
    row_id = (pos // GRID_W).astype(f32)
    col_id = (pos % GRID_W).astype(f32)
    half = MLA_ROPE // 2
    inv = ROPE_BASE ** (-jnp.arange(0, half, 2, dtype=f32) / half)
    ar, ac = row_id[:, None] * inv, col_id[:, None] * inv
    z = jnp.zeros((length, LANE - MLA_ROPE), f32)
    cos = jnp.concatenate([jnp.cos(ar), jnp.cos(ar), jnp.cos(ac), jnp.cos(ac), z], axis=1)
    sin = jnp.concatenate([-jnp.sin(ar), jnp.sin(ar), -jnp.sin(ac), jnp.sin(ac), z], axis=1)
    return cos, sin


def _gla_gate_weights(w2, b):
    lane_pad = ((0, 0), (0, 0), (0, 0), (0, HEAD_W - GLA_DK))
    w4 = jnp.pad(w2.reshape(2, GLA_RANK, N_HEADS, GLA_DK), lane_pad).reshape(2, GLA_RANK, BRANCH_W)
    wbig = jnp.stack([jnp.pad(w4[d], ((SM_GLA_G + d * GLA_RANK, _SEG - SM_GLA_G - (d + 1) * GLA_RANK), (0, 0)))
                      for d in range(2)])
    bbig = jnp.pad(b.reshape(2, 1, N_HEADS, GLA_DK), lane_pad).reshape(2, 1, BRANCH_W)
    return wbig.astype(bf16), bbig


def kernel(x_prompt, x_sample, c, state_gdn, state_gla, state_hgrn, cache_mla_ckv, cache_mla_kpe, c_ctx, w_ada, b_ada, w_in, gdn_conv, gdn_a_log, gdn_dt_bias, gdn_norm, gla_gate_w2, gla_gate_b, gla_norm, hgrn_lb, hgrn_norm, mla_q_norm, mla_wq_b, mla_kv_norm, mla_wkv_b, w_branch, b_gates, w_out, ln1_g, ln1_b, ln2_g, ln2_b, ffn_w1, ffn_w3, ffn_w2):
    nb_c, l_c, d = x_prompt.shape
    nb_l, l_l, _ = x_sample.shape
    depth = w_in.shape[0]
    t_c, t_l = nb_c * l_c, nb_l * l_l
    past = cache_mla_ckv.shape[2]
    alpha = (2.0 * depth) ** 0.25
    assert d == D_MODEL and t_c % 1024 == 0 and l_l % 1024 == 0 and l_c % CHUNK == 0

    n_cond = 1 + nb_l
    cc = jnp.concatenate([c_ctx[None, :], c, jnp.zeros((-n_cond % 8, d), f32)], axis=0)
    mods = _ada(cc, w_ada, b_ada).reshape(depth, cc.shape[0], 6, d)

    w_in_r = _relayout_w_in(w_in)
    mexp_np, lvl_np = _scan_consts()
    mexp = jnp.asarray(mexp_np, bf16)
    lvl = jnp.asarray(lvl_np, f32)
    cos_t, sin_t = _rope_tables(l_l)
    cum = jnp.cumsum(jax.nn.softmax(hgrn_lb.astype(f32), axis=0), axis=0)
    lower_bounds = cum - cum[:1]

    x = jnp.concatenate([x_prompt.reshape(t_c, d), x_sample.reshape(t_l, d)], axis=0)
    t = t_c + t_l
    zero_state = jnp.zeros((nb_c, N_HEADS, 2, HEAD_W, HEAD_W), f32)
    streams = (dict(nseq=nb_c, seq=l_c, row0=0), dict(nseq=nb_l, seq=l_l, row0=t_c))
    new_gdn, new_gla, new_hg, new_ckv, new_kpe = [], [], [], [], []

    for l in range(depth):
        mod = mods[l]
        p = _inproj(x, mod, w_in_r, l, t_c, l_l)

        gsm = p[:, OFF_SMALL + SM_GDN_B:OFF_SMALL + SM_GDN_B + 16].reshape(t // GROUP, GROUP, 4, N_HEADS)
        grow = jnp.pad(gsm.transpose(3, 0, 2, 1), ((0, 0), (0, 0), (0, 4), (0, 0)))
        zc = jnp.zeros((N_HEADS, 2), f32)
        hp = jnp.stack([jnp.concatenate([zc, gdn_a_log[l].T, zc, zc], axis=1),
                        jnp.concatenate([zc, gdn_dt_bias[l].T, zc, zc], axis=1)], axis=-1)

        la_f, la_b = _gla_prep(p, *_gla_gate_weights(gla_gate_w2[l], gla_gate_b[l]))
        hla_f, hla_b, hk_f, hk_b = _hgrn_prep(p, lower_bounds[l])
        wqa, wqb = _relayout_wq(mla_wq_b[l])
        wkv = _relayout_wkv(mla_wkv_b[l])
        qnw, kvnw = mla_q_norm[l][None, :], mla_kv_norm[l][None, :]

        outs = {k: [] for k in ('gdn', 'gla', 'hg', 'mla')}
        for si, st in enumerate(streams):
            nseq, seq, row0 = st['nseq'], st['seq'], st['row0']
            rb = row0 // seq
            if si == 0:
                s_gdn0 = s_gla0 = s_hg0 = zero_state
            else:
                s_gdn0 = state_gdn[:, l].transpose(0, 2, 1, 3, 4)
                s_gla0 = jnp.pad(state_gla[:, l], ((0, 0),) * 3 + ((0, HEAD_W - GLA_DK), (0, 0))).transpose(0, 2, 1, 4, 3)
                s_hg0 = state_hgrn[:, l].transpose(0, 2, 1, 4, 3)
            o_gdn, s_gdn = _gdn(p, gdn_conv[l], grow, hp, gdn_norm[l][None, :], s_gdn0, nseq=nseq, seq=seq, row_blk0=rb)
            o_gla, s_gla = _scan((p, OFF_GLA_Q, rb), (p, OFF_GLA_K, rb), (p, OFF_GLA_K, rb), (p, OFF_GLA_V, rb),
                                 la_f, la_b, (p, OFF_GLA_R, rb), gla_norm[l][None, :], s_gla0, mexp, lvl,
                                 nseq=nseq, seq=seq, row_blk0=rb, q_silu=False, q_scale=GLA_DK ** -0.5, gate_silu=True)
            o_hg, s_hg = _scan((p, OFF_HG_Q, rb), (hk_f, 0, rb), (hk_b, 0, rb), (p, OFF_HG_I, rb),
                               hla_f, hla_b, (p, OFF_HG_G, rb), hgrn_norm[l][None, :], s_hg0, mexp, lvl,
                               nseq=nseq, seq=seq, row_blk0=rb, q_silu=True, q_scale=1.0, gate_silu=False)
            rope = si == 1
            qn, qp, ckv, kn, vv, kp = _mla_proj(p, qnw, kvnw, wqa, wkv, wqb if rope else None,
                                                cos_t if rope else None, sin_t if rope else None,
                                                nrows=nseq * seq, seq=seq, row0=row0)
            ctx = None
            if si == 1:
                knc, vvc = _kv_proj(cache_mla_ckv[:, l].reshape(nb_l * past, -1), wkv)
                kpc = jnp.pad(cache_mla_kpe[:, l].reshape(nb_l * past, MLA_ROPE), ((0, 0), (0, LANE - MLA_ROPE))).astype(bf16)
                ctx = (knc, kpc, vvc)
            o_mla = _attention(qn, qp, kn, kp, vv, ctx, nseq=nseq, seq=seq)
            outs['gdn'].append(o_gdn)
            outs['gla'].append(o_gla)
            outs['hg'].append(o_hg)
            outs['mla'].append(o_mla)
            if si == 0:
                new_gdn.append(s_gdn.transpose(0, 2, 1, 3, 4))
                new_gla.append(s_gla.transpose(0, 2, 1, 4, 3)[:, :, :, :GLA_DK, :])
                new_hg.append(s_hg.transpose(0, 2, 1, 4, 3))
                new_ckv.append(ckv.reshape(nb_c, l_c, -1))
                new_kpe.append(p[:t_c, OFF_SMALL + SM_KPE:OFF_SMALL + SM_KPE + MLA_ROPE].reshape(nb_c, l_c, MLA_ROPE))

        branches = [jnp.concatenate(outs[k], axis=0) for k in ('gdn', 'gla', 'hg', 'mla')]
        wb = _cast_layer(w_branch.reshape(depth, 4 * BRANCH_W, d), l).reshape(4, BRANCH_W, d)
        x1 = _merge(branches, p, b_gates[l], wb, _cast_layer(w_out, l), x, mod,
                    ln1_g[l][None, :], ln1_b[l][None, :], alpha=alpha, t_ctx=t_c, l_lat=l_l)
        x = _ffn(x1, mod, _cast_layer(ffn_w1, l), _cast_layer(ffn_w3, l), _cast_layer(ffn_w2, l),
                 ln2_g[l][None, :], ln2_b[l][None, :], alpha=alpha, t_ctx=t_c, l_lat=l_l)

    sdt = x_prompt.dtype
    return (x[:t_c].reshape(nb_c, l_c, d), x[t_c:].reshape(nb_l, l_l, d),
            jnp.stack(new_gdn, axis=1).astype(sdt), jnp.stack(new_gla, axis=1).astype(sdt),
            jnp.stack(new_hg, axis=1).astype(sdt), jnp.stack(new_ckv, axis=1), jnp.stack(new_kpe, axis=1))
```

```python
import functools

import numpy as np
import jax
import jax.numpy as jnp
from jax import lax
from jax.experimental import pallas as pl
from jax.experimental.pallas import tpu as pltpu

f32 = jnp.float32
bf16 = jnp.bfloat16

D_MODEL = 2048
N_HEADS = 4
HEAD_W = 128
BRANCH_W = N_HEADS * HEAD_W
GLA_DK = 64
GLA_RANK = 16
GLA_TAU = 16.0
GDN_CONV = 5
MLA_NOPE = 128
MLA_ROPE = 64
ROPE_BASE = 10000.0
GRID_W = 64
NORM_EPS = 1e-6
CHUNK = 64
SUB = 16
LANE = 128
VMEM_LIMIT = 56 * 1024 * 1024

_SEG = 512
OFF_GDN_Q, OFF_GDN_K, OFF_GDN_V, OFF_GDN_Z = 0, 512, 1024, 1536
OFF_GLA_Q, OFF_GLA_K, OFF_GLA_V, OFF_GLA_R = 2048, 2560, 3072, 3584
OFF_HG_Q, OFF_HG_FF, OFF_HG_FB, OFF_HG_I, OFF_HG_G = 4096, 4608, 5120, 5632, 6144
OFF_MLA_QA, OFF_MLA_KVA, OFF_SMALL, OFF_GATES = 6656, 7168, 7680, 8192
N_PROJ = OFF_GATES + N_HEADS * D_MODEL
SM_KPE, SM_KPE_SW, SM_GLA_G, SM_GDN_B, SM_GDN_A = 0, 64, 128, 160, 168

_SRC = {}
_o = 0
for _n, _w in (('gdn_q', 512), ('gdn_k', 512), ('gdn_v', 512), ('gdn_z', 512), ('gdn_b', 8), ('gdn_a', 8),
               ('gla_q', 256), ('gla_k', 256), ('gla_v', 512), ('gla_r', 512), ('gla_g', 32),
               ('hg_q', 512), ('hg_f', 1024), ('hg_i', 512), ('hg_g', 512),
               ('mla_qa', 512), ('mla_kva', 512), ('mla_kpe', 64), ('gates', 4 * D_MODEL)):
    _SRC[_n] = _o
    _o += _w
IN_WIDTH = _o


def _cparams(sem):
    return pltpu.CompilerParams(dimension_semantics=sem, vmem_limit_bytes=VMEM_LIMIT)


def _dot(a, b):
    return jnp.dot(a, b, preferred_element_type=f32)


def _dot_nt(a, b):
    return lax.dot_general(a, b, (((1,), (1,)), ((), ())), preferred_element_type=f32)


def _dot_tn(a, b):
    return lax.dot_general(a, b, (((0,), (0,)), ((), ())), preferred_element_type=f32)


def _split2(x):
    hi = x.astype(bf16)
    lo = (x - hi.astype(f32)).astype(bf16)
    return hi, lo


def _mm3(a, b):
    ah, al = _split2(a)
    bh, bl = _split2(b)
    return _dot(ah, bh) + (_dot(ah, bl) + _dot(al, bh))


def _layernorm(y, g, b):
    mu = jnp.mean(y, axis=-1, keepdims=True)
    yc = y - mu
    var = jnp.mean(yc * yc, axis=-1, keepdims=True)
    return yc * lax.rsqrt(var + NORM_EPS) * g + b


def _rmsnorm(y, w):
    return y * lax.rsqrt(jnp.mean(y * y, axis=-1, keepdims=True) + NORM_EPS) * w


def _ada_kernel(c_ref, w_ref, b_ref, o_ref):
    cs = jax.nn.silu(c_ref[...]).astype(bf16)
    o_ref[...] = _dot(cs, w_ref[...].astype(bf16)) + b_ref[...]


def _ada(cc, w_ada, b_ada):
    depth, d, n6 = w_ada.shape
    rows = cc.shape[0]
    tn = 1024
    return pl.pallas_call(
        _ada_kernel,
        out_shape=jax.ShapeDtypeStruct((depth, rows, n6), f32),
        grid=(depth, n6 // tn),
        in_specs=[pl.BlockSpec((rows, d), lambda l, j: (0, 0)),
                  pl.BlockSpec((None, d, tn), lambda l, j: (l, 0, j)),
                  pl.BlockSpec((None, 1, tn), lambda l, j: (l, 0, j))],
        out_specs=pl.BlockSpec((None, rows, tn), lambda l, j: (l, 0, j)),
        compiler_params=_cparams(("parallel", "parallel")),
        name="ada_mod",
    )(cc, w_ada, b_ada.reshape(depth, 1, n6))


def _inproj_kernel(x_ref, mod_ref, w_ref, o_ref, xb_ref):
    @pl.when(pl.program_id(1) == 0)
    def _():
        shift = mod_ref[0:1, :]
        scale = mod_ref[1:2, :]
        xb_ref[...] = (x_ref[...] * (1.0 + scale) + shift).astype(bf16)

    o_ref[...] = _dot_nt(xb_ref[...], w_ref[...])


def _mod_index(tm, t_ctx, l_lat):
    def index(i):
        r = i * tm
        return jnp.where(r < t_ctx, 0, 1 + (r - t_ctx) // l_lat)
    return index


def _inproj(x, mod, w, layer, t_ctx, l_lat):
    t, d = x.shape
    n = w.shape[1]
    tm, tn = 1024, 1024
    midx = _mod_index(tm, t_ctx, l_lat)
    return pl.pallas_call(
        _inproj_kernel,
        out_shape=jax.ShapeDtypeStruct((t, n), f32),
        grid=(t // tm, n // tn),
        in_specs=[pl.BlockSpec((tm, d), lambda i, j: (i, 0)),
                  pl.BlockSpec((None, 6, d), lambda i, j: (midx(i), 0, 0)),
                  pl.BlockSpec((None, tn, d), lambda i, j: (layer, j, 0))],
        out_specs=pl.BlockSpec((tm, tn), lambda i, j: (i, j)),
        scratch_shapes=[pltpu.VMEM((tm, d), bf16)],
        compiler_params=_cparams(("parallel", "arbitrary")),
        name="in_proj",
    )(x, mod, w)


_PACK = 4
GROUP = _PACK * CHUNK
_GDN_HEADS = 2


def _gdn_kernel(q_ref, k_ref, v_ref, z_ref, cwq_ref, cwk_ref, cwv_ref, grow_ref, hp_ref, nw_ref, s0_ref,
                o_ref, so_ref, pad_ref, qs, ks, vs, gs, ub_s, wq_s, aq_s, kd_s, ct_s, of, ob, s_ref, *, seq):
    n = seq // CHUNK
    width = _GDN_HEADS * HEAD_W
    chains = [(hd, d) for hd in range(_GDN_HEADS) for d in range(2)]

    def conv_silu(x_ref, w_ref):
        pad_ref[0:8, :] = jnp.zeros((8, width), f32)
        pad_ref[8 + seq:16 + seq, :] = jnp.zeros((8, width), f32)
        pad_ref[8:8 + seq, :] = x_ref[...]
        acc = pad_ref[pl.ds(6, seq), :] * w_ref[0:1, :]
        for j in range(1, GDN_CONV):
            acc = acc + pad_ref[pl.ds(6 + j, seq), :] * w_ref[j:j + 1, :]
        return jax.nn.silu(acc)

    def l2norm_heads(x):
        parts = []
        for hd in range(_GDN_HEADS):
            xh = x[:, hd * HEAD_W:(hd + 1) * HEAD_W]
            parts.append(xh * lax.rsqrt(jnp.sum(xh * xh, axis=-1, keepdims=True) + NORM_EPS))
        return jnp.concatenate(parts, axis=1)

    qs[...] = l2norm_heads(conv_silu(q_ref, cwq_ref)) * (HEAD_W ** -0.5)
    ks[...] = l2norm_heads(conv_silu(k_ref, cwk_ref))
    vs[...] = conv_silu(v_ref, cwv_ref)

    x = grow_ref[...]
    a_log = hp_ref[:, :, 0:1][:, None]
    dt_b = hp_ref[:, :, 1:2][:, None]
    rows = lax.broadcasted_iota(jnp.int32, x.shape, 2)
    gs[...] = jnp.where(rows < 2, jax.nn.sigmoid(x), -jnp.exp(a_log) * jax.nn.softplus(x + dt_b))
    s_ref[...] = s0_ref[...].reshape(2 * _GDN_HEADS, HEAD_W, HEAD_W)

    ri = lax.broadcasted_iota(jnp.int32, (CHUNK, GROUP), 0)
    li = lax.broadcasted_iota(jnp.int32, (CHUNK, GROUP), 1)
    cj = li % CHUNK
    blk = [(li // CHUNK) == r for r in range(_PACK)]
    blk_bf = [jnp.where(b, 1.0, 0.0).astype(bf16) for b in blk]
    l1 = lax.broadcasted_iota(jnp.int32, (1, GROUP), 1) // CHUNK
    eye_b = ri == cj
    eye = jnp.where(eye_b, 1.0, 0.0).astype(f32)
    bd16 = (ri // SUB) == (cj // SUB)
    bd32 = (ri // (2 * SUB)) == (cj // (2 * SUB))
    r2 = lax.broadcasted_iota(jnp.int32, (GROUP, GROUP), 0)
    c2 = lax.broadcasted_iota(jnp.int32, (GROUP, GROUP), 1)
    same_blk = (r2 // CHUNK) == (c2 // CHUNK)

    def bdiag(yb):
        return jnp.concatenate([yb * mk for mk in blk_bf], axis=0)

    def unpack_diag(xf):
        out = xf[0:CHUNK]
        for r in range(1, _PACK):
            out = jnp.where(blk[r], xf[r * CHUNK:(r + 1) * CHUNK], out)
        return out

    def pmm3(pairs):
        sp = [(_split2(a), _split2(b)) for a, b in pairs]
        r1 = [_dot(jnp.concatenate([ah, al], axis=0), bdiag(bh)) for (ah, al), (bh, _) in sp]
        r2 = [_dot(ah, bdiag(bl)) for (ah, _), (_, bl) in sp]
        return [x1[:CHUNK] + x1[CHUNK:] + x2 for x1, x2 in zip(r1, r2)]

    def unit_tri_inverse(ms):
        k = len(ms)
        m32 = [jnp.where(bd32, m, 0.0) for m in ms]
        dg = [jnp.where(bd16, m, 0.0) for m in ms]
        d2 = pmm3([(a, a) for a in dg])
        both = pmm3([(a, a) for a in d2] + [(eye - a, eye + b) for a, b in zip(dg, d2)])
        d4, xi = both[:k], both[k:]
        both = pmm3([(a, a) for a in d4] + [(a, eye + b) for a, b in zip(xi, d4)])
        d8, xi = both[:k], both[k:]
        xi = pmm3([(a, eye + b) for a, b in zip(xi, d8)])
        for lo, hi in ((dg, m32), (m32, ms)):
            t = pmm3([(a, h_ - l_) for a, h_, l_ in zip(xi, hi, lo)])
            t = pmm3(list(zip(t, xi)))
            xi = [a - b for a, b in zip(xi, t)]
        return xi

    def block_cols(xp):
        return [jnp.sum(jnp.where(blk[r], xp, 0.0), axis=1, keepdims=True) for r in range(_PACK)]

    def spread(cols):
        out = jnp.broadcast_to(cols[0], (CHUNK, GROUP))
        for r in range(1, _PACK):
            out = jnp.where(blk[r], cols[r], out)
        return out

    def prepare_group(g, carry):
        rows_g = pl.ds(pl.multiple_of(g * GROUP, GROUP), GROUP)
        heads = range(_GDN_HEADS)
        k4 = [ks[rows_g, hd * HEAD_W:(hd + 1) * HEAD_W] for hd in heads]
        q4 = [qs[rows_g, hd * HEAD_W:(hd + 1) * HEAD_W] for hd in heads]
        v4 = [vs[rows_g, hd * HEAD_W:(hd + 1) * HEAD_W] for hd in heads]
        k4b = [x.astype(bf16) for x in k4]
        kq = [_dot_nt(jnp.concatenate([k4b[hd], q4[hd].astype(bf16)], axis=0), k4b[hd]) for hd in heads]
        kk_p = [unpack_diag(x[:GROUP]) for x in kq]
        qk_p = [unpack_diag(x[GROUP:]) for x in kq]
        g8 = [gs[hd, g] for hd in heads]
        ids = range(len(chains))
        beta_r = [g8[hd][d:d + 1, :] for hd, d in chains]
        g_r = [g8[hd][2 + d:3 + d, :] for hd, d in chains]
        tri2 = [cj <= ri, cj >= ri]
        tri = [tri2[d] for _, d in chains]
        strict = [jnp.logical_and(t, jnp.logical_not(eye_b)) for t in tri]
        tg = [jnp.where(tri[i], g_r[i], 0.0) for i in ids]
        gam_cols = [block_cols(tg[i]) for i in ids]
        beta_cols = [block_cols(jnp.where(eye_b, beta_r[i], 0.0)) for i in ids]
        tot_cols = [[jnp.sum(jnp.where(l1 == r, g_r[i], 0.0), axis=1, keepdims=True) for r in range(_PACK)]
                    for i in ids]
        strict2 = [(c2 % CHUNK) < (r2 % CHUNK), (c2 % CHUNK) > (r2 % CHUNK)]
        sm = [jnp.where(jnp.logical_and(same_blk, s_), 1.0, 0.0).astype(bf16) for s_ in strict2]
        r3 = []
        for i, (_, d) in enumerate(chains):
            th, tl = _split2(tg[i])
            tl2 = (tg[i] - th.astype(f32) - tl.astype(f32)).astype(bf16)
            r3.append(_dot(jnp.concatenate([th, tl, tl2], axis=0), sm[d]))
        dlt = [x[:CHUNK] + x[CHUNK:2 * CHUNK] + x[2 * CHUNK:] for x in r3]
        decay = [jnp.where(tri[i], jnp.exp(jnp.minimum(dlt[i], 0.0)), 0.0) for i in ids]
        t_inv = unit_tri_inverse([jnp.where(strict[i], kk_p[hd] * spread(beta_cols[i]) * decay[i], 0.0)
                                  for i, (hd, _) in enumerate(chains)])
        gam = [jnp.concatenate(gam_cols[i], axis=0) for i in ids]
        beta = [jnp.concatenate(beta_cols[i], axis=0) for i in ids]
        tot = [jnp.concatenate([jnp.broadcast_to(t, (CHUNK, 1)) for t in tot_cols[i]], axis=0) for i in ids]
        eg = [jnp.exp(x) for x in gam]
        rhs = [jnp.concatenate([v4[hd] * beta[i], k4[hd] * beta[i] * eg[i]], axis=1)
               for i, (hd, _) in enumerate(chains)]
        isp = [_split2(x) for x in t_inv]
        rsp = [_split2(x) for x in rhs]
        bih = [bdiag(isp[i][0]) for i in ids]
        s1 = [_dot(jnp.concatenate([bih[i], bdiag(isp[i][1])], axis=0), rsp[i][0]) for i in ids]
        s2 = [_dot(bih[i], rsp[i][1]) for i in ids]
        for i, (hd, d) in enumerate(chains):
            sol = s1[i][:GROUP] + s1[i][GROUP:] + s2[i]
            ub_s[i, rows_g, :] = sol[:, :HEAD_W]
            w_b = sol[:, HEAD_W:].astype(bf16)
            qd_b = (q4[hd] * eg[i]).astype(bf16)
            a_p = qk_p[hd] * decay[i]
            for r_ in range(_PACK):
                c = g * _PACK + r_
                wq_s[i, c, 0:CHUNK, :] = w_b[r_ * CHUNK:(r_ + 1) * CHUNK]
                wq_s[i, c, CHUNK:2 * CHUNK, :] = qd_b[r_ * CHUNK:(r_ + 1) * CHUNK]
                aq_s[i, c] = a_p[:, r_ * CHUNK:(r_ + 1) * CHUNK].astype(bf16)
                ct_s[i, c] = jnp.broadcast_to(jnp.exp(tot_cols[i][r_]), (8, HEAD_W))
            kd_s[i, rows_g, :] = (k4[hd] * jnp.exp(tot[i] - gam[i])).astype(bf16)
        return carry

    lax.fori_loop(0, n // _PACK, prepare_group, 0)

    def step(t, carry):
        ids = range(len(chains))
        cs = [t if d == 0 else n - 1 - t for _, d in chains]
        rows = [pl.ds(pl.multiple_of(c * CHUNK, CHUNK), CHUNK) for c in cs]
        s = [s_ref[i] for i in ids]
        sb = [x.astype(bf16) for x in s]
        r = [_dot(wq_s[i, cs[i]], sb[i]) for i in ids]
        u = [(ub_s[i, rows[i], :] - r[i][:CHUNK]).astype(bf16) for i in ids]
        o_c = [r[i][CHUNK:] + _dot(aq_s[i, cs[i]], u[i]) for i in ids]
        s_new = [ct_s[i, cs[i]][0:1, :] * s[i] + _dot_tn(kd_s[i, rows[i], :], u[i]) for i in ids]
        for i, (hd, d) in enumerate(chains):
            (of if d == 0 else ob)[rows[i], hd * HEAD_W:(hd + 1) * HEAD_W] = o_c[i]
            s_ref[i] = s_new[i]
        return carry

    lax.fori_loop(0, n, step, 0)
    o = of[...] + ob[...]
    z = z_ref[...]
    nw = nw_ref[...]
    o_ref[...] = jnp.concatenate(
        [_rmsnorm(o[:, hd * HEAD_W:(hd + 1) * HEAD_W], nw) * jax.nn.silu(z[:, hd * HEAD_W:(hd + 1) * HEAD_W])
         for hd in range(_GDN_HEADS)], axis=1).astype(bf16)
    so_ref[...] = s_ref[...].reshape(_GDN_HEADS, 2, HEAD_W, HEAD_W)


def _gdn(p, conv_w, grow, hp, norm_w, s0, *, nseq, seq, row_blk0):
    n = seq // CHUNK
    ng = seq // GROUP
    hh = _GDN_HEADS
    width = hh * HEAD_W
    nch = 2 * hh

    def pcol(off):
        return pl.BlockSpec((seq, width), lambda b, h: (row_blk0 + b, off // width + h))

    def wcol(off):
        return pl.BlockSpec((GDN_CONV, width), lambda b, h: (0, off // width + h))

    kern = functools.partial(_gdn_kernel, seq=seq)
    return pl.pallas_call(
        kern,
        out_shape=(jax.ShapeDtypeStruct((nseq * seq, BRANCH_W), bf16),
                   jax.ShapeDtypeStruct((nseq, N_HEADS, 2, HEAD_W, HEAD_W), f32)),
        grid=(nseq, N_HEADS // hh),
        in_specs=[pcol(OFF_GDN_Q), pcol(OFF_GDN_K), pcol(OFF_GDN_V), pcol(OFF_GDN_Z),
                  wcol(0), wcol(BRANCH_W), wcol(2 * BRANCH_W),
                  pl.BlockSpec((hh, ng, 8, GROUP), lambda b, h: (h, row_blk0 + b, 0, 0)),
                  pl.BlockSpec((hh, 8, 2), lambda b, h: (h, 0, 0)),
                  pl.BlockSpec((1, HEAD_W), lambda b, h: (0, 0)),
                  pl.BlockSpec((None, hh, 2, HEAD_W, HEAD_W), lambda b, h: (b, h, 0, 0, 0))],
        out_specs=(pl.BlockSpec((seq, width), lambda b, h: (b, h)),
                   pl.BlockSpec((None, hh, 2, HEAD_W, HEAD_W), lambda b, h: (b, h, 0, 0, 0))),
        scratch_shapes=[pltpu.VMEM((seq + 16, width), f32),
                        pltpu.VMEM((seq, width), f32), pltpu.VMEM((seq, width), f32),
                        pltpu.VMEM((seq, width), f32), pltpu.VMEM((hh, ng, 8, GROUP), f32),
                        pltpu.VMEM((nch, seq, HEAD_W), f32), pltpu.VMEM((nch, n, 2 * CHUNK, HEAD_W), bf16),
                        pltpu.VMEM((nch, n, CHUNK, CHUNK), bf16),
                        pltpu.VMEM((nch, seq, HEAD_W), bf16), pltpu.VMEM((nch, n, 8, HEAD_W), f32),
                        pltpu.VMEM((seq, width), f32), pltpu.VMEM((seq, width), f32),
                        pltpu.VMEM((nch, HEAD_W, HEAD_W), f32)],
        compiler_params=_cparams(("parallel", "parallel")),
        name="gdn",
    )(p, p, p, p, conv_w, conv_w, conv_w, grow, hp, norm_w, s0)


_N_LEVELS = 6
_N_MM_LEVELS = 4
_ROW_EQ = _N_MM_LEVELS * CHUNK
_ROW_EK, _ROW_TOT, _ROWS_EXP = _ROW_EQ + CHUNK, _ROW_EQ + 2 * CHUNK, _ROW_EQ + 2 * CHUNK + 8
_SCAN_UNROLL = 4


def _scan_consts():
    c = CHUNK
    mexp = np.zeros((2, _ROWS_EXP, c), np.float32)
    lvl = np.full((2, c, c), -1.0, np.float32)
    for lv in range(_N_LEVELS):
        s = 32 >> lv
        for i in range(c):
            p = (i // (2 * s)) * (2 * s) + s
            right = (i % (2 * s)) >= s
            if lv >= _N_MM_LEVELS:
                pass
            elif right:
                mexp[0, lv * c + i, p:i + 1] = 1.0
                mexp[1, lv * c + i, p:i] = 1.0
            else:
                mexp[0, lv * c + i, i + 1:p] = 1.0
                mexp[1, lv * c + i, i:p] = 1.0
            for j in range(c):
                if (i // (2 * s)) != (j // (2 * s)):
                    continue
                jright = (j % (2 * s)) >= s
                if right and not jright:
                    lvl[0, i, j] = lv
                if (not right) and jright:
                    lvl[1, i, j] = lv
    for i in range(c):
        lvl[:, i, i] = _N_LEVELS
        mexp[0, _ROW_EQ + i, :i + 1] = 1.0
        mexp[0, _ROW_EK + i, i + 1:] = 1.0
        mexp[1, _ROW_EQ + i, i:] = 1.0
        mexp[1, _ROW_EK + i, :i] = 1.0
    mexp[:, _ROW_TOT:, :] = 1.0
    return mexp, lvl


_LOG_DECAY_FLOOR = -1.0e4


def _scan_kernel(*refs, seq, hgrn, q_scale):
    if hgrn:
        q_ref, v_ref, gate_ref, zf_ref, zb_ref, lb_ref = refs[:6]
    else:
        q_ref, v_ref, gate_ref, k_ref, glr_ref, w2_ref, b2_ref = refs[:7]
    nw_ref, s0_ref, mexp_ref, lvl_ref, o_ref, so_ref, of, ob, qd_s, g_s, ct_s, stp_s, st_ref = refs[-13:]
    q_silu = hgrn
    gate_silu = not hgrn
    n = seq // CHUNK
    st_ref[...] = s0_ref[...]
    outs = (of, ob)

    def rows_of(c):
        return pl.ds(pl.multiple_of(c * CHUNK, CHUNK), CHUNK)

    def prepare_group(g, carry):
        todo = [(d, g * _SCAN_UNROLL + u) for u in range(_SCAN_UNROLL) for d in range(2)]
        items = range(len(todo))
        q, k, v, log_decay = [], [], [], []
        for d, c in todo:
            rows_c = rows_of(c)
            qi = q_ref[rows_c, :]
            q.append(jax.nn.silu(qi) if q_silu else qi * q_scale)
            v.append(v_ref[rows_c, :].astype(bf16))
        if hgrn:
            for d, c in todo:
                z = (zf_ref if d == 0 else zb_ref)[rows_of(c), :]
                lb = lb_ref[d:d + 1, :]
                t0 = jnp.log(lb)
                t1 = jnp.log1p(-lb) + jax.nn.log_sigmoid(z)
                mx = jnp.maximum(t0, t1)
                lse = mx + jnp.log(jnp.exp(t0 - mx) + jnp.exp(t1 - mx))
                log_decay.append(jnp.where(mx == -jnp.inf, -jnp.inf, lse))
                k.append((1.0 - lb) * jax.nn.sigmoid(-z))
        else:
            logits = [_dot(glr_ref[rows_of(c), :].astype(bf16), w2_ref[d]) for d, c in todo]
            log_decay = [jax.nn.log_sigmoid(logits[i] + b2_ref[d]) * (1.0 / GLA_TAU) for i, (d, _) in enumerate(todo)]
            k = [k_ref[rows_of(c), :] for _, c in todo]
        la = [jnp.maximum(x, _LOG_DECAY_FLOOR) for x in log_decay]
        e2 = []
        for i, (d, _) in enumerate(todo):
            hi, lo = _split2(la[i])
            e2.append(_dot(mexp_ref[d], jnp.concatenate([hi, lo], axis=1)))
        e = [x[:, :HEAD_W] + x[:, HEAD_W:] for x in e2]
        r4 = lax.broadcasted_iota(jnp.int32, (CHUNK, HEAD_W), 0) % 4
        fine = []
        for i, (d, _) in enumerate(todo):
            prev = pltpu.roll(la[i], 1, 0)
            nxt = pltpu.roll(la[i], CHUNK - 1, 0)
            if d == 0:
                w2 = jnp.where(r4 == 0, nxt, jnp.where(r4 == 1, 0.0, jnp.where(r4 == 2, la[i], la[i] + prev)))
                w1 = jnp.where(r4 % 2 == 1, la[i], 0.0)
            else:
                w2 = jnp.where(r4 == 0, la[i] + nxt, jnp.where(r4 == 1, la[i], jnp.where(r4 == 2, 0.0, prev)))
                w1 = jnp.where(r4 % 2 == 0, la[i], 0.0)
            fine.append((w2, w1))
        lvl = [lvl_ref[d] for d, _ in todo]
        qk = [_dot_nt(q[i].astype(bf16), k[i].astype(bf16)) for i in items]
        a = [jnp.where(lvl[i] == float(_N_LEVELS), qk[i], 0.0) for i in items]
        for lv in range(_N_LEVELS):
            if lv < _N_MM_LEVELS:
                w = [jnp.exp(e[i][lv * CHUNK:(lv + 1) * CHUNK]) for i in items]
            else:
                w = [jnp.exp(fine[i][lv - _N_MM_LEVELS]) for i in items]
            p = [_dot_nt((q[i] * w[i]).astype(bf16), (k[i] * w[i]).astype(bf16)) for i in items]
            a = [jnp.where(lvl[i] == float(lv), p[i], a[i]) for i in items]
        o_intra = [_dot(a[i].astype(bf16), v[i]) for i in items]
        incr = [_dot_tn(v[i], (k[i] * jnp.exp(e[i][_ROW_EK:_ROW_EK + CHUNK])).astype(bf16)) for i in items]
        for i, (d, c) in enumerate(todo):
            rows_c = rows_of(c)
            outs[d][rows_c, :] = o_intra[i]
            qd_s[d, rows_c, :] = (q[i] * jnp.exp(e[i][_ROW_EQ:_ROW_EQ + CHUNK])).astype(bf16)
            g_s[d, c] = incr[i]
            ct_s[d, c] = jnp.exp(e[i][_ROW_TOT:_ROW_TOT + 8])
        return carry

    lax.fori_loop(0, n // _SCAN_UNROLL, prepare_group, 0)

    def scan_step(i, carry):
        for d, c in ((0, i), (1, n - 1 - i)):
            st = st_ref[d]
            stp_s[d, c] = st.astype(bf16)
            st_ref[d] = st * ct_s[d, c][0:1, :] + g_s[d, c]
        return carry

    lax.fori_loop(0, n, scan_step, 0)

    def inter_group(g, carry):
        todo = [(d, g * _SCAN_UNROLL + u) for u in range(_SCAN_UNROLL) for d in range(2)]
        res = [_dot_nt(qd_s[d, rows_of(c), :], stp_s[d, c]) + outs[d][rows_of(c), :] for d, c in todo]
        for (d, c), o_c in zip(todo, res):
            outs[d][rows_of(c), :] = o_c
        return carry

    lax.fori_loop(0, n // _SCAN_UNROLL, inter_group, 0)
    o = of[...] + ob[...]
    g = gate_ref[...]
    g = jax.nn.silu(g) if gate_silu else jax.nn.sigmoid(g)
    o_ref[...] = (_rmsnorm(o, nw_ref[...]) * g).astype(bf16)
    so_ref[...] = st_ref[...]


def _scan(p, offs, extra, norm_w, s0t, mexp, lvl, *, nseq, seq, row_blk0, hgrn, q_scale):
    def pspec(off, per_head=True):
        return pl.BlockSpec((seq, HEAD_W), lambda b, h: (row_blk0 + b, off // HEAD_W + (h if per_head else 0)))

    if hgrn:
        (lower_bound,) = extra
        in_specs = [pspec(o) for o in offs] + [pl.BlockSpec((2, HEAD_W), lambda b, h: (0, h))]
        args = [p] * len(offs) + [lower_bound]
    else:
        w2, b2 = extra
        in_specs = ([pspec(o) for o in offs[:4]] + [pspec(offs[4], per_head=False)]
                    + [pl.BlockSpec((2, HEAD_W, HEAD_W), lambda b, h: (0, 0, h)),
                       pl.BlockSpec((2, 1, HEAD_W), lambda b, h: (0, 0, h))])
        args = [p] * len(offs) + [w2, b2]
    kern = functools.partial(_scan_kernel, seq=seq, hgrn=hgrn, q_scale=q_scale)
    return pl.pallas_call(
        kern,
        out_shape=(jax.ShapeDtypeStruct((nseq * seq, BRANCH_W), bf16),
                   jax.ShapeDtypeStruct((nseq, N_HEADS, 2, HEAD_W, HEAD_W), f32)),
        grid=(nseq, N_HEADS),
        in_specs=in_specs + [
            pl.BlockSpec((1, HEAD_W), lambda b, h: (0, 0)),
            pl.BlockSpec((None, None, 2, HEAD_W, HEAD_W), lambda b, h: (b, h, 0, 0, 0)),
            pl.BlockSpec((2, _ROWS_EXP, CHUNK), lambda b, h: (0, 0, 0)),
            pl.BlockSpec((2, CHUNK, CHUNK), lambda b, h: (0, 0, 0))],
        out_specs=(pl.BlockSpec((seq, HEAD_W), lambda b, h: (b, h)),
                   pl.BlockSpec((None, None, 2, HEAD_W, HEAD_W), lambda b, h: (b, h, 0, 0, 0))),
        scratch_shapes=[pltpu.VMEM((seq, HEAD_W), f32), pltpu.VMEM((seq, HEAD_W), f32),
                        pltpu.VMEM((2, seq, HEAD_W), bf16),
                        pltpu.VMEM((2, seq // CHUNK, HEAD_W, HEAD_W), f32),
                        pltpu.VMEM((2, seq // CHUNK, 8, HEAD_W), f32),
                        pltpu.VMEM((2, seq // CHUNK, HEAD_W, HEAD_W), bf16),
                        pltpu.VMEM((2, HEAD_W, HEAD_W), f32)],
        compiler_params=_cparams(("parallel", "parallel")),
        name="decay_scan",
    )(*args, norm_w, s0t, mexp, lvl)


def _mla_proj_kernel(*refs, rope):
    if rope:
        (qa_ref, kva_ref, sm_ref, qnw_ref, kvnw_ref, wqa_ref, wkv_ref, wqb_ref, cos_ref, sin_ref,
         qn_ref, qp_ref, ckv_ref, kn_ref, vv_ref, kp_ref) = refs
    else:
        (qa_ref, kva_ref, sm_ref, qnw_ref, kvnw_ref, wqa_ref, wkv_ref,
         qn_ref, qp_ref, ckv_ref, kn_ref, vv_ref, kp_ref) = refs
    qh = _rmsnorm(qa_ref[...], qnw_ref[...]).astype(bf16)
    qa = _dot(qh, wqa_ref[...])
    qn_ref[...] = qa[:, :BRANCH_W].astype(bf16)
    pe = qa[:, BRANCH_W:]
    kpe = sm_ref[:, 0:LANE]
    if rope:
        cos = cos_ref[...]
        sin = sin_ref[...]
        cos4 = jnp.concatenate([cos] * N_HEADS, axis=1)
        sin4 = jnp.concatenate([sin] * N_HEADS, axis=1)
        pe = pe * cos4 + _dot(qh, wqb_ref[...]) * sin4
        kpe = kpe * cos + pltpu.roll(kpe, MLA_ROPE, 1) * sin
    qp_ref[...] = pe.astype(bf16)
    kp_ref[...] = kpe.astype(bf16)
    ckv = _rmsnorm(kva_ref[...], kvnw_ref[...])
    ckv_ref[...] = ckv
    kv = _dot(ckv.astype(bf16), wkv_ref[...])
    kn_ref[...] = kv[:, :BRANCH_W].astype(bf16)
    vv_ref[...] = kv[:, BRANCH_W:].astype(bf16)


def _mla_proj(p, qnw, kvnw, wqa, wkv, wqb, cos, sin, *, nrows, seq, row0):
    tm = 256
    rope = cos is not None
    rb0 = row0 // tm
    per_seq = seq // tm

    def pspec(off):
        return pl.BlockSpec((tm, _SEG), lambda i: (rb0 + i, off // _SEG))

    def full(a):
        return pl.BlockSpec(a.shape, lambda i: (0,) * a.ndim)

    in_specs = [pspec(OFF_MLA_QA), pspec(OFF_MLA_KVA), pspec(OFF_SMALL), full(qnw), full(kvnw), full(wqa), full(wkv)]
    args = [p, p, p, qnw, kvnw, wqa, wkv]
    if rope:
        tspec = pl.BlockSpec((tm, LANE), lambda i: (i % per_seq, 0))
        in_specs += [full(wqb), tspec, tspec]
        args += [wqb, cos, sin]
    wide = lambda dt: jax.ShapeDtypeStruct((nrows, BRANCH_W), dt)
    ospec = pl.BlockSpec((tm, BRANCH_W), lambda i: (i, 0))
    return pl.pallas_call(
        functools.partial(_mla_proj_kernel, rope=rope),
        out_shape=(wide(bf16), wide(bf16), wide(f32), wide(bf16), wide(bf16),
                   jax.ShapeDtypeStruct((nrows, LANE), bf16)),
        grid=(nrows // tm,),
        in_specs=in_specs,
        out_specs=(ospec, ospec, ospec, ospec, ospec, pl.BlockSpec((tm, LANE), lambda i: (i, 0))),
        compiler_params=_cparams(("parallel",)),
        name="mla_proj",
    )(*args)


def _kv_kernel(ckv_ref, w_ref, kn_ref, vv_ref):
    kv = _dot(ckv_ref[...].astype(bf16), w_ref[...])
    kn_ref[...] = kv[:, :BRANCH_W].astype(bf16)
    vv_ref[...] = kv[:, BRANCH_W:].astype(bf16)


def _kv_proj(ckv, wkv):
    rows = ckv.shape[0]
    tm = 256
    out = jax.ShapeDtypeStruct((rows, BRANCH_W), bf16)
    ospec = pl.BlockSpec((tm, BRANCH_W), lambda i: (i, 0))
    return pl.pallas_call(
        _kv_kernel, out_shape=(out, out), grid=(rows // tm,),
        in_specs=[pl.BlockSpec((tm, ckv.shape[1]), lambda i: (i, 0)),
                  pl.BlockSpec(wkv.shape, lambda i: (0, 0))],
        out_specs=(ospec, ospec),
        compiler_params=_cparams(("parallel",)),
        name="mla_ctx_kv",
    )(ckv, wkv)


def _attn_kernel(*refs, has_ctx):
    if has_ctx:
        qn_ref, qp_ref, kn_ref, kp_ref, vv_ref, knc_ref, kpc_ref, vvc_ref, o_ref = refs
    else:
        qn_ref, qp_ref, kn_ref, kp_ref, vv_ref, o_ref = refs
    scale = (MLA_NOPE + MLA_ROPE) ** -0.5
    heads = range(N_HEADS)

    def head(ref, h):
        return ref[:, h * HEAD_W:(h + 1) * HEAD_W]

    kp = kp_ref[...]
    s1 = [(_dot_nt(head(qn_ref, h), head(kn_ref, h)) + _dot_nt(head(qp_ref, h), kp)) * scale for h in heads]
    mx = [jnp.max(x, axis=-1, keepdims=True) for x in s1]
    if has_ctx:
        kpc = kpc_ref[...]
        s2 = [(_dot_nt(head(qn_ref, h), head(knc_ref, h)) + _dot_nt(head(qp_ref, h), kpc)) * scale for h in heads]
        mx = [jnp.maximum(m, jnp.max(x, axis=-1, keepdims=True)) for m, x in zip(mx, s2)]
    p1 = [jnp.exp(x - m) for x, m in zip(s1, mx)]
    den = [jnp.sum(x, axis=-1, keepdims=True) for x in p1]
    if has_ctx:
        p2 = [jnp.exp(x - m) for x, m in zip(s2, mx)]
        den = [d_ + jnp.sum(x, axis=-1, keepdims=True) for d_, x in zip(den, p2)]
    inv = [1.0 / d_ for d_ in den]
    o = [_dot((p1[h] * inv[h]).astype(bf16), head(vv_ref, h)) for h in heads]
    if has_ctx:
        o = [o[h] + _dot((p2[h] * inv[h]).astype(bf16), head(vvc_ref, h)) for h in heads]
    o_ref[...] = jnp.concatenate(o, axis=1).astype(bf16)


def _attention(qn, qp, kn, kp, vv, ctx, *, nseq, seq):
    tq = 256
    nq = seq // tq
    has_ctx = ctx is not None
    qspec = pl.BlockSpec((tq, BRANCH_W), lambda b, i: (b * nq + i, 0))
    kspec = pl.BlockSpec((seq, BRANCH_W), lambda b, i: (b, 0))
    kpspec = pl.BlockSpec((seq, LANE), lambda b, i: (b, 0))
    in_specs = [qspec, qspec, kspec, kpspec, kspec]
    args = [qn, qp, kn, kp, vv]
    if has_ctx:
        knc, kpc, vvc = ctx
        lc = knc.shape[0] // nseq
        in_specs += [pl.BlockSpec((lc, BRANCH_W), lambda b, i: (b, 0)),
                     pl.BlockSpec((lc, LANE), lambda b, i: (b, 0)),
                     pl.BlockSpec((lc, BRANCH_W), lambda b, i: (b, 0))]
        args += [knc, kpc, vvc]
    return pl.pallas_call(
        functools.partial(_attn_kernel, has_ctx=has_ctx),
        out_shape=jax.ShapeDtypeStruct((nseq * seq, BRANCH_W), bf16),
        grid=(nseq, nq),
        in_specs=in_specs,
        out_specs=pl.BlockSpec((tq, BRANCH_W), lambda b, i: (b * nq + i, 0)),
        compiler_params=_cparams(("parallel", "parallel")),
        name="mla_attn",
    )(*args)


def _merge_kernel(o0_ref, o1_ref, o2_ref, o3_ref, g0_ref, g1_ref, g2_ref, g3_ref, bg_ref, wb_ref, wo_ref,
                  x_ref, mod_ref, lng_ref, lnb_ref, out_ref, *, alpha):
    m = None
    for k, (o_ref, g_ref) in enumerate(((o0_ref, g0_ref), (o1_ref, g1_ref), (o2_ref, g2_ref), (o3_ref, g3_ref))):
        term = jax.nn.sigmoid(g_ref[...] + bg_ref[k:k + 1, :]) * _dot(o_ref[...], wb_ref[k])
        m = term if m is None else m + term
    mix = _dot(m.astype(bf16), wo_ref[...])
    gate1 = mod_ref[2:3, :]
    out_ref[...] = _layernorm(alpha * x_ref[...] + gate1 * mix, lng_ref[...], lnb_ref[...])


def _merge(branches, p, bg, wb, wo, x, mod, lng, lnb, *, alpha, t_ctx, l_lat):
    t, d = x.shape
    tm = 256
    midx = _mod_index(tm, t_ctx, l_lat)
    ospec = pl.BlockSpec((tm, BRANCH_W), lambda i: (i, 0))
    gspecs = [pl.BlockSpec((tm, d), functools.partial(lambda i, k: (i, OFF_GATES // d + k), k=k)) for k in range(4)]
    single = dict(pipeline_mode=pl.Buffered(1))
    return pl.pallas_call(
        functools.partial(_merge_kernel, alpha=alpha),
        out_shape=jax.ShapeDtypeStruct((t, d), f32),
        grid=(t // tm,),
        in_specs=[ospec, ospec, ospec, ospec] + gspecs + [
            pl.BlockSpec((4, d), lambda i: (0, 0)),
            pl.BlockSpec((4, BRANCH_W, d), lambda i: (0, 0, 0), **single),
            pl.BlockSpec((d, d), lambda i: (0, 0), **single),
            pl.BlockSpec((tm, d), lambda i: (i, 0)),
            pl.BlockSpec((None, 6, d), lambda i: (midx(i), 0, 0)),
            pl.BlockSpec((1, d), lambda i: (0, 0)),
            pl.BlockSpec((1, d), lambda i: (0, 0))],
        out_specs=pl.BlockSpec((tm, d), lambda i: (i, 0)),
        compiler_params=_cparams(("parallel",)),
        name="merge_out_ln",
    )(*branches, p, p, p, p, bg, wb, wo, x, mod, lng, lnb)


def _ffn_kernel(x_ref, mod_ref, w1_ref, w3_ref, w2_ref, lng_ref, lnb_ref, out_ref, hb_ref, acc_ref, *, alpha):
    f = pl.program_id(1)

    @pl.when(f == 0)
    def _():
        shift = mod_ref[3:4, :]
        scale = mod_ref[4:5, :]
        hb_ref[...] = (x_ref[...] * (1.0 + scale) + shift).astype(bf16)
        acc_ref[...] = jnp.zeros_like(acc_ref)

    h = hb_ref[...]
    g = (jax.nn.silu(_dot(h, w1_ref[...])) * _dot(h, w3_ref[...])).astype(bf16)
    acc_ref[...] += _dot(g, w2_ref[...])

    @pl.when(f == pl.num_programs(1) - 1)
    def _():
        gate2 = mod_ref[5:6, :]
        out_ref[...] = _layernorm(alpha * x_ref[...] + gate2 * acc_ref[...], lng_ref[...], lnb_ref[...])


def _ffn(x, mod, w1, w3, w2, lng, lnb, *, alpha, t_ctx, l_lat):
    t, d = x.shape
    dff = w1.shape[1]
    tm, tf = 512, 512
    midx = _mod_index(tm, t_ctx, l_lat)
    return pl.pallas_call(
        functools.partial(_ffn_kernel, alpha=alpha),
        out_shape=jax.ShapeDtypeStruct((t, d), f32),
        grid=(t // tm, dff // tf),
        in_specs=[pl.BlockSpec((tm, d), lambda i, f: (i, 0)),
                  pl.BlockSpec((None, 6, d), lambda i, f: (midx(i), 0, 0)),
                  pl.BlockSpec((d, tf), lambda i, f: (0, f)),
                  pl.BlockSpec((d, tf), lambda i, f: (0, f)),
                  pl.BlockSpec((tf, d), lambda i, f: (f, 0)),
                  pl.BlockSpec((1, d), lambda i, f: (0, 0)),
                  pl.BlockSpec((1, d), lambda i, f: (0, 0))],
        out_specs=pl.BlockSpec((tm, d), lambda i, f: (i, 0)),
        scratch_shapes=[pltpu.VMEM((tm, d), bf16), pltpu.VMEM((tm, d), f32)],
        compiler_params=_cparams(("parallel", "arbitrary")),
        name="ffn_ln",
    )(x, mod, w1, w3, w2, lng, lnb)


def _cast_kernel(x_ref, o_ref):
    o_ref[...] = x_ref[...].astype(bf16)


def _cast_layer(w, layer):
    _, rows, cols = w.shape
    tr = 256 if cols > 4096 else 512
    return pl.pallas_call(
        _cast_kernel,
        out_shape=jax.ShapeDtypeStruct((rows, cols), bf16),
        grid=(rows // tr,),
        in_specs=[pl.BlockSpec((None, tr, cols), lambda i: (layer, i, 0))],
        out_specs=pl.BlockSpec((tr, cols), lambda i: (i, 0)),
        compiler_params=_cparams(("parallel",)),
        name="cast_bf16",
    )(w)


_ROW_UNIT = 16


def _relayout_plan():
    plain = lambda src: (0, src)
    special = lambda k: (1, k)
    segs = [plain(_SRC[n]) for n in ('gdn_q', 'gdn_k', 'gdn_v', 'gdn_z')]
    segs += [special(0), special(1)]
    segs += [plain(_SRC['gla_v']), plain(_SRC['gla_r']), plain(_SRC['hg_q']), plain(_SRC['hg_f']),
             plain(_SRC['hg_f'] + 512), plain(_SRC['hg_i']), plain(_SRC['hg_g']), plain(_SRC['mla_qa']),
             plain(_SRC['mla_kva'])]
    segs += [special(2)]
    segs += [plain(_SRC['gates'] + _SEG * i) for i in range(N_HEADS * D_MODEL // _SEG)]
    assert len(segs) == N_PROJ // _SEG
    table = np.zeros((3, len(segs)), np.int32)
    row, blk = segs[0][1], 0
    for j, (is_special, val) in enumerate(segs):
        if is_special:
            blk = val
        else:
            row = val
        assert row % _ROW_UNIT == 0
        table[:, j] = (is_special, row // _ROW_UNIT, blk)
    return table


def _special_rows(wt, layer):
    d = wt.shape[2]

    def rows(name, start, n):
        s = _SRC[name] + start
        return lax.slice(wt, (layer, s, 0), (layer + 1, s + n, d))[0]

    def head_padded(name):
        parts = []
        for h in range(N_HEADS):
            parts += [rows(name, h * GLA_DK, GLA_DK), jnp.zeros((HEAD_W - GLA_DK, d), wt.dtype)]
        return parts

    q4 = MLA_ROPE // 4
    small = [rows('mla_kpe', 0, MLA_ROPE), rows('mla_kpe', q4, q4), rows('mla_kpe', 0, q4),
             rows('mla_kpe', 3 * q4, q4), rows('mla_kpe', 2 * q4, q4), rows('gla_g', 0, 2 * GLA_RANK),
             rows('gdn_b', 0, 8), rows('gdn_a', 0, 8), jnp.zeros((_SEG - SM_GDN_A - 8, d), wt.dtype)]
    return jnp.concatenate(head_padded('gla_q') + head_padded('gla_k') + small, axis=0)


def _relayout_kernel(tab_ref, wt_ref, sp_ref, o_ref):
    is_special = tab_ref[0, pl.program_id(1)]

    @pl.when(is_special == 0)
    def _():
        o_ref[...] = wt_ref[0].astype(bf16)

    @pl.when(is_special == 1)
    def _():
        o_ref[...] = sp_ref[...].astype(bf16)


def _relayout_w_in(w_in):
    depth, d, _ = w_in.shape
    wt = jnp.swapaxes(w_in, 1, 2)
    special = jnp.stack([_special_rows(wt, l) for l in range(depth)])
    table = _relayout_plan()
    nseg = N_PROJ // _SEG
    return pl.pallas_call(
        _relayout_kernel,
        out_shape=jax.ShapeDtypeStruct((depth, N_PROJ, d), bf16),
        grid_spec=pltpu.PrefetchScalarGridSpec(
            num_scalar_prefetch=1,
            grid=(depth, nseg),
            in_specs=[pl.BlockSpec((pl.Element(1), pl.Element(_SEG), pl.Element(d)),
                                   lambda l, j, tab: (l, tab[1, j] * _ROW_UNIT, 0)),
                      pl.BlockSpec((None, _SEG, d), lambda l, j, tab: (l, tab[2, j], 0))],
            out_specs=pl.BlockSpec((None, _SEG, d), lambda l, j, tab: (l, j, 0))),
        compiler_params=_cparams(("parallel", "parallel")),
        name="w_in_relayout",
    )(jnp.asarray(table), wt, special)


def _relayout_wq(wq):
    hw = MLA_NOPE + MLA_ROPE
    z = jnp.zeros((wq.shape[0], HEAD_W - MLA_ROPE), wq.dtype)
    q4 = MLA_ROPE // 4
    nope, pe, pes = [], [], []
    for h in range(N_HEADS):
        base = h * hw
        nope.append(wq[:, base:base + MLA_NOPE])
        r = wq[:, base + MLA_NOPE:base + hw]
        pe += [r, z]
        pes += [r[:, q4:2 * q4], r[:, 0:q4], r[:, 3 * q4:], r[:, 2 * q4:3 * q4], z]
    return jnp.concatenate(nope + pe, axis=1).astype(bf16), jnp.concatenate(pes, axis=1).astype(bf16)


def _relayout_wkv(wkv):
    hw = MLA_NOPE + HEAD_W
    kn = [wkv[:, h * hw:h * hw + MLA_NOPE] for h in range(N_HEADS)]
    vv = [wkv[:, h * hw + MLA_NOPE:(h + 1) * hw] for h in range(N_HEADS)]
    return jnp.concatenate(kn + vv, axis=1).astype(bf16)


def _rope_tables(length):
    pos = jnp.arange(length)
    row_id = (pos // GRID_W).astype(f32)
    col_id = (pos % GRID_W).astype(f32)
    half = MLA_ROPE // 2
    inv = ROPE_BASE ** (-jnp.arange(0, half, 2, dtype=f32) / half)
    ar, ac = row_id[:, None] * inv, col_id[:, None] * inv
    z = jnp.zeros((length, LANE - MLA_ROPE), f32)
    cos = jnp.concatenate([jnp.cos(ar), jnp.cos(ar), jnp.cos(ac), jnp.cos(ac), z], axis=1)
    sin = jnp.concatenate([-jnp.sin(ar), jnp.sin(ar), -jnp.sin(ac), jnp.sin(ac), z], axis=1)
    return cos, sin


def _gla_gate_weights(w2, b):
    assert SM_GLA_G % LANE == 0
    lane_pad = ((0, 0), (0, 0), (0, 0), (0, HEAD_W - GLA_DK))
    w4 = jnp.pad(w2.reshape(2, GLA_RANK, N_HEADS, GLA_DK), lane_pad).reshape(2, GLA_RANK, BRANCH_W)
    wbig = jnp.stack([jnp.pad(w4[d], ((d * GLA_RANK, LANE - (d + 1) * GLA_RANK), (0, 0))) for d in range(2)])
    bbig = jnp.pad(b.reshape(2, 1, N_HEADS, GLA_DK), lane_pad).reshape(2, 1, BRANCH_W)
    return wbig.astype(bf16), bbig


def kernel(x_prompt, x_sample, c, state_gdn, state_gla, state_hgrn, cache_mla_ckv, cache_mla_kpe, c_ctx, w_ada, b_ada, w_in, gdn_conv, gdn_a_log, gdn_dt_bias, gdn_norm, gla_gate_w2, gla_gate_b, gla_norm, hgrn_lb, hgrn_norm, mla_q_norm, mla_wq_b, mla_kv_norm, mla_wkv_b, w_branch, b_gates, w_out, ln1_g, ln1_b, ln2_g, ln2_b, ffn_w1, ffn_w3, ffn_w2):
    nb_c, l_c, d = x_prompt.shape
    nb_l, l_l, _ = x_sample.shape
    depth = w_in.shape[0]
    t_c, t_l = nb_c * l_c, nb_l * l_l
    past = cache_mla_ckv.shape[2]
    alpha = (2.0 * depth) ** 0.25
    assert d == D_MODEL and t_c % 1024 == 0 and l_l % 1024 == 0 and l_c % CHUNK == 0

    n_cond = 1 + nb_l
    cc = jnp.concatenate([c_ctx[None, :], c, jnp.zeros((-n_cond % 8, d), f32)], axis=0)
    mods = _ada(cc, w_ada, b_ada).reshape(depth, cc.shape[0], 6, d)

    w_in_r = _relayout_w_in(w_in)
    mexp_np, lvl_np = _scan_consts()
    mexp = jnp.asarray(mexp_np, bf16)
    lvl = jnp.asarray(lvl_np, f32)
    cos_t, sin_t = _rope_tables(l_l)
    cum = jnp.cumsum(jax.nn.softmax(hgrn_lb.astype(f32), axis=0), axis=0)
    lower_bounds = cum - cum[:1]

    x = jnp.concatenate([x_prompt.reshape(t_c, d), x_sample.reshape(t_l, d)], axis=0)
    t = t_c + t_l
    zero_state = jnp.zeros((nb_c, N_HEADS, 2, HEAD_W, HEAD_W), f32)
    streams = (dict(nseq=nb_c, seq=l_c, row0=0), dict(nseq=nb_l, seq=l_l, row0=t_c))
    new_gdn, new_gla, new_hg, new_ckv, new_kpe = [], [], [], [], []

    for l in range(depth):
        mod = mods[l]
        p = _inproj(x, mod, w_in_r, l, t_c, l_l)

        gsm = p[:, OFF_SMALL + SM_GDN_B:OFF_SMALL + SM_GDN_B + 16].reshape(t // GROUP, GROUP, 4, N_HEADS)
        grow = jnp.pad(gsm.transpose(3, 0, 2, 1), ((0, 0), (0, 0), (0, 4), (0, 0)))
        zc = jnp.zeros((N_HEADS, 2), f32)
        hp = jnp.stack([jnp.concatenate([zc, gdn_a_log[l].T, zc, zc], axis=1),
                        jnp.concatenate([zc, gdn_dt_bias[l].T, zc, zc], axis=1)], axis=-1)

        gla_gate = _gla_gate_weights(gla_gate_w2[l], gla_gate_b[l])
        wqa, wqb = _relayout_wq(mla_wq_b[l])
        wkv = _relayout_wkv(mla_wkv_b[l])
        qnw, kvnw = mla_q_norm[l][None, :], mla_kv_norm[l][None, :]

        outs = {k: [] for k in ('gdn', 'gla', 'hg', 'mla')}
        for si, st in enumerate(streams):
            nseq, seq, row0 = st['nseq'], st['seq'], st['row0']
            rb = row0 // seq
            if si == 0:
                s_gdn0 = s_gla0 = s_hg0 = zero_state
            else:
                s_gdn0 = state_gdn[:, l].transpose(0, 2, 1, 3, 4)
                s_gla0 = jnp.pad(state_gla[:, l], ((0, 0),) * 3 + ((0, HEAD_W - GLA_DK), (0, 0))).transpose(0, 2, 1, 4, 3)
                s_hg0 = state_hgrn[:, l].transpose(0, 2, 1, 4, 3)
            o_gdn, s_gdn = _gdn(p, gdn_conv[l], grow, hp, gdn_norm[l][None, :], s_gdn0, nseq=nseq, seq=seq, row_blk0=rb)
            o_gla, s_gla = _scan(p, (OFF_GLA_Q, OFF_GLA_V, OFF_GLA_R, OFF_GLA_K, OFF_SMALL + SM_GLA_G), gla_gate,
                                 gla_norm[l][None, :], s_gla0, mexp, lvl,
                                 nseq=nseq, seq=seq, row_blk0=rb, hgrn=False, q_scale=GLA_DK ** -0.5)
            o_hg, s_hg = _scan(p, (OFF_HG_Q, OFF_HG_I, OFF_HG_G, OFF_HG_FF, OFF_HG_FB), (lower_bounds[l],),
                               hgrn_norm[l][None, :], s_hg0, mexp, lvl,
                               nseq=nseq, seq=seq, row_blk0=rb, hgrn=True, q_scale=1.0)
            rope = si == 1
            qn, qp, ckv, kn, vv, kp = _mla_proj(p, qnw, kvnw, wqa, wkv, wqb if rope else None,
                                                cos_t if rope else None, sin_t if rope else None,
                                                nrows=nseq * seq, seq=seq, row0=row0)
            ctx = None
            if si == 1:
                knc, vvc = _kv_proj(cache_mla_ckv[:, l].reshape(nb_l * past, -1), wkv)
                kpc = jnp.pad(cache_mla_kpe[:, l].reshape(nb_l * past, MLA_ROPE), ((0, 0), (0, LANE - MLA_ROPE))).astype(bf16)
                ctx = (knc, kpc, vvc)
            o_mla = _attention(qn, qp, kn, kp, vv, ctx, nseq=nseq, seq=seq)
            outs['gdn'].append(o_gdn)
            outs['gla'].append(o_gla)
            outs['hg'].append(o_hg)
            outs['mla'].append(o_mla)
            if si == 0:
                new_gdn.append(s_gdn.transpose(0, 2, 1, 3, 4))
                new_gla.append(s_gla.transpose(0, 2, 1, 4, 3)[:, :, :, :GLA_DK, :])
                new_hg.append(s_hg.transpose(0, 2, 1, 4, 3))
                new_ckv.append(ckv.reshape(nb_c, l_c, -1))
                new_kpe.append(p[:t_c, OFF_SMALL + SM_KPE:OFF_SMALL + SM_KPE + MLA_ROPE].reshape(nb_c, l_c, MLA_ROPE))

        branches = [jnp.concatenate(outs[k], axis=0) for k in ('gdn', 'gla', 'hg', 'mla')]
        wb = _cast_layer(w_branch.reshape(depth, 4 * BRANCH_W, d), l).reshape(4, BRANCH_W, d)
        x1 = _merge(branches, p, b_gates[l], wb, _cast_layer(w_out, l), x, mod,
                    ln1_g[l][None, :], ln1_b[l][None, :], alpha=alpha, t_ctx=t_c, l_lat=l_l)
        x = _ffn(x1, mod, _cast_layer(ffn_w1, l), _cast_layer(ffn_w3, l), _cast_layer(ffn_w2, l),
                 ln2_g[l][None, :], ln2_b[l][None, :], alpha=alpha, t_ctx=t_c, l_lat=l_l)

    sdt = x_prompt.dtype
    return (x[:t_c].reshape(nb_c, l_c, d), x[t_c:].reshape(nb_l, l_l, d),
            jnp.stack(new_gdn, axis=1).astype(sdt), jnp.stack(new_gla, axis=1).astype(sdt),
            jnp.stack(new_hg, axis=1).astype(sdt), jnp.stack(new_ckv, axis=1), jnp.stack(new_kpe, axis=1))
```

```python
import functools

import numpy as np
import jax
import jax.numpy as jnp
from jax import lax
from jax.experimental import pallas as pl
from jax.experimental.pallas import tpu as pltpu

f32 = jnp.float32
bf16 = jnp.bfloat16

D_MODEL = 2048
N_HEADS = 4
HEAD_W = 128
BRANCH_W = N_HEADS * HEAD_W
GLA_DK = 64
GLA_RANK = 16
GLA_TAU = 16.0
GDN_CONV = 5
MLA_NOPE = 128
MLA_ROPE = 64
ROPE_BASE = 10000.0
GRID_W = 64
NORM_EPS = 1e-6
CHUNK = 64
SUB = 16
LANE = 128
VMEM_LIMIT = 56 * 1024 * 1024

_SEG = 512
OFF_GDN_Q, OFF_GDN_K, OFF_GDN_V, OFF_GDN_Z = 0, 512, 1024, 1536
OFF_GLA_Q, OFF_GLA_K, OFF_GLA_V, OFF_GLA_R = 2048, 2560, 3072, 3584
OFF_HG_Q, OFF_HG_FF, OFF_HG_FB, OFF_HG_I, OFF_HG_G = 4096, 4608, 5120, 5632, 6144
OFF_MLA_QA, OFF_MLA_KVA, OFF_SMALL, OFF_GATES = 6656, 7168, 7680, 8192
N_PROJ = OFF_GATES + N_HEADS * D_MODEL
SM_KPE, SM_KPE_SW, SM_GLA_G, SM_GDN_B, SM_GDN_A = 0, 64, 128, 160, 168

_SRC = {}
_o = 0
for _n, _w in (('gdn_q', 512), ('gdn_k', 512), ('gdn_v', 512), ('gdn_z', 512), ('gdn_b', 8), ('gdn_a', 8),
               ('gla_q', 256), ('gla_k', 256), ('gla_v', 512), ('gla_r', 512), ('gla_g', 32),
               ('hg_q', 512), ('hg_f', 1024), ('hg_i', 512), ('hg_g', 512),
               ('mla_qa', 512), ('mla_kva', 512), ('mla_kpe', 64), ('gates', 4 * D_MODEL)):
    _SRC[_n] = _o
    _o += _w
IN_WIDTH = _o


def _cparams(sem):
    return pltpu.CompilerParams(dimension_semantics=sem, vmem_limit_bytes=VMEM_LIMIT)


def _dot(a, b):
    return jnp.dot(a, b, preferred_element_type=f32)


def _dot_nt(a, b):
    return lax.dot_general(a, b, (((1,), (1,)), ((), ())), preferred_element_type=f32)


def _dot_tn(a, b):
    return lax.dot_general(a, b, (((0,), (0,)), ((), ())), preferred_element_type=f32)


def _split2(x):
    hi = x.astype(bf16)
    lo = (x - hi.astype(f32)).astype(bf16)
    return hi, lo


def _mm3(a, b):
    ah, al = _split2(a)
    bh, bl = _split2(b)
    return _dot(ah, bh) + (_dot(ah, bl) + _dot(al, bh))


def _layernorm(y, g, b):
    mu = jnp.mean(y, axis=-1, keepdims=True)
    yc = y - mu
    var = jnp.mean(yc * yc, axis=-1, keepdims=True)
    return yc * lax.rsqrt(var + NORM_EPS) * g + b


def _rmsnorm(y, w):
    return y * lax.rsqrt(jnp.mean(y * y, axis=-1, keepdims=True) + NORM_EPS) * w


def _ada_kernel(c_ref, w_ref, b_ref, o_ref):
    cs = jax.nn.silu(c_ref[...]).astype(bf16)
    o_ref[...] = _dot(cs, w_ref[...].astype(bf16)) + b_ref[...]


def _ada(cc, w_ada, b_ada):
    depth, d, n6 = w_ada.shape
    rows = cc.shape[0]
    tn = 1024
    return pl.pallas_call(
        _ada_kernel,
        out_shape=jax.ShapeDtypeStruct((depth, rows, n6), f32),
        grid=(depth, n6 // tn),
        in_specs=[pl.BlockSpec((rows, d), lambda l, j: (0, 0)),
                  pl.BlockSpec((None, d, tn), lambda l, j: (l, 0, j)),
                  pl.BlockSpec((None, 1, tn), lambda l, j: (l, 0, j))],
        out_specs=pl.BlockSpec((None, rows, tn), lambda l, j: (l, 0, j)),
        compiler_params=_cparams(("parallel", "parallel")),
        name="ada_mod",
    )(cc, w_ada, b_ada.reshape(depth, 1, n6))


def _inproj_kernel(x_ref, mod_ref, w_ref, o_ref, og_ref, xb_ref, *, n_main):
    j = pl.program_id(1)

    @pl.when(j == 0)
    def _():
        shift = mod_ref[0:1, :]
        scale = mod_ref[1:2, :]
        xb_ref[...] = (x_ref[...] * (1.0 + scale) + shift).astype(bf16)

    r = _dot_nt(xb_ref[...], w_ref[...])

    @pl.when(j < n_main)
    def _():
        o_ref[...] = r

    @pl.when(j >= n_main)
    def _():
        og_ref[...] = r.astype(bf16)


def _mod_index(tm, t_ctx, l_lat):
    def index(i):
        r = i * tm
        return jnp.where(r < t_ctx, 0, 1 + (r - t_ctx) // l_lat)
    return index


def _inproj(x, mod, w, layer, t_ctx, l_lat):
    t, d = x.shape
    n = w.shape[1]
    tm, tn = 1024, 1024
    n_main = OFF_GATES // tn
    midx = _mod_index(tm, t_ctx, l_lat)
    return pl.pallas_call(
        functools.partial(_inproj_kernel, n_main=n_main),
        out_shape=(jax.ShapeDtypeStruct((t, OFF_GATES), f32), jax.ShapeDtypeStruct((t, n - OFF_GATES), bf16)),
        grid=(t // tm, n // tn),
        in_specs=[pl.BlockSpec((tm, d), lambda i, j: (i, 0)),
                  pl.BlockSpec((None, 6, d), lambda i, j: (midx(i), 0, 0)),
                  pl.BlockSpec((None, tn, d), lambda i, j: (layer, j, 0))],
        out_specs=(pl.BlockSpec((tm, tn), lambda i, j: (i, jnp.minimum(j, n_main - 1))),
                   pl.BlockSpec((tm, tn), lambda i, j: (i, jnp.maximum(j - n_main, 0)))),
        scratch_shapes=[pltpu.VMEM((tm, d), bf16)],
        compiler_params=_cparams(("parallel", "arbitrary")),
        name="in_proj",
    )(x, mod, w)


_PACK = 4
GROUP = _PACK * CHUNK
_GDN_HEADS = 2


def _gdn_kernel(q_ref, k_ref, v_ref, z_ref, cwq_ref, cwk_ref, cwv_ref, grow_ref, hp_ref, nw_ref, s0_ref,
                o_ref, so_ref, pad_ref, qs, ks, vs, gs, ub_s, wq_s, aq_s, kd_s, ct_s, of, ob, s_ref, *, seq):
    n = seq // CHUNK
    width = _GDN_HEADS * HEAD_W
    chains = [(hd, d) for hd in range(_GDN_HEADS) for d in range(2)]

    def conv_silu(x_ref, w_ref):
        pad_ref[0:8, :] = jnp.zeros((8, width), f32)
        pad_ref[8 + seq:16 + seq, :] = jnp.zeros((8, width), f32)
        pad_ref[8:8 + seq, :] = x_ref[...]
        acc = pad_ref[pl.ds(6, seq), :] * w_ref[0:1, :]
        for j in range(1, GDN_CONV):
            acc = acc + pad_ref[pl.ds(6 + j, seq), :] * w_ref[j:j + 1, :]
        return jax.nn.silu(acc)

    def l2norm_heads(x):
        parts = []
        for hd in range(_GDN_HEADS):
            xh = x[:, hd * HEAD_W:(hd + 1) * HEAD_W]
            parts.append(xh * lax.rsqrt(jnp.sum(xh * xh, axis=-1, keepdims=True) + NORM_EPS))
        return jnp.concatenate(parts, axis=1)

    qs[...] = l2norm_heads(conv_silu(q_ref, cwq_ref)) * (HEAD_W ** -0.5)
    ks[...] = l2norm_heads(conv_silu(k_ref, cwk_ref))
    vs[...] = conv_silu(v_ref, cwv_ref)

    x = grow_ref[...]
    a_log = hp_ref[:, :, 0:1][:, None]
    dt_b = hp_ref[:, :, 1:2][:, None]
    rows = lax.broadcasted_iota(jnp.int32, x.shape, 2)
    gs[...] = jnp.where(rows < 2, jax.nn.sigmoid(x), -jnp.exp(a_log) * jax.nn.softplus(x + dt_b))
    s_ref[...] = s0_ref[...].reshape(2 * _GDN_HEADS, HEAD_W, HEAD_W)

    ri = lax.broadcasted_iota(jnp.int32, (CHUNK, GROUP), 0)
    li = lax.broadcasted_iota(jnp.int32, (CHUNK, GROUP), 1)
    cj = li % CHUNK
    blk = [(li // CHUNK) == r for r in range(_PACK)]
    blk_bf = [jnp.where(b, 1.0, 0.0).astype(bf16) for b in blk]
    l1 = lax.broadcasted_iota(jnp.int32, (1, GROUP), 1) // CHUNK
    eye_b = ri == cj
    eye = jnp.where(eye_b, 1.0, 0.0).astype(f32)
    bd16 = (ri // SUB) == (cj // SUB)
    bd32 = (ri // (2 * SUB)) == (cj // (2 * SUB))
    r2 = lax.broadcasted_iota(jnp.int32, (GROUP, GROUP), 0)
    c2 = lax.broadcasted_iota(jnp.int32, (GROUP, GROUP), 1)
    same_blk = (r2 // CHUNK) == (c2 // CHUNK)

    def bdiag(yb):
        return jnp.concatenate([yb * mk for mk in blk_bf], axis=0)

    def unpack_diag(xf):
        out = xf[0:CHUNK]
        for r in range(1, _PACK):
            out = jnp.where(blk[r], xf[r * CHUNK:(r + 1) * CHUNK], out)
        return out

    def pmm3(pairs):
        sp = [(_split2(a), _split2(b)) for a, b in pairs]
        r1 = [_dot(jnp.concatenate([ah, al], axis=0), bdiag(bh)) for (ah, al), (bh, _) in sp]
        r2 = [_dot(ah, bdiag(bl)) for (ah, _), (_, bl) in sp]
        return [x1[:CHUNK] + x1[CHUNK:] + x2 for x1, x2 in zip(r1, r2)]

    def unit_tri_inverse(ms):
        k = len(ms)
        m32 = [jnp.where(bd32, m, 0.0) for m in ms]
        dg = [jnp.where(bd16, m, 0.0) for m in ms]
        d2 = pmm3([(a, a) for a in dg])
        both = pmm3([(a, a) for a in d2] + [(eye - a, eye + b) for a, b in zip(dg, d2)])
        d4, xi = both[:k], both[k:]
        both = pmm3([(a, a) for a in d4] + [(a, eye + b) for a, b in zip(xi, d4)])
        d8, xi = both[:k], both[k:]
        xi = pmm3([(a, eye + b) for a, b in zip(xi, d8)])
        for lo, hi in ((dg, m32), (m32, ms)):
            t = pmm3([(a, h_ - l_) for a, h_, l_ in zip(xi, hi, lo)])
            t = pmm3(list(zip(t, xi)))
            xi = [a - b for a, b in zip(xi, t)]
        return xi

    def block_cols(xp):
        return [jnp.sum(jnp.where(blk[r], xp, 0.0), axis=1, keepdims=True) for r in range(_PACK)]

    def spread(cols):
        out = jnp.broadcast_to(cols[0], (CHUNK, GROUP))
        for r in range(1, _PACK):
            out = jnp.where(blk[r], cols[r], out)
        return out

    def prepare_group(g, carry):
        rows_g = pl.ds(pl.multiple_of(g * GROUP, GROUP), GROUP)
        heads = range(_GDN_HEADS)
        k4 = [ks[rows_g, hd * HEAD_W:(hd + 1) * HEAD_W] for hd in heads]
        q4 = [qs[rows_g, hd * HEAD_W:(hd + 1) * HEAD_W] for hd in heads]
        v4 = [vs[rows_g, hd * HEAD_W:(hd + 1) * HEAD_W] for hd in heads]
        k4b = [x.astype(bf16) for x in k4]
        kq = [_dot_nt(jnp.concatenate([k4b[hd], q4[hd].astype(bf16)], axis=0), k4b[hd]) for hd in heads]
        kk_p = [unpack_diag(x[:GROUP]) for x in kq]
        qk_p = [unpack_diag(x[GROUP:]) for x in kq]
        g8 = [gs[hd, g] for hd in heads]
        ids = range(len(chains))
        beta_r = [g8[hd][d:d + 1, :] for hd, d in chains]
        g_r = [g8[hd][2 + d:3 + d, :] for hd, d in chains]
        tri2 = [cj <= ri, cj >= ri]
        tri = [tri2[d] for _, d in chains]
        strict = [jnp.logical_and(t, jnp.logical_not(eye_b)) for t in tri]
        tg = [jnp.where(tri[i], g_r[i], 0.0) for i in ids]
        gam_cols = [block_cols(tg[i]) for i in ids]
        beta_cols = [block_cols(jnp.where(eye_b, beta_r[i], 0.0)) for i in ids]
        tot_cols = [[jnp.sum(jnp.where(l1 == r, g_r[i], 0.0), axis=1, keepdims=True) for r in range(_PACK)]
                    for i in ids]
        strict2 = [(c2 % CHUNK) < (r2 % CHUNK), (c2 % CHUNK) > (r2 % CHUNK)]
        sm = [jnp.where(jnp.logical_and(same_blk, s_), 1.0, 0.0).astype(bf16) for s_ in strict2]
        r3 = []
        for i, (_, d) in enumerate(chains):
            th, tl = _split2(tg[i])
            tl2 = (tg[i] - th.astype(f32) - tl.astype(f32)).astype(bf16)
            r3.append(_dot(jnp.concatenate([th, tl, tl2], axis=0), sm[d]))
        dlt = [x[:CHUNK] + x[CHUNK:2 * CHUNK] + x[2 * CHUNK:] for x in r3]
        decay = [jnp.where(tri[i], jnp.exp(jnp.minimum(dlt[i], 0.0)), 0.0) for i in ids]
        t_inv = unit_tri_inverse([jnp.where(strict[i], kk_p[hd] * spread(beta_cols[i]) * decay[i], 0.0)
                                  for i, (hd, _) in enumerate(chains)])
        gam = [jnp.concatenate(gam_cols[i], axis=0) for i in ids]
        beta = [jnp.concatenate(beta_cols[i], axis=0) for i in ids]
        tot = [jnp.concatenate([jnp.broadcast_to(t, (CHUNK, 1)) for t in tot_cols[i]], axis=0) for i in ids]
        eg = [jnp.exp(x) for x in gam]
        rhs = [jnp.concatenate([v4[hd] * beta[i], k4[hd] * beta[i] * eg[i]], axis=1)
               for i, (hd, _) in enumerate(chains)]
        isp = [_split2(x) for x in t_inv]
        rsp = [_split2(x) for x in rhs]
        bih = [bdiag(isp[i][0]) for i in ids]
        s1 = [_dot(jnp.concatenate([bih[i], bdiag(isp[i][1])], axis=0), rsp[i][0]) for i in ids]
        s2 = [_dot(bih[i], rsp[i][1]) for i in ids]
        for i, (hd, d) in enumerate(chains):
            sol = s1[i][:GROUP] + s1[i][GROUP:] + s2[i]
            ub_s[i, rows_g, :] = sol[:, :HEAD_W]
            w_b = sol[:, HEAD_W:].astype(bf16)
            qd_b = (q4[hd] * eg[i]).astype(bf16)
            a_p = qk_p[hd] * decay[i]
            for r_ in range(_PACK):
                c = g * _PACK + r_
                wq_s[i, c, 0:CHUNK, :] = w_b[r_ * CHUNK:(r_ + 1) * CHUNK]
                wq_s[i, c, CHUNK:2 * CHUNK, :] = qd_b[r_ * CHUNK:(r_ + 1) * CHUNK]
                aq_s[i, c] = a_p[:, r_ * CHUNK:(r_ + 1) * CHUNK].astype(bf16)
                ct_s[i, c] = jnp.broadcast_to(jnp.exp(tot_cols[i][r_]), (8, HEAD_W))
            kd_s[i, rows_g, :] = (k4[hd] * jnp.exp(tot[i] - gam[i])).astype(bf16)
        return carry

    lax.fori_loop(0, n // _PACK, prepare_group, 0)

    def step(t, carry):
        ids = range(len(chains))
        cs = [t if d == 0 else n - 1 - t for _, d in chains]
        rows = [pl.ds(pl.multiple_of(c * CHUNK, CHUNK), CHUNK) for c in cs]
        s = [s_ref[i] for i in ids]
        sb = [x.astype(bf16) for x in s]
        r = [_dot(wq_s[i, cs[i]], sb[i]) for i in ids]
        u = [(ub_s[i, rows[i], :] - r[i][:CHUNK]).astype(bf16) for i in ids]
        o_c = [r[i][CHUNK:] + _dot(aq_s[i, cs[i]], u[i]) for i in ids]
        s_new = [ct_s[i, cs[i]][0:1, :] * s[i] + _dot_tn(kd_s[i, rows[i], :], u[i]) for i in ids]
        for i, (hd, d) in enumerate(chains):
            (of if d == 0 else ob)[rows[i], hd * HEAD_W:(hd + 1) * HEAD_W] = o_c[i]
            s_ref[i] = s_new[i]
        return carry

    lax.fori_loop(0, n, step, 0)
    o = of[...] + ob[...]
    z = z_ref[...]
    nw = nw_ref[...]
    o_ref[...] = jnp.concatenate(
        [_rmsnorm(o[:, hd * HEAD_W:(hd + 1) * HEAD_W], nw) * jax.nn.silu(z[:, hd * HEAD_W:(hd + 1) * HEAD_W])
         for hd in range(_GDN_HEADS)], axis=1).astype(bf16)
    so_ref[...] = s_ref[...].reshape(_GDN_HEADS, 2, HEAD_W, HEAD_W)


def _gdn(p, conv_w, grow, hp, norm_w, s0, *, nseq, seq, row_blk0):
    n = seq // CHUNK
    ng = seq // GROUP
    hh = _GDN_HEADS
    width = hh * HEAD_W
    nch = 2 * hh

    def pcol(off):
        return pl.BlockSpec((seq, width), lambda b, h: (row_blk0 + b, off // width + h))

    def wcol(off):
        return pl.BlockSpec((GDN_CONV, width), lambda b, h: (0, off // width + h))

    kern = functools.partial(_gdn_kernel, seq=seq)
    return pl.pallas_call(
        kern,
        out_shape=(jax.ShapeDtypeStruct((nseq * seq, BRANCH_W), bf16),
                   jax.ShapeDtypeStruct((nseq, N_HEADS, 2, HEAD_W, HEAD_W), f32)),
        grid=(nseq, N_HEADS // hh),
        in_specs=[pcol(OFF_GDN_Q), pcol(OFF_GDN_K), pcol(OFF_GDN_V), pcol(OFF_GDN_Z),
                  wcol(0), wcol(BRANCH_W), wcol(2 * BRANCH_W),
                  pl.BlockSpec((hh, ng, 8, GROUP), lambda b, h: (h, row_blk0 + b, 0, 0)),
                  pl.BlockSpec((hh, 8, 2), lambda b, h: (h, 0, 0)),
                  pl.BlockSpec((1, HEAD_W), lambda b, h: (0, 0)),
                  pl.BlockSpec((None, hh, 2, HEAD_W, HEAD_W), lambda b, h: (b, h, 0, 0, 0))],
        out_specs=(pl.BlockSpec((seq, width), lambda b, h: (b, h)),
                   pl.BlockSpec((None, hh, 2, HEAD_W, HEAD_W), lambda b, h: (b, h, 0, 0, 0))),
        scratch_shapes=[pltpu.VMEM((seq + 16, width), f32),
                        pltpu.VMEM((seq, width), f32), pltpu.VMEM((seq, width), f32),
                        pltpu.VMEM((seq, width), f32), pltpu.VMEM((hh, ng, 8, GROUP), f32),
                        pltpu.VMEM((nch, seq, HEAD_W), f32), pltpu.VMEM((nch, n, 2 * CHUNK, HEAD_W), bf16),
                        pltpu.VMEM((nch, n, CHUNK, CHUNK), bf16),
                        pltpu.VMEM((nch, seq, HEAD_W), bf16), pltpu.VMEM((nch, n, 8, HEAD_W), f32),
                        pltpu.VMEM((seq, width), f32), pltpu.VMEM((seq, width), f32),
                        pltpu.VMEM((nch, HEAD_W, HEAD_W), f32)],
        compiler_params=_cparams(("parallel", "parallel")),
        name="gdn",
    )(p, p, p, p, conv_w, conv_w, conv_w, grow, hp, norm_w, s0)


_N_LEVELS = 6
_N_MM_LEVELS = 4
_ROW_EQ = _N_MM_LEVELS * CHUNK
_ROW_EK, _ROW_TOT, _ROWS_EXP = _ROW_EQ + CHUNK, _ROW_EQ + 2 * CHUNK, _ROW_EQ + 2 * CHUNK + 8
_SCAN_UNROLL = 4


def _scan_consts():
    c = CHUNK
    mexp = np.zeros((2, _ROWS_EXP, c), np.float32)
    lvl = np.full((2, c, c), -1.0, np.float32)
    for lv in range(_N_LEVELS):
        s = 32 >> lv
        for i in range(c):
            p = (i // (2 * s)) * (2 * s) + s
            right = (i % (2 * s)) >= s
            if lv >= _N_MM_LEVELS:
                pass
            elif right:
                mexp[0, lv * c + i, p:i + 1] = 1.0
                mexp[1, lv * c + i, p:i] = 1.0
            else:
                mexp[0, lv * c + i, i + 1:p] = 1.0
                mexp[1, lv * c + i, i:p] = 1.0
            for j in range(c):
                if (i // (2 * s)) != (j // (2 * s)):
                    continue
                jright = (j % (2 * s)) >= s
                if right and not jright:
                    lvl[0, i, j] = lv
                if (not right) and jright:
                    lvl[1, i, j] = lv
    for i in range(c):
        lvl[:, i, i] = _N_LEVELS
        mexp[0, _ROW_EQ + i, :i + 1] = 1.0
        mexp[0, _ROW_EK + i, i + 1:] = 1.0
        mexp[1, _ROW_EQ + i, i:] = 1.0
        mexp[1, _ROW_EK + i, :i] = 1.0
    mexp[:, _ROW_TOT:, :] = 1.0
    return mexp, lvl


_LOG_DECAY_FLOOR = -1.0e4


def _scan_kernel(*refs, seq, hgrn, q_scale):
    if hgrn:
        q_ref, v_ref, gate_ref, zf_ref, zb_ref, lb_ref = refs[:6]
    else:
        q_ref, v_ref, gate_ref, k_ref, glr_ref, w2_ref, b2_ref = refs[:7]
    nw_ref, s0_ref, mexp_ref, lvl_ref, o_ref, so_ref, of, ob, qd_s, g_s, ct_s, stp_s, st_ref = refs[-13:]
    q_silu = hgrn
    gate_silu = not hgrn
    n = seq // CHUNK
    st_ref[...] = s0_ref[...]
    outs = (of, ob)

    def rows_of(c):
        return pl.ds(pl.multiple_of(c * CHUNK, CHUNK), CHUNK)

    def prepare_group(g, carry):
        todo = [(d, g * _SCAN_UNROLL + u) for u in range(_SCAN_UNROLL) for d in range(2)]
        items = range(len(todo))
        q, k, v, log_decay = [], [], [], []
        for d, c in todo:
            rows_c = rows_of(c)
            qi = q_ref[rows_c, :]
            q.append(jax.nn.silu(qi) if q_silu else qi * q_scale)
            v.append(v_ref[rows_c, :].astype(bf16))
        if hgrn:
            for d, c in todo:
                z = (zf_ref if d == 0 else zb_ref)[rows_of(c), :]
                lb = lb_ref[d:d + 1, :]
                t0 = jnp.log(lb)
                t1 = jnp.log1p(-lb) + jax.nn.log_sigmoid(z)
                mx = jnp.maximum(t0, t1)
                lse = mx + jnp.log(jnp.exp(t0 - mx) + jnp.exp(t1 - mx))
                log_decay.append(jnp.where(mx == -jnp.inf, -jnp.inf, lse))
                k.append((1.0 - lb) * jax.nn.sigmoid(-z))
        else:
            logits = [_dot(glr_ref[rows_of(c), :].astype(bf16), w2_ref[d]) for d, c in todo]
            log_decay = [jax.nn.log_sigmoid(logits[i] + b2_ref[d]) * (1.0 / GLA_TAU) for i, (d, _) in enumerate(todo)]
            k = [k_ref[rows_of(c), :] for _, c in todo]
        la = [jnp.maximum(x, _LOG_DECAY_FLOOR) for x in log_decay]
        e2 = []
        for i, (d, _) in enumerate(todo):
            hi, lo = _split2(la[i])
            e2.append(_dot(mexp_ref[d], jnp.concatenate([hi, lo], axis=1)))
        e = [x[:, :HEAD_W] + x[:, HEAD_W:] for x in e2]
        r4 = lax.broadcasted_iota(jnp.int32, (CHUNK, HEAD_W), 0) % 4
        fine = []
        for i, (d, _) in enumerate(todo):
            prev = pltpu.roll(la[i], 1, 0)
            nxt = pltpu.roll(la[i], CHUNK - 1, 0)
            if d == 0:
                w2 = jnp.where(r4 == 0, nxt, jnp.where(r4 == 1, 0.0, jnp.where(r4 == 2, la[i], la[i] + prev)))
                w1 = jnp.where(r4 % 2 == 1, la[i], 0.0)
            else:
                w2 = jnp.where(r4 == 0, la[i] + nxt, jnp.where(r4 == 1, la[i], jnp.where(r4 == 2, 0.0, prev)))
                w1 = jnp.where(r4 % 2 == 0, la[i], 0.0)
            fine.append((w2, w1))
        lvl = [lvl_ref[d] for d, _ in todo]
        qk = [_dot_nt(q[i].astype(bf16), k[i].astype(bf16)) for i in items]
        a = [jnp.where(lvl[i] == float(_N_LEVELS), qk[i], 0.0) for i in items]
        for lv in range(_N_LEVELS):
            if lv < _N_MM_LEVELS:
                w = [jnp.exp(e[i][lv * CHUNK:(lv + 1) * CHUNK]) for i in items]
            else:
                w = [jnp.exp(fine[i][lv - _N_MM_LEVELS]) for i in items]
            p = [_dot_nt((q[i] * w[i]).astype(bf16), (k[i] * w[i]).astype(bf16)) for i in items]
            a = [jnp.where(lvl[i] == float(lv), p[i], a[i]) for i in items]
        o_intra = [_dot(a[i].astype(bf16), v[i]) for i in items]
        incr = [_dot_tn(v[i], (k[i] * jnp.exp(e[i][_ROW_EK:_ROW_EK + CHUNK])).astype(bf16)) for i in items]
        for i, (d, c) in enumerate(todo):
            rows_c = rows_of(c)
            outs[d][rows_c, :] = o_intra[i]
            qd_s[d, rows_c, :] = (q[i] * jnp.exp(e[i][_ROW_EQ:_ROW_EQ + CHUNK])).astype(bf16)
            g_s[d, c] = incr[i]
            ct_s[d, c] = jnp.exp(e[i][_ROW_TOT:_ROW_TOT + 8])
        return carry

    lax.fori_loop(0, n // _SCAN_UNROLL, prepare_group, 0)

    def scan_step(i, carry):
        for d, c in ((0, i), (1, n - 1 - i)):
            st = st_ref[d]
            stp_s[d, c] = st.astype(bf16)
            st_ref[d] = st * ct_s[d, c][0:1, :] + g_s[d, c]
        return carry

    lax.fori_loop(0, n, scan_step, 0)

    def inter_group(g, carry):
        todo = [(d, g * _SCAN_UNROLL + u) for u in range(_SCAN_UNROLL) for d in range(2)]
        res = [_dot_nt(qd_s[d, rows_of(c), :], stp_s[d, c]) + outs[d][rows_of(c), :] for d, c in todo]
        for (d, c), o_c in zip(todo, res):
            outs[d][rows_of(c), :] = o_c
        return carry

    lax.fori_loop(0, n // _SCAN_UNROLL, inter_group, 0)
    o = of[...] + ob[...]
    g = gate_ref[...]
    g = jax.nn.silu(g) if gate_silu else jax.nn.sigmoid(g)
    o_ref[...] = (_rmsnorm(o, nw_ref[...]) * g).astype(bf16)
    so_ref[...] = st_ref[...]


def _scan(p, offs, extra, norm_w, s0t, mexp, lvl, *, nseq, seq, row_blk0, hgrn, q_scale):
    def pspec(off, per_head=True):
        return pl.BlockSpec((seq, HEAD_W), lambda b, h: (row_blk0 + b, off // HEAD_W + (h if per_head else 0)))

    if hgrn:
        (lower_bound,) = extra
        in_specs = [pspec(o) for o in offs] + [pl.BlockSpec((2, HEAD_W), lambda b, h: (0, h))]
        args = [p] * len(offs) + [lower_bound]
    else:
        w2, b2 = extra
        in_specs = ([pspec(o) for o in offs[:4]] + [pspec(offs[4], per_head=False)]
                    + [pl.BlockSpec((2, HEAD_W, HEAD_W), lambda b, h: (0, 0, h)),
                       pl.BlockSpec((2, 1, HEAD_W), lambda b, h: (0, 0, h))])
        args = [p] * len(offs) + [w2, b2]
    kern = functools.partial(_scan_kernel, seq=seq, hgrn=hgrn, q_scale=q_scale)
    return pl.pallas_call(
        kern,
        out_shape=(jax.ShapeDtypeStruct((nseq * seq, BRANCH_W), bf16),
                   jax.ShapeDtypeStruct((nseq, N_HEADS, 2, HEAD_W, HEAD_W), f32)),
        grid=(nseq, N_HEADS),
        in_specs=in_specs + [
            pl.BlockSpec((1, HEAD_W), lambda b, h: (0, 0)),
            pl.BlockSpec((None, None, 2, HEAD_W, HEAD_W), lambda b, h: (b, h, 0, 0, 0)),
            pl.BlockSpec((2, _ROWS_EXP, CHUNK), lambda b, h: (0, 0, 0)),
            pl.BlockSpec((2, CHUNK, CHUNK), lambda b, h: (0, 0, 0))],
        out_specs=(pl.BlockSpec((seq, HEAD_W), lambda b, h: (b, h)),
                   pl.BlockSpec((None, None, 2, HEAD_W, HEAD_W), lambda b, h: (b, h, 0, 0, 0))),
        scratch_shapes=[pltpu.VMEM((seq, HEAD_W), f32), pltpu.VMEM((seq, HEAD_W), f32),
                        pltpu.VMEM((2, seq, HEAD_W), bf16),
                        pltpu.VMEM((2, seq // CHUNK, HEAD_W, HEAD_W), f32),
                        pltpu.VMEM((2, seq // CHUNK, 8, HEAD_W), f32),
                        pltpu.VMEM((2, seq // CHUNK, HEAD_W, HEAD_W), bf16),
                        pltpu.VMEM((2, HEAD_W, HEAD_W), f32)],
        compiler_params=_cparams(("parallel", "parallel")),
        name="decay_scan",
    )(*args, norm_w, s0t, mexp, lvl)


def _mla_proj_kernel(*refs, rope):
    if rope:
        (qa_ref, kva_ref, sm_ref, qnw_ref, kvnw_ref, wqa_ref, wkv_ref, wqb_ref, cos_ref, sin_ref,
         qn_ref, qp_ref, ckv_ref, kn_ref, vv_ref, kp_ref) = refs
    else:
        (qa_ref, kva_ref, sm_ref, qnw_ref, kvnw_ref, wqa_ref, wkv_ref,
         qn_ref, qp_ref, ckv_ref, kn_ref, vv_ref, kp_ref) = refs
    qh = _rmsnorm(qa_ref[...], qnw_ref[...]).astype(bf16)
    qa = _dot(qh, wqa_ref[...])
    qn_ref[...] = qa[:, :BRANCH_W].astype(bf16)
    pe = qa[:, BRANCH_W:]
    kpe = sm_ref[:, 0:LANE]
    if rope:
        cos = cos_ref[...]
        sin = sin_ref[...]
        cos4 = jnp.concatenate([cos] * N_HEADS, axis=1)
        sin4 = jnp.concatenate([sin] * N_HEADS, axis=1)
        pe = pe * cos4 + _dot(qh, wqb_ref[...]) * sin4
        kpe = kpe * cos + pltpu.roll(kpe, MLA_ROPE, 1) * sin
    qp_ref[...] = pe.astype(bf16)
    kp_ref[...] = kpe.astype(bf16)
    ckv = _rmsnorm(kva_ref[...], kvnw_ref[...])
    ckv_ref[...] = ckv
    kv = _dot(ckv.astype(bf16), wkv_ref[...])
    kn_ref[...] = kv[:, :BRANCH_W].astype(bf16)
    vv_ref[...] = kv[:, BRANCH_W:].astype(bf16)


def _mla_proj(p, qnw, kvnw, wqa, wkv, wqb, cos, sin, *, nrows, seq, row0):
    tm = 256
    rope = cos is not None
    rb0 = row0 // tm
    per_seq = seq // tm

    def pspec(off):
        return pl.BlockSpec((tm, _SEG), lambda i: (rb0 + i, off // _SEG))

    def full(a):
        return pl.BlockSpec(a.shape, lambda i: (0,) * a.ndim)

    in_specs = [pspec(OFF_MLA_QA), pspec(OFF_MLA_KVA), pspec(OFF_SMALL), full(qnw), full(kvnw), full(wqa), full(wkv)]
    args = [p, p, p, qnw, kvnw, wqa, wkv]
    if rope:
        tspec = pl.BlockSpec((tm, LANE), lambda i: (i % per_seq, 0))
        in_specs += [full(wqb), tspec, tspec]
        args += [wqb, cos, sin]
    wide = lambda dt: jax.ShapeDtypeStruct((nrows, BRANCH_W), dt)
    ospec = pl.BlockSpec((tm, BRANCH_W), lambda i: (i, 0))
    return pl.pallas_call(
        functools.partial(_mla_proj_kernel, rope=rope),
        out_shape=(wide(bf16), wide(bf16), wide(f32), wide(bf16), wide(bf16),
                   jax.ShapeDtypeStruct((nrows, LANE), bf16)),
        grid=(nrows // tm,),
        in_specs=in_specs,
        out_specs=(ospec, ospec, ospec, ospec, ospec, pl.BlockSpec((tm, LANE), lambda i: (i, 0))),
        compiler_params=_cparams(("parallel",)),
        name="mla_proj",
    )(*args)


def _kv_kernel(ckv_ref, w_ref, kn_ref, vv_ref):
    kv = _dot(ckv_ref[...].astype(bf16), w_ref[...])
    kn_ref[...] = kv[:, :BRANCH_W].astype(bf16)
    vv_ref[...] = kv[:, BRANCH_W:].astype(bf16)


def _kv_proj(ckv, wkv):
    rows = ckv.shape[0]
    tm = 256
    out = jax.ShapeDtypeStruct((rows, BRANCH_W), bf16)
    ospec = pl.BlockSpec((tm, BRANCH_W), lambda i: (i, 0))
    return pl.pallas_call(
        _kv_kernel, out_shape=(out, out), grid=(rows // tm,),
        in_specs=[pl.BlockSpec((tm, ckv.shape[1]), lambda i: (i, 0)),
                  pl.BlockSpec(wkv.shape, lambda i: (0, 0))],
        out_specs=(ospec, ospec),
        compiler_params=_cparams(("parallel",)),
        name="mla_ctx_kv",
    )(ckv, wkv)


def _attn_kernel(*refs, has_ctx):
    if has_ctx:
        qn_ref, qp_ref, kn_ref, kp_ref, vv_ref, knc_ref, kpc_ref, vvc_ref, o_ref = refs
    else:
        qn_ref, qp_ref, kn_ref, kp_ref, vv_ref, o_ref = refs
    scale = (MLA_NOPE + MLA_ROPE) ** -0.5
    heads = range(N_HEADS)

    def head(ref, h):
        return ref[:, h * HEAD_W:(h + 1) * HEAD_W]

    kp = kp_ref[...]
    s1 = [(_dot_nt(head(qn_ref, h), head(kn_ref, h)) + _dot_nt(head(qp_ref, h), kp)) * scale for h in heads]
    mx = [jnp.max(x, axis=-1, keepdims=True) for x in s1]
    if has_ctx:
        kpc = kpc_ref[...]
        s2 = [(_dot_nt(head(qn_ref, h), head(knc_ref, h)) + _dot_nt(head(qp_ref, h), kpc)) * scale for h in heads]
        mx = [jnp.maximum(m, jnp.max(x, axis=-1, keepdims=True)) for m, x in zip(mx, s2)]
    p1 = [jnp.exp(x - m) for x, m in zip(s1, mx)]
    den = [jnp.sum(x, axis=-1, keepdims=True) for x in p1]
    if has_ctx:
        p2 = [jnp.exp(x - m) for x, m in zip(s2, mx)]
        den = [d_ + jnp.sum(x, axis=-1, keepdims=True) for d_, x in zip(den, p2)]
    inv = [1.0 / d_ for d_ in den]
    o = [_dot((p1[h] * inv[h]).astype(bf16), head(vv_ref, h)) for h in heads]
    if has_ctx:
        o = [o[h] + _dot((p2[h] * inv[h]).astype(bf16), head(vvc_ref, h)) for h in heads]
    o_ref[...] = jnp.concatenate(o, axis=1).astype(bf16)


def _attention(qn, qp, kn, kp, vv, ctx, *, nseq, seq):
    tq = 256
    nq = seq // tq
    has_ctx = ctx is not None
    qspec = pl.BlockSpec((tq, BRANCH_W), lambda b, i: (b * nq + i, 0))
    kspec = pl.BlockSpec((seq, BRANCH_W), lambda b, i: (b, 0))
    kpspec = pl.BlockSpec((seq, LANE), lambda b, i: (b, 0))
    in_specs = [qspec, qspec, kspec, kpspec, kspec]
    args = [qn, qp, kn, kp, vv]
    if has_ctx:
        knc, kpc, vvc = ctx
        lc = knc.shape[0] // nseq
        in_specs += [pl.BlockSpec((lc, BRANCH_W), lambda b, i: (b, 0)),
                     pl.BlockSpec((lc, LANE), lambda b, i: (b, 0)),
                     pl.BlockSpec((lc, BRANCH_W), lambda b, i: (b, 0))]
        args += [knc, kpc, vvc]
    return pl.pallas_call(
        functools.partial(_attn_kernel, has_ctx=has_ctx),
        out_shape=jax.ShapeDtypeStruct((nseq * seq, BRANCH_W), bf16),
        grid=(nseq, nq),
        in_specs=in_specs,
        out_specs=pl.BlockSpec((tq, BRANCH_W), lambda b, i: (b * nq + i, 0)),
        compiler_params=_cparams(("parallel", "parallel")),
        name="mla_attn",
    )(*args)


def _merge_kernel(o0_ref, o1_ref, o2_ref, o3_ref, g0_ref, g1_ref, g2_ref, g3_ref, bg_ref, wb_ref, wo_ref,
                  x_ref, mod_ref, lng_ref, lnb_ref, out_ref, *, alpha):
    m = None
    for k, (o_ref, g_ref) in enumerate(((o0_ref, g0_ref), (o1_ref, g1_ref), (o2_ref, g2_ref), (o3_ref, g3_ref))):
        term = jax.nn.sigmoid(g_ref[...].astype(f32) + bg_ref[k:k + 1, :]) * _dot(o_ref[...], wb_ref[k])
        m = term if m is None else m + term
    mix = _dot(m.astype(bf16), wo_ref[...])
    gate1 = mod_ref[2:3, :]
    out_ref[...] = _layernorm(alpha * x_ref[...] + gate1 * mix, lng_ref[...], lnb_ref[...])


def _merge(branches, p, bg, wb, wo, x, mod, lng, lnb, *, alpha, t_ctx, l_lat):
    t, d = x.shape
    tm = 256
    midx = _mod_index(tm, t_ctx, l_lat)
    ospec = pl.BlockSpec((tm, BRANCH_W), lambda i: (i, 0))
    gspecs = [pl.BlockSpec((tm, d), functools.partial(lambda i, k: (i, k), k=k)) for k in range(4)]
    single = dict(pipeline_mode=pl.Buffered(1))
    return pl.pallas_call(
        functools.partial(_merge_kernel, alpha=alpha),
        out_shape=jax.ShapeDtypeStruct((t, d), f32),
        grid=(t // tm,),
        in_specs=[ospec, ospec, ospec, ospec] + gspecs + [
            pl.BlockSpec((4, d), lambda i: (0, 0)),
            pl.BlockSpec((4, BRANCH_W, d), lambda i: (0, 0, 0), **single),
            pl.BlockSpec((d, d), lambda i: (0, 0), **single),
            pl.BlockSpec((tm, d), lambda i: (i, 0)),
            pl.BlockSpec((None, 6, d), lambda i: (midx(i), 0, 0)),
            pl.BlockSpec((1, d), lambda i: (0, 0)),
            pl.BlockSpec((1, d), lambda i: (0, 0))],
        out_specs=pl.BlockSpec((tm, d), lambda i: (i, 0)),
        compiler_params=_cparams(("parallel",)),
        name="merge_out_ln",
    )(*branches, p, p, p, p, bg, wb, wo, x, mod, lng, lnb)


def _ffn_kernel(x_ref, mod_ref, w1_ref, w3_ref, w2_ref, lng_ref, lnb_ref, *rest, alpha, ctx_tiles):
    if ctx_tiles is None:
        out_ref, hb_ref, acc_ref = rest
    else:
        out_ref, out_lat_ref, hb_ref, acc_ref = rest
    f = pl.program_id(1)

    @pl.when(f == 0)
    def _():
        shift = mod_ref[3:4, :]
        scale = mod_ref[4:5, :]
        hb_ref[...] = (x_ref[...] * (1.0 + scale) + shift).astype(bf16)
        acc_ref[...] = jnp.zeros_like(acc_ref)

    h = hb_ref[...]
    g = (jax.nn.silu(_dot(h, w1_ref[...])) * _dot(h, w3_ref[...])).astype(bf16)
    acc_ref[...] += _dot(g, w2_ref[...])

    last = f == pl.num_programs(1) - 1

    def result():
        gate2 = mod_ref[5:6, :]
        return _layernorm(alpha * x_ref[...] + gate2 * acc_ref[...], lng_ref[...], lnb_ref[...])

    if ctx_tiles is None:
        @pl.when(last)
        def _():
            out_ref[...] = result()
    else:
        is_ctx = pl.program_id(0) < ctx_tiles

        @pl.when(jnp.logical_and(last, is_ctx))
        def _():
            out_ref[...] = result()

        @pl.when(jnp.logical_and(last, jnp.logical_not(is_ctx)))
        def _():
            out_lat_ref[...] = result()


def _ffn(x, mod, w1, w3, w2, lng, lnb, *, alpha, t_ctx, l_lat, split):
    t, d = x.shape
    dff = w1.shape[1]
    tm, tf = 512, 512
    midx = _mod_index(tm, t_ctx, l_lat)
    if split:
        ctx_tiles = t_ctx // tm
        out_shape = (jax.ShapeDtypeStruct((t_ctx, d), f32), jax.ShapeDtypeStruct((t - t_ctx, d), f32))
        out_specs = (pl.BlockSpec((tm, d), lambda i, f: (jnp.minimum(i, ctx_tiles - 1), 0)),
                     pl.BlockSpec((tm, d), lambda i, f: (jnp.maximum(i - ctx_tiles, 0), 0)))
    else:
        ctx_tiles = None
        out_shape = jax.ShapeDtypeStruct((t, d), f32)
        out_specs = pl.BlockSpec((tm, d), lambda i, f: (i, 0))
    return pl.pallas_call(
        functools.partial(_ffn_kernel, alpha=alpha, ctx_tiles=ctx_tiles),
        out_shape=out_shape,
        grid=(t // tm, dff // tf),
        in_specs=[pl.BlockSpec((tm, d), lambda i, f: (i, 0)),
                  pl.BlockSpec((None, 6, d), lambda i, f: (midx(i), 0, 0)),
                  pl.BlockSpec((d, tf), lambda i, f: (0, f)),
                  pl.BlockSpec((d, tf), lambda i, f: (0, f)),
                  pl.BlockSpec((tf, d), lambda i, f: (f, 0)),
                  pl.BlockSpec((1, d), lambda i, f: (0, 0)),
                  pl.BlockSpec((1, d), lambda i, f: (0, 0))],
        out_specs=out_specs,
        scratch_shapes=[pltpu.VMEM((tm, d), bf16), pltpu.VMEM((tm, d), f32)],
        compiler_params=_cparams(("arbitrary" if split else "parallel", "arbitrary")),
        name="ffn_ln",
    )(x, mod, w1, w3, w2, lng, lnb)


def _cast_kernel(x_ref, o_ref):
    o_ref[...] = x_ref[...].astype(bf16)


def _cast_layer(w, layer):
    _, rows, cols = w.shape
    tr = 256 if cols > 4096 else 512
    return pl.pallas_call(
        _cast_kernel,
        out_shape=jax.ShapeDtypeStruct((rows, cols), bf16),
        grid=(rows // tr,),
        in_specs=[pl.BlockSpec((None, tr, cols), lambda i: (layer, i, 0))],
        out_specs=pl.BlockSpec((tr, cols), lambda i: (i, 0)),
        compiler_params=_cparams(("parallel",)),
        name="cast_bf16",
    )(w)


_ROW_UNIT = 16


def _relayout_plan():
    plain = lambda src: (0, src)
    special = lambda k: (1, k)
    segs = [plain(_SRC[n]) for n in ('gdn_q', 'gdn_k', 'gdn_v', 'gdn_z')]
    segs += [special(0), special(1)]
    segs += [plain(_SRC['gla_v']), plain(_SRC['gla_r']), plain(_SRC['hg_q']), plain(_SRC['hg_f']),
             plain(_SRC['hg_f'] + 512), plain(_SRC['hg_i']), plain(_SRC['hg_g']), plain(_SRC['mla_qa']),
             plain(_SRC['mla_kva'])]
    segs += [special(2)]
    segs += [plain(_SRC['gates'] + _SEG * i) for i in range(N_HEADS * D_MODEL // _SEG)]
    assert len(segs) == N_PROJ // _SEG
    table = np.zeros((3, len(segs)), np.int32)
    row, blk = segs[0][1], 0
    for j, (is_special, val) in enumerate(segs):
        if is_special:
            blk = val
        else:
            row = val
        assert row % _ROW_UNIT == 0
        table[:, j] = (is_special, row // _ROW_UNIT, blk)
    return table


def _special_rows(wt, layer):
    d = wt.shape[2]

    def rows(name, start, n):
        s = _SRC[name] + start
        return lax.slice(wt, (layer, s, 0), (layer + 1, s + n, d))[0]

    def head_padded(name):
        parts = []
        for h in range(N_HEADS):
            parts += [rows(name, h * GLA_DK, GLA_DK), jnp.zeros((HEAD_W - GLA_DK, d), wt.dtype)]
        return parts

    q4 = MLA_ROPE // 4
    small = [rows('mla_kpe', 0, MLA_ROPE), rows('mla_kpe', q4, q4), rows('mla_kpe', 0, q4),
             rows('mla_kpe', 3 * q4, q4), rows('mla_kpe', 2 * q4, q4), rows('gla_g', 0, 2 * GLA_RANK),
             rows('gdn_b', 0, 8), rows('gdn_a', 0, 8), jnp.zeros((_SEG - SM_GDN_A - 8, d), wt.dtype)]
    return jnp.concatenate(head_padded('gla_q') + head_padded('gla_k') + small, axis=0)


def _relayout_kernel(tab_ref, wt_ref, sp_ref, o_ref):
    is_special = tab_ref[0, pl.program_id(1)]

    @pl.when(is_special == 0)
    def _():
        o_ref[...] = wt_ref[0].astype(bf16)

    @pl.when(is_special == 1)
    def _():
        o_ref[...] = sp_ref[...].astype(bf16)


def _relayout_w_in(w_in):
    depth, d, _ = w_in.shape
    wt = jnp.swapaxes(w_in, 1, 2)
    special = jnp.stack([_special_rows(wt, l) for l in range(depth)])
    table = _relayout_plan()
    nseg = N_PROJ // _SEG
    return pl.pallas_call(
        _relayout_kernel,
        out_shape=jax.ShapeDtypeStruct((depth, N_PROJ, d), bf16),
        grid_spec=pltpu.PrefetchScalarGridSpec(
            num_scalar_prefetch=1,
            grid=(depth, nseg),
            in_specs=[pl.BlockSpec((pl.Element(1), pl.Element(_SEG), pl.Element(d)),
                                   lambda l, j, tab: (l, tab[1, j] * _ROW_UNIT, 0)),
                      pl.BlockSpec((None, _SEG, d), lambda l, j, tab: (l, tab[2, j], 0))],
            out_specs=pl.BlockSpec((None, _SEG, d), lambda l, j, tab: (l, j, 0))),
        compiler_params=_cparams(("parallel", "parallel")),
        name="w_in_relayout",
    )(jnp.asarray(table), wt, special)


def _relayout_wq(wq):
    hw = MLA_NOPE + MLA_ROPE
    z = jnp.zeros((wq.shape[0], HEAD_W - MLA_ROPE), wq.dtype)
    q4 = MLA_ROPE // 4
    nope, pe, pes = [], [], []
    for h in range(N_HEADS):
        base = h * hw
        nope.append(wq[:, base:base + MLA_NOPE])
        r = wq[:, base + MLA_NOPE:base + hw]
        pe += [r, z]
        pes += [r[:, q4:2 * q4], r[:, 0:q4], r[:, 3 * q4:], r[:, 2 * q4:3 * q4], z]
    return jnp.concatenate(nope + pe, axis=1).astype(bf16), jnp.concatenate(pes, axis=1).astype(bf16)


def _relayout_wkv(wkv):
    hw = MLA_NOPE + HEAD_W
    kn = [wkv[:, h * hw:h * hw + MLA_NOPE] for h in range(N_HEADS)]
    vv = [wkv[:, h * hw + MLA_NOPE:(h + 1) * hw] for h in range(N_HEADS)]
    return jnp.concatenate(kn + vv, axis=1).astype(bf16)


def _rope_tables(length):
    pos = jnp.arange(length)
    row_id = (pos // GRID_W).astype(f32)
    col_id = (pos % GRID_W).astype(f32)
    half = MLA_ROPE // 2
    inv = ROPE_BASE ** (-jnp.arange(0, half, 2, dtype=f32) / half)
    ar, ac = row_id[:, None] * inv, col_id[:, None] * inv
    z = jnp.zeros((length, LANE - MLA_ROPE), f32)
    cos = jnp.concatenate([jnp.cos(ar), jnp.cos(ar), jnp.cos(ac), jnp.cos(ac), z], axis=1)
    sin = jnp.concatenate([-jnp.sin(ar), jnp.sin(ar), -jnp.sin(ac), jnp.sin(ac), z], axis=1)
    return cos, sin


def _gla_gate_weights(w2, b):
    assert SM_GLA_G % LANE == 0
    lane_pad = ((0, 0), (0, 0), (0, 0), (0, HEAD_W - GLA_DK))
    w4 = jnp.pad(w2.reshape(2, GLA_RANK, N_HEADS, GLA_DK), lane_pad).reshape(2, GLA_RANK, BRANCH_W)
    wbig = jnp.stack([jnp.pad(w4[d], ((d * GLA_RANK, LANE - (d + 1) * GLA_RANK), (0, 0))) for d in range(2)])
    bbig = jnp.pad(b.reshape(2, 1, N_HEADS, GLA_DK), lane_pad).reshape(2, 1, BRANCH_W)
    return wbig.astype(bf16), bbig


def kernel(x_prompt, x_sample, c, state_gdn, state_gla, state_hgrn, cache_mla_ckv, cache_mla_kpe, c_ctx, w_ada, b_ada, w_in, gdn_conv, gdn_a_log, gdn_dt_bias, gdn_norm, gla_gate_w2, gla_gate_b, gla_norm, hgrn_lb, hgrn_norm, mla_q_norm, mla_wq_b, mla_kv_norm, mla_wkv_b, w_branch, b_gates, w_out, ln1_g, ln1_b, ln2_g, ln2_b, ffn_w1, ffn_w3, ffn_w2):
    nb_c, l_c, d = x_prompt.shape
    nb_l, l_l, _ = x_sample.shape
    depth = w_in.shape[0]
    t_c, t_l = nb_c * l_c, nb_l * l_l
    past = cache_mla_ckv.shape[2]
    alpha = (2.0 * depth) ** 0.25
    assert d == D_MODEL and t_c % 1024 == 0 and l_l % 1024 == 0 and l_c % CHUNK == 0

    n_cond = 1 + nb_l
    cc = jnp.concatenate([c_ctx[None, :], c, jnp.zeros((-n_cond % 8, d), f32)], axis=0)
    mods = _ada(cc, w_ada, b_ada).reshape(depth, cc.shape[0], 6, d)

    w_in_r = _relayout_w_in(w_in)
    mexp_np, lvl_np = _scan_consts()
    mexp = jnp.asarray(mexp_np, bf16)
    lvl = jnp.asarray(lvl_np, f32)
    cos_t, sin_t = _rope_tables(l_l)
    cum = jnp.cumsum(jax.nn.softmax(hgrn_lb.astype(f32), axis=0), axis=0)
    lower_bounds = cum - cum[:1]

    x = jnp.concatenate([x_prompt.reshape(t_c, d), x_sample.reshape(t_l, d)], axis=0)
    t = t_c + t_l
    zero_state = jnp.zeros((nb_c, N_HEADS, 2, HEAD_W, HEAD_W), f32)
    streams = (dict(nseq=nb_c, seq=l_c, row0=0), dict(nseq=nb_l, seq=l_l, row0=t_c))
    new_gdn, new_gla, new_hg, new_ckv, new_kpe = [], [], [], [], []

    for l in range(depth):
        mod = mods[l]
        p, p_gates = _inproj(x, mod, w_in_r, l, t_c, l_l)

        gsm = p[:, OFF_SMALL + SM_GDN_B:OFF_SMALL + SM_GDN_B + 16].reshape(t // GROUP, GROUP, 4, N_HEADS)
        grow = jnp.pad(gsm.transpose(3, 0, 2, 1), ((0, 0), (0, 0), (0, 4), (0, 0)))
        zc = jnp.zeros((N_HEADS, 2), f32)
        hp = jnp.stack([jnp.concatenate([zc, gdn_a_log[l].T, zc, zc], axis=1),
                        jnp.concatenate([zc, gdn_dt_bias[l].T, zc, zc], axis=1)], axis=-1)

        gla_gate = _gla_gate_weights(gla_gate_w2[l], gla_gate_b[l])
        wqa, wqb = _relayout_wq(mla_wq_b[l])
        wkv = _relayout_wkv(mla_wkv_b[l])
        qnw, kvnw = mla_q_norm[l][None, :], mla_kv_norm[l][None, :]

        outs = {k: [] for k in ('gdn', 'gla', 'hg', 'mla')}
        for si, st in enumerate(streams):
            nseq, seq, row0 = st['nseq'], st['seq'], st['row0']
            rb = row0 // seq
            if si == 0:
                s_gdn0 = s_gla0 = s_hg0 = zero_state
            else:
                s_gdn0 = state_gdn[:, l].transpose(0, 2, 1, 3, 4)
                s_gla0 = jnp.pad(state_gla[:, l], ((0, 0),) * 3 + ((0, HEAD_W - GLA_DK), (0, 0))).transpose(0, 2, 1, 4, 3)
                s_hg0 = state_hgrn[:, l].transpose(0, 2, 1, 4, 3)
            o_gdn, s_gdn = _gdn(p, gdn_conv[l], grow, hp, gdn_norm[l][None, :], s_gdn0, nseq=nseq, seq=seq, row_blk0=rb)
            o_gla, s_gla = _scan(p, (OFF_GLA_Q, OFF_GLA_V, OFF_GLA_R, OFF_GLA_K, OFF_SMALL + SM_GLA_G), gla_gate,
                                 gla_norm[l][None, :], s_gla0, mexp, lvl,
                                 nseq=nseq, seq=seq, row_blk0=rb, hgrn=False, q_scale=GLA_DK ** -0.5)
            o_hg, s_hg = _scan(p, (OFF_HG_Q, OFF_HG_I, OFF_HG_G, OFF_HG_FF, OFF_HG_FB), (lower_bounds[l],),
                               hgrn_norm[l][None, :], s_hg0, mexp, lvl,
                               nseq=nseq, seq=seq, row_blk0=rb, hgrn=True, q_scale=1.0)
            rope = si == 1
            qn, qp, ckv, kn, vv, kp = _mla_proj(p, qnw, kvnw, wqa, wkv, wqb if rope else None,
                                                cos_t if rope else None, sin_t if rope else None,
                                                nrows=nseq * seq, seq=seq, row0=row0)
            ctx = None
            if si == 1:
                knc, vvc = _kv_proj(cache_mla_ckv[:, l].reshape(nb_l * past, -1), wkv)
                kpc = jnp.pad(cache_mla_kpe[:, l].reshape(nb_l * past, MLA_ROPE), ((0, 0), (0, LANE - MLA_ROPE))).astype(bf16)
                ctx = (knc, kpc, vvc)
            o_mla = _attention(qn, qp, kn, kp, vv, ctx, nseq=nseq, seq=seq)
            outs['gdn'].append(o_gdn)
            outs['gla'].append(o_gla)
            outs['hg'].append(o_hg)
            outs['mla'].append(o_mla)
            if si == 0:
                new_gdn.append(s_gdn.transpose(0, 2, 1, 3, 4))
                new_gla.append(s_gla.transpose(0, 2, 1, 4, 3)[:, :, :, :GLA_DK, :])
                new_hg.append(s_hg.transpose(0, 2, 1, 4, 3))
                new_ckv.append(ckv.reshape(nb_c, l_c, -1))
                new_kpe.append(p[:t_c, OFF_SMALL + SM_KPE:OFF_SMALL + SM_KPE + MLA_ROPE].reshape(nb_c, l_c, MLA_ROPE))

        branches = [jnp.concatenate(outs[k], axis=0) for k in ('gdn', 'gla', 'hg', 'mla')]
        wb = _cast_layer(w_branch.reshape(depth, 4 * BRANCH_W, d), l).reshape(4, BRANCH_W, d)
        x1 = _merge(branches, p_gates, b_gates[l], wb, _cast_layer(w_out, l), x, mod,
                    ln1_g[l][None, :], ln1_b[l][None, :], alpha=alpha, t_ctx=t_c, l_lat=l_l)
        x = _ffn(x1, mod, _cast_layer(ffn_w1, l), _cast_layer(ffn_w3, l), _cast_layer(ffn_w2, l),
                 ln2_g[l][None, :], ln2_b[l][None, :], alpha=alpha, t_ctx=t_c, l_lat=l_l, split=l == depth - 1)

    y_ctx, y_lat = x
    sdt = x_prompt.dtype
    return (y_ctx.reshape(nb_c, l_c, d), y_lat.reshape(nb_l, l_l, d),
            jnp.stack(new_gdn, axis=1).astype(sdt), jnp.stack(new_gla, axis=1).astype(sdt),
            jnp.stack(new_hg, axis=1).astype(sdt), jnp.stack(new_ckv, axis=1), jnp.stack(new_kpe, axis=1))
```

```python
import functools

import numpy as np
import jax
import jax.numpy as jnp
from jax import lax
from jax.experimental import pallas as pl
from jax.experimental.pallas import tpu as pltpu

f32 = jnp.float32
bf16 = jnp.bfloat16

D_MODEL = 2048
N_HEADS = 4
HEAD_W = 128
BRANCH_W = N_HEADS * HEAD_W
GLA_DK = 64
GLA_RANK = 16
GLA_TAU = 16.0
GDN_CONV = 5
MLA_NOPE = 128
MLA_ROPE = 64
ROPE_BASE = 10000.0
GRID_W = 64
NORM_EPS = 1e-6
CHUNK = 64
SUB = 16
LANE = 128
VMEM_LIMIT = 56 * 1024 * 1024

_SEG = 512
OFF_GDN_Q, OFF_GDN_K, OFF_GDN_V, OFF_GDN_Z = 0, 512, 1024, 1536
OFF_GLA_Q, OFF_GLA_K, OFF_GLA_V, OFF_GLA_R = 2048, 2560, 3072, 3584
OFF_HG_Q, OFF_HG_FF, OFF_HG_FB, OFF_HG_I, OFF_HG_G = 4096, 4608, 5120, 5632, 6144
OFF_MLA_QA, OFF_MLA_KVA, OFF_SMALL, OFF_GATES = 6656, 7168, 7680, 8192
N_PROJ = OFF_GATES + N_HEADS * D_MODEL
SM_KPE, SM_KPE_SW, SM_GLA_G, SM_GDN_B, SM_GDN_A = 0, 64, 128, 160, 168

_SRC = {}
_o = 0
for _n, _w in (('gdn_q', 512), ('gdn_k', 512), ('gdn_v', 512), ('gdn_z', 512), ('gdn_b', 8), ('gdn_a', 8),
               ('gla_q', 256), ('gla_k', 256), ('gla_v', 512), ('gla_r', 512), ('gla_g', 32),
               ('hg_q', 512), ('hg_f', 1024), ('hg_i', 512), ('hg_g', 512),
               ('mla_qa', 512), ('mla_kva', 512), ('mla_kpe', 64), ('gates', 4 * D_MODEL)):
    _SRC[_n] = _o
    _o += _w
IN_WIDTH = _o


def _cparams(sem):
    return pltpu.CompilerParams(dimension_semantics=sem, vmem_limit_bytes=VMEM_LIMIT)


def _dot(a, b):
    return jnp.dot(a, b, preferred_element_type=f32)


def _dot_nt(a, b):
    return lax.dot_general(a, b, (((1,), (1,)), ((), ())), preferred_element_type=f32)


def _dot_tn(a, b):
    return lax.dot_general(a, b, (((0,), (0,)), ((), ())), preferred_element_type=f32)


def _split2(x):
    hi = x.astype(bf16)
    lo = (x - hi.astype(f32)).astype(bf16)
    return hi, lo


def _mm3(a, b):
    ah, al = _split2(a)
    bh, bl = _split2(b)
    return _dot(ah, bh) + (_dot(ah, bl) + _dot(al, bh))


def _layernorm(y, g, b):
    mu = jnp.mean(y, axis=-1, keepdims=True)
    yc = y - mu
    var = jnp.mean(yc * yc, axis=-1, keepdims=True)
    return yc * lax.rsqrt(var + NORM_EPS) * g + b


def _rmsnorm(y, w):
    return y * lax.rsqrt(jnp.mean(y * y, axis=-1, keepdims=True) + NORM_EPS) * w


def _ada_kernel(c_ref, w_ref, b_ref, o_ref):
    cs = jax.nn.silu(c_ref[...]).astype(bf16)
    o_ref[...] = _dot(cs, w_ref[...].astype(bf16)) + b_ref[...]


def _ada(cc, w_ada, b_ada):
    depth, d, n6 = w_ada.shape
    rows = cc.shape[0]
    tn = 1024
    return pl.pallas_call(
        _ada_kernel,
        out_shape=jax.ShapeDtypeStruct((depth, rows, n6), f32),
        grid=(depth, n6 // tn),
        in_specs=[pl.BlockSpec((rows, d), lambda l, j: (0, 0)),
                  pl.BlockSpec((None, d, tn), lambda l, j: (l, 0, j)),
                  pl.BlockSpec((None, 1, tn), lambda l, j: (l, 0, j))],
        out_specs=pl.BlockSpec((None, rows, tn), lambda l, j: (l, 0, j)),
        compiler_params=_cparams(("parallel", "parallel")),
        name="ada_mod",
    )(cc, w_ada, b_ada.reshape(depth, 1, n6))


def _inproj_kernel(x_ref, mod_ref, w_ref, o_ref, xb_ref):
    @pl.when(pl.program_id(1) == 0)
    def _():
        shift = mod_ref[0:1, :]
        scale = mod_ref[1:2, :]
        xb_ref[...] = (x_ref[...] * (1.0 + scale) + shift).astype(bf16)

    o_ref[...] = _dot_nt(xb_ref[...], w_ref[...])


def _mod_index(tm, t_ctx, l_lat):
    def index(i):
        r = i * tm
        return jnp.where(r < t_ctx, 0, 1 + (r - t_ctx) // l_lat)
    return index


def _inproj(x, mod, w, layer, t_ctx, l_lat):
    t, d = x.shape
    n = w.shape[1]
    tm, tn = 1024, 1024
    midx = _mod_index(tm, t_ctx, l_lat)
    return pl.pallas_call(
        _inproj_kernel,
        out_shape=jax.ShapeDtypeStruct((t, n), f32),
        grid=(t // tm, n // tn),
        in_specs=[pl.BlockSpec((tm, d), lambda i, j: (i, 0)),
                  pl.BlockSpec((None, 6, d), lambda i, j: (midx(i), 0, 0)),
                  pl.BlockSpec((None, tn, d), lambda i, j: (layer, j, 0))],
        out_specs=pl.BlockSpec((tm, tn), lambda i, j: (i, j)),
        scratch_shapes=[pltpu.VMEM((tm, d), bf16)],
        compiler_params=_cparams(("parallel", "arbitrary")),
        name="in_proj",
    )(x, mod, w)


_PACK = 4
GROUP = _PACK * CHUNK
_GDN_HEADS = 4


def _gdn_kernel(q_ref, k_ref, v_ref, z_ref, cwq_ref, cwk_ref, cwv_ref, grow_ref, hp_ref, nw_ref, s0_ref,
                o_ref, so_ref, pad_ref, qs, ks, vs, gs, ub_s, wq_s, aq_s, kd_s, ct_s, of, ob, s_ref, *, seq):
    n = seq // CHUNK
    width = _GDN_HEADS * HEAD_W
    chains = [(hd, d) for hd in range(_GDN_HEADS) for d in range(2)]

    def conv_silu(x_ref, w_ref):
        pad_ref[0:8, :] = jnp.zeros((8, width), f32)
        pad_ref[8 + seq:16 + seq, :] = jnp.zeros((8, width), f32)
        pad_ref[8:8 + seq, :] = x_ref[...]
        acc = pad_ref[pl.ds(6, seq), :] * w_ref[0:1, :]
        for j in range(1, GDN_CONV):
            acc = acc + pad_ref[pl.ds(6 + j, seq), :] * w_ref[j:j + 1, :]
        return jax.nn.silu(acc)

    def l2norm_heads(x):
        parts = []
        for hd in range(_GDN_HEADS):
            xh = x[:, hd * HEAD_W:(hd + 1) * HEAD_W]
            parts.append(xh * lax.rsqrt(jnp.sum(xh * xh, axis=-1, keepdims=True) + NORM_EPS))
        return jnp.concatenate(parts, axis=1)

    qs[...] = l2norm_heads(conv_silu(q_ref, cwq_ref)) * (HEAD_W ** -0.5)
    ks[...] = l2norm_heads(conv_silu(k_ref, cwk_ref))
    vs[...] = conv_silu(v_ref, cwv_ref)

    x = grow_ref[...]
    a_log = hp_ref[:, :, 0:1][:, None]
    dt_b = hp_ref[:, :, 1:2][:, None]
    rows = lax.broadcasted_iota(jnp.int32, x.shape, 2)
    gs[...] = jnp.where(rows < 2, jax.nn.sigmoid(x), -jnp.exp(a_log) * jax.nn.softplus(x + dt_b))
    s_ref[...] = s0_ref[...].reshape(2 * _GDN_HEADS, HEAD_W, HEAD_W)

    ri = lax.broadcasted_iota(jnp.int32, (CHUNK, GROUP), 0)
    li = lax.broadcasted_iota(jnp.int32, (CHUNK, GROUP), 1)
    cj = li % CHUNK
    blk = [(li // CHUNK) == r for r in range(_PACK)]
    blk_bf = [jnp.where(b, 1.0, 0.0).astype(bf16) for b in blk]
    l1 = lax.broadcasted_iota(jnp.int32, (1, GROUP), 1) // CHUNK
    eye_b = ri == cj
    eye = jnp.where(eye_b, 1.0, 0.0).astype(f32)
    bd16 = (ri // SUB) == (cj // SUB)
    bd32 = (ri // (2 * SUB)) == (cj // (2 * SUB))
    r2 = lax.broadcasted_iota(jnp.int32, (GROUP, GROUP), 0)
    c2 = lax.broadcasted_iota(jnp.int32, (GROUP, GROUP), 1)
    same_blk = (r2 // CHUNK) == (c2 // CHUNK)

    def bdiag(yb):
        return jnp.concatenate([yb * mk for mk in blk_bf], axis=0)

    def unpack_diag(xf):
        out = xf[0:CHUNK]
        for r in range(1, _PACK):
            out = jnp.where(blk[r], xf[r * CHUNK:(r + 1) * CHUNK], out)
        return out

    def pmm3(pairs):
        sp = [(_split2(a), _split2(b)) for a, b in pairs]
        r1 = [_dot(jnp.concatenate([ah, al], axis=0), bdiag(bh)) for (ah, al), (bh, _) in sp]
        r2 = [_dot(ah, bdiag(bl)) for (ah, _), (_, bl) in sp]
        return [x1[:CHUNK] + x1[CHUNK:] + x2 for x1, x2 in zip(r1, r2)]

    def unit_tri_inverse(ms):
        k = len(ms)
        m32 = [jnp.where(bd32, m, 0.0) for m in ms]
        dg = [jnp.where(bd16, m, 0.0) for m in ms]
        d2 = pmm3([(a, a) for a in dg])
        both = pmm3([(a, a) for a in d2] + [(eye - a, eye + b) for a, b in zip(dg, d2)])
        d4, xi = both[:k], both[k:]
        both = pmm3([(a, a) for a in d4] + [(a, eye + b) for a, b in zip(xi, d4)])
        d8, xi = both[:k], both[k:]
        xi = pmm3([(a, eye + b) for a, b in zip(xi, d8)])
        for lo, hi in ((dg, m32), (m32, ms)):
            t = pmm3([(a, h_ - l_) for a, h_, l_ in zip(xi, hi, lo)])
            t = pmm3(list(zip(t, xi)))
            xi = [a - b for a, b in zip(xi, t)]
        return xi

    def block_cols(xp):
        return [jnp.sum(jnp.where(blk[r], xp, 0.0), axis=1, keepdims=True) for r in range(_PACK)]

    def spread(cols):
        out = jnp.broadcast_to(cols[0], (CHUNK, GROUP))
        for r in range(1, _PACK):
            out = jnp.where(blk[r], cols[r], out)
        return out

    def prepare_group(g, carry):
        rows_g = pl.ds(pl.multiple_of(g * GROUP, GROUP), GROUP)
        heads = range(_GDN_HEADS)
        k4 = [ks[rows_g, hd * HEAD_W:(hd + 1) * HEAD_W] for hd in heads]
        q4 = [qs[rows_g, hd * HEAD_W:(hd + 1) * HEAD_W] for hd in heads]
        v4 = [vs[rows_g, hd * HEAD_W:(hd + 1) * HEAD_W] for hd in heads]
        k4b = [x.astype(bf16) for x in k4]
        kq = [_dot_nt(jnp.concatenate([k4b[hd], q4[hd].astype(bf16)], axis=0), k4b[hd]) for hd in heads]
        kk_p = [unpack_diag(x[:GROUP]) for x in kq]
        qk_p = [unpack_diag(x[GROUP:]) for x in kq]
        g8 = [gs[hd, g] for hd in heads]
        ids = range(len(chains))
        beta_r = [g8[hd][d:d + 1, :] for hd, d in chains]
        g_r = [g8[hd][2 + d:3 + d, :] for hd, d in chains]
        tri2 = [cj <= ri, cj >= ri]
        tri = [tri2[d] for _, d in chains]
        strict = [jnp.logical_and(t, jnp.logical_not(eye_b)) for t in tri]
        tg = [jnp.where(tri[i], g_r[i], 0.0) for i in ids]
        gam_cols = [block_cols(tg[i]) for i in ids]
        beta_cols = [block_cols(jnp.where(eye_b, beta_r[i], 0.0)) for i in ids]
        tot_cols = [[jnp.sum(jnp.where(l1 == r, g_r[i], 0.0), axis=1, keepdims=True) for r in range(_PACK)]
                    for i in ids]
        strict2 = [(c2 % CHUNK) < (r2 % CHUNK), (c2 % CHUNK) > (r2 % CHUNK)]
        sm = [jnp.where(jnp.logical_and(same_blk, s_), 1.0, 0.0).astype(bf16) for s_ in strict2]
        r3 = []
        for i, (_, d) in enumerate(chains):
            th, tl = _split2(tg[i])
            tl2 = (tg[i] - th.astype(f32) - tl.astype(f32)).astype(bf16)
            r3.append(_dot(jnp.concatenate([th, tl, tl2], axis=0), sm[d]))
        dlt = [x[:CHUNK] + x[CHUNK:2 * CHUNK] + x[2 * CHUNK:] for x in r3]
        decay = [jnp.where(tri[i], jnp.exp(jnp.minimum(dlt[i], 0.0)), 0.0) for i in ids]
        t_inv = unit_tri_inverse([jnp.where(strict[i], kk_p[hd] * spread(beta_cols[i]) * decay[i], 0.0)
                                  for i, (hd, _) in enumerate(chains)])
        gam = [jnp.concatenate(gam_cols[i], axis=0) for i in ids]
        beta = [jnp.concatenate(beta_cols[i], axis=0) for i in ids]
        tot = [jnp.concatenate([jnp.broadcast_to(t, (CHUNK, 1)) for t in tot_cols[i]], axis=0) for i in ids]
        eg = [jnp.exp(x) for x in gam]
        rhs = [jnp.concatenate([v4[hd] * beta[i], k4[hd] * beta[i] * eg[i]], axis=1)
               for i, (hd, _) in enumerate(chains)]
        isp = [_split2(x) for x in t_inv]
        rsp = [_split2(x) for x in rhs]
        bih = [bdiag(isp[i][0]) for i in ids]
        s1 = [_dot(jnp.concatenate([bih[i], bdiag(isp[i][1])], axis=0), rsp[i][0]) for i in ids]
        s2 = [_dot(bih[i], rsp[i][1]) for i in ids]
        for i, (hd, d) in enumerate(chains):
            sol = s1[i][:GROUP] + s1[i][GROUP:] + s2[i]
            ub_s[i, rows_g, :] = sol[:, :HEAD_W]
            w_b = sol[:, HEAD_W:].astype(bf16)
            qd_b = (q4[hd] * eg[i]).astype(bf16)
            a_p = qk_p[hd] * decay[i]
            for r_ in range(_PACK):
                c = g * _PACK + r_
                wq_s[i, c, 0:CHUNK, :] = w_b[r_ * CHUNK:(r_ + 1) * CHUNK]
                wq_s[i, c, CHUNK:2 * CHUNK, :] = qd_b[r_ * CHUNK:(r_ + 1) * CHUNK]
                aq_s[i, c] = a_p[:, r_ * CHUNK:(r_ + 1) * CHUNK].astype(bf16)
                ct_s[i, c] = jnp.broadcast_to(jnp.exp(tot_cols[i][r_]), (8, HEAD_W))
            kd_s[i, rows_g, :] = (k4[hd] * jnp.exp(tot[i] - gam[i])).astype(bf16)
        return carry

    lax.fori_loop(0, n // _PACK, prepare_group, 0)

    def step(t, carry):
        ids = range(len(chains))
        cs = [t if d == 0 else n - 1 - t for _, d in chains]
        rows = [pl.ds(pl.multiple_of(c * CHUNK, CHUNK), CHUNK) for c in cs]
        s = [s_ref[i] for i in ids]
        sb = [x.astype(bf16) for x in s]
        r = [_dot(wq_s[i, cs[i]], sb[i]) for i in ids]
        u = [(ub_s[i, rows[i], :] - r[i][:CHUNK]).astype(bf16) for i in ids]
        o_c = [r[i][CHUNK:] + _dot(aq_s[i, cs[i]], u[i]) for i in ids]
        s_new = [ct_s[i, cs[i]][0:1, :] * s[i] + _dot_tn(kd_s[i, rows[i], :], u[i]) for i in ids]
        for i, (hd, d) in enumerate(chains):
            (of if d == 0 else ob)[rows[i], hd * HEAD_W:(hd + 1) * HEAD_W] = o_c[i]
            s_ref[i] = s_new[i]
        return carry

    lax.fori_loop(0, n, step, 0)
    o = of[...] + ob[...]
    z = z_ref[...]
    nw = nw_ref[...]
    o_ref[...] = jnp.concatenate(
        [_rmsnorm(o[:, hd * HEAD_W:(hd + 1) * HEAD_W], nw) * jax.nn.silu(z[:, hd * HEAD_W:(hd + 1) * HEAD_W])
         for hd in range(_GDN_HEADS)], axis=1).astype(bf16)
    so_ref[...] = s_ref[...].reshape(_GDN_HEADS, 2, HEAD_W, HEAD_W)


def _gdn(p, conv_w, grow, hp, norm_w, s0, *, nseq, seq, row_blk0):
    n = seq // CHUNK
    ng = seq // GROUP
    hh = _GDN_HEADS
    width = hh * HEAD_W
    nch = 2 * hh

    def pcol(off):
        return pl.BlockSpec((seq, width), lambda b, h: (row_blk0 + b, off // width + h))

    def wcol(off):
        return pl.BlockSpec((GDN_CONV, width), lambda b, h: (0, off // width + h))

    kern = functools.partial(_gdn_kernel, seq=seq)
    return pl.pallas_call(
        kern,
        out_shape=(jax.ShapeDtypeStruct((nseq * seq, BRANCH_W), bf16),
                   jax.ShapeDtypeStruct((nseq, N_HEADS, 2, HEAD_W, HEAD_W), f32)),
        grid=(nseq, N_HEADS // hh),
        in_specs=[pcol(OFF_GDN_Q), pcol(OFF_GDN_K), pcol(OFF_GDN_V), pcol(OFF_GDN_Z),
                  wcol(0), wcol(BRANCH_W), wcol(2 * BRANCH_W),
                  pl.BlockSpec((hh, ng, 8, GROUP), lambda b, h: (h, row_blk0 + b, 0, 0)),
                  pl.BlockSpec((hh, 8, 2), lambda b, h: (h, 0, 0)),
                  pl.BlockSpec((1, HEAD_W), lambda b, h: (0, 0)),
                  pl.BlockSpec((None, hh, 2, HEAD_W, HEAD_W), lambda b, h: (b, h, 0, 0, 0))],
        out_specs=(pl.BlockSpec((seq, width), lambda b, h: (b, h)),
                   pl.BlockSpec((None, hh, 2, HEAD_W, HEAD_W), lambda b, h: (b, h, 0, 0, 0))),
        scratch_shapes=[pltpu.VMEM((seq + 16, width), f32),
                        pltpu.VMEM((seq, width), f32), pltpu.VMEM((seq, width), f32),
                        pltpu.VMEM((seq, width), f32), pltpu.VMEM((hh, ng, 8, GROUP), f32),
                        pltpu.VMEM((nch, seq, HEAD_W), f32), pltpu.VMEM((nch, n, 2 * CHUNK, HEAD_W), bf16),
                        pltpu.VMEM((nch, n, CHUNK, CHUNK), bf16),
                        pltpu.VMEM((nch, seq, HEAD_W), bf16), pltpu.VMEM((nch, n, 8, HEAD_W), f32),
                        pltpu.VMEM((seq, width), f32), pltpu.VMEM((seq, width), f32),
                        pltpu.VMEM((nch, HEAD_W, HEAD_W), f32)],
        compiler_params=_cparams(("parallel", "parallel")),
        name="gdn",
    )(p, p, p, p, conv_w, conv_w, conv_w, grow, hp, norm_w, s0)


_N_LEVELS = 6
_N_MM_LEVELS = 4
_ROW_EQ = _N_MM_LEVELS * CHUNK
_ROW_EK, _ROW_TOT, _ROWS_EXP = _ROW_EQ + CHUNK, _ROW_EQ + 2 * CHUNK, _ROW_EQ + 2 * CHUNK + 8
_SCAN_UNROLL = 4


def _scan_consts():
    c = CHUNK
    mexp = np.zeros((2, _ROWS_EXP, c), np.float32)
    lvl = np.full((2, c, c), -1.0, np.float32)
    for lv in range(_N_LEVELS):
        s = 32 >> lv
        for i in range(c):
            p = (i // (2 * s)) * (2 * s) + s
            right = (i % (2 * s)) >= s
            if lv >= _N_MM_LEVELS:
                pass
            elif right:
                mexp[0, lv * c + i, p:i + 1] = 1.0
                mexp[1, lv * c + i, p:i] = 1.0
            else:
                mexp[0, lv * c + i, i + 1:p] = 1.0
                mexp[1, lv * c + i, i:p] = 1.0
            for j in range(c):
                if (i // (2 * s)) != (j // (2 * s)):
                    continue
                jright = (j % (2 * s)) >= s
                if right and not jright:
                    lvl[0, i, j] = lv
                if (not right) and jright:
                    lvl[1, i, j] = lv
    for i in range(c):
        lvl[:, i, i] = _N_LEVELS
        mexp[0, _ROW_EQ + i, :i + 1] = 1.0
        mexp[0, _ROW_EK + i, i + 1:] = 1.0
        mexp[1, _ROW_EQ + i, i:] = 1.0
        mexp[1, _ROW_EK + i, :i] = 1.0
    mexp[:, _ROW_TOT:, :] = 1.0
    return mexp, lvl


_LOG_DECAY_FLOOR = -1.0e4


def _scan_kernel(*refs, seq, hgrn, q_scale):
    if hgrn:
        q_ref, v_ref, gate_ref, zf_ref, zb_ref, lb_ref = refs[:6]
    else:
        q_ref, v_ref, gate_ref, k_ref, glr_ref, w2_ref, b2_ref = refs[:7]
    nw_ref, s0_ref, mexp_ref, lvl_ref, o_ref, so_ref, of, ob, qd_s, g_s, ct_s, stp_s, st_ref = refs[-13:]
    q_silu = hgrn
    gate_silu = not hgrn
    n = seq // CHUNK
    st_ref[...] = s0_ref[...]
    outs = (of, ob)

    def rows_of(c):
        return pl.ds(pl.multiple_of(c * CHUNK, CHUNK), CHUNK)

    def prepare_group(g, carry):
        todo = [(d, g * _SCAN_UNROLL + u) for u in range(_SCAN_UNROLL) for d in range(2)]
        items = range(len(todo))
        q, k, v, log_decay = [], [], [], []
        for d, c in todo:
            rows_c = rows_of(c)
            qi = q_ref[rows_c, :]
            q.append(jax.nn.silu(qi) if q_silu else qi * q_scale)
            v.append(v_ref[rows_c, :].astype(bf16))
        if hgrn:
            for d, c in todo:
                z = (zf_ref if d == 0 else zb_ref)[rows_of(c), :]
                lb = lb_ref[d:d + 1, :]
                t0 = jnp.log(lb)
                t1 = jnp.log1p(-lb) + jax.nn.log_sigmoid(z)
                mx = jnp.maximum(t0, t1)
                lse = mx + jnp.log(jnp.exp(t0 - mx) + jnp.exp(t1 - mx))
                log_decay.append(jnp.where(mx == -jnp.inf, -jnp.inf, lse))
                k.append((1.0 - lb) * jax.nn.sigmoid(-z))
        else:
            logits = [_dot(glr_ref[rows_of(c), :].astype(bf16), w2_ref[d]) for d, c in todo]
            log_decay = [jax.nn.log_sigmoid(logits[i] + b2_ref[d]) * (1.0 / GLA_TAU) for i, (d, _) in enumerate(todo)]
            k = [k_ref[rows_of(c), :] for _, c in todo]
        la = [jnp.maximum(x, _LOG_DECAY_FLOOR) for x in log_decay]
        e2 = []
        for i, (d, _) in enumerate(todo):
            hi, lo = _split2(la[i])
            e2.append(_dot(mexp_ref[d], jnp.concatenate([hi, lo], axis=1)))
        e = [x[:, :HEAD_W] + x[:, HEAD_W:] for x in e2]
        r4 = lax.broadcasted_iota(jnp.int32, (CHUNK, HEAD_W), 0) % 4
        fine = []
        for i, (d, _) in enumerate(todo):
            prev = pltpu.roll(la[i], 1, 0)
            nxt = pltpu.roll(la[i], CHUNK - 1, 0)
            if d == 0:
                w2 = jnp.where(r4 == 0, nxt, jnp.where(r4 == 1, 0.0, jnp.where(r4 == 2, la[i], la[i] + prev)))
                w1 = jnp.where(r4 % 2 == 1, la[i], 0.0)
            else:
                w2 = jnp.where(r4 == 0, la[i] + nxt, jnp.where(r4 == 1, la[i], jnp.where(r4 == 2, 0.0, prev)))
                w1 = jnp.where(r4 % 2 == 0, la[i], 0.0)
            fine.append((w2, w1))
        lvl = [lvl_ref[d] for d, _ in todo]
        qk = [_dot_nt(q[i].astype(bf16), k[i].astype(bf16)) for i in items]
        a = [jnp.where(lvl[i] == float(_N_LEVELS), qk[i], 0.0) for i in items]
        for lv in range(_N_LEVELS):
            if lv < _N_MM_LEVELS:
                w = [jnp.exp(e[i][lv * CHUNK:(lv + 1) * CHUNK]) for i in items]
            else:
                w = [jnp.exp(fine[i][lv - _N_MM_LEVELS]) for i in items]
            p = [_dot_nt((q[i] * w[i]).astype(bf16), (k[i] * w[i]).astype(bf16)) for i in items]
            a = [jnp.where(lvl[i] == float(lv), p[i], a[i]) for i in items]
        o_intra = [_dot(a[i].astype(bf16), v[i]) for i in items]
        incr = [_dot_tn(v[i], (k[i] * jnp.exp(e[i][_ROW_EK:_ROW_EK + CHUNK])).astype(bf16)) for i in items]
        for i, (d, c) in enumerate(todo):
            rows_c = rows_of(c)
            outs[d][rows_c, :] = o_intra[i]
            qd_s[d, rows_c, :] = (q[i] * jnp.exp(e[i][_ROW_EQ:_ROW_EQ + CHUNK])).astype(bf16)
            g_s[d, c] = incr[i]
            ct_s[d, c] = jnp.exp(e[i][_ROW_TOT:_ROW_TOT + 8])
        return carry

    lax.fori_loop(0, n // _SCAN_UNROLL, prepare_group, 0)

    def scan_step(i, carry):
        for d, c in ((0, i), (1, n - 1 - i)):
            st = st_ref[d]
            stp_s[d, c] = st.astype(bf16)
            st_ref[d] = st * ct_s[d, c][0:1, :] + g_s[d, c]
        return carry

    lax.fori_loop(0, n, scan_step, 0)

    def inter_group(g, carry):
        todo = [(d, g * _SCAN_UNROLL + u) for u in range(_SCAN_UNROLL) for d in range(2)]
        res = [_dot_nt(qd_s[d, rows_of(c), :], stp_s[d, c]) + outs[d][rows_of(c), :] for d, c in todo]
        for (d, c), o_c in zip(todo, res):
            outs[d][rows_of(c), :] = o_c
        return carry

    lax.fori_loop(0, n // _SCAN_UNROLL, inter_group, 0)
    o = of[...] + ob[...]
    g = gate_ref[...]
    g = jax.nn.silu(g) if gate_silu else jax.nn.sigmoid(g)
    o_ref[...] = (_rmsnorm(o, nw_ref[...]) * g).astype(bf16)
    so_ref[...] = st_ref[...]


def _scan(p, offs, extra, norm_w, s0t, mexp, lvl, *, nseq, seq, row_blk0, hgrn, q_scale):
    def pspec(off, per_head=True):
        return pl.BlockSpec((seq, HEAD_W), lambda b, h: (row_blk0 + b, off // HEAD_W + (h if per_head else 0)))

    if hgrn:
        (lower_bound,) = extra
        in_specs = [pspec(o) for o in offs] + [pl.BlockSpec((2, HEAD_W), lambda b, h: (0, h))]
        args = [p] * len(offs) + [lower_bound]
    else:
        w2, b2 = extra
        in_specs = ([pspec(o) for o in offs[:4]] + [pspec(offs[4], per_head=False)]
                    + [pl.BlockSpec((2, HEAD_W, HEAD_W), lambda b, h: (0, 0, h)),
                       pl.BlockSpec((2, 1, HEAD_W), lambda b, h: (0, 0, h))])
        args = [p] * len(offs) + [w2, b2]
    kern = functools.partial(_scan_kernel, seq=seq, hgrn=hgrn, q_scale=q_scale)
    return pl.pallas_call(
        kern,
        out_shape=(jax.ShapeDtypeStruct((nseq * seq, BRANCH_W), bf16),
                   jax.ShapeDtypeStruct((nseq, N_HEADS, 2, HEAD_W, HEAD_W), f32)),
        grid=(nseq, N_HEADS),
        in_specs=in_specs + [
            pl.BlockSpec((1, HEAD_W), lambda b, h: (0, 0)),
            pl.BlockSpec((None, None, 2, HEAD_W, HEAD_W), lambda b, h: (b, h, 0, 0, 0)),
            pl.BlockSpec((2, _ROWS_EXP, CHUNK), lambda b, h: (0, 0, 0)),
            pl.BlockSpec((2, CHUNK, CHUNK), lambda b, h: (0, 0, 0))],
        out_specs=(pl.BlockSpec((seq, HEAD_W), lambda b, h: (b, h)),
                   pl.BlockSpec((None, None, 2, HEAD_W, HEAD_W), lambda b, h: (b, h, 0, 0, 0))),
        scratch_shapes=[pltpu.VMEM((seq, HEAD_W), f32), pltpu.VMEM((seq, HEAD_W), f32),
                        pltpu.VMEM((2, seq, HEAD_W), bf16),
                        pltpu.VMEM((2, seq // CHUNK, HEAD_W, HEAD_W), f32),
                        pltpu.VMEM((2, seq // CHUNK, 8, HEAD_W), f32),
                        pltpu.VMEM((2, seq // CHUNK, HEAD_W, HEAD_W), bf16),
                        pltpu.VMEM((2, HEAD_W, HEAD_W), f32)],
        compiler_params=_cparams(("parallel", "parallel")),
        name="decay_scan",
    )(*args, norm_w, s0t, mexp, lvl)


def _mla_proj_kernel(*refs, rope):
    if rope:
        (qa_ref, kva_ref, sm_ref, qnw_ref, kvnw_ref, wqa_ref, wkv_ref, wqb_ref, cos_ref, sin_ref,
         qn_ref, qp_ref, ckv_ref, kn_ref, vv_ref, kp_ref) = refs
    else:
        (qa_ref, kva_ref, sm_ref, qnw_ref, kvnw_ref, wqa_ref, wkv_ref,
         qn_ref, qp_ref, ckv_ref, kn_ref, vv_ref, kp_ref) = refs
    qh = _rmsnorm(qa_ref[...], qnw_ref[...]).astype(bf16)
    qa = _dot(qh, wqa_ref[...])
    qn_ref[...] = qa[:, :BRANCH_W].astype(bf16)
    pe = qa[:, BRANCH_W:]
    kpe = sm_ref[:, 0:LANE]
    if rope:
        cos = cos_ref[...]
        sin = sin_ref[...]
        cos4 = jnp.concatenate([cos] * N_HEADS, axis=1)
        sin4 = jnp.concatenate([sin] * N_HEADS, axis=1)
        pe = pe * cos4 + _dot(qh, wqb_ref[...]) * sin4
        kpe = kpe * cos + pltpu.roll(kpe, MLA_ROPE, 1) * sin
    qp_ref[...] = pe.astype(bf16)
    kp_ref[...] = kpe.astype(bf16)
    ckv = _rmsnorm(kva_ref[...], kvnw_ref[...])
    ckv_ref[...] = ckv
    kv = _dot(ckv.astype(bf16), wkv_ref[...])
    kn_ref[...] = kv[:, :BRANCH_W].astype(bf16)
    vv_ref[...] = kv[:, BRANCH_W:].astype(bf16)


def _mla_proj(p, qnw, kvnw, wqa, wkv, wqb, cos, sin, *, nrows, seq, row0):
    tm = 256
    rope = cos is not None
    rb0 = row0 // tm
    per_seq = seq // tm

    def pspec(off):
        return pl.BlockSpec((tm, _SEG), lambda i: (rb0 + i, off // _SEG))

    def full(a):
        return pl.BlockSpec(a.shape, lambda i: (0,) * a.ndim)

    in_specs = [pspec(OFF_MLA_QA), pspec(OFF_MLA_KVA), pspec(OFF_SMALL), full(qnw), full(kvnw), full(wqa), full(wkv)]
    args = [p, p, p, qnw, kvnw, wqa, wkv]
    if rope:
        tspec = pl.BlockSpec((tm, LANE), lambda i: (i % per_seq, 0))
        in_specs += [full(wqb), tspec, tspec]
        args += [wqb, cos, sin]
    wide = lambda dt: jax.ShapeDtypeStruct((nrows, BRANCH_W), dt)
    ospec = pl.BlockSpec((tm, BRANCH_W), lambda i: (i, 0))
    return pl.pallas_call(
        functools.partial(_mla_proj_kernel, rope=rope),
        out_shape=(wide(bf16), wide(bf16), wide(f32), wide(bf16), wide(bf16),
                   jax.ShapeDtypeStruct((nrows, LANE), bf16)),
        grid=(nrows // tm,),
        in_specs=in_specs,
        out_specs=(ospec, ospec, ospec, ospec, ospec, pl.BlockSpec((tm, LANE), lambda i: (i, 0))),
        compiler_params=_cparams(("parallel",)),
        name="mla_proj",
    )(*args)


def _kv_kernel(ckv_ref, w_ref, kn_ref, vv_ref):
    kv = _dot(ckv_ref[...].astype(bf16), w_ref[...])
    kn_ref[...] = kv[:, :BRANCH_W].astype(bf16)
    vv_ref[...] = kv[:, BRANCH_W:].astype(bf16)


def _kv_proj(ckv, wkv):
    rows = ckv.shape[0]
    tm = 256
    out = jax.ShapeDtypeStruct((rows, BRANCH_W), bf16)
    ospec = pl.BlockSpec((tm, BRANCH_W), lambda i: (i, 0))
    return pl.pallas_call(
        _kv_kernel, out_shape=(out, out), grid=(rows // tm,),
        in_specs=[pl.BlockSpec((tm, ckv.shape[1]), lambda i: (i, 0)),
                  pl.BlockSpec(wkv.shape, lambda i: (0, 0))],
        out_specs=(ospec, ospec),
        compiler_params=_cparams(("parallel",)),
        name="mla_ctx_kv",
    )(ckv, wkv)


def _attn_kernel(*refs, has_ctx):
    if has_ctx:
        qn_ref, qp_ref, kn_ref, kp_ref, vv_ref, knc_ref, kpc_ref, vvc_ref, o_ref = refs
    else:
        qn_ref, qp_ref, kn_ref, kp_ref, vv_ref, o_ref = refs
    scale = (MLA_NOPE + MLA_ROPE) ** -0.5
    heads = range(N_HEADS)

    def head(ref, h):
        return ref[:, h * HEAD_W:(h + 1) * HEAD_W]

    kp = kp_ref[...]
    s1 = [(_dot_nt(head(qn_ref, h), head(kn_ref, h)) + _dot_nt(head(qp_ref, h), kp)) * scale for h in heads]
    mx = [jnp.max(x, axis=-1, keepdims=True) for x in s1]
    if has_ctx:
        kpc = kpc_ref[...]
        s2 = [(_dot_nt(head(qn_ref, h), head(knc_ref, h)) + _dot_nt(head(qp_ref, h), kpc)) * scale for h in heads]
        mx = [jnp.maximum(m, jnp.max(x, axis=-1, keepdims=True)) for m, x in zip(mx, s2)]
    p1 = [jnp.exp(x - m) for x, m in zip(s1, mx)]
    den = [jnp.sum(x, axis=-1, keepdims=True) for x in p1]
    if has_ctx:
        p2 = [jnp.exp(x - m) for x, m in zip(s2, mx)]
        den = [d_ + jnp.sum(x, axis=-1, keepdims=True) for d_, x in zip(den, p2)]
    inv = [1.0 / d_ for d_ in den]
    o = [_dot((p1[h] * inv[h]).astype(bf16), head(vv_ref, h)) for h in heads]
    if has_ctx:
        o = [o[h] + _dot((p2[h] * inv[h]).astype(bf16), head(vvc_ref, h)) for h in heads]
    o_ref[...] = jnp.concatenate(o, axis=1).astype(bf16)


def _attention(qn, qp, kn, kp, vv, ctx, *, nseq, seq):
    tq = 256
    nq = seq // tq
    has_ctx = ctx is not None
    qspec = pl.BlockSpec((tq, BRANCH_W), lambda b, i: (b * nq + i, 0))
    kspec = pl.BlockSpec((seq, BRANCH_W), lambda b, i: (b, 0))
    kpspec = pl.BlockSpec((seq, LANE), lambda b, i: (b, 0))
    in_specs = [qspec, qspec, kspec, kpspec, kspec]
    args = [qn, qp, kn, kp, vv]
    if has_ctx:
        knc, kpc, vvc = ctx
        lc = knc.shape[0] // nseq
        in_specs += [pl.BlockSpec((lc, BRANCH_W), lambda b, i: (b, 0)),
                     pl.BlockSpec((lc, LANE), lambda b, i: (b, 0)),
                     pl.BlockSpec((lc, BRANCH_W), lambda b, i: (b, 0))]
        args += [knc, kpc, vvc]
    return pl.pallas_call(
        functools.partial(_attn_kernel, has_ctx=has_ctx),
        out_shape=jax.ShapeDtypeStruct((nseq * seq, BRANCH_W), bf16),
        grid=(nseq, nq),
        in_specs=in_specs,
        out_specs=pl.BlockSpec((tq, BRANCH_W), lambda b, i: (b * nq + i, 0)),
        compiler_params=_cparams(("parallel", "parallel")),
        name="mla_attn",
    )(*args)


def _merge_kernel(o0_ref, o1_ref, o2_ref, o3_ref, g0_ref, g1_ref, g2_ref, g3_ref, bg_ref, wb_ref, wo_ref,
                  x_ref, mod_ref, lng_ref, lnb_ref, out_ref, *, alpha):
    m = None
    for k, (o_ref, g_ref) in enumerate(((o0_ref, g0_ref), (o1_ref, g1_ref), (o2_ref, g2_ref), (o3_ref, g3_ref))):
        term = jax.nn.sigmoid(g_ref[...] + bg_ref[k:k + 1, :]) * _dot(o_ref[...], wb_ref[k])
        m = term if m is None else m + term
    mix = _dot(m.astype(bf16), wo_ref[...])
    gate1 = mod_ref[2:3, :]
    out_ref[...] = _layernorm(alpha * x_ref[...] + gate1 * mix, lng_ref[...], lnb_ref[...])


def _merge(branches, p, bg, wb, wo, x, mod, lng, lnb, *, alpha, t_ctx, l_lat):
    t, d = x.shape
    tm = 256
    midx = _mod_index(tm, t_ctx, l_lat)
    ospec = pl.BlockSpec((tm, BRANCH_W), lambda i: (i, 0))
    gspecs = [pl.BlockSpec((tm, d), functools.partial(lambda i, k: (i, OFF_GATES // d + k), k=k)) for k in range(4)]
    single = dict(pipeline_mode=pl.Buffered(1))
    return pl.pallas_call(
        functools.partial(_merge_kernel, alpha=alpha),
        out_shape=jax.ShapeDtypeStruct((t, d), f32),
        grid=(t // tm,),
        in_specs=[ospec, ospec, ospec, ospec] + gspecs + [
            pl.BlockSpec((4, d), lambda i: (0, 0)),
            pl.BlockSpec((4, BRANCH_W, d), lambda i: (0, 0, 0), **single),
            pl.BlockSpec((d, d), lambda i: (0, 0), **single),
            pl.BlockSpec((tm, d), lambda i: (i, 0)),
            pl.BlockSpec((None, 6, d), lambda i: (midx(i), 0, 0)),
            pl.BlockSpec((1, d), lambda i: (0, 0)),
            pl.BlockSpec((1, d), lambda i: (0, 0))],
        out_specs=pl.BlockSpec((tm, d), lambda i: (i, 0)),
        compiler_params=_cparams(("parallel",)),
        name="merge_out_ln",
    )(*branches, p, p, p, p, bg, wb, wo, x, mod, lng, lnb)


def _ffn_kernel(x_ref, mod_ref, w1_ref, w3_ref, w2_ref, lng_ref, lnb_ref, *rest, alpha, ctx_tiles):
    if ctx_tiles is None:
        out_ref, hb_ref, acc_ref = rest
    else:
        out_ref, out_lat_ref, hb_ref, acc_ref = rest
    f = pl.program_id(1)

    @pl.when(f == 0)
    def _():
        shift = mod_ref[3:4, :]
        scale = mod_ref[4:5, :]
        hb_ref[...] = (x_ref[...] * (1.0 + scale) + shift).astype(bf16)
        acc_ref[...] = jnp.zeros_like(acc_ref)

    h = hb_ref[...]
    g = (jax.nn.silu(_dot(h, w1_ref[...])) * _dot(h, w3_ref[...])).astype(bf16)
    acc_ref[...] += _dot(g, w2_ref[...])

    last = f == pl.num_programs(1) - 1

    def result():
        gate2 = mod_ref[5:6, :]
        return _layernorm(alpha * x_ref[...] + gate2 * acc_ref[...], lng_ref[...], lnb_ref[...])

    if ctx_tiles is None:
        @pl.when(last)
        def _():
            out_ref[...] = result()
    else:
        is_ctx = pl.program_id(0) < ctx_tiles

        @pl.when(jnp.logical_and(last, is_ctx))
        def _():
            out_ref[...] = result()

        @pl.when(jnp.logical_and(last, jnp.logical_not(is_ctx)))
        def _():
            out_lat_ref[...] = result()


def _ffn(x, mod, w1, w3, w2, lng, lnb, *, alpha, t_ctx, l_lat, split):
    t, d = x.shape
    dff = w1.shape[1]
    tm, tf = 512, 512
    midx = _mod_index(tm, t_ctx, l_lat)
    if split:
        ctx_tiles = t_ctx // tm
        out_shape = (jax.ShapeDtypeStruct((t_ctx, d), f32), jax.ShapeDtypeStruct((t - t_ctx, d), f32))
        out_specs = (pl.BlockSpec((tm, d), lambda i, f: (jnp.minimum(i, ctx_tiles - 1), 0)),
                     pl.BlockSpec((tm, d), lambda i, f: (jnp.maximum(i - ctx_tiles, 0), 0)))
    else:
        ctx_tiles = None
        out_shape = jax.ShapeDtypeStruct((t, d), f32)
        out_specs = pl.BlockSpec((tm, d), lambda i, f: (i, 0))
    return pl.pallas_call(
        functools.partial(_ffn_kernel, alpha=alpha, ctx_tiles=ctx_tiles),
        out_shape=out_shape,
        grid=(t // tm, dff // tf),
        in_specs=[pl.BlockSpec((tm, d), lambda i, f: (i, 0)),
                  pl.BlockSpec((None, 6, d), lambda i, f: (midx(i), 0, 0)),
                  pl.BlockSpec((d, tf), lambda i, f: (0, f)),
                  pl.BlockSpec((d, tf), lambda i, f: (0, f)),
                  pl.BlockSpec((tf, d), lambda i, f: (f, 0)),
                  pl.BlockSpec((1, d), lambda i, f: (0, 0)),
                  pl.BlockSpec((1, d), lambda i, f: (0, 0))],
        out_specs=out_specs,
        scratch_shapes=[pltpu.VMEM((tm, d), bf16), pltpu.VMEM((tm, d), f32)],
        compiler_params=_cparams(("arbitrary" if split else "parallel", "arbitrary")),
        name="ffn_ln",
    )(x, mod, w1, w3, w2, lng, lnb)


def _cast_kernel(x_ref, o_ref):
    o_ref[...] = x_ref[...].astype(bf16)


def _cast_layer(w, layer):
    _, rows, cols = w.shape
    tr = 256 if cols > 4096 else 512
    return pl.pallas_call(
        _cast_kernel,
        out_shape=jax.ShapeDtypeStruct((rows, cols), bf16),
        grid=(rows // tr,),
        in_specs=[pl.BlockSpec((None, tr, cols), lambda i: (layer, i, 0))],
        out_specs=pl.BlockSpec((tr, cols), lambda i: (i, 0)),
        compiler_params=_cparams(("parallel",)),
        name="cast_bf16",
    )(w)


_ROW_UNIT = 16


def _relayout_plan():
    plain = lambda src: (0, src)
    special = lambda k: (1, k)
    segs = [plain(_SRC[n]) for n in ('gdn_q', 'gdn_k', 'gdn_v', 'gdn_z')]
    segs += [special(0), special(1)]
    segs += [plain(_SRC['gla_v']), plain(_SRC['gla_r']), plain(_SRC['hg_q']), plain(_SRC['hg_f']),
             plain(_SRC['hg_f'] + 512), plain(_SRC['hg_i']), plain(_SRC['hg_g']), plain(_SRC['mla_qa']),
             plain(_SRC['mla_kva'])]
    segs += [special(2)]
    segs += [plain(_SRC['gates'] + _SEG * i) for i in range(N_HEADS * D_MODEL // _SEG)]
    assert len(segs) == N_PROJ // _SEG
    table = np.zeros((3, len(segs)), np.int32)
    row, blk = segs[0][1], 0
    for j, (is_special, val) in enumerate(segs):
        if is_special:
            blk = val
        else:
            row = val
        assert row % _ROW_UNIT == 0
        table[:, j] = (is_special, row // _ROW_UNIT, blk)
    return table


def _special_rows(wt, layer):
    d = wt.shape[2]

    def rows(name, start, n):
        s = _SRC[name] + start
        return lax.slice(wt, (layer, s, 0), (layer + 1, s + n, d))[0]

    def head_padded(name):
        parts = []
        for h in range(N_HEADS):
            parts += [rows(name, h * GLA_DK, GLA_DK), jnp.zeros((HEAD_W - GLA_DK, d), wt.dtype)]
        return parts

    q4 = MLA_ROPE // 4
    small = [rows('mla_kpe', 0, MLA_ROPE), rows('mla_kpe', q4, q4), rows('mla_kpe', 0, q4),
             rows('mla_kpe', 3 * q4, q4), rows('mla_kpe', 2 * q4, q4), rows('gla_g', 0, 2 * GLA_RANK),
             rows('gdn_b', 0, 8), rows('gdn_a', 0, 8), jnp.zeros((_SEG - SM_GDN_A - 8, d), wt.dtype)]
    return jnp.concatenate(head_padded('gla_q') + head_padded('gla_k') + small, axis=0)


def _relayout_kernel(tab_ref, wt_ref, sp_ref, o_ref):
    is_special = tab_ref[0, pl.program_id(1)]

    @pl.when(is_special == 0)
    def _():
        o_ref[...] = wt_ref[0].astype(bf16)

    @pl.when(is_special == 1)
    def _():
        o_ref[...] = sp_ref[...].astype(bf16)


def _relayout_w_in(w_in):
    depth, d, _ = w_in.shape
    wt = jnp.swapaxes(w_in, 1, 2)
    special = jnp.stack([_special_rows(wt, l) for l in range(depth)])
    table = _relayout_plan()
    nseg = N_PROJ // _SEG
    return pl.pallas_call(
        _relayout_kernel,
        out_shape=jax.ShapeDtypeStruct((depth, N_PROJ, d), bf16),
        grid_spec=pltpu.PrefetchScalarGridSpec(
            num_scalar_prefetch=1,
            grid=(depth, nseg),
            in_specs=[pl.BlockSpec((pl.Element(1), pl.Element(_SEG), pl.Element(d)),
                                   lambda l, j, tab: (l, tab[1, j] * _ROW_UNIT, 0)),
                      pl.BlockSpec((None, _SEG, d), lambda l, j, tab: (l, tab[2, j], 0))],
            out_specs=pl.BlockSpec((None, _SEG, d), lambda l, j, tab: (l, j, 0))),
        compiler_params=_cparams(("parallel", "parallel")),
        name="w_in_relayout",
    )(jnp.asarray(table), wt, special)


def _relayout_wq(wq):
    hw = MLA_NOPE + MLA_ROPE
    z = jnp.zeros((wq.shape[0], HEAD_W - MLA_ROPE), wq.dtype)
    q4 = MLA_ROPE // 4
    nope, pe, pes = [], [], []
    for h in range(N_HEADS):
        base = h * hw
        nope.append(wq[:, base:base + MLA_NOPE])
        r = wq[:, base + MLA_NOPE:base + hw]
        pe += [r, z]
        pes += [r[:, q4:2 * q4], r[:, 0:q4], r[:, 3 * q4:], r[:, 2 * q4:3 * q4], z]
    return jnp.concatenate(nope + pe, axis=1).astype(bf16), jnp.concatenate(pes, axis=1).astype(bf16)


def _relayout_wkv(wkv):
    hw = MLA_NOPE + HEAD_W
    kn = [wkv[:, h * hw:h * hw + MLA_NOPE] for h in range(N_HEADS)]
    vv = [wkv[:, h * hw + MLA_NOPE:(h + 1) * hw] for h in range(N_HEADS)]
    return jnp.concatenate(kn + vv, axis=1).astype(bf16)


def _rope_tables(length):
    pos = jnp.arange(length)
    row_id = (pos // GRID_W).astype(f32)
    col_id = (pos % GRID_W).astype(f32)
    half = MLA_ROPE // 2
    inv = ROPE_BASE ** (-jnp.arange(0, half, 2, dtype=f32) / half)
    ar, ac = row_id[:, None] * inv, col_id[:, None] * inv
    z = jnp.zeros((length, LANE - MLA_ROPE), f32)
    cos = jnp.concatenate([jnp.cos(ar), jnp.cos(ar), jnp.cos(ac), jnp.cos(ac), z], axis=1)
    sin = jnp.concatenate([-jnp.sin(ar), jnp.sin(ar), -jnp.sin(ac), jnp.sin(ac), z], axis=1)
    return cos, sin


def _gla_gate_weights(w2, b):
    assert SM_GLA_G % LANE == 0
    lane_pad = ((0, 0), (0, 0), (0, 0), (0, HEAD_W - GLA_DK))
    w4 = jnp.pad(w2.reshape(2, GLA_RANK, N_HEADS, GLA_DK), lane_pad).reshape(2, GLA_RANK, BRANCH_W)
    wbig = jnp.stack([jnp.pad(w4[d], ((d * GLA_RANK, LANE - (d + 1) * GLA_RANK), (0, 0))) for d in range(2)])
    bbig = jnp.pad(b.reshape(2, 1, N_HEADS, GLA_DK), lane_pad).reshape(2, 1, BRANCH_W)
    return wbig.astype(bf16), bbig


def kernel(x_prompt, x_sample, c, state_gdn, state_gla, state_hgrn, cache_mla_ckv, cache_mla_kpe, c_ctx, w_ada, b_ada, w_in, gdn_conv, gdn_a_log, gdn_dt_bias, gdn_norm, gla_gate_w2, gla_gate_b, gla_norm, hgrn_lb, hgrn_norm, mla_q_norm, mla_wq_b, mla_kv_norm, mla_wkv_b, w_branch, b_gates, w_out, ln1_g, ln1_b, ln2_g, ln2_b, ffn_w1, ffn_w3, ffn_w2):
    nb_c, l_c, d = x_prompt.shape
    nb_l, l_l, _ = x_sample.shape
    depth = w_in.shape[0]
    t_c, t_l = nb_c * l_c, nb_l * l_l
    past = cache_mla_ckv.shape[2]
    alpha = (2.0 * depth) ** 0.25
    assert d == D_MODEL and t_c % 1024 == 0 and l_l % 1024 == 0 and l_c % CHUNK == 0

    n_cond = 1 + nb_l
    cc = jnp.concatenate([c_ctx[None, :], c, jnp.zeros((-n_cond % 8, d), f32)], axis=0)
    mods = _ada(cc, w_ada, b_ada).reshape(depth, cc.shape[0], 6, d)

    w_in_r = _relayout_w_in(w_in)
    mexp_np, lvl_np = _scan_consts()
    mexp = jnp.asarray(mexp_np, bf16)
    lvl = jnp.asarray(lvl_np, f32)
    cos_t, sin_t = _rope_tables(l_l)
    cum = jnp.cumsum(jax.nn.softmax(hgrn_lb.astype(f32), axis=0), axis=0)
    lower_bounds = cum - cum[:1]

    x = jnp.concatenate([x_prompt.reshape(t_c, d), x_sample.reshape(t_l, d)], axis=0)
    t = t_c + t_l
    zero_state = jnp.zeros((nb_c, N_HEADS, 2, HEAD_W, HEAD_W), f32)
    streams = (dict(nseq=nb_c, seq=l_c, row0=0), dict(nseq=nb_l, seq=l_l, row0=t_c))
    new_gdn, new_gla, new_hg, new_ckv, new_kpe = [], [], [], [], []

    for l in range(depth):
        mod = mods[l]
        p = _inproj(x, mod, w_in_r, l, t_c, l_l)

        gsm = p[:, OFF_SMALL + SM_GDN_B:OFF_SMALL + SM_GDN_B + 16].reshape(t // GROUP, GROUP, 4, N_HEADS)
        grow = jnp.pad(gsm.transpose(3, 0, 2, 1), ((0, 0), (0, 0), (0, 4), (0, 0)))
        zc = jnp.zeros((N_HEADS, 2), f32)
        hp = jnp.stack([jnp.concatenate([zc, gdn_a_log[l].T, zc, zc], axis=1),
                        jnp.concatenate([zc, gdn_dt_bias[l].T, zc, zc], axis=1)], axis=-1)

        gla_gate = _gla_gate_weights(gla_gate_w2[l], gla_gate_b[l])
        wqa, wqb = _relayout_wq(mla_wq_b[l])
        wkv = _relayout_wkv(mla_wkv_b[l])
        qnw, kvnw = mla_q_norm[l][None, :], mla_kv_norm[l][None, :]

        outs = {k: [] for k in ('gdn', 'gla', 'hg', 'mla')}
        for si, st in enumerate(streams):
            nseq, seq, row0 = st['nseq'], st['seq'], st['row0']
            rb = row0 // seq
            if si == 0:
                s_gdn0 = s_gla0 = s_hg0 = zero_state
            else:
                s_gdn0 = state_gdn[:, l].transpose(0, 2, 1, 3, 4)
                s_gla0 = jnp.pad(state_gla[:, l], ((0, 0),) * 3 + ((0, HEAD_W - GLA_DK), (0, 0))).transpose(0, 2, 1, 4, 3)
                s_hg0 = state_hgrn[:, l].transpose(0, 2, 1, 4, 3)
            o_gdn, s_gdn = _gdn(p, gdn_conv[l], grow, hp, gdn_norm[l][None, :], s_gdn0, nseq=nseq, seq=seq, row_blk0=rb)
            o_gla, s_gla = _scan(p, (OFF_GLA_Q, OFF_GLA_V, OFF_GLA_R, OFF_GLA_K, OFF_SMALL + SM_GLA_G), gla_gate,
                                 gla_norm[l][None, :], s_gla0, mexp, lvl,
                                 nseq=nseq, seq=seq, row_blk0=rb, hgrn=False, q_scale=GLA_DK ** -0.5)
            o_hg, s_hg = _scan(p, (OFF_HG_Q, OFF_HG_I, OFF_HG_G, OFF_HG_FF, OFF_HG_FB), (lower_bounds[l],),
                               hgrn_norm[l][None, :], s_hg0, mexp, lvl,
                               nseq=nseq, seq=seq, row_blk0=rb, hgrn=True, q_scale=1.0)
            rope = si == 1
            qn, qp, ckv, kn, vv, kp = _mla_proj(p, qnw, kvnw, wqa, wkv, wqb if rope else None,
                                                cos_t if rope else None, sin_t if rope else None,
                                                nrows=nseq * seq, seq=seq, row0=row0)
            ctx = None
            if si == 1:
                knc, vvc = _kv_proj(cache_mla_ckv[:, l].reshape(nb_l * past, -1), wkv)
                kpc = jnp.pad(cache_mla_kpe[:, l].reshape(nb_l * past, MLA_ROPE), ((0, 0), (0, LANE - MLA_ROPE))).astype(bf16)
                ctx = (knc, kpc, vvc)
            o_mla = _attention(qn, qp, kn, kp, vv, ctx, nseq=nseq, seq=seq)
            outs['gdn'].append(o_gdn)
            outs['gla'].append(o_gla)
            outs['hg'].append(o_hg)
            outs['mla'].append(o_mla)
            if si == 0:
                new_gdn.append(s_gdn.transpose(0, 2, 1, 3, 4))
                new_gla.append(s_gla.transpose(0, 2, 1, 4, 3)[:, :, :, :GLA_DK, :])
                new_hg.append(s_hg.transpose(0, 2, 1, 4, 3))
                new_ckv.append(ckv.reshape(nb_c, l_c, -1))
                new_kpe.append(p[:t_c, OFF_SMALL + SM_KPE:OFF_SMALL + SM_KPE + MLA_ROPE].reshape(nb_c, l_c, MLA_ROPE))

        branches = [jnp.concatenate(outs[k], axis=0) for k in ('gdn', 'gla', 'hg', 'mla')]
        wb = _cast_layer(w_branch.reshape(depth, 4 * BRANCH_W, d), l).reshape(4, BRANCH_W, d)
        x1 = _merge(branches, p, b_gates[l], wb, _cast_layer(w_out, l), x, mod,
                    ln1_g[l][None, :], ln1_b[l][None, :], alpha=alpha, t_ctx=t_c, l_lat=l_l)
        x = _ffn(x1, mod, _cast_layer(ffn_w1, l), _cast_layer(ffn_w3, l), _cast_layer(ffn_w2, l),
                 ln2_g[l][None, :], ln2_b[l][None, :], alpha=alpha, t_ctx=t_c, l_lat=l_l, split=l == depth - 1)

    y_ctx, y_lat = x
    sdt = x_prompt.dtype
    return (y_ctx.reshape(nb_c, l_c, d), y_lat.reshape(nb_l, l_l, d),
            jnp.stack(new_gdn, axis=1).astype(sdt), jnp.stack(new_gla, axis=1).astype(sdt),
            jnp.stack(new_hg, axis=1).astype(sdt), jnp.stack(new_ckv, axis=1), jnp.stack(new_kpe, axis=1))
```

```python
import functools

import numpy as np
import jax
import jax.numpy as jnp
from jax import lax
from jax.experimental import pallas as pl
from jax.experimental.pallas import tpu as pltpu

f32 = jnp.float32
bf16 = jnp.bfloat16

D_MODEL = 2048
N_HEADS = 4
HEAD_W = 128
BRANCH_W = N_HEADS * HEAD_W
GLA_DK = 64
GLA_RANK = 16
GLA_TAU = 16.0
GDN_CONV = 5
MLA_NOPE = 128
MLA_ROPE = 64
ROPE_BASE = 10000.0
GRID_W = 64
NORM_EPS = 1e-6
CHUNK = 64
SUB = 16
LANE = 128
VMEM_LIMIT = 56 * 1024 * 1024

_SEG = 512
OFF_GDN_Q, OFF_GDN_K, OFF_GDN_V, OFF_GDN_Z = 0, 512, 1024, 1536
OFF_GLA_Q, OFF_GLA_K, OFF_GLA_V, OFF_GLA_R = 2048, 2560, 3072, 3584
OFF_HG_Q, OFF_HG_FF, OFF_HG_FB, OFF_HG_I, OFF_HG_G = 4096, 4608, 5120, 5632, 6144
OFF_MLA_QA, OFF_MLA_KVA, OFF_SMALL, OFF_GATES = 6656, 7168, 7680, 8192
N_PROJ = OFF_GATES + N_HEADS * D_MODEL
SM_KPE, SM_KPE_SW, SM_GLA_G, SM_GDN_B, SM_GDN_A = 0, 64, 128, 160, 168

_SRC = {}
_o = 0
for _n, _w in (('gdn_q', 512), ('gdn_k', 512), ('gdn_v', 512), ('gdn_z', 512), ('gdn_b', 8), ('gdn_a', 8),
               ('gla_q', 256), ('gla_k', 256), ('gla_v', 512), ('gla_r', 512), ('gla_g', 32),
               ('hg_q', 512), ('hg_f', 1024), ('hg_i', 512), ('hg_g', 512),
               ('mla_qa', 512), ('mla_kva', 512), ('mla_kpe', 64), ('gates', 4 * D_MODEL)):
    _SRC[_n] = _o
    _o += _w
IN_WIDTH = _o


def _cparams(sem):
    return pltpu.CompilerParams(dimension_semantics=sem, vmem_limit_bytes=VMEM_LIMIT)


def _dot(a, b):
    return jnp.dot(a, b, preferred_element_type=f32)


def _dot_nt(a, b):
    return lax.dot_general(a, b, (((1,), (1,)), ((), ())), preferred_element_type=f32)


def _dot_tn(a, b):
    return lax.dot_general(a, b, (((0,), (0,)), ((), ())), preferred_element_type=f32)


def _split2(x):
    hi = x.astype(bf16)
    lo = (x - hi.astype(f32)).astype(bf16)
    return hi, lo


def _mm3(a, b):
    ah, al = _split2(a)
    bh, bl = _split2(b)
    return _dot(ah, bh) + (_dot(ah, bl) + _dot(al, bh))


def _layernorm(y, g, b):
    mu = jnp.mean(y, axis=-1, keepdims=True)
    yc = y - mu
    var = jnp.mean(yc * yc, axis=-1, keepdims=True)
    return yc * lax.rsqrt(var + NORM_EPS) * g + b


def _rmsnorm(y, w):
    return y * lax.rsqrt(jnp.mean(y * y, axis=-1, keepdims=True) + NORM_EPS) * w


def _ada_kernel(c_ref, w_ref, b_ref, o_ref):
    cs = jax.nn.silu(c_ref[...]).astype(bf16)
    o_ref[...] = _dot(cs, w_ref[...].astype(bf16)) + b_ref[...]


def _ada(cc, w_ada, b_ada):
    depth, d, n6 = w_ada.shape
    rows = cc.shape[0]
    tn = 1024
    return pl.pallas_call(
        _ada_kernel,
        out_shape=jax.ShapeDtypeStruct((depth, rows, n6), f32),
        grid=(depth, n6 // tn),
        in_specs=[pl.BlockSpec((rows, d), lambda l, j: (0, 0)),
                  pl.BlockSpec((None, d, tn), lambda l, j: (l, 0, j)),
                  pl.BlockSpec((None, 1, tn), lambda l, j: (l, 0, j))],
        out_specs=pl.BlockSpec((None, rows, tn), lambda l, j: (l, 0, j)),
        compiler_params=_cparams(("parallel", "parallel")),
        name="ada_mod",
    )(cc, w_ada, b_ada.reshape(depth, 1, n6))


def _inproj_kernel(x_ref, mod_ref, w_ref, o_ref, xb_ref):
    @pl.when(pl.program_id(1) == 0)
    def _():
        shift = mod_ref[0:1, :]
        scale = mod_ref[1:2, :]
        xb_ref[...] = (x_ref[...] * (1.0 + scale) + shift).astype(bf16)

    o_ref[...] = _dot_nt(xb_ref[...], w_ref[...])


def _mod_index(tm, t_ctx, l_lat):
    def index(i):
        r = i * tm
        return jnp.where(r < t_ctx, 0, 1 + (r - t_ctx) // l_lat)
    return index


def _inproj(x, mod, w, layer, t_ctx, l_lat):
    t, d = x.shape
    n = w.shape[1]
    tm, tn = 1024, 1024
    midx = _mod_index(tm, t_ctx, l_lat)
    return pl.pallas_call(
        _inproj_kernel,
        out_shape=jax.ShapeDtypeStruct((t, n), f32),
        grid=(t // tm, n // tn),
        in_specs=[pl.BlockSpec((tm, d), lambda i, j: (i, 0)),
                  pl.BlockSpec((None, 6, d), lambda i, j: (midx(i), 0, 0)),
                  pl.BlockSpec((None, tn, d), lambda i, j: (layer, j, 0))],
        out_specs=pl.BlockSpec((tm, tn), lambda i, j: (i, j)),
        scratch_shapes=[pltpu.VMEM((tm, d), bf16)],
        compiler_params=_cparams(("parallel", "arbitrary")),
        name="in_proj",
    )(x, mod, w)


_PACK = 4
GROUP = _PACK * CHUNK
_GDN_HEADS = 4


def _gdn_kernel(q_ref, k_ref, v_ref, z_ref, cwq_ref, cwk_ref, cwv_ref, grow_ref, hp_ref, nw_ref, s0_ref,
                o_ref, so_ref, pad_ref, qs, ks, vs, gs, ub_s, wq_s, aq_s, kd_s, ct_s, of, ob, s_ref, *, seq):
    n = seq // CHUNK
    width = _GDN_HEADS * HEAD_W
    chains = [(hd, d) for hd in range(_GDN_HEADS) for d in range(2)]

    def conv_silu(x_ref, w_ref):
        pad_ref[0:8, :] = jnp.zeros((8, width), f32)
        pad_ref[8 + seq:16 + seq, :] = jnp.zeros((8, width), f32)
        pad_ref[8:8 + seq, :] = x_ref[...]
        acc = pad_ref[pl.ds(6, seq), :] * w_ref[0:1, :]
        for j in range(1, GDN_CONV):
            acc = acc + pad_ref[pl.ds(6 + j, seq), :] * w_ref[j:j + 1, :]
        return jax.nn.silu(acc)

    def l2norm_heads(x):
        parts = []
        for hd in range(_GDN_HEADS):
            xh = x[:, hd * HEAD_W:(hd + 1) * HEAD_W]
            parts.append(xh * lax.rsqrt(jnp.sum(xh * xh, axis=-1, keepdims=True) + NORM_EPS))
        return jnp.concatenate(parts, axis=1)

    qs[...] = l2norm_heads(conv_silu(q_ref, cwq_ref)) * (HEAD_W ** -0.5)
    ks[...] = l2norm_heads(conv_silu(k_ref, cwk_ref))
    vs[...] = conv_silu(v_ref, cwv_ref)

    x = grow_ref[...]
    a_log = hp_ref[:, :, 0:1][:, None]
    dt_b = hp_ref[:, :, 1:2][:, None]
    rows = lax.broadcasted_iota(jnp.int32, x.shape, 2)
    gs[...] = jnp.where(rows < 2, jax.nn.sigmoid(x), -jnp.exp(a_log) * jax.nn.softplus(x + dt_b))
    s_ref[...] = s0_ref[...].reshape(2 * _GDN_HEADS, HEAD_W, HEAD_W)

    ri = lax.broadcasted_iota(jnp.int32, (CHUNK, GROUP), 0)
    li = lax.broadcasted_iota(jnp.int32, (CHUNK, GROUP), 1)
    cj = li % CHUNK
    blk = [(li // CHUNK) == r for r in range(_PACK)]
    blk_bf = [jnp.where(b, 1.0, 0.0).astype(bf16) for b in blk]
    l1 = lax.broadcasted_iota(jnp.int32, (1, GROUP), 1) // CHUNK
    eye_b = ri == cj
    eye = jnp.where(eye_b, 1.0, 0.0).astype(f32)
    bd16 = (ri // SUB) == (cj // SUB)
    bd32 = (ri // (2 * SUB)) == (cj // (2 * SUB))
    r2 = lax.broadcasted_iota(jnp.int32, (GROUP, GROUP), 0)
    c2 = lax.broadcasted_iota(jnp.int32, (GROUP, GROUP), 1)
    same_blk = (r2 // CHUNK) == (c2 // CHUNK)

    def bdiag(yb):
        return jnp.concatenate([yb * mk for mk in blk_bf], axis=0)

    def unpack_diag(xf):
        out = xf[0:CHUNK]
        for r in range(1, _PACK):
            out = jnp.where(blk[r], xf[r * CHUNK:(r + 1) * CHUNK], out)
        return out

    def pmm3(pairs):
        sp = [(_split2(a), _split2(b)) for a, b in pairs]
        r1 = [_dot(jnp.concatenate([ah, al], axis=0), bdiag(bh)) for (ah, al), (bh, _) in sp]
        r2 = [_dot(ah, bdiag(bl)) for (ah, _), (_, bl) in sp]
        return [x1[:CHUNK] + x1[CHUNK:] + x2 for x1, x2 in zip(r1, r2)]

    def unit_tri_inverse(ms):
        k = len(ms)
        m32 = [jnp.where(bd32, m, 0.0) for m in ms]
        dg = [jnp.where(bd16, m, 0.0) for m in ms]
        d2 = pmm3([(a, a) for a in dg])
        both = pmm3([(a, a) for a in d2] + [(eye - a, eye + b) for a, b in zip(dg, d2)])
        d4, xi = both[:k], both[k:]
        both = pmm3([(a, a) for a in d4] + [(a, eye + b) for a, b in zip(xi, d4)])
        d8, xi = both[:k], both[k:]
        xi = pmm3([(a, eye + b) for a, b in zip(xi, d8)])
        for lo, hi in ((dg, m32), (m32, ms)):
            t = pmm3([(a, h_ - l_) for a, h_, l_ in zip(xi, hi, lo)])
            t = pmm3(list(zip(t, xi)))
            xi = [a - b for a, b in zip(xi, t)]
        return xi

    def block_cols(xp):
        return [jnp.sum(jnp.where(blk[r], xp, 0.0), axis=1, keepdims=True) for r in range(_PACK)]

    def spread(cols):
        out = jnp.broadcast_to(cols[0], (CHUNK, GROUP))
        for r in range(1, _PACK):
            out = jnp.where(blk[r], cols[r], out)
        return out

    def prepare_group(g, carry):
        rows_g = pl.ds(pl.multiple_of(g * GROUP, GROUP), GROUP)
        heads = range(_GDN_HEADS)
        k4 = [ks[rows_g, hd * HEAD_W:(hd + 1) * HEAD_W] for hd in heads]
        q4 = [qs[rows_g, hd * HEAD_W:(hd + 1) * HEAD_W] for hd in heads]
        v4 = [vs[rows_g, hd * HEAD_W:(hd + 1) * HEAD_W] for hd in heads]
        k4b = [x.astype(bf16) for x in k4]
        kq = [_dot_nt(jnp.concatenate([k4b[hd], q4[hd].astype(bf16)], axis=0), k4b[hd]) for hd in heads]
        kk_p = [unpack_diag(x[:GROUP]) for x in kq]
        qk_p = [unpack_diag(x[GROUP:]) for x in kq]
        g8 = [gs[hd, g] for hd in heads]
        ids = range(len(chains))
        beta_r = [g8[hd][d:d + 1, :] for hd, d in chains]
        g_r = [g8[hd][2 + d:3 + d, :] for hd, d in chains]
        tri2 = [cj <= ri, cj >= ri]
        tri = [tri2[d] for _, d in chains]
        strict = [jnp.logical_and(t, jnp.logical_not(eye_b)) for t in tri]
        tg = [jnp.where(tri[i], g_r[i], 0.0) for i in ids]
        gam_cols = [block_cols(tg[i]) for i in ids]
        beta_cols = [block_cols(jnp.where(eye_b, beta_r[i], 0.0)) for i in ids]
        tot_cols = [[jnp.sum(jnp.where(l1 == r, g_r[i], 0.0), axis=1, keepdims=True) for r in range(_PACK)]
                    for i in ids]
        strict2 = [(c2 % CHUNK) < (r2 % CHUNK), (c2 % CHUNK) > (r2 % CHUNK)]
        sm = [jnp.where(jnp.logical_and(same_blk, s_), 1.0, 0.0).astype(bf16) for s_ in strict2]
        r3 = []
        for i, (_, d) in enumerate(chains):
            th, tl = _split2(tg[i])
            tl2 = (tg[i] - th.astype(f32) - tl.astype(f32)).astype(bf16)
            r3.append(_dot(jnp.concatenate([th, tl, tl2], axis=0), sm[d]))
        dlt = [x[:CHUNK] + x[CHUNK:2 * CHUNK] + x[2 * CHUNK:] for x in r3]
        decay = [jnp.where(tri[i], jnp.exp(jnp.minimum(dlt[i], 0.0)), 0.0) for i in ids]
        t_inv = unit_tri_inverse([jnp.where(strict[i], kk_p[hd] * spread(beta_cols[i]) * decay[i], 0.0)
                                  for i, (hd, _) in enumerate(chains)])
        gam = [jnp.concatenate(gam_cols[i], axis=0) for i in ids]
        beta = [jnp.concatenate(beta_cols[i], axis=0) for i in ids]
        tot = [jnp.concatenate([jnp.broadcast_to(t, (CHUNK, 1)) for t in tot_cols[i]], axis=0) for i in ids]
        eg = [jnp.exp(x) for x in gam]
        rhs = [jnp.concatenate([v4[hd] * beta[i], k4[hd] * beta[i] * eg[i]], axis=1)
               for i, (hd, _) in enumerate(chains)]
        isp = [_split2(x) for x in t_inv]
        rsp = [_split2(x) for x in rhs]
        bih = [bdiag(isp[i][0]) for i in ids]
        s1 = [_dot(jnp.concatenate([bih[i], bdiag(isp[i][1])], axis=0), rsp[i][0]) for i in ids]
        s2 = [_dot(bih[i], rsp[i][1]) for i in ids]
        for i, (hd, d) in enumerate(chains):
            sol = s1[i][:GROUP] + s1[i][GROUP:] + s2[i]
            ub_s[i, rows_g, :] = sol[:, :HEAD_W]
            w_b = sol[:, HEAD_W:].astype(bf16)
            qd_b = (q4[hd] * eg[i]).astype(bf16)
            a_p = qk_p[hd] * decay[i]
            for r_ in range(_PACK):
                c = g * _PACK + r_
                wq_s[i, c, 0:CHUNK, :] = w_b[r_ * CHUNK:(r_ + 1) * CHUNK]
                wq_s[i, c, CHUNK:2 * CHUNK, :] = qd_b[r_ * CHUNK:(r_ + 1) * CHUNK]
                aq_s[i, c] = a_p[:, r_ * CHUNK:(r_ + 1) * CHUNK].astype(bf16)
                ct_s[i, c] = jnp.broadcast_to(jnp.exp(tot_cols[i][r_]), (8, HEAD_W))
            kd_s[i, rows_g, :] = (k4[hd] * jnp.exp(tot[i] - gam[i])).astype(bf16)
        return carry

    lax.fori_loop(0, n // _PACK, prepare_group, 0)

    def step(t, carry):
        ids = range(len(chains))
        cs = [t if d == 0 else n - 1 - t for _, d in chains]
        rows = [pl.ds(pl.multiple_of(c * CHUNK, CHUNK), CHUNK) for c in cs]
        s = [s_ref[i] for i in ids]
        sb = [x.astype(bf16) for x in s]
        r = [_dot(wq_s[i, cs[i]], sb[i]) for i in ids]
        u = [(ub_s[i, rows[i], :] - r[i][:CHUNK]).astype(bf16) for i in ids]
        o_c = [r[i][CHUNK:] + _dot(aq_s[i, cs[i]], u[i]) for i in ids]
        s_new = [ct_s[i, cs[i]][0:1, :] * s[i] + _dot_tn(kd_s[i, rows[i], :], u[i]) for i in ids]
        for i, (hd, d) in enumerate(chains):
            (of if d == 0 else ob)[rows[i], hd * HEAD_W:(hd + 1) * HEAD_W] = o_c[i]
            s_ref[i] = s_new[i]
        return carry

    lax.fori_loop(0, n, step, 0)
    o = of[...] + ob[...]
    z = z_ref[...]
    nw = nw_ref[...]
    o_ref[...] = jnp.concatenate(
        [_rmsnorm(o[:, hd * HEAD_W:(hd + 1) * HEAD_W], nw) * jax.nn.silu(z[:, hd * HEAD_W:(hd + 1) * HEAD_W])
         for hd in range(_GDN_HEADS)], axis=1).astype(bf16)
    so_ref[...] = s_ref[...].reshape(_GDN_HEADS, 2, HEAD_W, HEAD_W)


def _gdn(p, conv_w, grow, hp, norm_w, s0, *, nseq, seq, row_blk0):
    n = seq // CHUNK
    ng = seq // GROUP
    hh = _GDN_HEADS
    width = hh * HEAD_W
    nch = 2 * hh

    def pcol(off):
        return pl.BlockSpec((seq, width), lambda b, h: (row_blk0 + b, off // width + h))

    def wcol(off):
        return pl.BlockSpec((GDN_CONV, width), lambda b, h: (0, off // width + h))

    kern = functools.partial(_gdn_kernel, seq=seq)
    return pl.pallas_call(
        kern,
        out_shape=(jax.ShapeDtypeStruct((nseq * seq, BRANCH_W), bf16),
                   jax.ShapeDtypeStruct((nseq, N_HEADS, 2, HEAD_W, HEAD_W), f32)),
        grid=(nseq, N_HEADS // hh),
        in_specs=[pcol(OFF_GDN_Q), pcol(OFF_GDN_K), pcol(OFF_GDN_V), pcol(OFF_GDN_Z),
                  wcol(0), wcol(BRANCH_W), wcol(2 * BRANCH_W),
                  pl.BlockSpec((hh, ng, 8, GROUP), lambda b, h: (h, row_blk0 + b, 0, 0)),
                  pl.BlockSpec((hh, 8, 2), lambda b, h: (h, 0, 0)),
                  pl.BlockSpec((1, HEAD_W), lambda b, h: (0, 0)),
                  pl.BlockSpec((None, hh, 2, HEAD_W, HEAD_W), lambda b, h: (b, h, 0, 0, 0))],
        out_specs=(pl.BlockSpec((seq, width), lambda b, h: (b, h)),
                   pl.BlockSpec((None, hh, 2, HEAD_W, HEAD_W), lambda b, h: (b, h, 0, 0, 0))),
        scratch_shapes=[pltpu.VMEM((seq + 16, width), f32),
                        pltpu.VMEM((seq, width), f32), pltpu.VMEM((seq, width), f32),
                        pltpu.VMEM((seq, width), f32), pltpu.VMEM((hh, ng, 8, GROUP), f32),
                        pltpu.VMEM((nch, seq, HEAD_W), f32), pltpu.VMEM((nch, n, 2 * CHUNK, HEAD_W), bf16),
                        pltpu.VMEM((nch, n, CHUNK, CHUNK), bf16),
                        pltpu.VMEM((nch, seq, HEAD_W), bf16), pltpu.VMEM((nch, n, 8, HEAD_W), f32),
                        pltpu.VMEM((seq, width), f32), pltpu.VMEM((seq, width), f32),
                        pltpu.VMEM((nch, HEAD_W, HEAD_W), f32)],
        compiler_params=_cparams(("parallel", "parallel")),
        name="gdn",
    )(p, p, p, p, conv_w, conv_w, conv_w, grow, hp, norm_w, s0)


_N_LEVELS = 6
_N_MM_LEVELS = 4
_ROW_EQ = _N_MM_LEVELS * CHUNK
_ROW_EK, _ROW_TOT, _ROWS_EXP = _ROW_EQ + CHUNK, _ROW_EQ + 2 * CHUNK, _ROW_EQ + 2 * CHUNK + 8
_SCAN_UNROLL = 4
_SCAN_HEADS = 2


def _scan_consts():
    c = CHUNK
    mexp = np.zeros((2, _ROWS_EXP, c), np.float32)
    lvl = np.full((2, c, c), -1.0, np.float32)
    for lv in range(_N_LEVELS):
        s = 32 >> lv
        for i in range(c):
            p = (i // (2 * s)) * (2 * s) + s
            right = (i % (2 * s)) >= s
            if lv >= _N_MM_LEVELS:
                pass
            elif right:
                mexp[0, lv * c + i, p:i + 1] = 1.0
                mexp[1, lv * c + i, p:i] = 1.0
            else:
                mexp[0, lv * c + i, i + 1:p] = 1.0
                mexp[1, lv * c + i, i:p] = 1.0
            for j in range(c):
                if (i // (2 * s)) != (j // (2 * s)):
                    continue
                jright = (j % (2 * s)) >= s
                if right and not jright:
                    lvl[0, i, j] = lv
                if (not right) and jright:
                    lvl[1, i, j] = lv
    for i in range(c):
        lvl[:, i, i] = _N_LEVELS
        mexp[0, _ROW_EQ + i, :i + 1] = 1.0
        mexp[0, _ROW_EK + i, i + 1:] = 1.0
        mexp[1, _ROW_EQ + i, i:] = 1.0
        mexp[1, _ROW_EK + i, :i] = 1.0
    mexp[:, _ROW_TOT:, :] = 1.0
    return mexp, lvl


_LOG_DECAY_FLOOR = -1.0e4


def _scan_kernel(*refs, seq, hgrn, q_scale):
    if hgrn:
        q_ref, v_ref, gate_ref, zf_ref, zb_ref, lb_ref = refs[:6]
    else:
        q_ref, v_ref, gate_ref, k_ref, glr_ref, w2_ref, b2_ref = refs[:7]
    nw_ref, s0_ref, mexp_ref, lvl_ref, o_ref, so_ref, of, ob, qd_s, g_s, ct_s, stp_s, st_ref = refs[-13:]
    q_silu = hgrn
    gate_silu = not hgrn
    n = seq // CHUNK
    st_ref[...] = s0_ref[...].reshape(2 * _SCAN_HEADS, HEAD_W, HEAD_W)
    outs = (of, ob)

    def rows_of(c):
        return pl.ds(pl.multiple_of(c * CHUNK, CHUNK), CHUNK)

    def lanes(hd):
        return slice(hd * HEAD_W, (hd + 1) * HEAD_W)

    def work(g):
        return [(hd, d, g * _SCAN_UNROLL + u) for u in range(_SCAN_UNROLL) for hd in range(_SCAN_HEADS)
                for d in range(2)]

    def prepare_group(g, carry):
        todo3 = work(g)
        todo = [(d, c) for _, d, c in todo3]
        items = range(len(todo))
        q, k, v, log_decay = [], [], [], []
        for hd, d, c in todo3:
            rows_c = rows_of(c)
            qi = q_ref[rows_c, lanes(hd)]
            q.append(jax.nn.silu(qi) if q_silu else qi * q_scale)
            v.append(v_ref[rows_c, lanes(hd)].astype(bf16))
        if hgrn:
            for hd, d, c in todo3:
                z = (zf_ref if d == 0 else zb_ref)[rows_of(c), lanes(hd)]
                lb = lb_ref[d:d + 1, lanes(hd)]
                t0 = jnp.log(lb)
                t1 = jnp.log1p(-lb) + jax.nn.log_sigmoid(z)
                mx = jnp.maximum(t0, t1)
                lse = mx + jnp.log(jnp.exp(t0 - mx) + jnp.exp(t1 - mx))
                log_decay.append(jnp.where(mx == -jnp.inf, -jnp.inf, lse))
                k.append((1.0 - lb) * jax.nn.sigmoid(-z))
        else:
            logits = [_dot(glr_ref[rows_of(c), :].astype(bf16), w2_ref[d][:, lanes(hd)]) for hd, d, c in todo3]
            log_decay = [jax.nn.log_sigmoid(logits[i] + b2_ref[d][:, lanes(hd)]) * (1.0 / GLA_TAU)
                         for i, (hd, d, _) in enumerate(todo3)]
            k = [k_ref[rows_of(c), lanes(hd)] for hd, _, c in todo3]
        la = [jnp.maximum(x, _LOG_DECAY_FLOOR) for x in log_decay]
        e2 = []
        for i, (d, _) in enumerate(todo):
            hi, lo = _split2(la[i])
            e2.append(_dot(mexp_ref[d], jnp.concatenate([hi, lo], axis=1)))
        e = [x[:, :HEAD_W] + x[:, HEAD_W:] for x in e2]
        r4 = lax.broadcasted_iota(jnp.int32, (CHUNK, HEAD_W), 0) % 4
        fine = []
        for i, (d, _) in enumerate(todo):
            prev = pltpu.roll(la[i], 1, 0)
            nxt = pltpu.roll(la[i], CHUNK - 1, 0)
            if d == 0:
                w2 = jnp.where(r4 == 0, nxt, jnp.where(r4 == 1, 0.0, jnp.where(r4 == 2, la[i], la[i] + prev)))
                w1 = jnp.where(r4 % 2 == 1, la[i], 0.0)
            else:
                w2 = jnp.where(r4 == 0, la[i] + nxt, jnp.where(r4 == 1, la[i], jnp.where(r4 == 2, 0.0, prev)))
                w1 = jnp.where(r4 % 2 == 0, la[i], 0.0)
            fine.append((w2, w1))
        lvl = [lvl_ref[d] for d, _ in todo]
        qk = [_dot_nt(q[i].astype(bf16), k[i].astype(bf16)) for i in items]
        a = [jnp.where(lvl[i] == float(_N_LEVELS), qk[i], 0.0) for i in items]
        for lv in range(_N_LEVELS):
            if lv < _N_MM_LEVELS:
                w = [jnp.exp(e[i][lv * CHUNK:(lv + 1) * CHUNK]) for i in items]
            else:
                w = [jnp.exp(fine[i][lv - _N_MM_LEVELS]) for i in items]
            p = [_dot_nt((q[i] * w[i]).astype(bf16), (k[i] * w[i]).astype(bf16)) for i in items]
            a = [jnp.where(lvl[i] == float(lv), p[i], a[i]) for i in items]
        o_intra = [_dot(a[i].astype(bf16), v[i]) for i in items]
        incr = [_dot_tn(v[i], (k[i] * jnp.exp(e[i][_ROW_EK:_ROW_EK + CHUNK])).astype(bf16)) for i in items]
        for i, (hd, d, c) in enumerate(todo3):
            rows_c = rows_of(c)
            slot = 2 * hd + d
            outs[d][rows_c, lanes(hd)] = o_intra[i]
            qd_s[slot, rows_c, :] = (q[i] * jnp.exp(e[i][_ROW_EQ:_ROW_EQ + CHUNK])).astype(bf16)
            g_s[slot, c] = incr[i]
            ct_s[slot, c] = jnp.exp(e[i][_ROW_TOT:_ROW_TOT + 8])
        return carry

    lax.fori_loop(0, n // _SCAN_UNROLL, prepare_group, 0)

    def scan_step(i, carry):
        for hd in range(_SCAN_HEADS):
            for d, c in ((0, i), (1, n - 1 - i)):
                slot = 2 * hd + d
                st = st_ref[slot]
                stp_s[slot, c] = st.astype(bf16)
                st_ref[slot] = st * ct_s[slot, c][0:1, :] + g_s[slot, c]
        return carry

    lax.fori_loop(0, n, scan_step, 0)

    def inter_group(g, carry):
        todo3 = work(g)
        res = [_dot_nt(qd_s[2 * hd + d, rows_of(c), :], stp_s[2 * hd + d, c]) + outs[d][rows_of(c), lanes(hd)]
               for hd, d, c in todo3]
        for (hd, d, c), o_c in zip(todo3, res):
            outs[d][rows_of(c), lanes(hd)] = o_c
        return carry

    lax.fori_loop(0, n // _SCAN_UNROLL, inter_group, 0)
    o = of[...] + ob[...]
    g = gate_ref[...]
    g = jax.nn.silu(g) if gate_silu else jax.nn.sigmoid(g)
    nw = nw_ref[...]
    o_ref[...] = jnp.concatenate([_rmsnorm(o[:, lanes(hd)], nw) * g[:, lanes(hd)] for hd in range(_SCAN_HEADS)],
                                 axis=1).astype(bf16)
    so_ref[...] = st_ref[...].reshape(_SCAN_HEADS, 2, HEAD_W, HEAD_W)


def _scan(p, offs, extra, norm_w, s0t, mexp, lvl, *, nseq, seq, row_blk0, hgrn, q_scale):
    hh = _SCAN_HEADS
    width = hh * HEAD_W
    slots = 2 * hh
    n = seq // CHUNK

    def pspec(off):
        return pl.BlockSpec((seq, width), lambda b, h: (row_blk0 + b, off // width + h))

    if hgrn:
        (lower_bound,) = extra
        in_specs = [pspec(o) for o in offs] + [pl.BlockSpec((2, width), lambda b, h: (0, h))]
        args = [p] * len(offs) + [lower_bound]
    else:
        w2, b2 = extra
        tile = offs[4] // HEAD_W
        in_specs = ([pspec(o) for o in offs[:4]] + [pl.BlockSpec((seq, HEAD_W), lambda b, h: (row_blk0 + b, tile))]
                    + [pl.BlockSpec((2, HEAD_W, width), lambda b, h: (0, 0, h)),
                       pl.BlockSpec((2, 1, width), lambda b, h: (0, 0, h))])
        args = [p] * len(offs) + [w2, b2]
    kern = functools.partial(_scan_kernel, seq=seq, hgrn=hgrn, q_scale=q_scale)
    return pl.pallas_call(
        kern,
        out_shape=(jax.ShapeDtypeStruct((nseq * seq, BRANCH_W), bf16),
                   jax.ShapeDtypeStruct((nseq, N_HEADS, 2, HEAD_W, HEAD_W), f32)),
        grid=(nseq, N_HEADS // hh),
        in_specs=in_specs + [
            pl.BlockSpec((1, HEAD_W), lambda b, h: (0, 0)),
            pl.BlockSpec((None, hh, 2, HEAD_W, HEAD_W), lambda b, h: (b, h, 0, 0, 0)),
            pl.BlockSpec((2, _ROWS_EXP, CHUNK), lambda b, h: (0, 0, 0)),
            pl.BlockSpec((2, CHUNK, CHUNK), lambda b, h: (0, 0, 0))],
        out_specs=(pl.BlockSpec((seq, width), lambda b, h: (b, h)),
                   pl.BlockSpec((None, hh, 2, HEAD_W, HEAD_W), lambda b, h: (b, h, 0, 0, 0))),
        scratch_shapes=[pltpu.VMEM((seq, width), f32), pltpu.VMEM((seq, width), f32),
                        pltpu.VMEM((slots, seq, HEAD_W), bf16),
                        pltpu.VMEM((slots, n, HEAD_W, HEAD_W), f32),
                        pltpu.VMEM((slots, n, 8, HEAD_W), f32),
                        pltpu.VMEM((slots, n, HEAD_W, HEAD_W), bf16),
                        pltpu.VMEM((slots, HEAD_W, HEAD_W), f32)],
        compiler_params=_cparams(("parallel", "parallel")),
        name="decay_scan",
    )(*args, norm_w, s0t, mexp, lvl)


def _mla_proj_kernel(*refs, rope):
    if rope:
        (qa_ref, kva_ref, sm_ref, qnw_ref, kvnw_ref, wqa_ref, wkv_ref, wqb_ref, cos_ref, sin_ref,
         qn_ref, qp_ref, ckv_ref, kn_ref, vv_ref, kp_ref) = refs
    else:
        (qa_ref, kva_ref, sm_ref, qnw_ref, kvnw_ref, wqa_ref, wkv_ref,
         qn_ref, qp_ref, ckv_ref, kn_ref, vv_ref, kp_ref) = refs
    qh = _rmsnorm(qa_ref[...], qnw_ref[...]).astype(bf16)
    qa = _dot(qh, wqa_ref[...])
    qn_ref[...] = qa[:, :BRANCH_W].astype(bf16)
    pe = qa[:, BRANCH_W:]
    kpe = sm_ref[:, 0:LANE]
    if rope:
        cos = cos_ref[...]
        sin = sin_ref[...]
        cos4 = jnp.concatenate([cos] * N_HEADS, axis=1)
        sin4 = jnp.concatenate([sin] * N_HEADS, axis=1)
        pe = pe * cos4 + _dot(qh, wqb_ref[...]) * sin4
        kpe = kpe * cos + pltpu.roll(kpe, MLA_ROPE, 1) * sin
    qp_ref[...] = pe.astype(bf16)
    kp_ref[...] = kpe.astype(bf16)
    ckv = _rmsnorm(kva_ref[...], kvnw_ref[...])
    ckv_ref[...] = ckv
    kv = _dot(ckv.astype(bf16), wkv_ref[...])
    kn_ref[...] = kv[:, :BRANCH_W].astype(bf16)
    vv_ref[...] = kv[:, BRANCH_W:].astype(bf16)


def _mla_proj(p, qnw, kvnw, wqa, wkv, wqb, cos, sin, *, nrows, seq, row0):
    tm = 256
    rope = cos is not None
    rb0 = row0 // tm
    per_seq = seq // tm

    def pspec(off):
        return pl.BlockSpec((tm, _SEG), lambda i: (rb0 + i, off // _SEG))

    def full(a):
        return pl.BlockSpec(a.shape, lambda i: (0,) * a.ndim)

    in_specs = [pspec(OFF_MLA_QA), pspec(OFF_MLA_KVA), pspec(OFF_SMALL), full(qnw), full(kvnw), full(wqa), full(wkv)]
    args = [p, p, p, qnw, kvnw, wqa, wkv]
    if rope:
        tspec = pl.BlockSpec((tm, LANE), lambda i: (i % per_seq, 0))
        in_specs += [full(wqb), tspec, tspec]
        args += [wqb, cos, sin]
    wide = lambda dt: jax.ShapeDtypeStruct((nrows, BRANCH_W), dt)
    ospec = pl.BlockSpec((tm, BRANCH_W), lambda i: (i, 0))
    return pl.pallas_call(
        functools.partial(_mla_proj_kernel, rope=rope),
        out_shape=(wide(bf16), wide(bf16), wide(f32), wide(bf16), wide(bf16),
                   jax.ShapeDtypeStruct((nrows, LANE), bf16)),
        grid=(nrows // tm,),
        in_specs=in_specs,
        out_specs=(ospec, ospec, ospec, ospec, ospec, pl.BlockSpec((tm, LANE), lambda i: (i, 0))),
        compiler_params=_cparams(("parallel",)),
        name="mla_proj",
    )(*args)


def _kv_kernel(ckv_ref, w_ref, kn_ref, vv_ref):
    kv = _dot(ckv_ref[...].astype(bf16), w_ref[...])
    kn_ref[...] = kv[:, :BRANCH_W].astype(bf16)
    vv_ref[...] = kv[:, BRANCH_W:].astype(bf16)


def _kv_proj(ckv, wkv):
    rows = ckv.shape[0]
    tm = 256
    out = jax.ShapeDtypeStruct((rows, BRANCH_W), bf16)
    ospec = pl.BlockSpec((tm, BRANCH_W), lambda i: (i, 0))
    return pl.pallas_call(
        _kv_kernel, out_shape=(out, out), grid=(rows // tm,),
        in_specs=[pl.BlockSpec((tm, ckv.shape[1]), lambda i: (i, 0)),
                  pl.BlockSpec(wkv.shape, lambda i: (0, 0))],
        out_specs=(ospec, ospec),
        compiler_params=_cparams(("parallel",)),
        name="mla_ctx_kv",
    )(ckv, wkv)


def _attn_kernel(*refs, has_ctx):
    if has_ctx:
        qn_ref, qp_ref, kn_ref, kp_ref, vv_ref, knc_ref, kpc_ref, vvc_ref, o_ref = refs
    else:
        qn_ref, qp_ref, kn_ref, kp_ref, vv_ref, o_ref = refs
    scale = (MLA_NOPE + MLA_ROPE) ** -0.5
    heads = range(N_HEADS)

    def head(ref, h):
        return ref[:, h * HEAD_W:(h + 1) * HEAD_W]

    kp = kp_ref[...]
    s1 = [(_dot_nt(head(qn_ref, h), head(kn_ref, h)) + _dot_nt(head(qp_ref, h), kp)) * scale for h in heads]
    mx = [jnp.max(x, axis=-1, keepdims=True) for x in s1]
    if has_ctx:
        kpc = kpc_ref[...]
        s2 = [(_dot_nt(head(qn_ref, h), head(knc_ref, h)) + _dot_nt(head(qp_ref, h), kpc)) * scale for h in heads]
        mx = [jnp.maximum(m, jnp.max(x, axis=-1, keepdims=True)) for m, x in zip(mx, s2)]
    p1 = [jnp.exp(x - m) for x, m in zip(s1, mx)]
    den = [jnp.sum(x, axis=-1, keepdims=True) for x in p1]
    if has_ctx:
        p2 = [jnp.exp(x - m) for x, m in zip(s2, mx)]
        den = [d_ + jnp.sum(x, axis=-1, keepdims=True) for d_, x in zip(den, p2)]
    inv = [1.0 / d_ for d_ in den]
    o = [_dot((p1[h] * inv[h]).astype(bf16), head(vv_ref, h)) for h in heads]
    if has_ctx:
        o = [o[h] + _dot((p2[h] * inv[h]).astype(bf16), head(vvc_ref, h)) for h in heads]
    o_ref[...] = jnp.concatenate(o, axis=1).astype(bf16)


def _attention(qn, qp, kn, kp, vv, ctx, *, nseq, seq):
    tq = 256
    nq = seq // tq
    has_ctx = ctx is not None
    qspec = pl.BlockSpec((tq, BRANCH_W), lambda b, i: (b * nq + i, 0))
    kspec = pl.BlockSpec((seq, BRANCH_W), lambda b, i: (b, 0))
    kpspec = pl.BlockSpec((seq, LANE), lambda b, i: (b, 0))
    in_specs = [qspec, qspec, kspec, kpspec, kspec]
    args = [qn, qp, kn, kp, vv]
    if has_ctx:
        knc, kpc, vvc = ctx
        lc = knc.shape[0] // nseq
        in_specs += [pl.BlockSpec((lc, BRANCH_W), lambda b, i: (b, 0)),
                     pl.BlockSpec((lc, LANE), lambda b, i: (b, 0)),
                     pl.BlockSpec((lc, BRANCH_W), lambda b, i: (b, 0))]
        args += [knc, kpc, vvc]
    return pl.pallas_call(
        functools.partial(_attn_kernel, has_ctx=has_ctx),
        out_shape=jax.ShapeDtypeStruct((nseq * seq, BRANCH_W), bf16),
        grid=(nseq, nq),
        in_specs=in_specs,
        out_specs=pl.BlockSpec((tq, BRANCH_W), lambda b, i: (b * nq + i, 0)),
        compiler_params=_cparams(("parallel", "parallel")),
        name="mla_attn",
    )(*args)


def _merge_kernel(o0_ref, o1_ref, o2_ref, o3_ref, g0_ref, g1_ref, g2_ref, g3_ref, bg_ref, wb_ref, wo_ref,
                  x_ref, mod_ref, lng_ref, lnb_ref, out_ref, *, alpha):
    m = None
    for k, (o_ref, g_ref) in enumerate(((o0_ref, g0_ref), (o1_ref, g1_ref), (o2_ref, g2_ref), (o3_ref, g3_ref))):
        term = jax.nn.sigmoid(g_ref[...] + bg_ref[k:k + 1, :]) * _dot(o_ref[...], wb_ref[k])
        m = term if m is None else m + term
    mix = _dot(m.astype(bf16), wo_ref[...])
    gate1 = mod_ref[2:3, :]
    out_ref[...] = _layernorm(alpha * x_ref[...] + gate1 * mix, lng_ref[...], lnb_ref[...])


def _merge(branches, p, bg, wb, wo, x, mod, lng, lnb, *, alpha, t_ctx, l_lat):
    t, d = x.shape
    tm = 256
    midx = _mod_index(tm, t_ctx, l_lat)
    ospec = pl.BlockSpec((tm, BRANCH_W), lambda i: (i, 0))
    gspecs = [pl.BlockSpec((tm, d), functools.partial(lambda i, k: (i, OFF_GATES // d + k), k=k)) for k in range(4)]
    single = dict(pipeline_mode=pl.Buffered(1))
    return pl.pallas_call(
        functools.partial(_merge_kernel, alpha=alpha),
        out_shape=jax.ShapeDtypeStruct((t, d), f32),
        grid=(t // tm,),
        in_specs=[ospec, ospec, ospec, ospec] + gspecs + [
            pl.BlockSpec((4, d), lambda i: (0, 0)),
            pl.BlockSpec((4, BRANCH_W, d), lambda i: (0, 0, 0), **single),
            pl.BlockSpec((d, d), lambda i: (0, 0), **single),
            pl.BlockSpec((tm, d), lambda i: (i, 0)),
            pl.BlockSpec((None, 6, d), lambda i: (midx(i), 0, 0)),
            pl.BlockSpec((1, d), lambda i: (0, 0)),
            pl.BlockSpec((1, d), lambda i: (0, 0))],
        out_specs=pl.BlockSpec((tm, d), lambda i: (i, 0)),
        compiler_params=_cparams(("parallel",)),
        name="merge_out_ln",
    )(*branches, p, p, p, p, bg, wb, wo, x, mod, lng, lnb)


def _ffn_kernel(x_ref, mod_ref, w1_ref, w3_ref, w2_ref, lng_ref, lnb_ref, *rest, alpha, ctx_tiles):
    if ctx_tiles is None:
        out_ref, hb_ref, acc_ref = rest
    else:
        out_ref, out_lat_ref, hb_ref, acc_ref = rest
    f = pl.program_id(1)

    @pl.when(f == 0)
    def _():
        shift = mod_ref[3:4, :]
        scale = mod_ref[4:5, :]
        hb_ref[...] = (x_ref[...] * (1.0 + scale) + shift).astype(bf16)
        acc_ref[...] = jnp.zeros_like(acc_ref)

    h = hb_ref[...]
    g = (jax.nn.silu(_dot(h, w1_ref[...])) * _dot(h, w3_ref[...])).astype(bf16)
    acc_ref[...] += _dot(g, w2_ref[...])

    last = f == pl.num_programs(1) - 1

    def result():
        gate2 = mod_ref[5:6, :]
        return _layernorm(alpha * x_ref[...] + gate2 * acc_ref[...], lng_ref[...], lnb_ref[...])

    if ctx_tiles is None:
        @pl.when(last)
        def _():
            out_ref[...] = result()
    else:
        is_ctx = pl.program_id(0) < ctx_tiles

        @pl.when(jnp.logical_and(last, is_ctx))
        def _():
            out_ref[...] = result()

        @pl.when(jnp.logical_and(last, jnp.logical_not(is_ctx)))
        def _():
            out_lat_ref[...] = result()


def _ffn(x, mod, w1, w3, w2, lng, lnb, *, alpha, t_ctx, l_lat, split):
    t, d = x.shape
    dff = w1.shape[1]
    tm, tf = 512, 512
    midx = _mod_index(tm, t_ctx, l_lat)
    if split:
        ctx_tiles = t_ctx // tm
        out_shape = (jax.ShapeDtypeStruct((t_ctx, d), f32), jax.ShapeDtypeStruct((t - t_ctx, d), f32))
        out_specs = (pl.BlockSpec((tm, d), lambda i, f: (jnp.minimum(i, ctx_tiles - 1), 0)),
                     pl.BlockSpec((tm, d), lambda i, f: (jnp.maximum(i - ctx_tiles, 0), 0)))
    else:
        ctx_tiles = None
        out_shape = jax.ShapeDtypeStruct((t, d), f32)
        out_specs = pl.BlockSpec((tm, d), lambda i, f: (i, 0))
    return pl.pallas_call(
        functools.partial(_ffn_kernel, alpha=alpha, ctx_tiles=ctx_tiles),
        out_shape=out_shape,
        grid=(t // tm, dff // tf),
        in_specs=[pl.BlockSpec((tm, d), lambda i, f: (i, 0)),
                  pl.BlockSpec((None, 6, d), lambda i, f: (midx(i), 0, 0)),
                  pl.BlockSpec((d, tf), lambda i, f: (0, f)),
                  pl.BlockSpec((d, tf), lambda i, f: (0, f)),
                  pl.BlockSpec((tf, d), lambda i, f: (f, 0)),
                  pl.BlockSpec((1, d), lambda i, f: (0, 0)),
                  pl.BlockSpec((1, d), lambda i, f: (0, 0))],
        out_specs=out_specs,
        scratch_shapes=[pltpu.VMEM((tm, d), bf16), pltpu.VMEM((tm, d), f32)],
        compiler_params=_cparams(("arbitrary" if split else "parallel", "arbitrary")),
        name="ffn_ln",
    )(x, mod, w1, w3, w2, lng, lnb)


def _cast_kernel(x_ref, o_ref):
    o_ref[...] = x_ref[...].astype(bf16)


def _cast_layer(w, layer):
    _, rows, cols = w.shape
    tr = 256 if cols > 4096 else 512
    return pl.pallas_call(
        _cast_kernel,
        out_shape=jax.ShapeDtypeStruct((rows, cols), bf16),
        grid=(rows // tr,),
        in_specs=[pl.BlockSpec((None, tr, cols), lambda i: (layer, i, 0))],
        out_specs=pl.BlockSpec((tr, cols), lambda i: (i, 0)),
        compiler_params=_cparams(("parallel",)),
        name="cast_bf16",
    )(w)


_ROW_UNIT = 16


def _relayout_plan():
    plain = lambda src: (0, src)
    special = lambda k: (1, k)
    segs = [plain(_SRC[n]) for n in ('gdn_q', 'gdn_k', 'gdn_v', 'gdn_z')]
    segs += [special(0), special(1)]
    segs += [plain(_SRC['gla_v']), plain(_SRC['gla_r']), plain(_SRC['hg_q']), plain(_SRC['hg_f']),
             plain(_SRC['hg_f'] + 512), plain(_SRC['hg_i']), plain(_SRC['hg_g']), plain(_SRC['mla_qa']),
             plain(_SRC['mla_kva'])]
    segs += [special(2)]
    segs += [plain(_SRC['gates'] + _SEG * i) for i in range(N_HEADS * D_MODEL // _SEG)]
    assert len(segs) == N_PROJ // _SEG
    table = np.zeros((3, len(segs)), np.int32)
    row, blk = segs[0][1], 0
    for j, (is_special, val) in enumerate(segs):
        if is_special:
            blk = val
        else:
            row = val
        assert row % _ROW_UNIT == 0
        table[:, j] = (is_special, row // _ROW_UNIT, blk)
    return table


def _special_rows(wt, layer):
    d = wt.shape[2]

    def rows(name, start, n):
        s = _SRC[name] + start
        return lax.slice(wt, (layer, s, 0), (layer + 1, s + n, d))[0]

    def head_padded(name):
        parts = []
        for h in range(N_HEADS):
            parts += [rows(name, h * GLA_DK, GLA_DK), jnp.zeros((HEAD_W - GLA_DK, d), wt.dtype)]
        return parts

    q4 = MLA_ROPE // 4
    small = [rows('mla_kpe', 0, MLA_ROPE), rows('mla_kpe', q4, q4), rows('mla_kpe', 0, q4),
             rows('mla_kpe', 3 * q4, q4), rows('mla_kpe', 2 * q4, q4), rows('gla_g', 0, 2 * GLA_RANK),
             rows('gdn_b', 0, 8), rows('gdn_a', 0, 8), jnp.zeros((_SEG - SM_GDN_A - 8, d), wt.dtype)]
    return jnp.concatenate(head_padded('gla_q') + head_padded('gla_k') + small, axis=0)


def _relayout_kernel(tab_ref, wt_ref, sp_ref, o_ref):
    is_special = tab_ref[0, pl.program_id(1)]

    @pl.when(is_special == 0)
    def _():
        o_ref[...] = wt_ref[0].astype(bf16)

    @pl.when(is_special == 1)
    def _():
        o_ref[...] = sp_ref[...].astype(bf16)


def _relayout_w_in(w_in):
    depth, d, _ = w_in.shape
    wt = jnp.swapaxes(w_in, 1, 2)
    special = jnp.stack([_special_rows(wt, l) for l in range(depth)])
    table = _relayout_plan()
    nseg = N_PROJ // _SEG
    return pl.pallas_call(
        _relayout_kernel,
        out_shape=jax.ShapeDtypeStruct((depth, N_PROJ, d), bf16),
        grid_spec=pltpu.PrefetchScalarGridSpec(
            num_scalar_prefetch=1,
            grid=(depth, nseg),
            in_specs=[pl.BlockSpec((pl.Element(1), pl.Element(_SEG), pl.Element(d)),
                                   lambda l, j, tab: (l, tab[1, j] * _ROW_UNIT, 0)),
                      pl.BlockSpec((None, _SEG, d), lambda l, j, tab: (l, tab[2, j], 0))],
            out_specs=pl.BlockSpec((None, _SEG, d), lambda l, j, tab: (l, j, 0))),
        compiler_params=_cparams(("parallel", "parallel")),
        name="w_in_relayout",
    )(jnp.asarray(table), wt, special)


def _relayout_wq(wq):
    hw = MLA_NOPE + MLA_ROPE
    z = jnp.zeros((wq.shape[0], HEAD_W - MLA_ROPE), wq.dtype)
    q4 = MLA_ROPE // 4
    nope, pe, pes = [], [], []
    for h in range(N_HEADS):
        base = h * hw
        nope.append(wq[:, base:base + MLA_NOPE])
        r = wq[:, base + MLA_NOPE:base + hw]
        pe += [r, z]
        pes += [r[:, q4:2 * q4], r[:, 0:q4], r[:, 3 * q4:], r[:, 2 * q4:3 * q4], z]
    return jnp.concatenate(nope + pe, axis=1).astype(bf16), jnp.concatenate(pes, axis=1).astype(bf16)


def _relayout_wkv(wkv):
    hw = MLA_NOPE + HEAD_W
    kn = [wkv[:, h * hw:h * hw + MLA_NOPE] for h in range(N_HEADS)]
    vv = [wkv[:, h * hw + MLA_NOPE:(h + 1) * hw] for h in range(N_HEADS)]
    return jnp.concatenate(kn + vv, axis=1).astype(bf16)


def _rope_tables(length):
    pos = jnp.arange(length)
    row_id = (pos // GRID_W).astype(f32)
    col_id = (pos % GRID_W).astype(f32)
    half = MLA_ROPE // 2
    inv = ROPE_BASE ** (-jnp.arange(0, half, 2, dtype=f32) / half)
    ar, ac = row_id[:, None] * inv, col_id[:, None] * inv
    z = jnp.zeros((length, LANE - MLA_ROPE), f32)
    cos = jnp.concatenate([jnp.cos(ar), jnp.cos(ar), jnp.cos(ac), jnp.cos(ac), z], axis=1)
    sin = jnp.concatenate([-jnp.sin(ar), jnp.sin(ar), -jnp.sin(ac), jnp.sin(ac), z], axis=1)
    return cos, sin


def _gla_gate_weights(w2, b):
    assert SM_GLA_G % LANE == 0
    lane_pad = ((0, 0), (0, 0), (0, 0), (0, HEAD_W - GLA_DK))
    w4 = jnp.pad(w2.reshape(2, GLA_RANK, N_HEADS, GLA_DK), lane_pad).reshape(2, GLA_RANK, BRANCH_W)
    wbig = jnp.stack([jnp.pad(w4[d], ((d * GLA_RANK, LANE - (d + 1) * GLA_RANK), (0, 0))) for d in range(2)])
    bbig = jnp.pad(b.reshape(2, 1, N_HEADS, GLA_DK), lane_pad).reshape(2, 1, BRANCH_W)
    return wbig.astype(bf16), bbig


def kernel(x_prompt, x_sample, c, state_gdn, state_gla, state_hgrn, cache_mla_ckv, cache_mla_kpe, c_ctx, w_ada, b_ada, w_in, gdn_conv, gdn_a_log, gdn_dt_bias, gdn_norm, gla_gate_w2, gla_gate_b, gla_norm, hgrn_lb, hgrn_norm, mla_q_norm, mla_wq_b, mla_kv_norm, mla_wkv_b, w_branch, b_gates, w_out, ln1_g, ln1_b, ln2_g, ln2_b, ffn_w1, ffn_w3, ffn_w2):
    nb_c, l_c, d = x_prompt.shape
    nb_l, l_l, _ = x_sample.shape
    depth = w_in.shape[0]
    t_c, t_l = nb_c * l_c, nb_l * l_l
    past = cache_mla_ckv.shape[2]
    alpha = (2.0 * depth) ** 0.25
    assert d == D_MODEL and t_c % 1024 == 0 and l_l % 1024 == 0 and l_c % CHUNK == 0

    n_cond = 1 + nb_l
    cc = jnp.concatenate([c_ctx[None, :], c, jnp.zeros((-n_cond % 8, d), f32)], axis=0)
    mods = _ada(cc, w_ada, b_ada).reshape(depth, cc.shape[0], 6, d)

    w_in_r = _relayout_w_in(w_in)
    mexp_np, lvl_np = _scan_consts()
    mexp = jnp.asarray(mexp_np, bf16)
    lvl = jnp.asarray(lvl_np, f32)
    cos_t, sin_t = _rope_tables(l_l)
    cum = jnp.cumsum(jax.nn.softmax(hgrn_lb.astype(f32), axis=0), axis=0)
    lower_bounds = cum - cum[:1]

    x = jnp.concatenate([x_prompt.reshape(t_c, d), x_sample.reshape(t_l, d)], axis=0)
    t = t_c + t_l
    zero_state = jnp.zeros((nb_c, N_HEADS, 2, HEAD_W, HEAD_W), f32)
    streams = (dict(nseq=nb_c, seq=l_c, row0=0), dict(nseq=nb_l, seq=l_l, row0=t_c))
    new_gdn, new_gla, new_hg, new_ckv, new_kpe = [], [], [], [], []

    for l in range(depth):
        mod = mods[l]
        p = _inproj(x, mod, w_in_r, l, t_c, l_l)

        gsm = p[:, OFF_SMALL + SM_GDN_B:OFF_SMALL + SM_GDN_B + 16].reshape(t // GROUP, GROUP, 4, N_HEADS)
        grow = jnp.pad(gsm.transpose(3, 0, 2, 1), ((0, 0), (0, 0), (0, 4), (0, 0)))
        zc = jnp.zeros((N_HEADS, 2), f32)
        hp = jnp.stack([jnp.concatenate([zc, gdn_a_log[l].T, zc, zc], axis=1),
                        jnp.concatenate([zc, gdn_dt_bias[l].T, zc, zc], axis=1)], axis=-1)

        gla_gate = _gla_gate_weights(gla_gate_w2[l], gla_gate_b[l])
        wqa, wqb = _relayout_wq(mla_wq_b[l])
        wkv = _relayout_wkv(mla_wkv_b[l])
        qnw, kvnw = mla_q_norm[l][None, :], mla_kv_norm[l][None, :]

        outs = {k: [] for k in ('gdn', 'gla', 'hg', 'mla')}
        for si, st in enumerate(streams):
            nseq, seq, row0 = st['nseq'], st['seq'], st['row0']
            rb = row0 // seq
            if si == 0:
                s_gdn0 = s_gla0 = s_hg0 = zero_state
            else:
                s_gdn0 = state_gdn[:, l].transpose(0, 2, 1, 3, 4)
                s_gla0 = jnp.pad(state_gla[:, l], ((0, 0),) * 3 + ((0, HEAD_W - GLA_DK), (0, 0))).transpose(0, 2, 1, 4, 3)
                s_hg0 = state_hgrn[:, l].transpose(0, 2, 1, 4, 3)
            o_gdn, s_gdn = _gdn(p, gdn_conv[l], grow, hp, gdn_norm[l][None, :], s_gdn0, nseq=nseq, seq=seq, row_blk0=rb)
            o_gla, s_gla = _scan(p, (OFF_GLA_Q, OFF_GLA_V, OFF_GLA_R, OFF_GLA_K, OFF_SMALL + SM_GLA_G), gla_gate,
                                 gla_norm[l][None, :], s_gla0, mexp, lvl,
                                 nseq=nseq, seq=seq, row_blk0=rb, hgrn=False, q_scale=GLA_DK ** -0.5)
            o_hg, s_hg = _scan(p, (OFF_HG_Q, OFF_HG_I, OFF_HG_G, OFF_HG_FF, OFF_HG_FB), (lower_bounds[l],),
                               hgrn_norm[l][None, :], s_hg0, mexp, lvl,
                               nseq=nseq, seq=seq, row_blk0=rb, hgrn=True, q_scale=1.0)
            rope = si == 1
            qn, qp, ckv, kn, vv, kp = _mla_proj(p, qnw, kvnw, wqa, wkv, wqb if rope else None,
                                                cos_t if rope else None, sin_t if rope else None,
                                                nrows=nseq * seq, seq=seq, row0=row0)
            ctx = None
            if si == 1:
                knc, vvc = _kv_proj(cache_mla_ckv[:, l].reshape(nb_l * past, -1), wkv)
                kpc = jnp.pad(cache_mla_kpe[:, l].reshape(nb_l * past, MLA_ROPE), ((0, 0), (0, LANE - MLA_ROPE))).astype(bf16)
                ctx = (knc, kpc, vvc)
            o_mla = _attention(qn, qp, kn, kp, vv, ctx, nseq=nseq, seq=seq)
            outs['gdn'].append(o_gdn)
            outs['gla'].append(o_gla)
            outs['hg'].append(o_hg)
            outs['mla'].append(o_mla)
            if si == 0:
                new_gdn.append(s_gdn.transpose(0, 2, 1, 3, 4))
                new_gla.append(s_gla.transpose(0, 2, 1, 4, 3)[:, :, :, :GLA_DK, :])
                new_hg.append(s_hg.transpose(0, 2, 1, 4, 3))
                new_ckv.append(ckv.reshape(nb_c, l_c, -1))
                new_kpe.append(p[:t_c, OFF_SMALL + SM_KPE:OFF_SMALL + SM_KPE + MLA_ROPE].reshape(nb_c, l_c, MLA_ROPE))

        branches = [jnp.concatenate(outs[k], axis=0) for k in ('gdn', 'gla', 'hg', 'mla')]
        wb = _cast_layer(w_branch.reshape(depth, 4 * BRANCH_W, d), l).reshape(4, BRANCH_W, d)
        x1 = _merge(branches, p, b_gates[l], wb, _cast_layer(w_out, l), x, mod,
                    ln1_g[l][None, :], ln1_b[l][None, :], alpha=alpha, t_ctx=t_c, l_lat=l_l)
        x = _ffn(x1, mod, _cast_layer(ffn_w1, l), _cast_layer(ffn_w3, l), _cast_layer(ffn_w2, l),
                 ln2_g[l][None, :], ln2_b[l][None, :], alpha=alpha, t_ctx=t_c, l_lat=l_l, split=l == depth - 1)

    y_ctx, y_lat = x
    sdt = x_prompt.dtype
    return (y_ctx.reshape(nb_c, l_c, d), y_lat.reshape(nb_l, l_l, d),
            jnp.stack(new_gdn, axis=1).astype(sdt), jnp.stack(new_gla, axis=1).astype(sdt),
            jnp.stack(new_hg, axis=1).astype(sdt), jnp.stack(new_ckv, axis=1), jnp.stack(new_kpe, axis=1))
```

```python
import functools

import numpy as np
import jax
import jax.numpy as jnp
from jax import lax
from jax.experimental import pallas as pl
from jax.experimental.pallas import tpu as pltpu

f32 = jnp.float32
bf16 = jnp.bfloat16

D_MODEL = 2048
N_HEADS = 4
HEAD_W = 128
BRANCH_W = N_HEADS * HEAD_W
GLA_DK = 64
GLA_RANK = 16
GLA_TAU = 16.0
GDN_CONV = 5
MLA_NOPE = 128
MLA_ROPE = 64
ROPE_BASE = 10000.0
GRID_W = 64
NORM_EPS = 1e-6
CHUNK = 64
SUB = 16
LANE = 128
VMEM_LIMIT = 56 * 1024 * 1024

_SEG = 512
OFF_GDN_Q, OFF_GDN_K, OFF_GDN_V, OFF_GDN_Z = 0, 512, 1024, 1536
OFF_GLA_Q, OFF_GLA_K, OFF_GLA_V, OFF_GLA_R = 2048, 2560, 3072, 3584
OFF_HG_Q, OFF_HG_FF, OFF_HG_FB, OFF_HG_I, OFF_HG_G = 4096, 4608, 5120, 5632, 6144
OFF_MLA_QA, OFF_MLA_KVA, OFF_SMALL, OFF_GATES = 6656, 7168, 7680, 8192
N_PROJ = OFF_GATES + N_HEADS * D_MODEL
SM_KPE, SM_KPE_SW, SM_GLA_G, SM_GDN_B, SM_GDN_A = 0, 64, 128, 160, 168

_SRC = {}
_o = 0
for _n, _w in (('gdn_q', 512), ('gdn_k', 512), ('gdn_v', 512), ('gdn_z', 512), ('gdn_b', 8), ('gdn_a', 8),
               ('gla_q', 256), ('gla_k', 256), ('gla_v', 512), ('gla_r', 512), ('gla_g', 32),
               ('hg_q', 512), ('hg_f', 1024), ('hg_i', 512), ('hg_g', 512),
               ('mla_qa', 512), ('mla_kva', 512), ('mla_kpe', 64), ('gates', 4 * D_MODEL)):
    _SRC[_n] = _o
    _o += _w
IN_WIDTH = _o


def _cparams(sem):
    return pltpu.CompilerParams(dimension_semantics=sem, vmem_limit_bytes=VMEM_LIMIT)


def _dot(a, b):
    return jnp.dot(a, b, preferred_element_type=f32)


def _dot_nt(a, b):
    return lax.dot_general(a, b, (((1,), (1,)), ((), ())), preferred_element_type=f32)


def _dot_tn(a, b):
    return lax.dot_general(a, b, (((0,), (0,)), ((), ())), preferred_element_type=f32)


def _split2(x):
    hi = x.astype(bf16)
    lo = (x - hi.astype(f32)).astype(bf16)
    return hi, lo


def _mm3(a, b):
    ah, al = _split2(a)
    bh, bl = _split2(b)
    return _dot(ah, bh) + (_dot(ah, bl) + _dot(al, bh))


def _layernorm(y, g, b):
    mu = jnp.mean(y, axis=-1, keepdims=True)
    yc = y - mu
    var = jnp.mean(yc * yc, axis=-1, keepdims=True)
    return yc * lax.rsqrt(var + NORM_EPS) * g + b


def _rmsnorm(y, w):
    return y * lax.rsqrt(jnp.mean(y * y, axis=-1, keepdims=True) + NORM_EPS) * w


def _ada_kernel(c_ref, w_ref, b_ref, o_ref):
    cs = jax.nn.silu(c_ref[...]).astype(bf16)
    o_ref[...] = _dot(cs, w_ref[...].astype(bf16)) + b_ref[...]


def _ada(cc, w_ada, b_ada):
    depth, d, n6 = w_ada.shape
    rows = cc.shape[0]
    tn = 1024
    return pl.pallas_call(
        _ada_kernel,
        out_shape=jax.ShapeDtypeStruct((depth, rows, n6), f32),
        grid=(depth, n6 // tn),
        in_specs=[pl.BlockSpec((rows, d), lambda l, j: (0, 0)),
                  pl.BlockSpec((None, d, tn), lambda l, j: (l, 0, j)),
                  pl.BlockSpec((None, 1, tn), lambda l, j: (l, 0, j))],
        out_specs=pl.BlockSpec((None, rows, tn), lambda l, j: (l, 0, j)),
        compiler_params=_cparams(("parallel", "parallel")),
        name="ada_mod",
    )(cc, w_ada, b_ada.reshape(depth, 1, n6))


def _inproj_kernel(x_ref, mod_ref, w_ref, o_ref, xb_ref):
    @pl.when(pl.program_id(1) == 0)
    def _():
        shift = mod_ref[0:1, :]
        scale = mod_ref[1:2, :]
        xb_ref[...] = (x_ref[...] * (1.0 + scale) + shift).astype(bf16)

    o_ref[...] = _dot_nt(xb_ref[...], w_ref[...])


def _mod_index(tm, t_ctx, l_lat):
    def index(i):
        r = i * tm
        return jnp.where(r < t_ctx, 0, 1 + (r - t_ctx) // l_lat)
    return index


def _inproj(x, mod, w, layer, t_ctx, l_lat):
    t, d = x.shape
    n = w.shape[1]
    tm, tn = 1024, 2048
    midx = _mod_index(tm, t_ctx, l_lat)
    return pl.pallas_call(
        _inproj_kernel,
        out_shape=jax.ShapeDtypeStruct((t, n), f32),
        grid=(t // tm, n // tn),
        in_specs=[pl.BlockSpec((tm, d), lambda i, j: (i, 0)),
                  pl.BlockSpec((None, 6, d), lambda i, j: (midx(i), 0, 0)),
                  pl.BlockSpec((None, tn, d), lambda i, j: (layer, j, 0))],
        out_specs=pl.BlockSpec((tm, tn), lambda i, j: (i, j)),
        scratch_shapes=[pltpu.VMEM((tm, d), bf16)],
        compiler_params=_cparams(("parallel", "arbitrary")),
        name="in_proj",
    )(x, mod, w)


_PACK = 4
GROUP = _PACK * CHUNK
_GDN_HEADS = 4


def _gdn_kernel(q_ref, k_ref, v_ref, z_ref, cwq_ref, cwk_ref, cwv_ref, grow_ref, hp_ref, nw_ref, s0_ref,
                o_ref, so_ref, pad_ref, qs, ks, vs, gs, ub_s, wq_s, aq_s, kd_s, ct_s, of, ob, s_ref, *, seq):
    n = seq // CHUNK
    width = _GDN_HEADS * HEAD_W
    chains = [(hd, d) for hd in range(_GDN_HEADS) for d in range(2)]

    def conv_silu(x_ref, w_ref):
        pad_ref[0:8, :] = jnp.zeros((8, width), f32)
        pad_ref[8 + seq:16 + seq, :] = jnp.zeros((8, width), f32)
        pad_ref[8:8 + seq, :] = x_ref[...]
        acc = pad_ref[pl.ds(6, seq), :] * w_ref[0:1, :]
        for j in range(1, GDN_CONV):
            acc = acc + pad_ref[pl.ds(6 + j, seq), :] * w_ref[j:j + 1, :]
        return jax.nn.silu(acc)

    def l2norm_heads(x):
        parts = []
        for hd in range(_GDN_HEADS):
            xh = x[:, hd * HEAD_W:(hd + 1) * HEAD_W]
            parts.append(xh * lax.rsqrt(jnp.sum(xh * xh, axis=-1, keepdims=True) + NORM_EPS))
        return jnp.concatenate(parts, axis=1)

    qs[...] = l2norm_heads(conv_silu(q_ref, cwq_ref)) * (HEAD_W ** -0.5)
    ks[...] = l2norm_heads(conv_silu(k_ref, cwk_ref))
    vs[...] = conv_silu(v_ref, cwv_ref)

    x = grow_ref[...]
    a_log = hp_ref[:, :, 0:1][:, None]
    dt_b = hp_ref[:, :, 1:2][:, None]
    rows = lax.broadcasted_iota(jnp.int32, x.shape, 2)
    gs[...] = jnp.where(rows < 2, jax.nn.sigmoid(x), -jnp.exp(a_log) * jax.nn.softplus(x + dt_b))
    s_ref[...] = s0_ref[...].reshape(2 * _GDN_HEADS, HEAD_W, HEAD_W)

    ri = lax.broadcasted_iota(jnp.int32, (CHUNK, GROUP), 0)
    li = lax.broadcasted_iota(jnp.int32, (CHUNK, GROUP), 1)
    cj = li % CHUNK
    blk = [(li // CHUNK) == r for r in range(_PACK)]
    blk_bf = [jnp.where(b, 1.0, 0.0).astype(bf16) for b in blk]
    l1 = lax.broadcasted_iota(jnp.int32, (1, GROUP), 1) // CHUNK
    eye_b = ri == cj
    eye = jnp.where(eye_b, 1.0, 0.0).astype(f32)
    bd16 = (ri // SUB) == (cj // SUB)
    bd32 = (ri // (2 * SUB)) == (cj // (2 * SUB))
    r2 = lax.broadcasted_iota(jnp.int32, (GROUP, GROUP), 0)
    c2 = lax.broadcasted_iota(jnp.int32, (GROUP, GROUP), 1)
    same_blk = (r2 // CHUNK) == (c2 // CHUNK)

    def bdiag(yb):
        return jnp.concatenate([yb * mk for mk in blk_bf], axis=0)

    def unpack_diag(xf):
        out = xf[0:CHUNK]
        for r in range(1, _PACK):
            out = jnp.where(blk[r], xf[r * CHUNK:(r + 1) * CHUNK], out)
        return out

    def pmm3(pairs):
        sp = [(_split2(a), _split2(b)) for a, b in pairs]
        r1 = [_dot(jnp.concatenate([ah, al], axis=0), bdiag(bh)) for (ah, al), (bh, _) in sp]
        r2 = [_dot(ah, bdiag(bl)) for (ah, _), (_, bl) in sp]
        return [x1[:CHUNK] + x1[CHUNK:] + x2 for x1, x2 in zip(r1, r2)]

    def unit_tri_inverse(ms):
        k = len(ms)
        m32 = [jnp.where(bd32, m, 0.0) for m in ms]
        dg = [jnp.where(bd16, m, 0.0) for m in ms]
        d2 = pmm3([(a, a) for a in dg])
        both = pmm3([(a, a) for a in d2] + [(eye - a, eye + b) for a, b in zip(dg, d2)])
        d4, xi = both[:k], both[k:]
        both = pmm3([(a, a) for a in d4] + [(a, eye + b) for a, b in zip(xi, d4)])
        d8, xi = both[:k], both[k:]
        xi = pmm3([(a, eye + b) for a, b in zip(xi, d8)])
        for lo, hi in ((dg, m32), (m32, ms)):
            t = pmm3([(a, h_ - l_) for a, h_, l_ in zip(xi, hi, lo)])
            t = pmm3(list(zip(t, xi)))
            xi = [a - b for a, b in zip(xi, t)]
        return xi

    def block_cols(xp):
        return [jnp.sum(jnp.where(blk[r], xp, 0.0), axis=1, keepdims=True) for r in range(_PACK)]

    def spread(cols):
        out = jnp.broadcast_to(cols[0], (CHUNK, GROUP))
        for r in range(1, _PACK):
            out = jnp.where(blk[r], cols[r], out)
        return out

    def prepare_group(g, carry):
        rows_g = pl.ds(pl.multiple_of(g * GROUP, GROUP), GROUP)
        heads = range(_GDN_HEADS)
        k4 = [ks[rows_g, hd * HEAD_W:(hd + 1) * HEAD_W] for hd in heads]
        q4 = [qs[rows_g, hd * HEAD_W:(hd + 1) * HEAD_W] for hd in heads]
        v4 = [vs[rows_g, hd * HEAD_W:(hd + 1) * HEAD_W] for hd in heads]
        k4b = [x.astype(bf16) for x in k4]
        kq = [_dot_nt(jnp.concatenate([k4b[hd], q4[hd].astype(bf16)], axis=0), k4b[hd]) for hd in heads]
        kk_p = [unpack_diag(x[:GROUP]) for x in kq]
        qk_p = [unpack_diag(x[GROUP:]) for x in kq]
        g8 = [gs[hd, g] for hd in heads]
        ids = range(len(chains))
        beta_r = [g8[hd][d:d + 1, :] for hd, d in chains]
        g_r = [g8[hd][2 + d:3 + d, :] for hd, d in chains]
        tri2 = [cj <= ri, cj >= ri]
        tri = [tri2[d] for _, d in chains]
        strict = [jnp.logical_and(t, jnp.logical_not(eye_b)) for t in tri]
        tg = [jnp.where(tri[i], g_r[i], 0.0) for i in ids]
        gam_cols = [block_cols(tg[i]) for i in ids]
        beta_cols = [block_cols(jnp.where(eye_b, beta_r[i], 0.0)) for i in ids]
        tot_cols = [[jnp.sum(jnp.where(l1 == r, g_r[i], 0.0), axis=1, keepdims=True) for r in range(_PACK)]
                    for i in ids]
        strict2 = [(c2 % CHUNK) < (r2 % CHUNK), (c2 % CHUNK) > (r2 % CHUNK)]
        sm = [jnp.where(jnp.logical_and(same_blk, s_), 1.0, 0.0).astype(bf16) for s_ in strict2]
        r3 = []
        for i, (_, d) in enumerate(chains):
            th, tl = _split2(tg[i])
            tl2 = (tg[i] - th.astype(f32) - tl.astype(f32)).astype(bf16)
            r3.append(_dot(jnp.concatenate([th, tl, tl2], axis=0), sm[d]))
        dlt = [x[:CHUNK] + x[CHUNK:2 * CHUNK] + x[2 * CHUNK:] for x in r3]
        decay = [jnp.where(tri[i], jnp.exp(jnp.minimum(dlt[i], 0.0)), 0.0) for i in ids]
        t_inv = unit_tri_inverse([jnp.where(strict[i], kk_p[hd] * spread(beta_cols[i]) * decay[i], 0.0)
                                  for i, (hd, _) in enumerate(chains)])
        gam = [jnp.concatenate(gam_cols[i], axis=0) for i in ids]
        beta = [jnp.concatenate(beta_cols[i], axis=0) for i in ids]
        tot = [jnp.concatenate([jnp.broadcast_to(t, (CHUNK, 1)) for t in tot_cols[i]], axis=0) for i in ids]
        eg = [jnp.exp(x) for x in gam]
        rhs = [jnp.concatenate([v4[hd] * beta[i], k4[hd] * beta[i] * eg[i]], axis=1)
               for i, (hd, _) in enumerate(chains)]
        isp = [_split2(x) for x in t_inv]
        rsp = [_split2(x) for x in rhs]
        bih = [bdiag(isp[i][0]) for i in ids]
        s1 = [_dot(jnp.concatenate([bih[i], bdiag(isp[i][1])], axis=0), rsp[i][0]) for i in ids]
        s2 = [_dot(bih[i], rsp[i][1]) for i in ids]
        for i, (hd, d) in enumerate(chains):
            sol = s1[i][:GROUP] + s1[i][GROUP:] + s2[i]
            ub_s[i, rows_g, :] = sol[:, :HEAD_W]
            w_b = sol[:, HEAD_W:].astype(bf16)
            qd_b = (q4[hd] * eg[i]).astype(bf16)
            a_p = qk_p[hd] * decay[i]
            for r_ in range(_PACK):
                c = g * _PACK + r_
                wq_s[i, c, 0:CHUNK, :] = w_b[r_ * CHUNK:(r_ + 1) * CHUNK]
                wq_s[i, c, CHUNK:2 * CHUNK, :] = qd_b[r_ * CHUNK:(r_ + 1) * CHUNK]
                aq_s[i, c] = a_p[:, r_ * CHUNK:(r_ + 1) * CHUNK].astype(bf16)
                ct_s[i, c] = jnp.broadcast_to(jnp.exp(tot_cols[i][r_]), (8, HEAD_W))
            kd_s[i, rows_g, :] = (k4[hd] * jnp.exp(tot[i] - gam[i])).astype(bf16)
        return carry

    lax.fori_loop(0, n // _PACK, prepare_group, 0)

    def step(t, carry):
        ids = range(len(chains))
        cs = [t if d == 0 else n - 1 - t for _, d in chains]
        rows = [pl.ds(pl.multiple_of(c * CHUNK, CHUNK), CHUNK) for c in cs]
        s = [s_ref[i] for i in ids]
        sb = [x.astype(bf16) for x in s]
        r = [_dot(wq_s[i, cs[i]], sb[i]) for i in ids]
        u = [(ub_s[i, rows[i], :] - r[i][:CHUNK]).astype(bf16) for i in ids]
        o_c = [r[i][CHUNK:] + _dot(aq_s[i, cs[i]], u[i]) for i in ids]
        s_new = [ct_s[i, cs[i]][0:1, :] * s[i] + _dot_tn(kd_s[i, rows[i], :], u[i]) for i in ids]
        for i, (hd, d) in enumerate(chains):
            (of if d == 0 else ob)[rows[i], hd * HEAD_W:(hd + 1) * HEAD_W] = o_c[i]
            s_ref[i] = s_new[i]
        return carry

    lax.fori_loop(0, n, step, 0)
    o = of[...] + ob[...]
    z = z_ref[...]
    nw = nw_ref[...]
    o_ref[...] = jnp.concatenate(
        [_rmsnorm(o[:, hd * HEAD_W:(hd + 1) * HEAD_W], nw) * jax.nn.silu(z[:, hd * HEAD_W:(hd + 1) * HEAD_W])
         for hd in range(_GDN_HEADS)], axis=1).astype(bf16)
    so_ref[...] = s_ref[...].reshape(_GDN_HEADS, 2, HEAD_W, HEAD_W)


def _gdn(p, conv_w, grow, hp, norm_w, s0, *, nseq, seq, row_blk0):
    n = seq // CHUNK
    ng = seq // GROUP
    hh = _GDN_HEADS
    width = hh * HEAD_W
    nch = 2 * hh

    def pcol(off):
        return pl.BlockSpec((seq, width), lambda b, h: (row_blk0 + b, off // width + h))

    def wcol(off):
        return pl.BlockSpec((GDN_CONV, width), lambda b, h: (0, off // width + h))

    kern = functools.partial(_gdn_kernel, seq=seq)
    return pl.pallas_call(
        kern,
        out_shape=(jax.ShapeDtypeStruct((nseq * seq, BRANCH_W), bf16),
                   jax.ShapeDtypeStruct((nseq, N_HEADS, 2, HEAD_W, HEAD_W), f32)),
        grid=(nseq, N_HEADS // hh),
        in_specs=[pcol(OFF_GDN_Q), pcol(OFF_GDN_K), pcol(OFF_GDN_V), pcol(OFF_GDN_Z),
                  wcol(0), wcol(BRANCH_W), wcol(2 * BRANCH_W),
                  pl.BlockSpec((hh, ng, 8, GROUP), lambda b, h: (h, row_blk0 + b, 0, 0)),
                  pl.BlockSpec((hh, 8, 2), lambda b, h: (h, 0, 0)),
                  pl.BlockSpec((1, HEAD_W), lambda b, h: (0, 0)),
                  pl.BlockSpec((None, hh, 2, HEAD_W, HEAD_W), lambda b, h: (b, h, 0, 0, 0))],
        out_specs=(pl.BlockSpec((seq, width), lambda b, h: (b, h)),
                   pl.BlockSpec((None, hh, 2, HEAD_W, HEAD_W), lambda b, h: (b, h, 0, 0, 0))),
        scratch_shapes=[pltpu.VMEM((seq + 16, width), f32),
                        pltpu.VMEM((seq, width), f32), pltpu.VMEM((seq, width), f32),
                        pltpu.VMEM((seq, width), f32), pltpu.VMEM((hh, ng, 8, GROUP), f32),
                        pltpu.VMEM((nch, seq, HEAD_W), f32), pltpu.VMEM((nch, n, 2 * CHUNK, HEAD_W), bf16),
                        pltpu.VMEM((nch, n, CHUNK, CHUNK), bf16),
                        pltpu.VMEM((nch, seq, HEAD_W), bf16), pltpu.VMEM((nch, n, 8, HEAD_W), f32),
                        pltpu.VMEM((seq, width), f32), pltpu.VMEM((seq, width), f32),
                        pltpu.VMEM((nch, HEAD_W, HEAD_W), f32)],
        compiler_params=_cparams(("parallel", "parallel")),
        name="gdn",
    )(p, p, p, p, conv_w, conv_w, conv_w, grow, hp, norm_w, s0)


_N_LEVELS = 6
_N_MM_LEVELS = 4
_ROW_EQ = _N_MM_LEVELS * CHUNK
_ROW_EK, _ROW_TOT, _ROWS_EXP = _ROW_EQ + CHUNK, _ROW_EQ + 2 * CHUNK, _ROW_EQ + 2 * CHUNK + 8
_SCAN_UNROLL = 4
_SCAN_HEADS = 2


def _scan_consts():
    c = CHUNK
    mexp = np.zeros((2, _ROWS_EXP, c), np.float32)
    lvl = np.full((2, c, c), -1.0, np.float32)
    for lv in range(_N_LEVELS):
        s = 32 >> lv
        for i in range(c):
            p = (i // (2 * s)) * (2 * s) + s
            right = (i % (2 * s)) >= s
            if lv >= _N_MM_LEVELS:
                pass
            elif right:
                mexp[0, lv * c + i, p:i + 1] = 1.0
                mexp[1, lv * c + i, p:i] = 1.0
            else:
                mexp[0, lv * c + i, i + 1:p] = 1.0
                mexp[1, lv * c + i, i:p] = 1.0
            for j in range(c):
                if (i // (2 * s)) != (j // (2 * s)):
                    continue
                jright = (j % (2 * s)) >= s
                if right and not jright:
                    lvl[0, i, j] = lv
                if (not right) and jright:
                    lvl[1, i, j] = lv
    for i in range(c):
        lvl[:, i, i] = _N_LEVELS
        mexp[0, _ROW_EQ + i, :i + 1] = 1.0
        mexp[0, _ROW_EK + i, i + 1:] = 1.0
        mexp[1, _ROW_EQ + i, i:] = 1.0
        mexp[1, _ROW_EK + i, :i] = 1.0
    mexp[:, _ROW_TOT:, :] = 1.0
    return mexp, lvl


_LOG_DECAY_FLOOR = -1.0e4


def _scan_kernel(*refs, seq, hgrn, q_scale):
    if hgrn:
        q_ref, v_ref, gate_ref, zf_ref, zb_ref, lb_ref = refs[:6]
    else:
        q_ref, v_ref, gate_ref, k_ref, glr_ref, w2_ref, b2_ref = refs[:7]
    nw_ref, s0_ref, mexp_ref, lvl_ref, o_ref, so_ref, of, ob, qd_s, g_s, ct_s, stp_s, st_ref = refs[-13:]
    q_silu = hgrn
    gate_silu = not hgrn
    n = seq // CHUNK
    st_ref[...] = s0_ref[...].reshape(2 * _SCAN_HEADS, HEAD_W, HEAD_W)
    outs = (of, ob)

    def rows_of(c):
        return pl.ds(pl.multiple_of(c * CHUNK, CHUNK), CHUNK)

    def lanes(hd):
        return slice(hd * HEAD_W, (hd + 1) * HEAD_W)

    def work(g):
        return [(hd, d, g * _SCAN_UNROLL + u) for u in range(_SCAN_UNROLL) for hd in range(_SCAN_HEADS)
                for d in range(2)]

    def prepare_group(g, carry):
        todo3 = work(g)
        todo = [(d, c) for _, d, c in todo3]
        items = range(len(todo))
        q, k, v, log_decay = [], [], [], []
        for hd, d, c in todo3:
            rows_c = rows_of(c)
            qi = q_ref[rows_c, lanes(hd)]
            q.append(jax.nn.silu(qi) if q_silu else qi * q_scale)
            v.append(v_ref[rows_c, lanes(hd)].astype(bf16))
        if hgrn:
            for hd, d, c in todo3:
                z = (zf_ref if d == 0 else zb_ref)[rows_of(c), lanes(hd)]
                lb = lb_ref[d:d + 1, lanes(hd)]
                t0 = jnp.log(lb)
                t1 = jnp.log1p(-lb) + jax.nn.log_sigmoid(z)
                mx = jnp.maximum(t0, t1)
                lse = mx + jnp.log(jnp.exp(t0 - mx) + jnp.exp(t1 - mx))
                log_decay.append(jnp.where(mx == -jnp.inf, -jnp.inf, lse))
                k.append((1.0 - lb) * jax.nn.sigmoid(-z))
        else:
            logits = [_dot(glr_ref[rows_of(c), :].astype(bf16), w2_ref[d][:, lanes(hd)]) for hd, d, c in todo3]
            log_decay = [jax.nn.log_sigmoid(logits[i] + b2_ref[d][:, lanes(hd)]) * (1.0 / GLA_TAU)
                         for i, (hd, d, _) in enumerate(todo3)]
            k = [k_ref[rows_of(c), lanes(hd)] for hd, _, c in todo3]
        la = [jnp.maximum(x, _LOG_DECAY_FLOOR) for x in log_decay]
        e2 = []
        for i, (d, _) in enumerate(todo):
            hi, lo = _split2(la[i])
            e2.append(_dot(mexp_ref[d], jnp.concatenate([hi, lo], axis=1)))
        e = [x[:, :HEAD_W] + x[:, HEAD_W:] for x in e2]
        r4 = lax.broadcasted_iota(jnp.int32, (CHUNK, HEAD_W), 0) % 4
        fine = []
        for i, (d, _) in enumerate(todo):
            prev = pltpu.roll(la[i], 1, 0)
            nxt = pltpu.roll(la[i], CHUNK - 1, 0)
            if d == 0:
                w2 = jnp.where(r4 == 0, nxt, jnp.where(r4 == 1, 0.0, jnp.where(r4 == 2, la[i], la[i] + prev)))
                w1 = jnp.where(r4 % 2 == 1, la[i], 0.0)
            else:
                w2 = jnp.where(r4 == 0, la[i] + nxt, jnp.where(r4 == 1, la[i], jnp.where(r4 == 2, 0.0, prev)))
                w1 = jnp.where(r4 % 2 == 0, la[i], 0.0)
            fine.append((w2, w1))
        lvl = [lvl_ref[d] for d, _ in todo]
        qk = [_dot_nt(q[i].astype(bf16), k[i].astype(bf16)) for i in items]
        a = [jnp.where(lvl[i] == float(_N_LEVELS), qk[i], 0.0) for i in items]
        for lv in range(_N_LEVELS):
            if lv < _N_MM_LEVELS:
                w = [jnp.exp(e[i][lv * CHUNK:(lv + 1) * CHUNK]) for i in items]
            else:
                w = [jnp.exp(fine[i][lv - _N_MM_LEVELS]) for i in items]
            p = [_dot_nt((q[i] * w[i]).astype(bf16), (k[i] * w[i]).astype(bf16)) for i in items]
            a = [jnp.where(lvl[i] == float(lv), p[i], a[i]) for i in items]
        o_intra = [_dot(a[i].astype(bf16), v[i]) for i in items]
        incr = [_dot_tn(v[i], (k[i] * jnp.exp(e[i][_ROW_EK:_ROW_EK + CHUNK])).astype(bf16)) for i in items]
        for i, (hd, d, c) in enumerate(todo3):
            rows_c = rows_of(c)
            slot = 2 * hd + d
            outs[d][rows_c, lanes(hd)] = o_intra[i]
            qd_s[slot, rows_c, :] = (q[i] * jnp.exp(e[i][_ROW_EQ:_ROW_EQ + CHUNK])).astype(bf16)
            g_s[slot, c] = incr[i]
            ct_s[slot, c] = jnp.exp(e[i][_ROW_TOT:_ROW_TOT + 8])
        return carry

    lax.fori_loop(0, n // _SCAN_UNROLL, prepare_group, 0)

    def scan_step(i, carry):
        for hd in range(_SCAN_HEADS):
            for d, c in ((0, i), (1, n - 1 - i)):
                slot = 2 * hd + d
                st = st_ref[slot]
                stp_s[slot, c] = st.astype(bf16)
                st_ref[slot] = st * ct_s[slot, c][0:1, :] + g_s[slot, c]
        return carry

    lax.fori_loop(0, n, scan_step, 0)

    def inter_group(g, carry):
        todo3 = work(g)
        res = [_dot_nt(qd_s[2 * hd + d, rows_of(c), :], stp_s[2 * hd + d, c]) + outs[d][rows_of(c), lanes(hd)]
               for hd, d, c in todo3]
        for (hd, d, c), o_c in zip(todo3, res):
            outs[d][rows_of(c), lanes(hd)] = o_c
        return carry

    lax.fori_loop(0, n // _SCAN_UNROLL, inter_group, 0)
    o = of[...] + ob[...]
    g = gate_ref[...]
    g = jax.nn.silu(g) if gate_silu else jax.nn.sigmoid(g)
    nw = nw_ref[...]
    o_ref[...] = jnp.concatenate([_rmsnorm(o[:, lanes(hd)], nw) * g[:, lanes(hd)] for hd in range(_SCAN_HEADS)],
                                 axis=1).astype(bf16)
    so_ref[...] = st_ref[...].reshape(_SCAN_HEADS, 2, HEAD_W, HEAD_W)


def _scan(p, offs, extra, norm_w, s0t, mexp, lvl, *, nseq, seq, row_blk0, hgrn, q_scale):
    hh = _SCAN_HEADS
    width = hh * HEAD_W
    slots = 2 * hh
    n = seq // CHUNK

    def pspec(off):
        return pl.BlockSpec((seq, width), lambda b, h: (row_blk0 + b, off // width + h))

    if hgrn:
        (lower_bound,) = extra
        in_specs = [pspec(o) for o in offs] + [pl.BlockSpec((2, width), lambda b, h: (0, h))]
        args = [p] * len(offs) + [lower_bound]
    else:
        w2, b2 = extra
        tile = offs[4] // HEAD_W
        in_specs = ([pspec(o) for o in offs[:4]] + [pl.BlockSpec((seq, HEAD_W), lambda b, h: (row_blk0 + b, tile))]
                    + [pl.BlockSpec((2, HEAD_W, width), lambda b, h: (0, 0, h)),
                       pl.BlockSpec((2, 1, width), lambda b, h: (0, 0, h))])
        args = [p] * len(offs) + [w2, b2]
    kern = functools.partial(_scan_kernel, seq=seq, hgrn=hgrn, q_scale=q_scale)
    return pl.pallas_call(
        kern,
        out_shape=(jax.ShapeDtypeStruct((nseq * seq, BRANCH_W), bf16),
                   jax.ShapeDtypeStruct((nseq, N_HEADS, 2, HEAD_W, HEAD_W), f32)),
        grid=(nseq, N_HEADS // hh),
        in_specs=in_specs + [
            pl.BlockSpec((1, HEAD_W), lambda b, h: (0, 0)),
            pl.BlockSpec((None, hh, 2, HEAD_W, HEAD_W), lambda b, h: (b, h, 0, 0, 0)),
            pl.BlockSpec((2, _ROWS_EXP, CHUNK), lambda b, h: (0, 0, 0)),
            pl.BlockSpec((2, CHUNK, CHUNK), lambda b, h: (0, 0, 0))],
        out_specs=(pl.BlockSpec((seq, width), lambda b, h: (b, h)),
                   pl.BlockSpec((None, hh, 2, HEAD_W, HEAD_W), lambda b, h: (b, h, 0, 0, 0))),
        scratch_shapes=[pltpu.VMEM((seq, width), f32), pltpu.VMEM((seq, width), f32),
                        pltpu.VMEM((slots, seq, HEAD_W), bf16),
                        pltpu.VMEM((slots, n, HEAD_W, HEAD_W), f32),
                        pltpu.VMEM((slots, n, 8, HEAD_W), f32),
                        pltpu.VMEM((slots, n, HEAD_W, HEAD_W), bf16),
                        pltpu.VMEM((slots, HEAD_W, HEAD_W), f32)],
        compiler_params=_cparams(("parallel", "parallel")),
        name="decay_scan",
    )(*args, norm_w, s0t, mexp, lvl)


def _mla_proj_kernel(*refs, rope):
    if rope:
        (qa_ref, kva_ref, sm_ref, qnw_ref, kvnw_ref, wqa_ref, wkv_ref, wqb_ref, cos_ref, sin_ref,
         qn_ref, qp_ref, ckv_ref, kn_ref, vv_ref, kp_ref) = refs
    else:
        (qa_ref, kva_ref, sm_ref, qnw_ref, kvnw_ref, wqa_ref, wkv_ref,
         qn_ref, qp_ref, ckv_ref, kn_ref, vv_ref, kp_ref) = refs
    qh = _rmsnorm(qa_ref[...], qnw_ref[...]).astype(bf16)
    qa = _dot(qh, wqa_ref[...])
    qn_ref[...] = qa[:, :BRANCH_W].astype(bf16)
    pe = qa[:, BRANCH_W:]
    kpe = sm_ref[:, 0:LANE]
    if rope:
        cos = cos_ref[...]
        sin = sin_ref[...]
        cos4 = jnp.concatenate([cos] * N_HEADS, axis=1)
        sin4 = jnp.concatenate([sin] * N_HEADS, axis=1)
        pe = pe * cos4 + _dot(qh, wqb_ref[...]) * sin4
        kpe = kpe * cos + pltpu.roll(kpe, MLA_ROPE, 1) * sin
    qp_ref[...] = pe.astype(bf16)
    kp_ref[...] = kpe.astype(bf16)
    ckv = _rmsnorm(kva_ref[...], kvnw_ref[...])
    ckv_ref[...] = ckv
    kv = _dot(ckv.astype(bf16), wkv_ref[...])
    kn_ref[...] = kv[:, :BRANCH_W].astype(bf16)
    vv_ref[...] = kv[:, BRANCH_W:].astype(bf16)


def _mla_proj(p, qnw, kvnw, wqa, wkv, wqb, cos, sin, *, nrows, seq, row0):
    tm = 256
    rope = cos is not None
    rb0 = row0 // tm
    per_seq = seq // tm

    def pspec(off):
        return pl.BlockSpec((tm, _SEG), lambda i: (rb0 + i, off // _SEG))

    def full(a):
        return pl.BlockSpec(a.shape, lambda i: (0,) * a.ndim)

    in_specs = [pspec(OFF_MLA_QA), pspec(OFF_MLA_KVA), pspec(OFF_SMALL), full(qnw), full(kvnw), full(wqa), full(wkv)]
    args = [p, p, p, qnw, kvnw, wqa, wkv]
    if rope:
        tspec = pl.BlockSpec((tm, LANE), lambda i: (i % per_seq, 0))
        in_specs += [full(wqb), tspec, tspec]
        args += [wqb, cos, sin]
    wide = lambda dt: jax.ShapeDtypeStruct((nrows, BRANCH_W), dt)
    ospec = pl.BlockSpec((tm, BRANCH_W), lambda i: (i, 0))
    return pl.pallas_call(
        functools.partial(_mla_proj_kernel, rope=rope),
        out_shape=(wide(bf16), wide(bf16), wide(f32), wide(bf16), wide(bf16),
                   jax.ShapeDtypeStruct((nrows, LANE), bf16)),
        grid=(nrows // tm,),
        in_specs=in_specs,
        out_specs=(ospec, ospec, ospec, ospec, ospec, pl.BlockSpec((tm, LANE), lambda i: (i, 0))),
        compiler_params=_cparams(("parallel",)),
        name="mla_proj",
    )(*args)


def _kv_kernel(ckv_ref, w_ref, kn_ref, vv_ref):
    kv = _dot(ckv_ref[...].astype(bf16), w_ref[...])
    kn_ref[...] = kv[:, :BRANCH_W].astype(bf16)
    vv_ref[...] = kv[:, BRANCH_W:].astype(bf16)


def _kv_proj(ckv, wkv):
    rows = ckv.shape[0]
    tm = 256
    out = jax.ShapeDtypeStruct((rows, BRANCH_W), bf16)
    ospec = pl.BlockSpec((tm, BRANCH_W), lambda i: (i, 0))
    return pl.pallas_call(
        _kv_kernel, out_shape=(out, out), grid=(rows // tm,),
        in_specs=[pl.BlockSpec((tm, ckv.shape[1]), lambda i: (i, 0)),
                  pl.BlockSpec(wkv.shape, lambda i: (0, 0))],
        out_specs=(ospec, ospec),
        compiler_params=_cparams(("parallel",)),
        name="mla_ctx_kv",
    )(ckv, wkv)


def _attn_kernel(*refs, has_ctx):
    if has_ctx:
        qn_ref, qp_ref, kn_ref, kp_ref, vv_ref, knc_ref, kpc_ref, vvc_ref, o_ref = refs
    else:
        qn_ref, qp_ref, kn_ref, kp_ref, vv_ref, o_ref = refs
    scale = (MLA_NOPE + MLA_ROPE) ** -0.5
    heads = range(N_HEADS)

    def head(ref, h):
        return ref[:, h * HEAD_W:(h + 1) * HEAD_W]

    kp = kp_ref[...]
    s1 = [(_dot_nt(head(qn_ref, h), head(kn_ref, h)) + _dot_nt(head(qp_ref, h), kp)) * scale for h in heads]
    mx = [jnp.max(x, axis=-1, keepdims=True) for x in s1]
    if has_ctx:
        kpc = kpc_ref[...]
        s2 = [(_dot_nt(head(qn_ref, h), head(knc_ref, h)) + _dot_nt(head(qp_ref, h), kpc)) * scale for h in heads]
        mx = [jnp.maximum(m, jnp.max(x, axis=-1, keepdims=True)) for m, x in zip(mx, s2)]
    p1 = [jnp.exp(x - m) for x, m in zip(s1, mx)]
    den = [jnp.sum(x, axis=-1, keepdims=True) for x in p1]
    if has_ctx:
        p2 = [jnp.exp(x - m) for x, m in zip(s2, mx)]
        den = [d_ + jnp.sum(x, axis=-1, keepdims=True) for d_, x in zip(den, p2)]
    inv = [1.0 / d_ for d_ in den]
    o = [_dot((p1[h] * inv[h]).astype(bf16), head(vv_ref, h)) for h in heads]
    if has_ctx:
        o = [o[h] + _dot((p2[h] * inv[h]).astype(bf16), head(vvc_ref, h)) for h in heads]
    o_ref[...] = jnp.concatenate(o, axis=1).astype(bf16)


def _attention(qn, qp, kn, kp, vv, ctx, *, nseq, seq):
    tq = 256
    nq = seq // tq
    has_ctx = ctx is not None
    qspec = pl.BlockSpec((tq, BRANCH_W), lambda b, i: (b * nq + i, 0))
    kspec = pl.BlockSpec((seq, BRANCH_W), lambda b, i: (b, 0))
    kpspec = pl.BlockSpec((seq, LANE), lambda b, i: (b, 0))
    in_specs = [qspec, qspec, kspec, kpspec, kspec]
    args = [qn, qp, kn, kp, vv]
    if has_ctx:
        knc, kpc, vvc = ctx
        lc = knc.shape[0] // nseq
        in_specs += [pl.BlockSpec((lc, BRANCH_W), lambda b, i: (b, 0)),
                     pl.BlockSpec((lc, LANE), lambda b, i: (b, 0)),
                     pl.BlockSpec((lc, BRANCH_W), lambda b, i: (b, 0))]
        args += [knc, kpc, vvc]
    return pl.pallas_call(
        functools.partial(_attn_kernel, has_ctx=has_ctx),
        out_shape=jax.ShapeDtypeStruct((nseq * seq, BRANCH_W), bf16),
        grid=(nseq, nq),
        in_specs=in_specs,
        out_specs=pl.BlockSpec((tq, BRANCH_W), lambda b, i: (b * nq + i, 0)),
        compiler_params=_cparams(("parallel", "parallel")),
        name="mla_attn",
    )(*args)


def _merge_kernel(o0_ref, o1_ref, o2_ref, o3_ref, g0_ref, g1_ref, g2_ref, g3_ref, bg_ref, wb_ref, wo_ref,
                  x_ref, mod_ref, lng_ref, lnb_ref, out_ref, *, alpha):
    m = None
    for k, (o_ref, g_ref) in enumerate(((o0_ref, g0_ref), (o1_ref, g1_ref), (o2_ref, g2_ref), (o3_ref, g3_ref))):
        term = jax.nn.sigmoid(g_ref[...] + bg_ref[k:k + 1, :]) * _dot(o_ref[...], wb_ref[k])
        m = term if m is None else m + term
    mix = _dot(m.astype(bf16), wo_ref[...])
    gate1 = mod_ref[2:3, :]
    out_ref[...] = _layernorm(alpha * x_ref[...] + gate1 * mix, lng_ref[...], lnb_ref[...])


def _merge(branches, p, bg, wb, wo, x, mod, lng, lnb, *, alpha, t_ctx, l_lat):
    t, d = x.shape
    tm = 256
    midx = _mod_index(tm, t_ctx, l_lat)
    ospec = pl.BlockSpec((tm, BRANCH_W), lambda i: (i, 0))
    gspecs = [pl.BlockSpec((tm, d), functools.partial(lambda i, k: (i, OFF_GATES // d + k), k=k)) for k in range(4)]
    single = dict(pipeline_mode=pl.Buffered(1))
    return pl.pallas_call(
        functools.partial(_merge_kernel, alpha=alpha),
        out_shape=jax.ShapeDtypeStruct((t, d), f32),
        grid=(t // tm,),
        in_specs=[ospec, ospec, ospec, ospec] + gspecs + [
            pl.BlockSpec((4, d), lambda i: (0, 0)),
            pl.BlockSpec((4, BRANCH_W, d), lambda i: (0, 0, 0), **single),
            pl.BlockSpec((d, d), lambda i: (0, 0), **single),
            pl.BlockSpec((tm, d), lambda i: (i, 0)),
            pl.BlockSpec((None, 6, d), lambda i: (midx(i), 0, 0)),
            pl.BlockSpec((1, d), lambda i: (0, 0)),
            pl.BlockSpec((1, d), lambda i: (0, 0))],
        out_specs=pl.BlockSpec((tm, d), lambda i: (i, 0)),
        compiler_params=_cparams(("parallel",)),
        name="merge_out_ln",
    )(*branches, p, p, p, p, bg, wb, wo, x, mod, lng, lnb)


def _ffn_kernel(x_ref, mod_ref, w1_ref, w3_ref, w2_ref, lng_ref, lnb_ref, *rest, alpha, ctx_tiles):
    if ctx_tiles is None:
        out_ref, hb_ref, acc_ref = rest
    else:
        out_ref, out_lat_ref, hb_ref, acc_ref = rest
    f = pl.program_id(1)

    @pl.when(f == 0)
    def _():
        shift = mod_ref[3:4, :]
        scale = mod_ref[4:5, :]
        hb_ref[...] = (x_ref[...] * (1.0 + scale) + shift).astype(bf16)
        acc_ref[...] = jnp.zeros_like(acc_ref)

    h = hb_ref[...]
    g = (jax.nn.silu(_dot(h, w1_ref[...])) * _dot(h, w3_ref[...])).astype(bf16)
    acc_ref[...] += _dot(g, w2_ref[...])

    last = f == pl.num_programs(1) - 1

    def result():
        gate2 = mod_ref[5:6, :]
        return _layernorm(alpha * x_ref[...] + gate2 * acc_ref[...], lng_ref[...], lnb_ref[...])

    if ctx_tiles is None:
        @pl.when(last)
        def _():
            out_ref[...] = result()
    else:
        is_ctx = pl.program_id(0) < ctx_tiles

        @pl.when(jnp.logical_and(last, is_ctx))
        def _():
            out_ref[...] = result()

        @pl.when(jnp.logical_and(last, jnp.logical_not(is_ctx)))
        def _():
            out_lat_ref[...] = result()


def _ffn(x, mod, w1, w3, w2, lng, lnb, *, alpha, t_ctx, l_lat, split):
    t, d = x.shape
    dff = w1.shape[1]
    tm, tf = 512, 512
    midx = _mod_index(tm, t_ctx, l_lat)
    if split:
        ctx_tiles = t_ctx // tm
        out_shape = (jax.ShapeDtypeStruct((t_ctx, d), f32), jax.ShapeDtypeStruct((t - t_ctx, d), f32))
        out_specs = (pl.BlockSpec((tm, d), lambda i, f: (jnp.minimum(i, ctx_tiles - 1), 0)),
                     pl.BlockSpec((tm, d), lambda i, f: (jnp.maximum(i - ctx_tiles, 0), 0)))
    else:
        ctx_tiles = None
        out_shape = jax.ShapeDtypeStruct((t, d), f32)
        out_specs = pl.BlockSpec((tm, d), lambda i, f: (i, 0))
    return pl.pallas_call(
        functools.partial(_ffn_kernel, alpha=alpha, ctx_tiles=ctx_tiles),
        out_shape=out_shape,
        grid=(t // tm, dff // tf),
        in_specs=[pl.BlockSpec((tm, d), lambda i, f: (i, 0)),
                  pl.BlockSpec((None, 6, d), lambda i, f: (midx(i), 0, 0)),
                  pl.BlockSpec((d, tf), lambda i, f: (0, f)),
                  pl.BlockSpec((d, tf), lambda i, f: (0, f)),
                  pl.BlockSpec((tf, d), lambda i, f: (f, 0)),
                  pl.BlockSpec((1, d), lambda i, f: (0, 0)),
                  pl.BlockSpec((1, d), lambda i, f: (0, 0))],
        out_specs=out_specs,
        scratch_shapes=[pltpu.VMEM((tm, d), bf16), pltpu.VMEM((tm, d), f32)],
        compiler_params=_cparams(("arbitrary" if split else "parallel", "arbitrary")),
        name="ffn_ln",
    )(x, mod, w1, w3, w2, lng, lnb)


def _cast_kernel(x_ref, o_ref):
    o_ref[...] = x_ref[...].astype(bf16)


def _cast_layer(w, layer):
    _, rows, cols = w.shape
    tr = 256 if cols > 4096 else 512
    return pl.pallas_call(
        _cast_kernel,
        out_shape=jax.ShapeDtypeStruct((rows, cols), bf16),
        grid=(rows // tr,),
        in_specs=[pl.BlockSpec((None, tr, cols), lambda i: (layer, i, 0))],
        out_specs=pl.BlockSpec((tr, cols), lambda i: (i, 0)),
        compiler_params=_cparams(("parallel",)),
        name="cast_bf16",
    )(w)


_ROW_UNIT = 16


def _relayout_plan():
    plain = lambda src: (0, src)
    special = lambda k: (1, k)
    segs = [plain(_SRC[n]) for n in ('gdn_q', 'gdn_k', 'gdn_v', 'gdn_z')]
    segs += [special(0), special(1)]
    segs += [plain(_SRC['gla_v']), plain(_SRC['gla_r']), plain(_SRC['hg_q']), plain(_SRC['hg_f']),
             plain(_SRC['hg_f'] + 512), plain(_SRC['hg_i']), plain(_SRC['hg_g']), plain(_SRC['mla_qa']),
             plain(_SRC['mla_kva'])]
    segs += [special(2)]
    segs += [plain(_SRC['gates'] + _SEG * i) for i in range(N_HEADS * D_MODEL // _SEG)]
    assert len(segs) == N_PROJ // _SEG
    table = np.zeros((3, len(segs)), np.int32)
    row, blk = segs[0][1], 0
    for j, (is_special, val) in enumerate(segs):
        if is_special:
            blk = val
        else:
            row = val
        assert row % _ROW_UNIT == 0
        table[:, j] = (is_special, row // _ROW_UNIT, blk)
    return table


def _special_rows(wt, layer):
    d = wt.shape[2]

    def rows(name, start, n):
        s = _SRC[name] + start
        return lax.slice(wt, (layer, s, 0), (layer + 1, s + n, d))[0]

    def head_padded(name):
        parts = []
        for h in range(N_HEADS):
            parts += [rows(name, h * GLA_DK, GLA_DK), jnp.zeros((HEAD_W - GLA_DK, d), wt.dtype)]
        return parts

    q4 = MLA_ROPE // 4
    small = [rows('mla_kpe', 0, MLA_ROPE), rows('mla_kpe', q4, q4), rows('mla_kpe', 0, q4),
             rows('mla_kpe', 3 * q4, q4), rows('mla_kpe', 2 * q4, q4), rows('gla_g', 0, 2 * GLA_RANK),
             rows('gdn_b', 0, 8), rows('gdn_a', 0, 8), jnp.zeros((_SEG - SM_GDN_A - 8, d), wt.dtype)]
    return jnp.concatenate(head_padded('gla_q') + head_padded('gla_k') + small, axis=0)


def _relayout_kernel(tab_ref, wt_ref, sp_ref, o_ref):
    is_special = tab_ref[0, pl.program_id(1)]

    @pl.when(is_special == 0)
    def _():
        o_ref[...] = wt_ref[0].astype(bf16)

    @pl.when(is_special == 1)
    def _():
        o_ref[...] = sp_ref[...].astype(bf16)


def _relayout_w_in(w_in):
    depth, d, _ = w_in.shape
    wt = jnp.swapaxes(w_in, 1, 2)
    special = jnp.stack([_special_rows(wt, l) for l in range(depth)])
    table = _relayout_plan()
    nseg = N_PROJ // _SEG
    return pl.pallas_call(
        _relayout_kernel,
        out_shape=jax.ShapeDtypeStruct((depth, N_PROJ, d), bf16),
        grid_spec=pltpu.PrefetchScalarGridSpec(
            num_scalar_prefetch=1,
            grid=(depth, nseg),
            in_specs=[pl.BlockSpec((pl.Element(1), pl.Element(_SEG), pl.Element(d)),
                                   lambda l, j, tab: (l, tab[1, j] * _ROW_UNIT, 0)),
                      pl.BlockSpec((None, _SEG, d), lambda l, j, tab: (l, tab[2, j], 0))],
            out_specs=pl.BlockSpec((None, _SEG, d), lambda l, j, tab: (l, j, 0))),
        compiler_params=_cparams(("parallel", "parallel")),
        name="w_in_relayout",
    )(jnp.asarray(table), wt, special)


def _relayout_wq(wq):
    hw = MLA_NOPE + MLA_ROPE
    z = jnp.zeros((wq.shape[0], HEAD_W - MLA_ROPE), wq.dtype)
    q4 = MLA_ROPE // 4
    nope, pe, pes = [], [], []
    for h in range(N_HEADS):
        base = h * hw
        nope.append(wq[:, base:base + MLA_NOPE])
        r = wq[:, base + MLA_NOPE:base + hw]
        pe += [r, z]
        pes += [r[:, q4:2 * q4], r[:, 0:q4], r[:, 3 * q4:], r[:, 2 * q4:3 * q4], z]
    return jnp.concatenate(nope + pe, axis=1).astype(bf16), jnp.concatenate(pes, axis=1).astype(bf16)


def _relayout_wkv(wkv):
    hw = MLA_NOPE + HEAD_W
    kn = [wkv[:, h * hw:h * hw + MLA_NOPE] for h in range(N_HEADS)]
    vv = [wkv[:, h * hw + MLA_NOPE:(h + 1) * hw] for h in range(N_HEADS)]
    return jnp.concatenate(kn + vv, axis=1).astype(bf16)


def _rope_tables(length):
    pos = jnp.arange(length)
    row_id = (pos // GRID_W).astype(f32)
    col_id = (pos % GRID_W).astype(f32)
    half = MLA_ROPE // 2
    inv = ROPE_BASE ** (-jnp.arange(0, half, 2, dtype=f32) / half)
    ar, ac = row_id[:, None] * inv, col_id[:, None] * inv
    z = jnp.zeros((length, LANE - MLA_ROPE), f32)
    cos = jnp.concatenate([jnp.cos(ar), jnp.cos(ar), jnp.cos(ac), jnp.cos(ac), z], axis=1)
    sin = jnp.concatenate([-jnp.sin(ar), jnp.sin(ar), -jnp.sin(ac), jnp.sin(ac), z], axis=1)
    return cos, sin


def _gla_gate_weights(w2, b):
    assert SM_GLA_G % LANE == 0
    lane_pad = ((0, 0), (0, 0), (0, 0), (0, HEAD_W - GLA_DK))
    w4 = jnp.pad(w2.reshape(2, GLA_RANK, N_HEADS, GLA_DK), lane_pad).reshape(2, GLA_RANK, BRANCH_W)
    wbig = jnp.stack([jnp.pad(w4[d], ((d * GLA_RANK, LANE - (d + 1) * GLA_RANK), (0, 0))) for d in range(2)])
    bbig = jnp.pad(b.reshape(2, 1, N_HEADS, GLA_DK), lane_pad).reshape(2, 1, BRANCH_W)
    return wbig.astype(bf16), bbig


def kernel(x_prompt, x_sample, c, state_gdn, state_gla, state_hgrn, cache_mla_ckv, cache_mla_kpe, c_ctx, w_ada, b_ada, w_in, gdn_conv, gdn_a_log, gdn_dt_bias, gdn_norm, gla_gate_w2, gla_gate_b, gla_norm, hgrn_lb, hgrn_norm, mla_q_norm, mla_wq_b, mla_kv_norm, mla_wkv_b, w_branch, b_gates, w_out, ln1_g, ln1_b, ln2_g, ln2_b, ffn_w1, ffn_w3, ffn_w2):
    nb_c, l_c, d = x_prompt.shape
    nb_l, l_l, _ = x_sample.shape
    depth = w_in.shape[0]
    t_c, t_l = nb_c * l_c, nb_l * l_l
    past = cache_mla_ckv.shape[2]
    alpha = (2.0 * depth) ** 0.25
    assert d == D_MODEL and t_c % 1024 == 0 and l_l % 1024 == 0 and l_c % CHUNK == 0

    n_cond = 1 + nb_l
    cc = jnp.concatenate([c_ctx[None, :], c, jnp.zeros((-n_cond % 8, d), f32)], axis=0)
    mods = _ada(cc, w_ada, b_ada).reshape(depth, cc.shape[0], 6, d)

    w_in_r = _relayout_w_in(w_in)
    mexp_np, lvl_np = _scan_consts()
    mexp = jnp.asarray(mexp_np, bf16)
    lvl = jnp.asarray(lvl_np, f32)
    cos_t, sin_t = _rope_tables(l_l)
    cum = jnp.cumsum(jax.nn.softmax(hgrn_lb.astype(f32), axis=0), axis=0)
    lower_bounds = cum - cum[:1]

    x = jnp.concatenate([x_prompt.reshape(t_c, d), x_sample.reshape(t_l, d)], axis=0)
    t = t_c + t_l
    zero_state = jnp.zeros((nb_c, N_HEADS, 2, HEAD_W, HEAD_W), f32)
    streams = (dict(nseq=nb_c, seq=l_c, row0=0), dict(nseq=nb_l, seq=l_l, row0=t_c))
    new_gdn, new_gla, new_hg, new_ckv, new_kpe = [], [], [], [], []

    for l in range(depth):
        mod = mods[l]
        p = _inproj(x, mod, w_in_r, l, t_c, l_l)

        gsm = p[:, OFF_SMALL + SM_GDN_B:OFF_SMALL + SM_GDN_B + 16].reshape(t // GROUP, GROUP, 4, N_HEADS)
        grow = jnp.pad(gsm.transpose(3, 0, 2, 1), ((0, 0), (0, 0), (0, 4), (0, 0)))
        zc = jnp.zeros((N_HEADS, 2), f32)
        hp = jnp.stack([jnp.concatenate([zc, gdn_a_log[l].T, zc, zc], axis=1),
                        jnp.concatenate([zc, gdn_dt_bias[l].T, zc, zc], axis=1)], axis=-1)

        gla_gate = _gla_gate_weights(gla_gate_w2[l], gla_gate_b[l])
        wqa, wqb = _relayout_wq(mla_wq_b[l])
        wkv = _relayout_wkv(mla_wkv_b[l])
        qnw, kvnw = mla_q_norm[l][None, :], mla_kv_norm[l][None, :]

        outs = {k: [] for k in ('gdn', 'gla', 'hg', 'mla')}
        for si, st in enumerate(streams):
            nseq, seq, row0 = st['nseq'], st['seq'], st['row0']
            rb = row0 // seq
            if si == 0:
                s_gdn0 = s_gla0 = s_hg0 = zero_state
            else:
                s_gdn0 = state_gdn[:, l].transpose(0, 2, 1, 3, 4)
                s_gla0 = jnp.pad(state_gla[:, l], ((0, 0),) * 3 + ((0, HEAD_W - GLA_DK), (0, 0))).transpose(0, 2, 1, 4, 3)
                s_hg0 = state_hgrn[:, l].transpose(0, 2, 1, 4, 3)
            o_gdn, s_gdn = _gdn(p, gdn_conv[l], grow, hp, gdn_norm[l][None, :], s_gdn0, nseq=nseq, seq=seq, row_blk0=rb)
            o_gla, s_gla = _scan(p, (OFF_GLA_Q, OFF_GLA_V, OFF_GLA_R, OFF_GLA_K, OFF_SMALL + SM_GLA_G), gla_gate,
                                 gla_norm[l][None, :], s_gla0, mexp, lvl,
                                 nseq=nseq, seq=seq, row_blk0=rb, hgrn=False, q_scale=GLA_DK ** -0.5)
            o_hg, s_hg = _scan(p, (OFF_HG_Q, OFF_HG_I, OFF_HG_G, OFF_HG_FF, OFF_HG_FB), (lower_bounds[l],),
                               hgrn_norm[l][None, :], s_hg0, mexp, lvl,
                               nseq=nseq, seq=seq, row_blk0=rb, hgrn=True, q_scale=1.0)
            rope = si == 1
            qn, qp, ckv, kn, vv, kp = _mla_proj(p, qnw, kvnw, wqa, wkv, wqb if rope else None,
                                                cos_t if rope else None, sin_t if rope else None,
                                                nrows=nseq * seq, seq=seq, row0=row0)
            ctx = None
            if si == 1:
                knc, vvc = _kv_proj(cache_mla_ckv[:, l].reshape(nb_l * past, -1), wkv)
                kpc = jnp.pad(cache_mla_kpe[:, l].reshape(nb_l * past, MLA_ROPE), ((0, 0), (0, LANE - MLA_ROPE))).astype(bf16)
                ctx = (knc, kpc, vvc)
            o_mla = _attention(qn, qp, kn, kp, vv, ctx, nseq=nseq, seq=seq)
            outs['gdn'].append(o_gdn)
            outs['gla'].append(o_gla)
            outs['hg'].append(o_hg)
            outs['mla'].append(o_mla)
            if si == 0:
                new_gdn.append(s_gdn.transpose(0, 2, 1, 3, 4))
                new_gla.append(s_gla.transpose(0, 2, 1, 4, 3)[:, :, :, :GLA_DK, :])
                new_hg.append(s_hg.transpose(0, 2, 1, 4, 3))
                new_ckv.append(ckv.reshape(nb_c, l_c, -1))
                new_kpe.append(p[:t_c, OFF_SMALL + SM_KPE:OFF_SMALL + SM_KPE + MLA_ROPE].reshape(nb_c, l_c, MLA_ROPE))

        branches = [jnp.concatenate(outs[k], axis=0) for k in ('gdn', 'gla', 'hg', 'mla')]
        wb = _cast_layer(w_branch.reshape(depth, 4 * BRANCH_W, d), l).reshape(4, BRANCH_W, d)
        x1 = _merge(branches, p, b_gates[l], wb, _cast_layer(w_out, l), x, mod,
                    ln1_g[l][None, :], ln1_b[l][None, :], alpha=alpha, t_ctx=t_c, l_lat=l_l)
        x = _ffn(x1, mod, _cast_layer(ffn_w1, l), _cast_layer(ffn_w3, l), _cast_layer(ffn_w2, l),
                 ln2_g[l][None, :], ln2_b[l][None, :], alpha=alpha, t_ctx=t_c, l_lat=l_l, split=l == depth - 1)

    y_ctx, y_lat = x
    sdt = x_prompt.dtype
    return (y_ctx.reshape(nb_c, l_c, d), y_lat.reshape(nb_l, l_l, d),
            jnp.stack(new_gdn, axis=1).astype(sdt), jnp.stack(new_gla, axis=1).astype(sdt),
            jnp.stack(new_hg, axis=1).astype(sdt), jnp.stack(new_ckv, axis=1), jnp.stack(new_kpe, axis=1))
```
